```python
import math
import jax
import jax.numpy as jnp
from jax import lax
import numpy as np

D_MODEL = 1024
BATCH = 2
SEQ = 8192
DEPTH = 4
DEC_BATCH = 128
DEC_SEQ = 4
PAST_LEN = 8192
PAGE_SIZE = 128

A_HEADS = 4
A_DK = 64
A_DV = 128
A_GATE_RANK = 16
A_GATE_NORM = 16.0
A_CHUNK = 64
B_HEADS = 8
B_KV_HEADS = 2
B_HEAD_DIM = 64
WINDOW = 128
ROPE_DIM = B_HEAD_DIM // 4
ROPE_THETA = 500000.0
C_WIDTH = 512
C_GROUP = 16
C_GROUPS = C_WIDTH // C_GROUP
C_STATE = 64
D_FF = 2816
MACARON_W = 0.5
NORM_EPS = 1e-6
N_MOD = 9

A_QK = A_HEADS * A_DK
A_V = A_HEADS * A_DV
B_Q = B_HEADS * B_HEAD_DIM
B_KV = B_KV_HEADS * B_HEAD_DIM
B_GROUP = B_HEADS // B_KV_HEADS
IN_SIZES = (A_QK, A_QK, A_V, A_GATE_RANK, A_V, B_Q, B_KV, B_KV, C_WIDTH, 3 * D_MODEL)
IN_WIDTH = A_QK + A_QK + A_V + A_GATE_RANK + A_V + B_Q + B_KV + B_KV + C_WIDTH + 3 * D_MODEL

kernel_name = "hybrid_gla_swa_s5_decoder_step"


def rmsnorm(x, g):
    xf = x.astype(jnp.float32)
    var = jnp.mean(xf * xf, axis=-1, keepdims=True)
    return (xf * lax.rsqrt(var + NORM_EPS)).astype(x.dtype) * g


def ada_norm(x, g, shift, scale):
    return rmsnorm(x, g) * (1 + scale[:, None, :]) + shift[:, None, :]


def swiglu(h, w_gate, w_up, w_down):
    return (jax.nn.silu(h @ w_gate) * (h @ w_up)) @ w_down


def partial_rotary(x, pos):
    half = ROPE_DIM // 2
    inv_freq = ROPE_THETA ** (-jnp.arange(half, dtype=jnp.float32) / half)
    ang = pos.astype(jnp.float32)[:, None] * inv_freq[None, :]
    cos = jnp.cos(ang)[:, None, :]
    sin = jnp.sin(ang)[:, None, :]
    xr = x[..., :ROPE_DIM].astype(jnp.float32)
    x1, x2 = xr[..., :half], xr[..., half:]
    rot = jnp.concatenate([x1 * cos - x2 * sin, x2 * cos + x1 * sin], axis=-1).astype(x.dtype)
    return jnp.concatenate([rot, x[..., ROPE_DIM:]], axis=-1)


def sink_softmax(scores, sink_logit):
    m = jnp.maximum(jnp.max(scores, axis=-1, keepdims=True), sink_logit)
    p = jnp.exp(scores - m)
    return p / (jnp.sum(p, axis=-1, keepdims=True) + jnp.exp(sink_logit - m))


def gla_chunked(q, k, v, log_a, s0):
    bsz, L, H, _ = q.shape
    C = min(A_CHUNK, L)
    n = -(-L // C)
    pad = n * C - L

    def blocks(t):
        t = jnp.pad(t.astype(jnp.float32), ((0, 0), (0, pad), (0, 0), (0, 0)))
        return jnp.moveaxis(t.reshape(bsz, n, C, H, t.shape[-1]), 1, 0)

    causal = jnp.tril(jnp.ones((C, C), dtype=bool))[None, :, :, None, None]

    def chunk_step(S, blk):
        qc, kc, vc, ac = blk
        b = jnp.cumsum(ac, axis=1)
        o_inter = jnp.einsum('bthk,bhkv->bthv', qc * jnp.exp(b), S)
        diff = b[:, :, None] - b[:, None, :]
        decay = jnp.exp(jnp.where(causal, diff, -jnp.inf))
        att = jnp.einsum('bthk,bshk,btshk->bths', qc, kc, decay)
        o_intra = jnp.einsum('bths,bshv->bthv', att, vc)
        b_last = b[:, -1]
        k_dec = kc * jnp.exp(b_last[:, None] - b)
        S_new = jnp.exp(b_last)[..., None] * S + jnp.einsum('bshk,bshv->bhkv', k_dec, vc)
        return S_new, o_inter + o_intra

    S_fin, o = lax.scan(chunk_step, s0.astype(jnp.float32),
                        (blocks(q), blocks(k), blocks(v), blocks(log_a)))
    o = jnp.moveaxis(o, 0, 1).reshape(bsz, n * C, H, v.shape[-1])[:, :L]
    return o, S_fin


def swa_banded(q, k, v, sinks):
    bsz, L = q.shape[:2]
    W = WINDOW
    nb = L // W
    qb = q.reshape(bsz, nb, W, B_KV_HEADS, B_GROUP, B_HEAD_DIM)
    kb = k.reshape(bsz, nb, W, B_KV_HEADS, B_HEAD_DIM)
    vb = v.reshape(bsz, nb, W, B_KV_HEADS, B_HEAD_DIM)

    def with_prev(t):
        prev = jnp.pad(t, ((0, 0), (1, 0), (0, 0), (0, 0), (0, 0)))[:, :-1]
        return jnp.concatenate([prev, t], axis=2)

    kk, vv = with_prev(kb), with_prev(vb)
    s = jnp.einsum('bnqhgd,bnkhd->bnhgqk', qb, kk).astype(jnp.float32) * (B_HEAD_DIM ** -0.5)
    qi = jnp.arange(W)[:, None] + W
    kj = jnp.arange(2 * W)[None, :]
    band = (kj <= qi) & (qi - kj <= W)
    has_prev = (jnp.arange(nb) > 0)[:, None, None]
    mask = band[None] & (has_prev | (kj >= W)[None])
    s = jnp.where(mask[None, :, None, None], s, -jnp.inf)
    sink = sinks.astype(jnp.float32).reshape(B_KV_HEADS, B_GROUP)[None, None, :, :, None, None]
    p = sink_softmax(s, sink)
    o = jnp.einsum('bnhgqk,bnkhd->bnqhgd', p.astype(v.dtype), vv)
    return o.reshape(bsz, L, B_Q)


def swa_with_buffer(q, k, v, k_buf, v_buf, sinks):
    bsz, T = q.shape[:2]
    Wb = k_buf.shape[1]
    kk = jnp.concatenate([k_buf.astype(k.dtype), k], axis=1)
    vv = jnp.concatenate([v_buf.astype(v.dtype), v], axis=1)
    qg = q.reshape(bsz, T, B_KV_HEADS, B_GROUP, B_HEAD_DIM)
    s = jnp.einsum('bqhgd,bkhd->bhgqk', qg, kk).astype(jnp.float32) * (B_HEAD_DIM ** -0.5)
    dist = Wb + jnp.arange(T)[:, None] - jnp.arange(Wb + T)[None, :]
    mask = (dist >= 0) & (dist <= WINDOW)
    s = jnp.where(mask[None, None, None], s, -jnp.inf)
    sink = sinks.astype(jnp.float32).reshape(B_KV_HEADS, B_GROUP)[None, :, :, None, None]
    p = sink_softmax(s, sink)
    o = jnp.einsum('bhgqk,bkhd->bqhgd', p.astype(v.dtype), vv).reshape(bsz, T, B_Q)
    return o, kk[:, -Wb:], vv[:, -Wb:]


def s5_scan(u, a_re, a_im, b_re, b_im, c_re, c_im, d, log_dt, h0_re, h0_im):
    f32 = jnp.float32
    bsz, L, _ = u.shape
    uf = u.astype(f32).reshape(bsz, L, C_GROUPS, C_GROUP)
    lr, li = a_re.astype(f32), a_im.astype(f32)
    dt = jnp.exp(log_dt.astype(f32))[:, None]
    mag = jnp.exp(lr * dt)
    abar_re, abar_im = mag * jnp.cos(li * dt), mag * jnp.sin(li * dt)
    den = lr * lr + li * li
    nr, ni = abar_re - 1.0, abar_im
    f_re = (nr * lr + ni * li) / den
    f_im = (ni * lr - nr * li) / den
    br, bi = b_re.astype(f32), b_im.astype(f32)
    bb_re = f_re[..., None] * br - f_im[..., None] * bi
    bb_im = f_re[..., None] * bi + f_im[..., None] * br
    bu_re = jnp.einsum('gpc,blgc->blgp', bb_re, uf)
    bu_im = jnp.einsum('gpc,blgc->blgp', bb_im, uf)
    ar = jnp.broadcast_to(abar_re, bu_re.shape)
    ai = jnp.broadcast_to(abar_im, bu_re.shape)

    def combine(e1, e2):
        a1r, a1i, b1r, b1i = e1
        a2r, a2i, b2r, b2i = e2
        return (a2r * a1r - a2i * a1i, a2r * a1i + a2i * a1r,
                a2r * b1r - a2i * b1i + b2r, a2r * b1i + a2i * b1r + b2i)

    cr, ci, hr, hi = lax.associative_scan(combine, (ar, ai, bu_re, bu_im), axis=1)
    if h0_re is not None:
        h0r = h0_re.astype(f32)[:, None]
        h0i = h0_im.astype(f32)[:, None]
        hr, hi = hr + cr * h0r - ci * h0i, hi + cr * h0i + ci * h0r
    y = (jnp.einsum('gcp,blgp->blgc', c_re.astype(f32), hr)
         - jnp.einsum('gcp,blgp->blgc', c_im.astype(f32), hi))
    y = y.reshape(bsz, L, C_WIDTH) + d.astype(f32) * uf.reshape(bsz, L, C_WIDTH)
    return y.astype(u.dtype), hr[:, -1], hi[:, -1]


def parallel_mixers(h, pos0, lp, gla_s0, swa_k_buf, swa_v_buf, s5_h0_re, s5_h0_im):
    bsz, L, _ = h.shape
    points = [int(p) for p in np.cumsum(IN_SIZES)[:-1]]
    z = h @ lp["w_in"]
    aq, ak, av, alr, ar, bq, bk, bv, cu, gl = jnp.split(z, points, axis=-1)
    pos = pos0 + jnp.arange(L, dtype=jnp.int32)

    q = aq.reshape(bsz, L, A_HEADS, A_DK) * (A_DK ** -0.5)
    k = ak.reshape(bsz, L, A_HEADS, A_DK)
    v = av.reshape(bsz, L, A_HEADS, A_DV)
    log_a = jax.nn.log_sigmoid((alr @ lp["gla_w_gate_up"] + lp["gla_b_gate"]).astype(jnp.float32)) / A_GATE_NORM
    log_a = log_a.reshape(bsz, L, A_HEADS, A_DK)
    s0 = jnp.zeros((bsz, A_HEADS, A_DK, A_DV), jnp.float32) if gla_s0 is None else gla_s0
    o_a, gla_state = gla_chunked(q, k, v, log_a, s0)
    o_a = rmsnorm(o_a.astype(h.dtype), lp["gla_norm"]).reshape(bsz, L, A_V) * jax.nn.silu(ar)
    br_a = o_a @ lp["w_branch_a"]

    q = partial_rotary(bq.reshape(bsz, L, B_HEADS, B_HEAD_DIM), pos)
    k = partial_rotary(bk.reshape(bsz, L, B_KV_HEADS, B_HEAD_DIM), pos)
    v = bv.reshape(bsz, L, B_KV_HEADS, B_HEAD_DIM)
    if swa_k_buf is None:
        o_b = swa_banded(q, k, v, lp["swa_sinks"])
        new_k, new_v = k[:, -WINDOW:], v[:, -WINDOW:]
    else:
        o_b, new_k, new_v = swa_with_buffer(q, k, v, swa_k_buf, swa_v_buf, lp["swa_sinks"])
    br_b = o_b @ lp["w_branch_b"]

    y, s5_re, s5_im = s5_scan(cu, lp["s5_a_re"], lp["s5_a_im"], lp["s5_b_re"], lp["s5_b_im"],
                              lp["s5_c_re"], lp["s5_c_im"], lp["s5_d"], lp["s5_log_dt"],
                              s5_h0_re, s5_h0_im)
    ga, gb = jnp.split(jax.nn.gelu(y) @ lp["s5_w_glu"], 2, axis=-1)
    br_c = (ga * jax.nn.sigmoid(gb)) @ lp["w_branch_c"]

    g_a, g_b, g_c = jnp.split(jax.nn.sigmoid(gl), 3, axis=-1)
    out = (g_a * br_a + g_b * br_b + g_c * br_c) @ lp["w_out"]
    return out, (gla_state, new_k, new_v, s5_re, s5_im)


def trunk_layer(x, c, pos0, lp, gla_s0, swa_k_buf, swa_v_buf, s5_h0_re, s5_h0_im):
    mod = jax.nn.silu(c) @ lp["w_ada"] + lp["b_ada"]
    sh1, sc1, gt1, sh2, sc2, gt2, sh3, sc3, gt3 = jnp.split(mod, N_MOD, axis=-1)
    h = ada_norm(x, lp["norm_ffn1"], sh1, sc1)
    x = x + MACARON_W * gt1[:, None] * swiglu(h, lp["ffn1_w_gate"], lp["ffn1_w_up"], lp["ffn1_w_down"])
    h = ada_norm(x, lp["norm_mix"], sh2, sc2)
    mix, new_state = parallel_mixers(h, pos0, lp, gla_s0, swa_k_buf, swa_v_buf, s5_h0_re, s5_h0_im)
    x = x + gt2[:, None] * mix
    h = ada_norm(x, lp["norm_ffn2"], sh3, sc3)
    x = x + MACARON_W * gt3[:, None] * swiglu(h, lp["ffn2_w_gate"], lp["ffn2_w_up"], lp["ffn2_w_down"])
    return x, new_state


def setup_inputs(seed: int = 0) -> dict:
    key = jax.random.key(seed)
    ks = iter(jax.random.split(key, 48))
    f32 = jnp.float32

    def nrm(shape, scale):
        return jax.random.normal(next(ks), shape, f32) * scale

    win = min(WINDOW, PAST_LEN)
    D = D_MODEL
    inp = {}
    inp["x_prompt"] = nrm((BATCH, SEQ, D), 1.0)
    inp["x_sample"] = nrm((DEC_BATCH, DEC_SEQ, D), 1.0)
    inp["state_gla"] = nrm((DEPTH, DEC_BATCH, A_HEADS, A_DK, A_DV), 2.0)
    inp["cache_swa_k"] = nrm((DEPTH, DEC_BATCH, win, B_KV_HEADS, B_HEAD_DIM), 1.0)
    inp["cache_swa_v"] = nrm((DEPTH, DEC_BATCH, win, B_KV_HEADS, B_HEAD_DIM), 1.0)
    inp["state_s5_re"] = nrm((DEPTH, DEC_BATCH, C_GROUPS, C_STATE), 1.0)
    inp["state_s5_im"] = nrm((DEPTH, DEC_BATCH, C_GROUPS, C_STATE), 1.0)
    inp["c_prompt"] = nrm((BATCH, D), 1.0)
    inp["c_sample"] = nrm((DEC_BATCH, D), 1.0)
    inp["w_ada"] = nrm((DEPTH, D, N_MOD * D), 0.5 * D ** -0.5)
    inp["b_ada"] = nrm((DEPTH, N_MOD * D), 0.01)
    inp["norm_ffn1"] = 1.0 + nrm((DEPTH, D), 0.02)
    inp["ffn1_w_gate"] = nrm((DEPTH, D, D_FF), D ** -0.5)
    inp["ffn1_w_up"] = nrm((DEPTH, D, D_FF), D ** -0.5)
    inp["ffn1_w_down"] = nrm((DEPTH, D_FF, D), D_FF ** -0.5)
    inp["norm_mix"] = 1.0 + nrm((DEPTH, D), 0.02)
    inp["w_in"] = nrm((DEPTH, D, IN_WIDTH), D ** -0.5)
    inp["gla_w_gate_up"] = nrm((DEPTH, A_GATE_RANK, A_QK), A_GATE_RANK ** -0.5)
    inp["gla_b_gate"] = nrm((DEPTH, A_QK), 0.1)
    inp["gla_norm"] = 1.0 + nrm((DEPTH, A_DV), 0.02)
    inp["swa_sinks"] = nrm((DEPTH, B_HEADS), 0.5)
    inp["s5_a_re"] = -0.5 + nrm((DEPTH, C_GROUPS, C_STATE), 0.01)
    inp["s5_a_im"] = jnp.pi * jnp.arange(C_STATE, dtype=f32)[None, None, :] + nrm((DEPTH, C_GROUPS, C_STATE), 0.01)
    inp["s5_b_re"] = nrm((DEPTH, C_GROUPS, C_STATE, C_GROUP), (2.0 * C_GROUP) ** -0.5)
    inp["s5_b_im"] = nrm((DEPTH, C_GROUPS, C_STATE, C_GROUP), (2.0 * C_GROUP) ** -0.5)
    inp["s5_c_re"] = nrm((DEPTH, C_GROUPS, C_GROUP, C_STATE), (2.0 * C_STATE) ** -0.5)
    inp["s5_c_im"] = nrm((DEPTH, C_GROUPS, C_GROUP, C_STATE), (2.0 * C_STATE) ** -0.5)
    inp["s5_d"] = nrm((DEPTH, C_WIDTH), 1.0)
    inp["s5_log_dt"] = jax.random.uniform(next(ks), (DEPTH, C_GROUPS), f32,
                                          minval=math.log(0.001), maxval=math.log(0.1))
    inp["s5_w_glu"] = nrm((DEPTH, C_WIDTH, 2 * C_WIDTH), C_WIDTH ** -0.5)
    inp["w_branch_a"] = nrm((DEPTH, A_V, D), A_V ** -0.5)
    inp["w_branch_b"] = nrm((DEPTH, B_Q, D), B_Q ** -0.5)
    inp["w_branch_c"] = nrm((DEPTH, C_WIDTH, D), C_WIDTH ** -0.5)
    inp["w_out"] = nrm((DEPTH, D, D), D ** -0.5)
    inp["norm_ffn2"] = 1.0 + nrm((DEPTH, D), 0.02)
    inp["ffn2_w_gate"] = nrm((DEPTH, D, D_FF), D ** -0.5)
    inp["ffn2_w_up"] = nrm((DEPTH, D, D_FF), D ** -0.5)
    inp["ffn2_w_down"] = nrm((DEPTH, D_FF, D), D_FF ** -0.5)
    inp["norm_final"] = 1.0 + nrm((D,), 0.02)
    return inp


def reference(x_prompt, x_sample, state_gla, cache_swa_k, cache_swa_v, state_s5_re, state_s5_im,
              c_prompt, c_sample, w_ada, b_ada, norm_ffn1, ffn1_w_gate, ffn1_w_up, ffn1_w_down,
              norm_mix, w_in, gla_w_gate_up, gla_b_gate, gla_norm, swa_sinks,
              s5_a_re, s5_a_im, s5_b_re, s5_b_im, s5_c_re, s5_c_im, s5_d, s5_log_dt, s5_w_glu,
              w_branch_a, w_branch_b, w_branch_c, w_out,
              norm_ffn2, ffn2_w_gate, ffn2_w_up, ffn2_w_down, norm_final):
    y_p, y_s = x_prompt, x_sample
    gla_p, gla_s, kp, ks_, vp, vs, rp, rs, ip, is_ = [], [], [], [], [], [], [], [], [], []
    for l in range(DEPTH):
        lp = dict(w_ada=w_ada[l], b_ada=b_ada[l], norm_ffn1=norm_ffn1[l],
                  ffn1_w_gate=ffn1_w_gate[l], ffn1_w_up=ffn1_w_up[l], ffn1_w_down=ffn1_w_down[l],
                  norm_mix=norm_mix[l], w_in=w_in[l], gla_w_gate_up=gla_w_gate_up[l],
                  gla_b_gate=gla_b_gate[l], gla_norm=gla_norm[l], swa_sinks=swa_sinks[l],
                  s5_a_re=s5_a_re[l], s5_a_im=s5_a_im[l], s5_b_re=s5_b_re[l], s5_b_im=s5_b_im[l],
                  s5_c_re=s5_c_re[l], s5_c_im=s5_c_im[l], s5_d=s5_d[l], s5_log_dt=s5_log_dt[l],
                  s5_w_glu=s5_w_glu[l], w_branch_a=w_branch_a[l], w_branch_b=w_branch_b[l],
                  w_branch_c=w_branch_c[l], w_out=w_out[l], norm_ffn2=norm_ffn2[l],
                  ffn2_w_gate=ffn2_w_gate[l], ffn2_w_up=ffn2_w_up[l], ffn2_w_down=ffn2_w_down[l])
        y_p, (g1, k1, v1, r1, i1) = trunk_layer(y_p, c_prompt, 0, lp, None, None, None, None, None)
        y_s, (g2, k2, v2, r2, i2) = trunk_layer(y_s, c_sample, PAST_LEN, lp, state_gla[l],
                                                cache_swa_k[l], cache_swa_v[l],
                                                state_s5_re[l], state_s5_im[l])
        gla_p.append(g1); gla_s.append(g2)
        kp.append(k1); ks_.append(k2)
        vp.append(v1); vs.append(v2)
        rp.append(r1); rs.append(r2)
        ip.append(i1); is_.append(i2)
    y_prompt = rmsnorm(y_p, norm_final)
    y_sample = rmsnorm(y_s, norm_final)
    return (y_prompt, y_sample,
            jnp.stack(gla_p), jnp.stack(gla_s),
            jnp.stack(kp), jnp.stack(ks_),
            jnp.stack(vp), jnp.stack(vs),
            jnp.stack(rp), jnp.stack(rs),
            jnp.stack(ip), jnp.stack(is_))
```

```python
import functools
import math

import jax
import jax.numpy as jnp
from jax import lax
from jax.experimental import pallas as pl
from jax.experimental.pallas import tpu as pltpu

D_MODEL = 1024
D_FF = 2816
PAST_LEN = 8192
A_HEADS, A_DK, A_DV = 4, 64, 128
A_GATE_RANK = 16
A_GATE_NORM = 16.0
B_HEADS, B_KV_HEADS, B_HEAD_DIM = 8, 2, 64
B_GROUP = B_HEADS // B_KV_HEADS
WINDOW = 128
ROPE_DIM = B_HEAD_DIM // 4
ROPE_THETA = 500000.0
C_WIDTH, C_GROUP, C_STATE = 512, 16, 64
C_GROUPS = C_WIDTH // C_GROUP
MACARON_W = 0.5
NORM_EPS = 1e-6
N_MOD = 9

A_QK = A_HEADS * A_DK
A_V = A_HEADS * A_DV
B_Q = B_HEADS * B_HEAD_DIM
B_KV = B_KV_HEADS * B_HEAD_DIM
S5_N = C_GROUPS * C_STATE

LANES = 128
TOKEN_TILE = 512
GLA_CHUNK = 64
GLA_BLOCK = 32
S5_CHUNK = 64
VMEM_LIMIT = 56 * 1024 * 1024

F32 = jnp.float32
BF16 = jnp.bfloat16
NEG_BIG = -1e30


def _cparams(sem):
    return pltpu.CompilerParams(dimension_semantics=sem, vmem_limit_bytes=VMEM_LIMIT)


def _dot(a, b):
    return jnp.dot(a, b, preferred_element_type=F32)


def _dot_nt(a, b):
    return lax.dot_general(a, b, (((1,), (1,)), ((), ())), preferred_element_type=F32)


def _sigmoid(x):
    return 1.0 / (1.0 + jnp.exp(-x))


def _silu(x):
    return x * _sigmoid(x)


def _gelu_tanh(x):
    return 0.5 * x * (1.0 + jnp.tanh(math.sqrt(2.0 / math.pi) * (x + 0.044715 * (x * x * x))))


def _log_sigmoid(x):
    return jnp.minimum(x, 0.0) - jnp.log1p(jnp.exp(-jnp.abs(x)))


def _resident(shape, index_map):
    return pl.BlockSpec(shape, index_map, pipeline_mode=pl.Buffered(1))


def _mod_kernel(c_ref, w_ref, b_ref, o_ref):
    c = c_ref[...]
    s = _silu(c).astype(BF16)
    o_ref[...] = _dot(s, w_ref[...].astype(BF16)) + b_ref[...]


def _mod_call(c_all, w_ada, b_ada):
    depth = w_ada.shape[0]
    rows = c_all.shape[0]
    return pl.pallas_call(
        _mod_kernel,
        out_shape=jax.ShapeDtypeStruct((depth, rows, N_MOD * D_MODEL), F32),
        grid=(depth, N_MOD),
        in_specs=[
            pl.BlockSpec((rows, D_MODEL), lambda l, j: (0, 0)),
            pl.BlockSpec((None, D_MODEL, D_MODEL), lambda l, j: (l, 0, j)),
            pl.BlockSpec((None, 1, D_MODEL), lambda l, j: (l, 0, j)),
        ],
        out_specs=pl.BlockSpec((None, rows, D_MODEL), lambda l, j: (l, 0, j)),
        compiler_params=_cparams(("arbitrary", "arbitrary")),
        name="ada_mod",
    )(c_all, w_ada, b_ada.reshape(depth, 1, N_MOD * D_MODEL))


class _Rows:
    def __init__(self, n_prompt_seq, seq_len, n_sample_seq, n_sample_tok):
        self.b = n_prompt_seq
        self.l = seq_len
        self.bs = n_sample_seq
        self.t = n_sample_tok
        self.np_rows = n_prompt_seq * seq_len
        self.ns_rows = n_sample_seq * n_sample_tok
        assert seq_len % TOKEN_TILE == 0 and self.ns_rows == TOKEN_TILE
        assert n_sample_seq % 8 == 0 and seq_len % WINDOW == 0
        self.rows = self.np_rows + self.ns_rows
        self.n_prompt_tiles = self.np_rows // TOKEN_TILE
        self.tiles_per_seq = seq_len // TOKEN_TILE
        self.n_tiles = self.n_prompt_tiles + 1
        self.mod_rows = n_sample_seq + 8
        assert n_prompt_seq <= 8


def _mod_specs(rows, layer, j):
    return [
        pl.BlockSpec((None, rows.bs, D_MODEL), lambda i, l=layer, j=j: (l, 0, j)),
        pl.BlockSpec((None, 8, D_MODEL), lambda i, l=layer, j=j, r=rows.bs // 8: (l, r, j)),
    ]


def _mod_value(rows, ms_ref, mp_ref):
    i = pl.program_id(0)
    is_sample = i == rows.n_prompt_tiles
    seq = jnp.minimum(i // rows.tiles_per_seq, rows.b - 1)
    s = ms_ref[...]
    s = jnp.concatenate([s] * rows.t, axis=0)
    p = mp_ref[pl.ds(seq, 1), :]
    return jnp.where(is_sample, s, p)


def _ada_norm(x, g, shift, scale):
    var = jnp.mean(x * x, axis=-1, keepdims=True)
    return (x * lax.rsqrt(var + NORM_EPS)) * g * (1.0 + scale) + shift


def _select_rows(rows, p_ref, s_ref):
    is_sample = pl.program_id(0) == rows.n_prompt_tiles
    return jnp.where(is_sample, s_ref[...], p_ref[...])


def _pair_specs(rows, width):
    last = rows.n_prompt_tiles - 1
    return [
        pl.BlockSpec((TOKEN_TILE, width), lambda i, last=last: (jnp.minimum(i, last), 0)),
        pl.BlockSpec((TOKEN_TILE, width), lambda i: (0, 0)),
    ]


FF_CHUNK = D_FF // 2


def _ffn_kernel(x_ref, g_ref, shs, shp, scs, scp, gts, gtp, wg_ref, wu_ref, wd_ref, o_ref, *, rows):
    x = x_ref[...]
    h = _ada_norm(x, g_ref[...], _mod_value(rows, shs, shp), _mod_value(rows, scs, scp)).astype(BF16)
    acc = jnp.zeros((TOKEN_TILE, D_MODEL), F32)
    for c in range(D_FF // FF_CHUNK):
        sl = slice(c * FF_CHUNK, (c + 1) * FF_CHUNK)
        gate = _dot(h, wg_ref[:, sl])
        up = _dot(h, wu_ref[:, sl])
        a = (_silu(gate) * up).astype(BF16)
        acc = acc + _dot(a, wd_ref[sl, :])
    o_ref[...] = x + (MACARON_W * _mod_value(rows, gts, gtp)) * acc


def _ffn_call(rows, layer, mod_base, x, mod, norm_g, wg, wu, wd):
    in_specs = [
        pl.BlockSpec((TOKEN_TILE, D_MODEL), lambda i: (i, 0)),
        pl.BlockSpec((None, 1, D_MODEL), lambda i, l=layer: (l, 0, 0)),
    ]
    operands = [x, norm_g]
    for j in range(3):
        in_specs += _mod_specs(rows, layer, mod_base + j)
        operands += [mod, mod]
    in_specs += [
        _resident((None, D_MODEL, D_FF), lambda i, l=layer: (l, 0, 0)),
        _resident((None, D_MODEL, D_FF), lambda i, l=layer: (l, 0, 0)),
        _resident((None, D_FF, D_MODEL), lambda i, l=layer: (l, 0, 0)),
    ]
    operands += [wg, wu, wd]
    return pl.pallas_call(
        functools.partial(_ffn_kernel, rows=rows),
        out_shape=jax.ShapeDtypeStruct((rows.rows, D_MODEL), F32),
        grid=(rows.n_tiles,),
        in_specs=in_specs,
        out_specs=pl.BlockSpec((TOKEN_TILE, D_MODEL), lambda i: (i, 0)),
        compiler_params=_cparams(("arbitrary",)),
        name="ada_swiglu",
    )(*operands)


W_MAIN = 2 * A_QK + A_V + B_Q + 2 * B_KV + C_WIDTH


def _rotary(x, cos, s_up, s_dn):
    half = ROPE_DIM // 2
    return x * cos + pltpu.roll(x, half, axis=1) * s_up + pltpu.roll(x, LANES - half, axis=1) * s_dn


def _inproj_kernel(x_ref, g_ref, shs, shp, scs, scp, wm_ref, wr_ref, wup_ref, bg_ref,
                   cos_ref, sup_ref, sdn_ref,
                   gq_ref, gk_ref, gv_ref, la_ref, sq_ref, sk_ref, sv_ref, su_ref, *, rows):
    x = x_ref[...]
    h = _ada_norm(x, g_ref[...], _mod_value(rows, shs, shp), _mod_value(rows, scs, scp)).astype(BF16)
    z = _dot(h, wm_ref[...])
    o = 0
    gq_ref[...] = z[:, o:o + A_QK] * (A_DK ** -0.5)
    o += A_QK
    gk_ref[...] = z[:, o:o + A_QK]
    o += A_QK
    gv_ref[...] = z[:, o:o + A_V]
    o += A_V
    cos, s_up, s_dn = cos_ref[...], sup_ref[...], sdn_ref[...]
    for c in range(B_Q // LANES):
        sq_ref[:, c * LANES:(c + 1) * LANES] = _rotary(z[:, o + c * LANES:o + (c + 1) * LANES], cos, s_up, s_dn)
    o += B_Q
    sk_ref[...] = _rotary(z[:, o:o + B_KV], cos, s_up, s_dn)
    o += B_KV
    sv_ref[...] = z[:, o:o + B_KV]
    o += B_KV
    su_ref[...] = z[:, o:o + C_WIDTH]
    low = _dot(h, wr_ref[...]).astype(BF16)
    pre = _dot(low, wup_ref[...]) + bg_ref[...]
    la_ref[...] = _log_sigmoid(pre) * (1.0 / A_GATE_NORM)


def _inproj_call(rows, layer, x, mod, norm_g, w_main, w_rank, w_up, b_gate, rot):
    row_spec = lambda w: pl.BlockSpec((TOKEN_TILE, w), lambda i: (i, 0))
    in_specs = [row_spec(D_MODEL), pl.BlockSpec((None, 1, D_MODEL), lambda i, l=layer: (l, 0, 0))]
    operands = [x, norm_g]
    for j in (3, 4):
        in_specs += _mod_specs(rows, layer, j)
        operands += [mod, mod]
    in_specs += [
        _resident((None, D_MODEL, W_MAIN), lambda i, l=layer: (l, 0, 0)),
        _resident((None, D_MODEL, LANES), lambda i, l=layer: (l, 0, 0)),
        _resident((None, LANES, A_QK), lambda i, l=layer: (l, 0, 0)),
        pl.BlockSpec((None, 1, A_QK), lambda i, l=layer: (l, 0, 0)),
        row_spec(LANES), row_spec(LANES), row_spec(LANES),
    ]
    operands += [w_main, w_rank, w_up, b_gate, *rot]
    widths = (A_QK, A_QK, A_V, A_QK, B_Q, B_KV, B_KV, C_WIDTH)
    return pl.pallas_call(
        functools.partial(_inproj_kernel, rows=rows),
        out_shape=[jax.ShapeDtypeStruct((rows.rows, w), F32) for w in widths],
        grid=(rows.n_tiles,),
        in_specs=in_specs,
        out_specs=[row_spec(w) for w in widths],
        compiler_params=_cparams(("arbitrary",)),
        name="mixer_in_proj",
    )(*operands)


def _cumsum_rows(x):
    n = x.shape[0]
    row = lax.broadcasted_iota(jnp.int32, x.shape, 0)
    s = 1
    while s < n:
        x = x + jnp.where(row >= s, pltpu.roll(x, s, axis=0), 0.0)
        s *= 2
    return x


def _head_blockdiag(x, lanes_per_head):
    lane_head = lax.broadcasted_iota(jnp.int32, x.shape, 1) // lanes_per_head
    return jnp.concatenate([jnp.where(lane_head == h, x, 0.0) for h in range(A_HEADS)], axis=0)


def _gla_kernel(*refs, chunk, block, n_chunks, has_state):
    if has_state:
        q_ref, k_ref, v_ref, la_ref, gn_ref, s0_ref, o_ref, so_ref, st_ref = refs
    else:
        q_ref, k_ref, v_ref, la_ref, gn_ref, o_ref, so_ref, st_ref = refs
    step = pl.program_id(1)
    n_blocks = chunk // block

    @pl.when(step == 0)
    def _():
        st_ref[...] = jnp.zeros(st_ref.shape, F32)
        if has_state:
            for h in range(A_HEADS):
                st_ref[h * A_DV:(h + 1) * A_DV, h * A_DK:(h + 1) * A_DK] = s0_ref[h].T

    st_row_head = lax.broadcasted_iota(jnp.int32, st_ref.shape, 0) // A_DV
    st_lane_head = lax.broadcasted_iota(jnp.int32, st_ref.shape, 1) // A_DK
    st_mask = st_row_head == st_lane_head
    srow = lax.broadcasted_iota(jnp.int32, (block, A_HEADS * block), 0)
    scol = lax.broadcasted_iota(jnp.int32, (block, A_HEADS * block), 1) % block
    causal = scol <= srow
    gn = gn_ref[...]

    def one_chunk(c, carry):
        r0 = pl.multiple_of(c * chunk, chunk)
        q = q_ref[pl.ds(r0, chunk), :]
        k = k_ref[pl.ds(r0, chunk), :]
        v = v_ref[pl.ds(r0, chunk), :]
        b = _cumsum_rows(la_ref[pl.ds(r0, chunk), :])
        b_last = b[chunk - 1:chunk, :]
        st = st_ref[...]
        o = _dot_nt((q * jnp.exp(b)).astype(BF16), st.astype(BF16))
        outs = []
        for i in range(n_blocks):
            lo, hi = i * block, (i + 1) * block
            qi, ki, bi = q[lo:hi], k[lo:hi], b[lo:hi]
            mid = b[lo + block // 2 - 1:lo + block // 2, :]
            qd = (qi * jnp.exp(bi - mid)).astype(BF16)
            kd = _head_blockdiag(ki * jnp.exp(mid - bi), A_DK).astype(BF16)
            att = jnp.where(causal, _dot_nt(qd, kd), 0.0).astype(BF16)
            oi = _dot(att, _head_blockdiag(v[lo:hi], A_DV).astype(BF16))
            if i > 0:
                ref = b[lo - 1:lo, :]
                qo = (qi * jnp.exp(bi - ref)).astype(BF16)
                ko = _head_blockdiag(k[:lo] * jnp.exp(ref - b[:lo]), A_DK).astype(BF16)
                oi = oi + _dot(_dot_nt(qo, ko).astype(BF16), _head_blockdiag(v[:lo], A_DV).astype(BF16))
            outs.append(oi)
        o = o + (jnp.concatenate(outs, axis=0) if n_blocks > 1 else outs[0])
        kdec = (k * jnp.exp(b_last - b)).astype(BF16)
        upd = _dot(v.T.astype(BF16), kdec)
        st_ref[...] = st * jnp.exp(b_last) + jnp.where(st_mask, upd, 0.0)
        normed = []
        for h in range(A_HEADS):
            oh = o[:, h * A_DV:(h + 1) * A_DV]
            var = jnp.mean(oh * oh, axis=-1, keepdims=True)
            normed.append(oh * lax.rsqrt(var + NORM_EPS) * gn)
        o_ref[pl.ds(r0, chunk), :] = jnp.concatenate(normed, axis=1)
        return carry

    lax.fori_loop(0, n_chunks, one_chunk, 0)

    @pl.when(step == pl.num_programs(1) - 1)
    def _():
        for h in range(A_HEADS):
            so_ref[h] = st_ref[h * A_DV:(h + 1) * A_DV, h * A_DK:(h + 1) * A_DK].T


def _gla_call(layer, q, k, v, la, gnorm, s0, n_seq, seq_rows, step_rows, chunk, block):
    steps = seq_rows // step_rows
    row_spec = lambda w: pl.BlockSpec((step_rows, w), lambda s, j, steps=steps: (s * steps + j, 0))
    in_specs = [row_spec(A_QK), row_spec(A_QK), row_spec(A_V), row_spec(A_QK),
                pl.BlockSpec((None, 1, A_DV), lambda s, j, l=layer: (l, 0, 0))]
    operands = [q, k, v, la, gnorm]
    if s0 is not None:
        in_specs.append(pl.BlockSpec((None, None, A_HEADS, A_DK, A_DV), lambda s, j, l=layer: (l, s, 0, 0, 0)))
        operands.append(s0)
    return pl.pallas_call(
        functools.partial(_gla_kernel, chunk=chunk, block=block, n_chunks=step_rows // chunk,
                          has_state=s0 is not None),
        out_shape=[jax.ShapeDtypeStruct((n_seq * seq_rows, A_V), F32),
                   jax.ShapeDtypeStruct((n_seq, A_HEADS, A_DK, A_DV), F32)],
        grid=(n_seq, steps),
        in_specs=in_specs,
        out_specs=[row_spec(A_V),
                   pl.BlockSpec((None, A_HEADS, A_DK, A_DV), lambda s, j: (s, 0, 0, 0))],
        scratch_shapes=[pltpu.VMEM((A_V, A_QK), F32)],
        compiler_params=_cparams(("arbitrary", "arbitrary")),
        name="gla_chunked",
    )(*operands)


def _group_blockdiag(x):
    n = x.shape[0]
    swapped = pltpu.roll(x, B_HEAD_DIM, axis=1)
    first = lax.broadcasted_iota(jnp.int32, x.shape, 1) < B_HEAD_DIM
    out = []
    for h in range(B_KV_HEADS):
        xh = jnp.where(first, x, swapped) if h == 0 else jnp.where(first, swapped, x)
        xh = jnp.concatenate([xh, xh], axis=1)
        slot = lax.broadcasted_iota(jnp.int32, (n, B_GROUP * B_HEAD_DIM), 1) // B_HEAD_DIM
        out.append(jnp.concatenate([jnp.where(slot == g, xh, 0.0) for g in range(B_GROUP)], axis=0).astype(BF16))
    return out


def _swa_kernel(sink_ref, q_ref, kc_ref, vc_ref, kp_ref, vp_ref, o_ref, *, q_rows, layer, prev_always):
    n = pl.program_id(1)
    tk = 2 * WINDOW
    pad = WINDOW - q_rows
    kparts = [kp_ref[...], kc_ref[...]]
    vparts = [vp_ref[...], vc_ref[...]]
    if pad:
        kparts.append(jnp.zeros((pad, B_KV), F32))
        vparts.append(jnp.zeros((pad, B_KV), F32))
    kbd = _group_blockdiag(jnp.concatenate(kparts, axis=0))
    vbd = _group_blockdiag(jnp.concatenate(vparts, axis=0))
    qi = lax.broadcasted_iota(jnp.int32, (q_rows, tk), 0) + WINDOW
    kj = lax.broadcasted_iota(jnp.int32, (q_rows, tk), 1)
    first_key = 0 if prev_always else jnp.where(n > 0, 0, WINDOW)
    valid = (kj <= qi) & (kj >= jnp.maximum(qi - WINDOW, first_key))
    gw = B_GROUP * B_HEAD_DIM
    slot = lax.broadcasted_iota(jnp.int32, (q_rows, gw), 1) // B_HEAD_DIM
    for h in range(B_KV_HEADS):
        qh = q_ref[:, h * gw:(h + 1) * gw].astype(BF16)
        s = _dot_nt(qh, kbd[h]) * (B_HEAD_DIM ** -0.5)
        probs = []
        inv = jnp.zeros((q_rows, gw), F32)
        for g in range(B_GROUP):
            sink = sink_ref[layer, h * B_GROUP + g]
            sg = jnp.where(valid, s[:, g * tk:(g + 1) * tk], NEG_BIG)
            m = jnp.maximum(jnp.max(sg, axis=-1, keepdims=True), sink)
            p = jnp.exp(sg - m)
            den = jnp.sum(p, axis=-1, keepdims=True) + jnp.exp(sink - m)
            probs.append(p.astype(BF16))
            inv = jnp.where(slot == g, 1.0 / den, inv)
        o_ref[:, h * gw:(h + 1) * gw] = _dot(jnp.concatenate(probs, axis=1), vbd[h]) * inv


def _swa_call(layer, sinks, q, k, v, k_prev, v_prev, n_seq, blocks_per_seq, q_rows):
    nb = blocks_per_seq
    cur = lambda w: pl.BlockSpec((q_rows, w), lambda s, n, nb=nb: (s * nb + n, 0))
    prev_always = k_prev is not None
    if prev_always:
        prev = pl.BlockSpec((None, None, WINDOW, B_KV), lambda s, n, l=layer: (l, s, 0, 0))
    else:
        prev = pl.BlockSpec((WINDOW, B_KV), lambda s, n, nb=nb: (s * nb + jnp.maximum(n - 1, 0), 0))
        k_prev, v_prev = k, v
    return pl.pallas_call(
        functools.partial(_swa_kernel, q_rows=q_rows, layer=layer, prev_always=prev_always),
        out_shape=jax.ShapeDtypeStruct((n_seq * nb * q_rows, B_Q), F32),
        grid=(n_seq, nb),
        in_specs=[pl.BlockSpec(memory_space=pltpu.SMEM), cur(B_Q), cur(B_KV), cur(B_KV), prev, prev],
        out_specs=cur(B_Q),
        compiler_params=_cparams(("arbitrary", "arbitrary")),
        name="swa_banded",
    )(sinks, q, k, v, k_prev, v_prev)


def _s5_prep_kernel(lr_ref, li_ref, ldt_ref, br_ref, bi_ref,
                    bbr_ref, bbi_ref, ar_ref, ai_ref, er_ref, ei_ref, fr_ref, fi_ref):
    lr, li = lr_ref[...], li_ref[...]
    dt = jnp.exp(ldt_ref[...])
    rho, th = lr * dt, li * dt
    mag = jnp.exp(rho)
    a_re, a_im = mag * jnp.cos(th), mag * jnp.sin(th)
    den = lr * lr + li * li
    nr, ni = a_re - 1.0, a_im
    f_re = (nr * lr + ni * li) / den
    f_im = (ni * lr - nr * li) / den
    br, bi = br_ref[...], bi_ref[...]
    bbr_ref[...] = f_re * br - f_im * bi
    bbi_ref[...] = f_re * bi + f_im * br
    ar_ref[...] = a_re
    ai_ref[...] = a_im
    s = (lax.broadcasted_iota(jnp.int32, er_ref.shape, 0) + 1).astype(F32)
    cs, sn = jnp.cos(s * th), jnp.sin(s * th)
    e_mag, f_mag = jnp.exp(-(s * rho)), jnp.exp(s * rho)
    er_ref[...] = e_mag * cs
    ei_ref[...] = -(e_mag * sn)
    fr_ref[...] = f_mag * cs
    fi_ref[...] = f_mag * sn


def _s5_prep_call(lr, li, ldt, b_re_t, b_im_t):
    depth = lr.shape[0]
    vec = pl.BlockSpec((None, 1, S5_N), lambda l: (l, 0, 0))
    mat = pl.BlockSpec((None, C_GROUP, S5_N), lambda l: (l, 0, 0))
    tab = pl.BlockSpec((None, S5_CHUNK, S5_N), lambda l: (l, 0, 0))
    sds = lambda r: jax.ShapeDtypeStruct((depth, r, S5_N), F32)
    return pl.pallas_call(
        _s5_prep_kernel,
        out_shape=[sds(C_GROUP), sds(C_GROUP), sds(1), sds(1)] + [sds(S5_CHUNK)] * 4,
        grid=(depth,),
        in_specs=[vec, vec, vec, mat, mat],
        out_specs=[mat, mat, vec, vec, tab, tab, tab, tab],
        compiler_params=_cparams(("arbitrary",)),
        name="s5_discretise",
    )(lr, li, ldt, b_re_t, b_im_t)


def _cmul(ar, ai, br, bi):
    return ar * br - ai * bi, ar * bi + ai * br


def _s5_prompt_kernel(u_ref, bw_ref, cr_ref, ci_ref, d_ref, er_ref, ei_ref, fr_ref, fi_ref,
                      y_ref, hs_ref, car_ref, cai_ref, *, n_chunks):
    step = pl.program_id(1)
    t = S5_CHUNK

    @pl.when(step == 0)
    def _():
        car_ref[...] = jnp.zeros(car_ref.shape, F32)
        cai_ref[...] = jnp.zeros(cai_ref.shape, F32)

    row = lax.broadcasted_iota(jnp.int32, (t, t), 0)
    col = lax.broadcasted_iota(jnp.int32, (t, t), 1)
    tril = jnp.where(col <= row, 1.0, 0.0).astype(BF16)

    def one_chunk(c, carry):
        r0 = pl.multiple_of(c * t, t)
        u = u_ref[pl.ds(r0, t), :]
        bu = _dot(u.astype(BF16), bw_ref[...])
        xr, xi = _cmul(er_ref[...], ei_ref[...], bu[:, :S5_N], bu[:, S5_N:])
        hr = _dot(tril, xr.astype(BF16)) + car_ref[...]
        hi = _dot(tril, xi.astype(BF16)) + cai_ref[...]
        hr, hi = _cmul(fr_ref[...], fi_ref[...], hr, hi)
        y_ref[pl.ds(r0, t), :] = (_dot(hr.astype(BF16), cr_ref[...]) - _dot(hi.astype(BF16), ci_ref[...])
                                  + d_ref[...] * u)
        car_ref[...] = hr[t - 1:t, :]
        cai_ref[...] = hi[t - 1:t, :]
        return carry

    lax.fori_loop(0, n_chunks, one_chunk, 0)

    @pl.when(step == pl.num_programs(1) - 1)
    def _():
        hs_ref[0:1, :] = car_ref[...]
        hs_ref[1:2, :] = cai_ref[...]


def _s5_prompt_call(layer, u, bw, c_re, c_im, d, tabs, n_seq, seq_rows, step_rows):
    steps = seq_rows // step_rows
    row_spec = pl.BlockSpec((step_rows, C_WIDTH), lambda s, j, steps=steps: (s * steps + j, 0))
    lsel = lambda s, j, l=layer: (l, 0, 0)
    tab = pl.BlockSpec((None, S5_CHUNK, S5_N), lsel)
    return pl.pallas_call(
        functools.partial(_s5_prompt_kernel, n_chunks=step_rows // S5_CHUNK),
        out_shape=[jax.ShapeDtypeStruct((n_seq * seq_rows, C_WIDTH), F32),
                   jax.ShapeDtypeStruct((n_seq, 2, S5_N), F32)],
        grid=(n_seq, steps),
        in_specs=[row_spec,
                  _resident((None, C_WIDTH, 2 * S5_N), lsel),
                  _resident((None, S5_N, C_WIDTH), lsel),
                  _resident((None, S5_N, C_WIDTH), lsel),
                  pl.BlockSpec((None, 1, C_WIDTH), lsel),
                  tab, tab, tab, tab],
        out_specs=[row_spec, pl.BlockSpec((None, 2, S5_N), lambda s, j: (s, 0, 0))],
        scratch_shapes=[pltpu.VMEM((1, S5_N), F32), pltpu.VMEM((1, S5_N), F32)],
        compiler_params=_cparams(("arbitrary", "arbitrary")),
        name="s5_chunked_scan",
    )(u, bw, c_re, c_im, d, *tabs)


def _s5_sample_kernel(u_ref, h0r_ref, h0i_ref, ar_ref, ai_ref, bw_ref, cr_ref, ci_ref, d_ref,
                      y_ref, hr_ref, hi_ref, *, n_tok, n_seq):
    hr, hi = h0r_ref[...], h0i_ref[...]
    a_re, a_im = ar_ref[...], ai_ref[...]
    for t in range(n_tok):
        u = u_ref[t * n_seq:(t + 1) * n_seq, :]
        bu = _dot(u.astype(BF16), bw_ref[...])
        hr, hi = _cmul(a_re, a_im, hr, hi)
        hr, hi = hr + bu[:, :S5_N], hi + bu[:, S5_N:]
        y_ref[t * n_seq:(t + 1) * n_seq, :] = (_dot(hr.astype(BF16), cr_ref[...]) - _dot(hi.astype(BF16), ci_ref[...])
                                               + d_ref[...] * u)
    hr_ref[...] = hr
    hi_ref[...] = hi


def _s5_sample_call(layer, u, h0_re, h0_im, a_re, a_im, bw, c_re, c_im, d, n_tok, n_seq):
    lsel = lambda i, l=layer: (l, 0, 0)
    full = lambda shape: pl.BlockSpec(shape, lambda i: (0,) * len(shape))
    return pl.pallas_call(
        functools.partial(_s5_sample_kernel, n_tok=n_tok, n_seq=n_seq),
        out_shape=[jax.ShapeDtypeStruct((n_tok * n_seq, C_WIDTH), F32),
                   jax.ShapeDtypeStruct((n_seq, S5_N), F32),
                   jax.ShapeDtypeStruct((n_seq, S5_N), F32)],
        grid=(1,),
        in_specs=[full((n_tok * n_seq, C_WIDTH)),
                  pl.BlockSpec((None, n_seq, S5_N), lsel),
                  pl.BlockSpec((None, n_seq, S5_N), lsel),
                  pl.BlockSpec((None, 1, S5_N), lsel),
                  pl.BlockSpec((None, 1, S5_N), lsel),
                  pl.BlockSpec((None, C_WIDTH, 2 * S5_N), lsel),
                  pl.BlockSpec((None, S5_N, C_WIDTH), lsel),
                  pl.BlockSpec((None, S5_N, C_WIDTH), lsel),
                  pl.BlockSpec((None, 1, C_WIDTH), lsel)],
        out_specs=[full((n_tok * n_seq, C_WIDTH)), full((n_seq, S5_N)), full((n_seq, S5_N))],
        compiler_params=_cparams(("arbitrary",)),
        name="s5_step_scan",
    )(u, h0_re, h0_im, a_re, a_im, bw, c_re, c_im, d)


W_GATES = A_V + 3 * D_MODEL


def _merge_kernel(x_ref, g_ref, shs, shp, scs, scp, gts, gtp,
                  oap, oas, obp, obs, ocp, ocs,
                  wz_ref, wa_ref, wb_ref, wc_ref, wglu_ref, wo_ref, o_ref, *, rows):
    x = x_ref[...]
    h = _ada_norm(x, g_ref[...], _mod_value(rows, shs, shp), _mod_value(rows, scs, scp)).astype(BF16)
    z = _dot(h, wz_ref[...])
    oa = _select_rows(rows, oap, oas) * _silu(z[:, :A_V])
    br_a = _dot(oa.astype(BF16), wa_ref[...])
    br_b = _dot(_select_rows(rows, obp, obs).astype(BF16), wb_ref[...])
    glu = _dot(_gelu_tanh(_select_rows(rows, ocp, ocs)).astype(BF16), wglu_ref[...])
    oc = glu[:, :C_WIDTH] * _sigmoid(glu[:, C_WIDTH:])
    br_c = _dot(oc.astype(BF16), wc_ref[...])
    o = A_V
    mix = (_sigmoid(z[:, o:o + D_MODEL]) * br_a
           + _sigmoid(z[:, o + D_MODEL:o + 2 * D_MODEL]) * br_b
           + _sigmoid(z[:, o + 2 * D_MODEL:o + 3 * D_MODEL]) * br_c)
    out = _dot(mix.astype(BF16), wo_ref[...])
    o_ref[...] = x + _mod_value(rows, gts, gtp) * out


def _merge_call(rows, layer, x, mod, norm_g, oa, ob, oc, wz, wa, wb, wc, wglu, wo):
    in_specs = [pl.BlockSpec((TOKEN_TILE, D_MODEL), lambda i: (i, 0)),
                pl.BlockSpec((None, 1, D_MODEL), lambda i, l=layer: (l, 0, 0))]
    operands = [x, norm_g]
    for j in (3, 4, 5):
        in_specs += _mod_specs(rows, layer, j)
        operands += [mod, mod]
    for pair, w in ((oa, A_V), (ob, B_Q), (oc, C_WIDTH)):
        in_specs += _pair_specs(rows, w)
        operands += list(pair)
    lsel = lambda i, l=layer: (l, 0, 0)
    in_specs += [_resident((None, D_MODEL, W_GATES), lsel),
                 _resident((None, A_V, D_MODEL), lsel),
                 _resident((None, B_Q, D_MODEL), lsel),
                 _resident((None, C_WIDTH, D_MODEL), lsel),
                 _resident((None, C_WIDTH, 2 * C_WIDTH), lsel),
                 _resident((None, D_MODEL, D_MODEL), lsel)]
    operands += [wz, wa, wb, wc, wglu, wo]
    return pl.pallas_call(
        functools.partial(_merge_kernel, rows=rows),
        out_shape=jax.ShapeDtypeStruct((rows.rows, D_MODEL), F32),
        grid=(rows.n_tiles,),
        in_specs=in_specs,
        out_specs=pl.BlockSpec((TOKEN_TILE, D_MODEL), lambda i: (i, 0)),
        compiler_params=_cparams(("arbitrary",)),
        name="branch_merge",
    )(*operands)


def _final_kernel(x_ref, g_ref, o_ref):
    x = x_ref[...]
    var = jnp.mean(x * x, axis=-1, keepdims=True)
    o_ref[...] = x * lax.rsqrt(var + NORM_EPS) * g_ref[...]


def _final_call(x, g, first_tile, n_tiles):
    return pl.pallas_call(
        _final_kernel,
        out_shape=jax.ShapeDtypeStruct((n_tiles * TOKEN_TILE, D_MODEL), F32),
        grid=(n_tiles,),
        in_specs=[pl.BlockSpec((TOKEN_TILE, D_MODEL), lambda i, f=first_tile: (i + f, 0)),
                  pl.BlockSpec((1, D_MODEL), lambda i: (0, 0))],
        out_specs=pl.BlockSpec((TOKEN_TILE, D_MODEL), lambda i: (i, 0)),
        compiler_params=_cparams(("arbitrary",)),
        name="final_norm",
    )(x, g)


def _rotary_tables(rows):
    half = ROPE_DIM // 2
    inv_freq = ROPE_THETA ** (-jnp.arange(half, dtype=F32) / half)
    pos_p = jnp.tile(jnp.arange(rows.l, dtype=jnp.int32), rows.b)
    pos_s = PAST_LEN + jnp.repeat(jnp.arange(rows.t, dtype=jnp.int32), rows.bs)
    pos = jnp.concatenate([pos_p, pos_s])
    ang = pos.astype(F32)[:, None] * inv_freq[None, :]
    cos, sin = jnp.cos(ang), jnp.sin(ang)
    n = pos.shape[0]
    ones = jnp.ones((n, B_HEAD_DIM - ROPE_DIM), F32)
    zeros = jnp.zeros((n, B_HEAD_DIM - ROPE_DIM), F32)
    z8 = jnp.zeros((n, half), F32)
    per_head = lambda parts: jnp.tile(jnp.concatenate(parts, axis=1), (1, LANES // B_HEAD_DIM))
    return (per_head([cos, cos, ones]), per_head([z8, sin, zeros]), per_head([-sin, z8, zeros]))


def _sample_to_seq(a, rows, pad_to):
    w = a.shape[-1]
    a = a.reshape(rows.t, rows.bs, w).transpose(1, 0, 2)
    a = jnp.pad(a, ((0, 0), (0, pad_to - rows.t), (0, 0)))
    return a.reshape(rows.bs * pad_to, w)


def _seq_to_sample(a, rows, pad_to):
    w = a.shape[-1]
    return a.reshape(rows.bs, pad_to, w)[:, :rows.t].transpose(1, 0, 2).reshape(rows.ns_rows, w)


def kernel(x_prompt, x_sample, state_gla, cache_swa_k, cache_swa_v, state_s5_re, state_s5_im, c_prompt, c_sample, w_ada, b_ada, norm_ffn1, ffn1_w_gate, ffn1_w_up, ffn1_w_down, norm_mix, w_in, gla_w_gate_up, gla_b_gate, gla_norm, swa_sinks, s5_a_re, s5_a_im, s5_b_re, s5_b_im, s5_c_re, s5_c_im, s5_d, s5_log_dt, s5_w_glu, w_branch_a, w_branch_b, w_branch_c, w_out, norm_ffn2, ffn2_w_gate, ffn2_w_up, ffn2_w_down, norm_final):
    depth = w_ada.shape[0]
    b, l, _ = x_prompt.shape
    bs, t, _ = x_sample.shape
    rows = _Rows(b, l, bs, t)
    np_rows = rows.np_rows

    x = jnp.concatenate([x_prompt.reshape(np_rows, D_MODEL),
                         x_sample.transpose(1, 0, 2).reshape(rows.ns_rows, D_MODEL)], axis=0)
    c_all = jnp.concatenate([c_sample, c_prompt, jnp.zeros((8 - b, D_MODEL), F32)], axis=0)
    mod = _mod_call(c_all, w_ada, b_ada)

    pts = [0, A_QK, 2 * A_QK, 2 * A_QK + A_V]
    o_rank = pts[-1]
    o_ar = o_rank + A_GATE_RANK
    o_bq = o_ar + A_V
    o_cu_end = o_bq + B_Q + 2 * B_KV + C_WIDTH
    w_main = jnp.concatenate([w_in[:, :, :o_rank], w_in[:, :, o_bq:o_cu_end]], axis=-1).astype(BF16)
    w_rank = jnp.pad(w_in[:, :, o_rank:o_ar], ((0, 0), (0, 0), (0, LANES - A_GATE_RANK))).astype(BF16)
    w_up = jnp.pad(gla_w_gate_up, ((0, 0), (0, LANES - A_GATE_RANK), (0, 0))).astype(BF16)
    w_gates = jnp.concatenate([w_in[:, :, o_ar:o_bq], w_in[:, :, o_cu_end:]], axis=-1).astype(BF16)
    cast = lambda w: w.astype(BF16)
    f1g, f1u, f1d = cast(ffn1_w_gate), cast(ffn1_w_up), cast(ffn1_w_down)
    f2g, f2u, f2d = cast(ffn2_w_gate), cast(ffn2_w_up), cast(ffn2_w_down)
    wa, wb, wc, wglu, wo = cast(w_branch_a), cast(w_branch_b), cast(w_branch_c), cast(s5_w_glu), cast(w_out)
    vec3 = lambda a: a.reshape(depth, 1, a.shape[-1])
    n1, nm, n2 = vec3(norm_ffn1), vec3(norm_mix), vec3(norm_ffn2)
    b_gate, g_norm, s5d = vec3(gla_b_gate), vec3(gla_norm), vec3(s5_d)
    rot = _rotary_tables(rows)

    flat = lambda a: a.reshape(depth, 1, S5_N)
    ldt = flat(jnp.broadcast_to(s5_log_dt[:, :, None], (depth, C_GROUPS, C_STATE)))
    to_cols = lambda bm: bm.transpose(0, 3, 1, 2).reshape(depth, C_GROUP, S5_N)
    bb_re, bb_im, a_re, a_im, e_re, e_im, f_re, f_im = _s5_prep_call(
        flat(s5_a_re), flat(s5_a_im), ldt, to_cols(s5_b_re), to_cols(s5_b_im))
    eye = jnp.eye(C_GROUPS, dtype=F32)
    place_b = lambda bb: (bb.reshape(depth, 1, C_GROUP, C_GROUPS, C_STATE)
                          * eye[None, :, None, :, None]).reshape(depth, C_WIDTH, S5_N)
    s5_bw = jnp.concatenate([place_b(bb_re), place_b(bb_im)], axis=-1).astype(BF16)
    place_c = lambda cm: (cm.transpose(0, 1, 3, 2)[:, :, :, None, :]
                          * eye[None, :, None, :, None]).reshape(depth, S5_N, C_WIDTH).astype(BF16)
    s5_cr, s5_ci = place_c(s5_c_re), place_c(s5_c_im)
    tabs = (e_re, e_im, f_re, f_im)
    h0_re = state_s5_re.reshape(depth, bs, S5_N)
    h0_im = state_s5_im.reshape(depth, bs, S5_N)
    cache_k = cache_swa_k.reshape(depth, bs, WINDOW, B_KV)
    cache_v = cache_swa_v.reshape(depth, bs, WINDOW, B_KV)

    outs = {k: [] for k in ("gla_p", "gla_s", "kp", "ks", "vp", "vs", "rp", "rs", "ip", "is")}
    pad8 = 8
    for layer in range(depth):
        x = _ffn_call(rows, layer, 0, x, mod, n1, f1g, f1u, f1d)
        gq, gk, gv, la, sq, sk, sv, su = _inproj_call(rows, layer, x, mod, nm, w_main, w_rank, w_up, b_gate, rot)

        oa_p, st_p = _gla_call(layer, gq, gk, gv, la, g_norm, None, b, l, TOKEN_TILE, GLA_CHUNK, GLA_BLOCK)
        sseq = lambda a: _sample_to_seq(a[np_rows:], rows, pad8)
        oa_s, st_s = _gla_call(layer, sseq(gq), sseq(gk), sseq(gv), sseq(la), g_norm, state_gla,
                               bs, pad8, pad8, pad8, pad8)
        oa_s = _seq_to_sample(oa_s, rows, pad8)

        ob_p = _swa_call(layer, swa_sinks, sq, sk, sv, None, None, b, l // WINDOW, WINDOW)
        ob_s = _swa_call(layer, swa_sinks, sseq(sq), sseq(sk), sseq(sv), cache_k, cache_v, bs, 1, pad8)
        ob_s = _seq_to_sample(ob_s, rows, pad8)
        k_new = sk[np_rows:].reshape(t, bs, B_KV).transpose(1, 0, 2)
        v_new = sv[np_rows:].reshape(t, bs, B_KV).transpose(1, 0, 2)

        oc_p, hs_p = _s5_prompt_call(layer, su, s5_bw, s5_cr, s5_ci, s5d, tabs, b, l, TOKEN_TILE)
        oc_s, hr_s, hi_s = _s5_sample_call(layer, su[np_rows:], h0_re, h0_im, a_re, a_im,
                                           s5_bw, s5_cr, s5_ci, s5d, t, bs)

        x = _merge_call(rows, layer, x, mod, nm, (oa_p, oa_s), (ob_p, ob_s), (oc_p, oc_s),
                        w_gates, wa, wb, wc, wglu, wo)
        x = _ffn_call(rows, layer, 6, x, mod, n2, f2g, f2u, f2d)

        outs["gla_p"].append(st_p)
        outs["gla_s"].append(st_s)
        kv_shape = (b, WINDOW, B_KV_HEADS, B_HEAD_DIM)
        outs["kp"].append(sk[:np_rows].reshape(b, l, B_KV)[:, l - WINDOW:].reshape(kv_shape))
        outs["vp"].append(sv[:np_rows].reshape(b, l, B_KV)[:, l - WINDOW:].reshape(kv_shape))
        kvs_shape = (bs, WINDOW, B_KV_HEADS, B_HEAD_DIM)
        outs["ks"].append(jnp.concatenate([cache_k[layer][:, t:], k_new], axis=1).reshape(kvs_shape))
        outs["vs"].append(jnp.concatenate([cache_v[layer][:, t:], v_new], axis=1).reshape(kvs_shape))
        outs["rp"].append(hs_p[:, 0].reshape(b, C_GROUPS, C_STATE))
        outs["ip"].append(hs_p[:, 1].reshape(b, C_GROUPS, C_STATE))
        outs["rs"].append(hr_s.reshape(bs, C_GROUPS, C_STATE))
        outs["is"].append(hi_s.reshape(bs, C_GROUPS, C_STATE))

    g_fin = norm_final.reshape(1, D_MODEL)
    y_p = _final_call(x, g_fin, 0, rows.n_prompt_tiles).reshape(b, l, D_MODEL)
    y_s = _final_call(x, g_fin, rows.n_prompt_tiles, 1).reshape(t, bs, D_MODEL).transpose(1, 0, 2)
    st = lambda k: jnp.stack(outs[k])
    return (y_p, y_s, st("gla_p"), st("gla_s"), st("kp"), st("ks"), st("vp"), st("vs"),
            st("rp"), st("rs"), st("ip"), st("is"))
```

```python
import functools
import math

import jax
import jax.numpy as jnp
from jax import lax
from jax.experimental import pallas as pl
from jax.experimental.pallas import tpu as pltpu

D_MODEL = 1024
D_FF = 2816
PAST_LEN = 8192
A_HEADS, A_DK, A_DV = 4, 64, 128
A_GATE_RANK = 16
A_GATE_NORM = 16.0
B_HEADS, B_KV_HEADS, B_HEAD_DIM = 8, 2, 64
B_GROUP = B_HEADS // B_KV_HEADS
WINDOW = 128
ROPE_DIM = B_HEAD_DIM // 4
ROPE_THETA = 500000.0
C_WIDTH, C_GROUP, C_STATE = 512, 16, 64
C_GROUPS = C_WIDTH // C_GROUP
MACARON_W = 0.5
NORM_EPS = 1e-6
N_MOD = 9

A_QK = A_HEADS * A_DK
A_V = A_HEADS * A_DV
B_Q = B_HEADS * B_HEAD_DIM
B_KV = B_KV_HEADS * B_HEAD_DIM
S5_N = C_GROUPS * C_STATE

LANES = 128
TOKEN_TILE = 512
GLA_CHUNK = 64
GLA_BLOCK = 32
GLA_SAMPLE_SEQS = 8
S5_CHUNK = 64
S5_KT = 256
VMEM_LIMIT = 56 * 1024 * 1024

F32 = jnp.float32
BF16 = jnp.bfloat16
NEG_BIG = -1e30


def _cparams(sem):
    return pltpu.CompilerParams(dimension_semantics=sem, vmem_limit_bytes=VMEM_LIMIT)


def _dot(a, b):
    return jnp.dot(a, b, preferred_element_type=F32)


def _dot_nt(a, b):
    return lax.dot_general(a, b, (((1,), (1,)), ((), ())), preferred_element_type=F32)


def _sigmoid(x):
    return 1.0 / (1.0 + jnp.exp(-x))


def _silu(x):
    return x * _sigmoid(x)


def _gelu_tanh(x):
    return 0.5 * x * (1.0 + jnp.tanh(math.sqrt(2.0 / math.pi) * (x + 0.044715 * (x * x * x))))


def _log_sigmoid(x):
    return jnp.minimum(x, 0.0) - jnp.log1p(jnp.exp(-jnp.abs(x)))


def _resident(shape, index_map):
    return pl.BlockSpec(shape, index_map, pipeline_mode=pl.Buffered(1))


def _mod_kernel(c_ref, w_ref, b_ref, o_ref):
    c = c_ref[...]
    s = _silu(c).astype(BF16)
    o_ref[...] = _dot(s, w_ref[...].astype(BF16)) + b_ref[...]


def _mod_call(c_all, w_ada, b_ada):
    depth = w_ada.shape[0]
    rows = c_all.shape[0]
    return pl.pallas_call(
        _mod_kernel,
        out_shape=jax.ShapeDtypeStruct((depth, rows, N_MOD * D_MODEL), F32),
        grid=(depth, N_MOD),
        in_specs=[
            pl.BlockSpec((rows, D_MODEL), lambda l, j: (0, 0)),
            pl.BlockSpec((None, D_MODEL, D_MODEL), lambda l, j: (l, 0, j)),
            pl.BlockSpec((None, 1, D_MODEL), lambda l, j: (l, 0, j)),
        ],
        out_specs=pl.BlockSpec((None, rows, D_MODEL), lambda l, j: (l, 0, j)),
        compiler_params=_cparams(("arbitrary", "arbitrary")),
        name="ada_mod",
    )(c_all, w_ada, b_ada.reshape(depth, 1, N_MOD * D_MODEL))


class _Rows:
    def __init__(self, n_prompt_seq, seq_len, n_sample_seq, n_sample_tok):
        self.b = n_prompt_seq
        self.l = seq_len
        self.bs = n_sample_seq
        self.t = n_sample_tok
        self.np_rows = n_prompt_seq * seq_len
        self.ns_rows = n_sample_seq * n_sample_tok
        assert seq_len % TOKEN_TILE == 0 and self.ns_rows == TOKEN_TILE
        assert n_sample_seq % 8 == 0 and seq_len % WINDOW == 0
        self.rows = self.np_rows + self.ns_rows
        self.n_prompt_tiles = self.np_rows // TOKEN_TILE
        self.tiles_per_seq = seq_len // TOKEN_TILE
        self.n_tiles = self.n_prompt_tiles + 1
        self.mod_rows = n_sample_seq + 8
        assert n_prompt_seq <= 8


def _mod_specs(rows, layer, j):
    return [
        pl.BlockSpec((None, rows.bs, D_MODEL), lambda i, l=layer, j=j: (l, 0, j)),
        pl.BlockSpec((None, 8, D_MODEL), lambda i, l=layer, j=j, r=rows.bs // 8: (l, r, j)),
    ]


def _mod_value(rows, ms_ref, mp_ref):
    i = pl.program_id(0)
    is_sample = i == rows.n_prompt_tiles
    seq = jnp.minimum(i // rows.tiles_per_seq, rows.b - 1)
    s = ms_ref[...]
    s = jnp.concatenate([s] * rows.t, axis=0)
    p = mp_ref[pl.ds(seq, 1), :]
    return jnp.where(is_sample, s, p)


def _ada_norm(x, g, shift, scale):
    var = jnp.mean(x * x, axis=-1, keepdims=True)
    return (x * lax.rsqrt(var + NORM_EPS)) * g * (1.0 + scale) + shift


def _select_rows(rows, p_ref, s_ref):
    is_sample = pl.program_id(0) == rows.n_prompt_tiles
    return jnp.where(is_sample, s_ref[...], p_ref[...])


def _pair_specs(rows, width):
    last = rows.n_prompt_tiles - 1
    return [
        pl.BlockSpec((TOKEN_TILE, width), lambda i, last=last: (jnp.minimum(i, last), 0)),
        pl.BlockSpec((TOKEN_TILE, width), lambda i: (0, 0)),
    ]


FF_CHUNK = D_FF // 2


def _ffn_kernel(*refs, rows, x_pair):
    if x_pair:
        xp_ref, xs_ref, g_ref, shs, shp, scs, scp, gts, gtp, wg_ref, wu_ref, wd_ref, o_ref = refs
        x = _select_rows(rows, xp_ref, xs_ref)
    else:
        x_ref, g_ref, shs, shp, scs, scp, gts, gtp, wg_ref, wu_ref, wd_ref, o_ref = refs
        x = x_ref[...]
    h = _ada_norm(x, g_ref[...], _mod_value(rows, shs, shp), _mod_value(rows, scs, scp)).astype(BF16)
    acc = jnp.zeros((TOKEN_TILE, D_MODEL), F32)
    for c in range(D_FF // FF_CHUNK):
        sl = slice(c * FF_CHUNK, (c + 1) * FF_CHUNK)
        gate = _dot(h, wg_ref[:, sl])
        up = _dot(h, wu_ref[:, sl])
        a = (_silu(gate) * up).astype(BF16)
        acc = acc + _dot(a, wd_ref[sl, :])
    o_ref[...] = x + (MACARON_W * _mod_value(rows, gts, gtp)) * acc


def _ffn_call(rows, layer, mod_base, x, mod, norm_g, wg, wu, wd):
    x_pair = isinstance(x, tuple)
    if x_pair:
        in_specs = _pair_specs(rows, D_MODEL)
        operands = list(x)
    else:
        in_specs = [pl.BlockSpec((TOKEN_TILE, D_MODEL), lambda i: (i, 0))]
        operands = [x]
    in_specs.append(pl.BlockSpec((None, 1, D_MODEL), lambda i, l=layer: (l, 0, 0)))
    operands.append(norm_g)
    for j in range(3):
        in_specs += _mod_specs(rows, layer, mod_base + j)
        operands += [mod, mod]
    in_specs += [
        _resident((None, D_MODEL, D_FF), lambda i, l=layer: (l, 0, 0)),
        _resident((None, D_MODEL, D_FF), lambda i, l=layer: (l, 0, 0)),
        _resident((None, D_FF, D_MODEL), lambda i, l=layer: (l, 0, 0)),
    ]
    operands += [wg, wu, wd]
    return pl.pallas_call(
        functools.partial(_ffn_kernel, rows=rows, x_pair=x_pair),
        out_shape=jax.ShapeDtypeStruct((rows.rows, D_MODEL), F32),
        grid=(rows.n_tiles,),
        in_specs=in_specs,
        out_specs=pl.BlockSpec((TOKEN_TILE, D_MODEL), lambda i: (i, 0)),
        compiler_params=_cparams(("arbitrary",)),
        name="ada_swiglu",
    )(*operands)


W_MAIN = 2 * A_QK + A_V + B_Q + 2 * B_KV + C_WIDTH


def _rotary(x, cos, s_up, s_dn):
    half = ROPE_DIM // 2
    return x * cos + pltpu.roll(x, half, axis=1) * s_up + pltpu.roll(x, LANES - half, axis=1) * s_dn


def _inproj_kernel(x_ref, g_ref, shs, shp, scs, scp, wm_ref, wr_ref, wup_ref, bg_ref,
                   cos_ref, sup_ref, sdn_ref,
                   gq_ref, gk_ref, gv_ref, la_ref, sq_ref, sk_ref, sv_ref, su_ref, *, rows):
    x = x_ref[...]
    h = _ada_norm(x, g_ref[...], _mod_value(rows, shs, shp), _mod_value(rows, scs, scp)).astype(BF16)
    z = _dot(h, wm_ref[...])
    o = 0
    gq_ref[...] = z[:, o:o + A_QK] * (A_DK ** -0.5)
    o += A_QK
    gk_ref[...] = z[:, o:o + A_QK]
    o += A_QK
    gv_ref[...] = z[:, o:o + A_V]
    o += A_V
    cos, s_up, s_dn = cos_ref[...], sup_ref[...], sdn_ref[...]
    for c in range(B_Q // LANES):
        sq_ref[:, c * LANES:(c + 1) * LANES] = _rotary(z[:, o + c * LANES:o + (c + 1) * LANES], cos, s_up, s_dn)
    o += B_Q
    sk_ref[...] = _rotary(z[:, o:o + B_KV], cos, s_up, s_dn)
    o += B_KV
    sv_ref[...] = z[:, o:o + B_KV]
    o += B_KV
    su_ref[...] = z[:, o:o + C_WIDTH]
    low = _dot(h, wr_ref[...]).astype(BF16)
    pre = _dot(low, wup_ref[...]) + bg_ref[...]
    la_ref[...] = _log_sigmoid(pre) * (1.0 / A_GATE_NORM)


def _inproj_call(rows, layer, x, mod, norm_g, w_main, w_rank, w_up, b_gate, rot):
    row_spec = lambda w: pl.BlockSpec((TOKEN_TILE, w), lambda i: (i, 0))
    in_specs = [row_spec(D_MODEL), pl.BlockSpec((None, 1, D_MODEL), lambda i, l=layer: (l, 0, 0))]
    operands = [x, norm_g]
    for j in (3, 4):
        in_specs += _mod_specs(rows, layer, j)
        operands += [mod, mod]
    in_specs += [
        _resident((None, D_MODEL, W_MAIN), lambda i, l=layer: (l, 0, 0)),
        _resident((None, D_MODEL, LANES), lambda i, l=layer: (l, 0, 0)),
        _resident((None, LANES, A_QK), lambda i, l=layer: (l, 0, 0)),
        pl.BlockSpec((None, 1, A_QK), lambda i, l=layer: (l, 0, 0)),
        row_spec(LANES), row_spec(LANES), row_spec(LANES),
    ]
    operands += [w_main, w_rank, w_up, b_gate, *rot]
    widths = (A_QK, A_QK, A_V, A_QK, B_Q, B_KV, B_KV, C_WIDTH)
    return pl.pallas_call(
        functools.partial(_inproj_kernel, rows=rows),
        out_shape=[jax.ShapeDtypeStruct((rows.rows, w), F32) for w in widths],
        grid=(rows.n_tiles,),
        in_specs=in_specs,
        out_specs=[row_spec(w) for w in widths],
        compiler_params=_cparams(("arbitrary",)),
        name="mixer_in_proj",
    )(*operands)


def _cumsum_rows(x):
    n = x.shape[0]
    row = lax.broadcasted_iota(jnp.int32, x.shape, 0)
    s = 1
    while s < n:
        x = x + jnp.where(row >= s, pltpu.roll(x, s, axis=0), 0.0)
        s *= 2
    return x


def _head_blockdiag(x, lanes_per_head):
    lane_head = lax.broadcasted_iota(jnp.int32, x.shape, 1) // lanes_per_head
    return jnp.concatenate([jnp.where(lane_head == h, x, 0.0) for h in range(A_HEADS)], axis=0)


def _gla_kernel(*refs, chunk, block, n_chunks, n_seqs, single_step, has_state):
    if has_state:
        q_ref, k_ref, v_ref, la_ref, gn_ref, s0_ref, o_ref, so_ref, st_ref = refs
    else:
        q_ref, k_ref, v_ref, la_ref, gn_ref, o_ref, so_ref, st_ref = refs
    n_blocks = chunk // block
    st_shape = (A_V, A_QK)
    st_row_head = lax.broadcasted_iota(jnp.int32, st_shape, 0) // A_DV
    st_lane_head = lax.broadcasted_iota(jnp.int32, st_shape, 1) // A_DK
    st_mask = st_row_head == st_lane_head
    srow = lax.broadcasted_iota(jnp.int32, (block, A_HEADS * block), 0)
    scol = lax.broadcasted_iota(jnp.int32, (block, A_HEADS * block), 1) % block
    causal = scol <= srow
    gn = gn_ref[...]

    def init_state(sq):
        st_ref[sq] = jnp.zeros(st_shape, F32)
        if has_state:
            for h in range(A_HEADS):
                st_ref[sq, h * A_DV:(h + 1) * A_DV, h * A_DK:(h + 1) * A_DK] = s0_ref[sq, h].T

    def emit_state(sq):
        for h in range(A_HEADS):
            so_ref[sq, h] = st_ref[sq, h * A_DV:(h + 1) * A_DV, h * A_DK:(h + 1) * A_DK].T

    if single_step:
        for sq in range(n_seqs):
            init_state(sq)
    else:
        pl.when(pl.program_id(1) == 0)(lambda: init_state(0))

    for sq, c in [(sq, c) for sq in range(n_seqs) for c in range(n_chunks)]:
        r0 = (sq * n_chunks + c) * chunk
        rs = slice(r0, r0 + chunk)
        q = q_ref[rs, :]
        k = k_ref[rs, :]
        v = v_ref[rs, :]
        b = _cumsum_rows(la_ref[rs, :])
        b_last = b[chunk - 1:chunk, :]
        st = st_ref[sq]
        o = _dot_nt((q * jnp.exp(b)).astype(BF16), st.astype(BF16))
        outs = []
        for i in range(n_blocks):
            lo, hi = i * block, (i + 1) * block
            qi, ki, bi = q[lo:hi], k[lo:hi], b[lo:hi]
            mid = b[lo + block // 2 - 1:lo + block // 2, :]
            qd = (qi * jnp.exp(bi - mid)).astype(BF16)
            kd = _head_blockdiag(ki * jnp.exp(mid - bi), A_DK).astype(BF16)
            att = jnp.where(causal, _dot_nt(qd, kd), 0.0).astype(BF16)
            oi = _dot(att, _head_blockdiag(v[lo:hi], A_DV).astype(BF16))
            if i > 0:
                ref = b[lo - 1:lo, :]
                qo = (qi * jnp.exp(bi - ref)).astype(BF16)
                ko = _head_blockdiag(k[:lo] * jnp.exp(ref - b[:lo]), A_DK).astype(BF16)
                oi = oi + _dot(_dot_nt(qo, ko).astype(BF16), _head_blockdiag(v[:lo], A_DV).astype(BF16))
            outs.append(oi)
        o = o + (jnp.concatenate(outs, axis=0) if n_blocks > 1 else outs[0])
        kdec = (k * jnp.exp(b_last - b)).astype(BF16)
        upd = _dot(v.T.astype(BF16), kdec)
        st_ref[sq] = st * jnp.exp(b_last) + jnp.where(st_mask, upd, 0.0)
        normed = []
        for h in range(A_HEADS):
            oh = o[:, h * A_DV:(h + 1) * A_DV]
            var = jnp.mean(oh * oh, axis=-1, keepdims=True)
            normed.append(oh * lax.rsqrt(var + NORM_EPS) * gn)
        o_ref[rs, :] = jnp.concatenate(normed, axis=1)

    if single_step:
        for sq in range(n_seqs):
            emit_state(sq)
    else:
        pl.when(pl.program_id(1) == pl.num_programs(1) - 1)(lambda: emit_state(0))


def _gla_call(layer, q, k, v, la, gnorm, s0, n_seq, seq_rows, step_rows, chunk, block, seqs_per_step=1):
    steps = seq_rows // step_rows
    sps = seqs_per_step
    assert sps == 1 or steps == 1
    blk_rows = sps * step_rows
    row_spec = lambda w: pl.BlockSpec((blk_rows, w), lambda s, j, steps=steps: (s * steps + j, 0))
    in_specs = [row_spec(A_QK), row_spec(A_QK), row_spec(A_V), row_spec(A_QK),
                pl.BlockSpec((None, 1, A_DV), lambda s, j, l=layer: (l, 0, 0))]
    operands = [q, k, v, la, gnorm]
    if s0 is not None:
        in_specs.append(pl.BlockSpec((None, sps, A_HEADS, A_DK, A_DV), lambda s, j, l=layer: (l, s, 0, 0, 0)))
        operands.append(s0)
    return pl.pallas_call(
        functools.partial(_gla_kernel, chunk=chunk, block=block, n_chunks=step_rows // chunk,
                          n_seqs=sps, single_step=steps == 1, has_state=s0 is not None),
        out_shape=[jax.ShapeDtypeStruct((n_seq * seq_rows, A_V), F32),
                   jax.ShapeDtypeStruct((n_seq, A_HEADS, A_DK, A_DV), F32)],
        grid=(n_seq // sps, steps),
        in_specs=in_specs,
        out_specs=[row_spec(A_V),
                   pl.BlockSpec((sps, A_HEADS, A_DK, A_DV), lambda s, j: (s, 0, 0, 0))],
        scratch_shapes=[pltpu.VMEM((sps, A_V, A_QK), F32)],
        compiler_params=_cparams(("arbitrary", "arbitrary")),
        name="gla_chunked",
    )(*operands)


SWA_STEP_BLOCKS = 4


def _swa_kernel(sink_ref, q_ref, kc_ref, vc_ref, kp_ref, vp_ref, o_ref, *, layer):
    tk = 2 * WINDOW
    kj = lax.broadcasted_iota(jnp.int32, (tk, WINDOW), 0)
    qi = lax.broadcasted_iota(jnp.int32, (tk, WINDOW), 1) + WINDOW
    band = (kj <= qi) & (kj >= qi - WINDOW)
    zeros = jnp.zeros((B_HEAD_DIM, WINDOW), F32)
    for blk in range(SWA_STEP_BLOCKS):
        cur = slice(blk * WINDOW, (blk + 1) * WINDOW)
        if blk == 0:
            k_prev, v_prev = kp_ref[...], vp_ref[...]
            valid = band & (kj >= jnp.where(pl.program_id(1) > 0, 0, WINDOW))
        else:
            prev = slice((blk - 1) * WINDOW, blk * WINDOW)
            k_prev, v_prev = kc_ref[prev, :], vc_ref[prev, :]
            valid = band
        o_ref[cur, :] = _swa_block(sink_ref, layer, q_ref[cur, :], k_prev, kc_ref[cur, :],
                                   v_prev, vc_ref[cur, :], valid, zeros)


def _swa_block(sink_ref, layer, q, k_prev, k_cur, v_prev, v_cur, valid, zeros):
    keys = jnp.concatenate([k_prev, k_cur], axis=0).astype(BF16)
    vals_t = jnp.concatenate([v_prev, v_cur], axis=0).T.astype(BF16)
    q_t = q.T
    out_rows = []
    for h in range(B_KV_HEADS):
        v_h = vals_t[h * B_HEAD_DIM:(h + 1) * B_HEAD_DIM]
        for pair in range(B_GROUP // 2):
            heads = [h * B_GROUP + 2 * pair + j for j in range(2)]
            cols = []
            for hd in heads:
                qg = q_t[hd * B_HEAD_DIM:(hd + 1) * B_HEAD_DIM]
                cols.append(jnp.concatenate([qg, zeros] if h == 0 else [zeros, qg], axis=0))
            s = _dot(keys, jnp.concatenate(cols, axis=1).astype(BF16)) * (B_HEAD_DIM ** -0.5)
            probs, inv = [], []
            for j, hd in enumerate(heads):
                sink = sink_ref[layer, hd]
                sg = jnp.where(valid, s[:, j * WINDOW:(j + 1) * WINDOW], NEG_BIG)
                m = jnp.maximum(jnp.max(sg, axis=0, keepdims=True), sink)
                p = jnp.exp(sg - m)
                inv.append(1.0 / (jnp.sum(p, axis=0, keepdims=True) + jnp.exp(sink - m)))
                probs.append(p.astype(BF16))
            o_t = _dot(v_h, jnp.concatenate(probs, axis=1))
            for j in range(2):
                out_rows.append(o_t[:, j * WINDOW:(j + 1) * WINDOW] * inv[j])
    return jnp.concatenate(out_rows, axis=0).T


def _swa_call(layer, sinks, q, k, v, n_seq, blocks_per_seq):
    sb = SWA_STEP_BLOCKS
    assert blocks_per_seq % sb == 0
    steps = blocks_per_seq // sb
    cur = lambda w: pl.BlockSpec((sb * WINDOW, w), lambda s, n, steps=steps: (s * steps + n, 0))
    prev = pl.BlockSpec((WINDOW, B_KV),
                        lambda s, n, nb=blocks_per_seq, sb=sb: (s * nb + jnp.maximum(n * sb - 1, 0), 0))
    return pl.pallas_call(
        functools.partial(_swa_kernel, layer=layer),
        out_shape=jax.ShapeDtypeStruct((n_seq * blocks_per_seq * WINDOW, B_Q), F32),
        grid=(n_seq, steps),
        in_specs=[pl.BlockSpec(memory_space=pltpu.SMEM), cur(B_Q), cur(B_KV), cur(B_KV), prev, prev],
        out_specs=cur(B_Q),
        compiler_params=_cparams(("arbitrary", "arbitrary")),
        name="swa_banded",
    )(sinks, q, k, v, k, v)


SWA_SAMPLE_SEQS = 8
SWA_SAMPLE_ROWS = 8


def _swa_sample_kernel(sink_ref, q_ref, kn_ref, vn_ref, kt_ref, vt_ref, o_ref, nk_ref, nv_ref, *, layer, n_tok):
    pr = SWA_SAMPLE_ROWS
    keep = WINDOW - n_tok
    lane = lax.broadcasted_iota(jnp.int32, (B_HEAD_DIM, WINDOW), 1)
    rows = B_GROUP * pr
    tok = lax.broadcasted_iota(jnp.int32, (rows, WINDOW), 0) % pr
    grp = lax.broadcasted_iota(jnp.int32, (rows, 1), 0) // pr
    pos = lax.broadcasted_iota(jnp.int32, (rows, WINDOW), 1)
    valid_new = pos - keep <= tok
    valid_old = (pos >= tok) & (pos < n_tok)

    def shifted_in(old, new_rows):
        padded = jnp.concatenate([jnp.zeros((WINDOW - pr, B_KV), F32), new_rows], axis=0)
        cols = pltpu.roll(padded, n_tok, axis=0).T
        return [jnp.where(lane < keep, pltpu.roll(old[h], keep, axis=1),
                          cols[h * B_HEAD_DIM:(h + 1) * B_HEAD_DIM]) for h in range(B_KV_HEADS)]

    for i in range(SWA_SAMPLE_SEQS):
        rs = slice(i * pr, (i + 1) * pr)
        k_old = [kt_ref[i, h] for h in range(B_KV_HEADS)]
        v_old = [vt_ref[i, h] for h in range(B_KV_HEADS)]
        k_new = shifted_in(k_old, kn_ref[rs, :])
        v_new = shifted_in(v_old, vn_ref[rs, :])
        for h in range(B_KV_HEADS):
            nk_ref[i, h] = k_new[h]
            nv_ref[i, h] = v_new[h]
            q = q_ref[i, h].astype(BF16)
            s_new = jnp.where(valid_new, _dot(q, k_new[h].astype(BF16)) * (B_HEAD_DIM ** -0.5), NEG_BIG)
            s_old = jnp.where(valid_old, _dot(q, k_old[h].astype(BF16)) * (B_HEAD_DIM ** -0.5), NEG_BIG)
            sink = jnp.zeros((rows, 1), F32)
            for g in range(B_GROUP):
                sink = jnp.where(grp == g, sink_ref[layer, h * B_GROUP + g], sink)
            m = jnp.maximum(jnp.maximum(jnp.max(s_new, axis=-1, keepdims=True),
                                        jnp.max(s_old, axis=-1, keepdims=True)), sink)
            p_new, p_old = jnp.exp(s_new - m), jnp.exp(s_old - m)
            den = (jnp.sum(p_new, axis=-1, keepdims=True) + jnp.sum(p_old, axis=-1, keepdims=True)
                   + jnp.exp(sink - m))
            o = _dot_nt(p_new.astype(BF16), v_new[h].astype(BF16)) + _dot_nt(p_old.astype(BF16), v_old[h].astype(BF16))
            o_ref[i, h] = o * (1.0 / den)


def _swa_sample_call(layer, sinks, q, k_new, v_new, kt, vt, n_seq, n_tok):
    sb, pr = SWA_SAMPLE_SEQS, SWA_SAMPLE_ROWS
    qspec = pl.BlockSpec((sb, B_KV_HEADS, B_GROUP * pr, B_HEAD_DIM), lambda s: (s, 0, 0, 0))
    new_rows = pl.BlockSpec((sb * pr, B_KV), lambda s: (s, 0))
    buf_in = pl.BlockSpec((None, sb, B_KV_HEADS, B_HEAD_DIM, WINDOW), lambda s, l=layer: (l, s, 0, 0, 0))
    buf_out = pl.BlockSpec((sb, B_KV_HEADS, B_HEAD_DIM, WINDOW), lambda s: (s, 0, 0, 0))
    buf_sds = jax.ShapeDtypeStruct((n_seq, B_KV_HEADS, B_HEAD_DIM, WINDOW), F32)
    return pl.pallas_call(
        functools.partial(_swa_sample_kernel, layer=layer, n_tok=n_tok),
        out_shape=[jax.ShapeDtypeStruct((n_seq, B_KV_HEADS, B_GROUP * pr, B_HEAD_DIM), F32), buf_sds, buf_sds],
        grid=(n_seq // sb,),
        in_specs=[pl.BlockSpec(memory_space=pltpu.SMEM), qspec, new_rows, new_rows, buf_in, buf_in],
        out_specs=[qspec, buf_out, buf_out],
        compiler_params=_cparams(("arbitrary",)),
        name="swa_decode",
    )(sinks, q, k_new, v_new, kt, vt)


def _s5_prep_kernel(lr_ref, li_ref, ldt_ref, br_ref, bi_ref, ctr_ref, cti_ref,
                    bw_ref, cr_ref, ci_ref, ar_ref, ai_ref, er_ref, ei_ref, fr_ref, fi_ref):
    lr, li = lr_ref[...], li_ref[...]
    dt = jnp.exp(ldt_ref[...])
    rho, th = lr * dt, li * dt
    mag = jnp.exp(rho)
    a_re, a_im = mag * jnp.cos(th), mag * jnp.sin(th)
    den = lr * lr + li * li
    nr, ni = a_re - 1.0, a_im
    f_re = (nr * lr + ni * li) / den
    f_im = (ni * lr - nr * li) / den
    br, bi = br_ref[...], bi_ref[...]
    bb_re = f_re * br - f_im * bi
    bb_im = f_re * bi + f_im * br
    lane_group = lax.broadcasted_iota(jnp.int32, bb_re.shape, 1) // C_STATE
    place = lambda bb: jnp.concatenate(
        [jnp.where(lane_group == g, bb, 0.0) for g in range(C_GROUPS)], axis=0).astype(BF16)
    bw_ref[:, :S5_N] = place(bb_re)
    bw_ref[:, S5_N:] = place(bb_im)
    row_group = lax.broadcasted_iota(jnp.int32, cr_ref.shape, 0) // C_STATE
    col_group = lax.broadcasted_iota(jnp.int32, cr_ref.shape, 1) // C_GROUP
    own = row_group == col_group
    cr_ref[...] = jnp.where(own, ctr_ref[...], 0.0).astype(BF16)
    ci_ref[...] = jnp.where(own, cti_ref[...], 0.0).astype(BF16)
    ar_ref[...] = a_re
    ai_ref[...] = a_im
    s = (lax.broadcasted_iota(jnp.int32, er_ref.shape, 0) + 1).astype(F32)
    cs, sn = jnp.cos(s * th), jnp.sin(s * th)
    e_mag, f_mag = jnp.exp(-(s * rho)), jnp.exp(s * rho)
    er_ref[...] = e_mag * cs
    ei_ref[...] = -(e_mag * sn)
    fr_ref[...] = f_mag * cs
    fi_ref[...] = f_mag * sn


def _s5_prep_call(lr, li, ldt, b_re_t, b_im_t, c_re_t, c_im_t):
    depth = lr.shape[0]
    lsel = lambda l: (l, 0, 0)
    vec = pl.BlockSpec((None, 1, S5_N), lsel)
    mat = pl.BlockSpec((None, C_GROUP, S5_N), lsel)
    cmat = pl.BlockSpec((None, S5_N, C_WIDTH), lsel)
    tab = pl.BlockSpec((None, S5_CHUNK, S5_N), lsel)
    sds = lambda r: jax.ShapeDtypeStruct((depth, r, S5_N), F32)
    c_sds = jax.ShapeDtypeStruct((depth, S5_N, C_WIDTH), BF16)
    return pl.pallas_call(
        _s5_prep_kernel,
        out_shape=[jax.ShapeDtypeStruct((depth, C_WIDTH, 2 * S5_N), BF16), c_sds, c_sds,
                   sds(1), sds(1)] + [sds(S5_CHUNK)] * 4,
        grid=(depth,),
        in_specs=[vec, vec, vec, mat, mat, cmat, cmat],
        out_specs=[pl.BlockSpec((None, C_WIDTH, 2 * S5_N), lsel), cmat, cmat, vec, vec, tab, tab, tab, tab],
        compiler_params=_cparams(("arbitrary",)),
        name="s5_discretise",
    )(lr, li, ldt, b_re_t, b_im_t, c_re_t, c_im_t)


def _cmul(ar, ai, br, bi):
    return ar * br - ai * bi, ar * bi + ai * br


def _s5_prompt_kernel(u_ref, bw_ref, cr_ref, ci_ref, d_ref, er_ref, ei_ref, fr_ref, fi_ref,
                      y_ref, hs_ref, car_ref, cai_ref, bu_ref, hr_ref, hi_ref, *, n_chunks):
    step = pl.program_id(1)
    t = S5_CHUNK

    @pl.when(step == 0)
    def _():
        car_ref[...] = jnp.zeros(car_ref.shape, F32)
        cai_ref[...] = jnp.zeros(cai_ref.shape, F32)

    u = u_ref[...]
    ub = u.astype(BF16)
    kt_cols = S5_N // (C_WIDTH // S5_KT)
    for part in range(2):
        for kt in range(C_WIDTH // S5_KT):
            cols = slice(part * S5_N + kt * kt_cols, part * S5_N + (kt + 1) * kt_cols)
            rows_k = slice(kt * S5_KT, (kt + 1) * S5_KT)
            bu_ref[:, cols] = _dot(ub[:, rows_k], bw_ref[rows_k, cols])

    row = lax.broadcasted_iota(jnp.int32, (t, t), 0)
    col = lax.broadcasted_iota(jnp.int32, (t, t), 1)
    tril = jnp.where(col <= row, 1.0, 0.0).astype(BF16)

    car, cai = car_ref[...], cai_ref[...]
    for c in range(n_chunks):
        rs = slice(c * t, (c + 1) * t)
        xr, xi = _cmul(er_ref[...], ei_ref[...], bu_ref[rs, :S5_N], bu_ref[rs, S5_N:])
        hr = _dot(tril, xr.astype(BF16)) + car
        hi = _dot(tril, xi.astype(BF16)) + cai
        hr, hi = _cmul(fr_ref[...], fi_ref[...], hr, hi)
        hr_ref[rs, :] = hr.astype(BF16)
        hi_ref[rs, :] = hi.astype(BF16)
        car, cai = hr[t - 1:t, :], hi[t - 1:t, :]
    car_ref[...] = car
    cai_ref[...] = cai

    n_out = C_WIDTH // S5_KT
    k_rows = S5_N // n_out
    for j in range(n_out):
        ks = slice(j * k_rows, (j + 1) * k_rows)
        cs = slice(j * S5_KT, (j + 1) * S5_KT)
        y_ref[:, cs] = (_dot(hr_ref[:, ks], cr_ref[ks, cs]) - _dot(hi_ref[:, ks], ci_ref[ks, cs])
                        + d_ref[:, cs] * u[:, cs])

    @pl.when(step == pl.num_programs(1) - 1)
    def _():
        hs_ref[0:1, :] = car
        hs_ref[1:2, :] = cai


def _s5_prompt_call(layer, u, bw, c_re, c_im, d, tabs, n_seq, seq_rows, step_rows):
    steps = seq_rows // step_rows
    row_spec = pl.BlockSpec((step_rows, C_WIDTH), lambda s, j, steps=steps: (s * steps + j, 0))
    lsel = lambda s, j, l=layer: (l, 0, 0)
    tab = pl.BlockSpec((None, S5_CHUNK, S5_N), lsel)
    return pl.pallas_call(
        functools.partial(_s5_prompt_kernel, n_chunks=step_rows // S5_CHUNK),
        out_shape=[jax.ShapeDtypeStruct((n_seq * seq_rows, C_WIDTH), F32),
                   jax.ShapeDtypeStruct((n_seq, 2, S5_N), F32)],
        grid=(n_seq, steps),
        in_specs=[row_spec,
                  _resident((None, C_WIDTH, 2 * S5_N), lsel),
                  _resident((None, S5_N, C_WIDTH), lsel),
                  _resident((None, S5_N, C_WIDTH), lsel),
                  pl.BlockSpec((None, 1, C_WIDTH), lsel),
                  tab, tab, tab, tab],
        out_specs=[row_spec, pl.BlockSpec((None, 2, S5_N), lambda s, j: (s, 0, 0))],
        scratch_shapes=[pltpu.VMEM((1, S5_N), F32), pltpu.VMEM((1, S5_N), F32),
                        pltpu.VMEM((step_rows, 2 * S5_N), F32),
                        pltpu.VMEM((step_rows, S5_N), BF16), pltpu.VMEM((step_rows, S5_N), BF16)],
        compiler_params=_cparams(("arbitrary", "arbitrary")),
        name="s5_chunked_scan",
    )(u, bw, c_re, c_im, d, *tabs)


def _s5_sample_kernel(u_ref, h0r_ref, h0i_ref, ar_ref, ai_ref, bw_ref, cr_ref, ci_ref, d_ref,
                      y_ref, hr_ref, hi_ref, *, n_tok, n_seq):
    hr, hi = h0r_ref[...], h0i_ref[...]
    a_re, a_im = ar_ref[...], ai_ref[...]
    for t in range(n_tok):
        u = u_ref[t * n_seq:(t + 1) * n_seq, :]
        bu = _dot(u.astype(BF16), bw_ref[...])
        hr, hi = _cmul(a_re, a_im, hr, hi)
        hr, hi = hr + bu[:, :S5_N], hi + bu[:, S5_N:]
        y_ref[t * n_seq:(t + 1) * n_seq, :] = (_dot(hr.astype(BF16), cr_ref[...]) - _dot(hi.astype(BF16), ci_ref[...])
                                               + d_ref[...] * u)
    hr_ref[...] = hr
    hi_ref[...] = hi


def _s5_sample_call(layer, u, h0_re, h0_im, a_re, a_im, bw, c_re, c_im, d, n_tok, n_seq):
    lsel = lambda i, l=layer: (l, 0, 0)
    full = lambda shape: pl.BlockSpec(shape, lambda i: (0,) * len(shape))
    return pl.pallas_call(
        functools.partial(_s5_sample_kernel, n_tok=n_tok, n_seq=n_seq),
        out_shape=[jax.ShapeDtypeStruct((n_tok * n_seq, C_WIDTH), F32),
                   jax.ShapeDtypeStruct((n_seq, S5_N), F32),
                   jax.ShapeDtypeStruct((n_seq, S5_N), F32)],
        grid=(1,),
        in_specs=[full((n_tok * n_seq, C_WIDTH)),
                  pl.BlockSpec((None, n_seq, S5_N), lsel),
                  pl.BlockSpec((None, n_seq, S5_N), lsel),
                  pl.BlockSpec((None, 1, S5_N), lsel),
                  pl.BlockSpec((None, 1, S5_N), lsel),
                  pl.BlockSpec((None, C_WIDTH, 2 * S5_N), lsel),
                  pl.BlockSpec((None, S5_N, C_WIDTH), lsel),
                  pl.BlockSpec((None, S5_N, C_WIDTH), lsel),
                  pl.BlockSpec((None, 1, C_WIDTH), lsel)],
        out_specs=[full((n_tok * n_seq, C_WIDTH)), full((n_seq, S5_N)), full((n_seq, S5_N))],
        compiler_params=_cparams(("arbitrary",)),
        name="s5_step_scan",
    )(u, h0_re, h0_im, a_re, a_im, bw, c_re, c_im, d)


W_GATES = A_V + 3 * D_MODEL


def _merge_kernel(x_ref, g_ref, shs, shp, scs, scp, gts, gtp,
                  oap, oas, obp, obs, ocp, ocs,
                  wz_ref, wa_ref, wb_ref, wc_ref, wglu_ref, wo_ref, o_ref, *, rows):
    x = x_ref[...]
    h = _ada_norm(x, g_ref[...], _mod_value(rows, shs, shp), _mod_value(rows, scs, scp)).astype(BF16)
    z = _dot(h, wz_ref[...])
    oa = _select_rows(rows, oap, oas) * _silu(z[:, :A_V])
    br_a = _dot(oa.astype(BF16), wa_ref[...])
    br_b = _dot(_select_rows(rows, obp, obs).astype(BF16), wb_ref[...])
    glu = _dot(_gelu_tanh(_select_rows(rows, ocp, ocs)).astype(BF16), wglu_ref[...])
    oc = glu[:, :C_WIDTH] * _sigmoid(glu[:, C_WIDTH:])
    br_c = _dot(oc.astype(BF16), wc_ref[...])
    o = A_V
    mix = (_sigmoid(z[:, o:o + D_MODEL]) * br_a
           + _sigmoid(z[:, o + D_MODEL:o + 2 * D_MODEL]) * br_b
           + _sigmoid(z[:, o + 2 * D_MODEL:o + 3 * D_MODEL]) * br_c)
    out = _dot(mix.astype(BF16), wo_ref[...])
    o_ref[...] = x + _mod_value(rows, gts, gtp) * out


def _merge_call(rows, layer, x, mod, norm_g, oa, ob, oc, wz, wa, wb, wc, wglu, wo):
    in_specs = [pl.BlockSpec((TOKEN_TILE, D_MODEL), lambda i: (i, 0)),
                pl.BlockSpec((None, 1, D_MODEL), lambda i, l=layer: (l, 0, 0))]
    operands = [x, norm_g]
    for j in (3, 4, 5):
        in_specs += _mod_specs(rows, layer, j)
        operands += [mod, mod]
    for pair, w in ((oa, A_V), (ob, B_Q), (oc, C_WIDTH)):
        in_specs += _pair_specs(rows, w)
        operands += list(pair)
    lsel = lambda i, l=layer: (l, 0, 0)
    in_specs += [_resident((None, D_MODEL, W_GATES), lsel),
                 _resident((None, A_V, D_MODEL), lsel),
                 _resident((None, B_Q, D_MODEL), lsel),
                 _resident((None, C_WIDTH, D_MODEL), lsel),
                 _resident((None, C_WIDTH, 2 * C_WIDTH), lsel),
                 _resident((None, D_MODEL, D_MODEL), lsel)]
    operands += [wz, wa, wb, wc, wglu, wo]
    return pl.pallas_call(
        functools.partial(_merge_kernel, rows=rows),
        out_shape=jax.ShapeDtypeStruct((rows.rows, D_MODEL), F32),
        grid=(rows.n_tiles,),
        in_specs=in_specs,
        out_specs=pl.BlockSpec((TOKEN_TILE, D_MODEL), lambda i: (i, 0)),
        compiler_params=_cparams(("arbitrary",)),
        name="branch_merge",
    )(*operands)


def _final_kernel(x_ref, g_ref, o_ref):
    x = x_ref[...]
    var = jnp.mean(x * x, axis=-1, keepdims=True)
    o_ref[...] = x * lax.rsqrt(var + NORM_EPS) * g_ref[...]


def _final_call(x, g, first_tile, n_tiles):
    return pl.pallas_call(
        _final_kernel,
        out_shape=jax.ShapeDtypeStruct((n_tiles * TOKEN_TILE, D_MODEL), F32),
        grid=(n_tiles,),
        in_specs=[pl.BlockSpec((TOKEN_TILE, D_MODEL), lambda i, f=first_tile: (i + f, 0)),
                  pl.BlockSpec((1, D_MODEL), lambda i: (0, 0))],
        out_specs=pl.BlockSpec((TOKEN_TILE, D_MODEL), lambda i: (i, 0)),
        compiler_params=_cparams(("arbitrary",)),
        name="final_norm",
    )(x, g)


def _rotary_tables(rows):
    half = ROPE_DIM // 2
    inv_freq = ROPE_THETA ** (-jnp.arange(half, dtype=F32) / half)
    pos_p = jnp.tile(jnp.arange(rows.l, dtype=jnp.int32), rows.b)
    pos_s = PAST_LEN + jnp.repeat(jnp.arange(rows.t, dtype=jnp.int32), rows.bs)
    pos = jnp.concatenate([pos_p, pos_s])
    ang = pos.astype(F32)[:, None] * inv_freq[None, :]
    cos, sin = jnp.cos(ang), jnp.sin(ang)
    n = pos.shape[0]
    ones = jnp.ones((n, B_HEAD_DIM - ROPE_DIM), F32)
    zeros = jnp.zeros((n, B_HEAD_DIM - ROPE_DIM), F32)
    z8 = jnp.zeros((n, half), F32)
    per_head = lambda parts: jnp.tile(jnp.concatenate(parts, axis=1), (1, LANES // B_HEAD_DIM))
    return (per_head([cos, cos, ones]), per_head([z8, sin, zeros]), per_head([-sin, z8, zeros]))


def _sample_to_seq(a, rows, pad_to):
    w = a.shape[-1]
    a = a.reshape(rows.t, rows.bs, w).transpose(1, 0, 2)
    a = jnp.pad(a, ((0, 0), (0, pad_to - rows.t), (0, 0)))
    return a.reshape(rows.bs * pad_to, w)


def _seq_to_sample(a, rows, pad_to):
    w = a.shape[-1]
    return a.reshape(rows.bs, pad_to, w)[:, :rows.t].transpose(1, 0, 2).reshape(rows.ns_rows, w)


def kernel(x_prompt, x_sample, state_gla, cache_swa_k, cache_swa_v, state_s5_re, state_s5_im, c_prompt, c_sample, w_ada, b_ada, norm_ffn1, ffn1_w_gate, ffn1_w_up, ffn1_w_down, norm_mix, w_in, gla_w_gate_up, gla_b_gate, gla_norm, swa_sinks, s5_a_re, s5_a_im, s5_b_re, s5_b_im, s5_c_re, s5_c_im, s5_d, s5_log_dt, s5_w_glu, w_branch_a, w_branch_b, w_branch_c, w_out, norm_ffn2, ffn2_w_gate, ffn2_w_up, ffn2_w_down, norm_final):
    depth = w_ada.shape[0]
    b, l, _ = x_prompt.shape
    bs, t, _ = x_sample.shape
    rows = _Rows(b, l, bs, t)
    np_rows = rows.np_rows

    x = (x_prompt.reshape(np_rows, D_MODEL), x_sample.transpose(1, 0, 2).reshape(rows.ns_rows, D_MODEL))
    c_all = jnp.concatenate([c_sample, c_prompt, jnp.zeros((8 - b, D_MODEL), F32)], axis=0)
    mod = _mod_call(c_all, w_ada, b_ada)

    pts = [0, A_QK, 2 * A_QK, 2 * A_QK + A_V]
    o_rank = pts[-1]
    o_ar = o_rank + A_GATE_RANK
    o_bq = o_ar + A_V
    o_cu_end = o_bq + B_Q + 2 * B_KV + C_WIDTH
    w_main = jnp.concatenate([w_in[:, :, :o_rank], w_in[:, :, o_bq:o_cu_end]], axis=-1).astype(BF16)
    w_rank = jnp.pad(w_in[:, :, o_rank:o_ar], ((0, 0), (0, 0), (0, LANES - A_GATE_RANK))).astype(BF16)
    w_up = jnp.pad(gla_w_gate_up, ((0, 0), (0, LANES - A_GATE_RANK), (0, 0))).astype(BF16)
    w_gates = jnp.concatenate([w_in[:, :, o_ar:o_bq], w_in[:, :, o_cu_end:]], axis=-1).astype(BF16)
    cast = lambda w: w.astype(BF16)
    f1g, f1u, f1d = cast(ffn1_w_gate), cast(ffn1_w_up), cast(ffn1_w_down)
    f2g, f2u, f2d = cast(ffn2_w_gate), cast(ffn2_w_up), cast(ffn2_w_down)
    wa, wb, wc, wglu, wo = cast(w_branch_a), cast(w_branch_b), cast(w_branch_c), cast(s5_w_glu), cast(w_out)
    vec3 = lambda a: a.reshape(depth, 1, a.shape[-1])
    n1, nm, n2 = vec3(norm_ffn1), vec3(norm_mix), vec3(norm_ffn2)
    b_gate, g_norm, s5d = vec3(gla_b_gate), vec3(gla_norm), vec3(s5_d)
    rot = _rotary_tables(rows)

    flat = lambda a: a.reshape(depth, 1, S5_N)
    ldt = flat(jnp.broadcast_to(s5_log_dt[:, :, None], (depth, C_GROUPS, C_STATE)))
    to_cols = lambda bm: bm.transpose(0, 3, 1, 2).reshape(depth, C_GROUP, S5_N)
    to_rows = lambda cm: jnp.tile(cm.transpose(0, 1, 3, 2).reshape(depth, S5_N, C_GROUP), (1, 1, C_GROUPS))
    s5_bw, s5_cr, s5_ci, a_re, a_im, e_re, e_im, f_re, f_im = _s5_prep_call(
        flat(s5_a_re), flat(s5_a_im), ldt, to_cols(s5_b_re), to_cols(s5_b_im), to_rows(s5_c_re), to_rows(s5_c_im))
    tabs = (e_re, e_im, f_re, f_im)
    h0_re = state_s5_re.reshape(depth, bs, S5_N)
    h0_im = state_s5_im.reshape(depth, bs, S5_N)
    cache_kt = cache_swa_k.transpose(0, 1, 3, 4, 2)
    cache_vt = cache_swa_v.transpose(0, 1, 3, 4, 2)

    outs = {k: [] for k in ("gla_p", "gla_s", "kp", "ks", "vp", "vs", "rp", "rs", "ip", "is")}
    pad8 = SWA_SAMPLE_ROWS
    for layer in range(depth):
        x = _ffn_call(rows, layer, 0, x, mod, n1, f1g, f1u, f1d)
        gq, gk, gv, la, sq, sk, sv, su = _inproj_call(rows, layer, x, mod, nm, w_main, w_rank, w_up, b_gate, rot)

        oa_p, st_p = _gla_call(layer, gq, gk, gv, la, g_norm, None, b, l, TOKEN_TILE, GLA_CHUNK, GLA_BLOCK)
        sseq = lambda a: _sample_to_seq(a[np_rows:], rows, pad8)
        oa_s, st_s = _gla_call(layer, sseq(gq), sseq(gk), sseq(gv), sseq(la), g_norm, state_gla,
                               bs, pad8, pad8, pad8, pad8, seqs_per_step=GLA_SAMPLE_SEQS)
        oa_s = _seq_to_sample(oa_s, rows, pad8)

        ob_p = _swa_call(layer, swa_sinks, sq, sk, sv, b, l // WINDOW)
        q_s = sq[np_rows:].reshape(t, bs, B_KV_HEADS, B_GROUP, B_HEAD_DIM).transpose(1, 2, 3, 0, 4)
        q_s = jnp.pad(q_s, ((0, 0), (0, 0), (0, 0), (0, pad8 - t), (0, 0)))
        q_s = q_s.reshape(bs, B_KV_HEADS, B_GROUP * pad8, B_HEAD_DIM)
        ob_s, kt_new, vt_new = _swa_sample_call(layer, swa_sinks, q_s, sseq(sk), sseq(sv), cache_kt, cache_vt, bs, t)
        ob_s = ob_s.reshape(bs, B_KV_HEADS, B_GROUP, pad8, B_HEAD_DIM)[:, :, :, :t]
        ob_s = ob_s.transpose(3, 0, 1, 2, 4).reshape(rows.ns_rows, B_Q)

        oc_p, hs_p = _s5_prompt_call(layer, su, s5_bw, s5_cr, s5_ci, s5d, tabs, b, l, TOKEN_TILE)
        oc_s, hr_s, hi_s = _s5_sample_call(layer, su[np_rows:], h0_re, h0_im, a_re, a_im,
                                           s5_bw, s5_cr, s5_ci, s5d, t, bs)

        x = _merge_call(rows, layer, x, mod, nm, (oa_p, oa_s), (ob_p, ob_s), (oc_p, oc_s),
                        w_gates, wa, wb, wc, wglu, wo)
        x = _ffn_call(rows, layer, 6, x, mod, n2, f2g, f2u, f2d)

        outs["gla_p"].append(st_p)
        outs["gla_s"].append(st_s)
        kv_shape = (b, WINDOW, B_KV_HEADS, B_HEAD_DIM)
        outs["kp"].append(sk[:np_rows].reshape(b, l, B_KV)[:, l - WINDOW:].reshape(kv_shape))
        outs["vp"].append(sv[:np_rows].reshape(b, l, B_KV)[:, l - WINDOW:].reshape(kv_shape))
        outs["ks"].append(kt_new.transpose(0, 3, 1, 2))
        outs["vs"].append(vt_new.transpose(0, 3, 1, 2))
        outs["rp"].append(hs_p[:, 0].reshape(b, C_GROUPS, C_STATE))
        outs["ip"].append(hs_p[:, 1].reshape(b, C_GROUPS, C_STATE))
        outs["rs"].append(hr_s.reshape(bs, C_GROUPS, C_STATE))
        outs["is"].append(hi_s.reshape(bs, C_GROUPS, C_STATE))

    g_fin = norm_final.reshape(1, D_MODEL)
    y_p = _final_call(x, g_fin, 0, rows.n_prompt_tiles).reshape(b, l, D_MODEL)
    y_s = _final_call(x, g_fin, rows.n_prompt_tiles, 1).reshape(t, bs, D_MODEL).transpose(1, 0, 2)
    st = lambda k: jnp.stack(outs[k])
    return (y_p, y_s, st("gla_p"), st("gla_s"), st("kp"), st("ks"), st("vp"), st("vs"),
            st("rp"), st("rs"), st("ip"), st("is"))
```

```python
import functools
import math

import jax
import jax.numpy as jnp
from jax import lax
from jax.experimental import pallas as pl
from jax.experimental.pallas import tpu as pltpu

D_MODEL = 1024
D_FF = 2816
PAST_LEN = 8192
A_HEADS, A_DK, A_DV = 4, 64, 128
A_GATE_RANK = 16
A_GATE_NORM = 16.0
B_HEADS, B_KV_HEADS, B_HEAD_DIM = 8, 2, 64
B_GROUP = B_HEADS // B_KV_HEADS
WINDOW = 128
ROPE_DIM = B_HEAD_DIM // 4
ROPE_THETA = 500000.0
C_WIDTH, C_GROUP, C_STATE = 512, 16, 64
C_GROUPS = C_WIDTH // C_GROUP
MACARON_W = 0.5
NORM_EPS = 1e-6
N_MOD = 9

A_QK = A_HEADS * A_DK
A_V = A_HEADS * A_DV
B_Q = B_HEADS * B_HEAD_DIM
B_KV = B_KV_HEADS * B_HEAD_DIM
S5_N = C_GROUPS * C_STATE

LANES = 128
TOKEN_TILE = 512
GLA_CHUNK = 64
GLA_BLOCK = 32
GLA_SAMPLE_SEQS = 8
S5_CHUNK = 64
S5_KT = 256
VMEM_LIMIT = 56 * 1024 * 1024

F32 = jnp.float32
BF16 = jnp.bfloat16
NEG_BIG = -1e30


def _cparams(sem):
    return pltpu.CompilerParams(dimension_semantics=sem, vmem_limit_bytes=VMEM_LIMIT)


def _dot(a, b):
    return jnp.dot(a, b, preferred_element_type=F32)


def _dot_nt(a, b):
    return lax.dot_general(a, b, (((1,), (1,)), ((), ())), preferred_element_type=F32)


def _sigmoid(x):
    return 1.0 / (1.0 + jnp.exp(-x))


def _silu(x):
    return x * _sigmoid(x)


def _gelu_tanh(x):
    return 0.5 * x * (1.0 + jnp.tanh(math.sqrt(2.0 / math.pi) * (x + 0.044715 * (x * x * x))))


def _log_sigmoid(x):
    return jnp.minimum(x, 0.0) - jnp.log1p(jnp.exp(-jnp.abs(x)))


def _resident(shape, index_map):
    return pl.BlockSpec(shape, index_map, pipeline_mode=pl.Buffered(1))


def _mod_kernel(c_ref, w_ref, b_ref, o_ref):
    c = c_ref[...]
    s = _silu(c).astype(BF16)
    o_ref[...] = _dot(s, w_ref[...].astype(BF16)) + b_ref[...]


def _mod_call(c_all, w_ada, b_ada):
    depth = w_ada.shape[0]
    rows = c_all.shape[0]
    return pl.pallas_call(
        _mod_kernel,
        out_shape=jax.ShapeDtypeStruct((depth, rows, N_MOD * D_MODEL), F32),
        grid=(depth, N_MOD),
        in_specs=[
            pl.BlockSpec((rows, D_MODEL), lambda l, j: (0, 0)),
            pl.BlockSpec((None, D_MODEL, D_MODEL), lambda l, j: (l, 0, j)),
            pl.BlockSpec((None, 1, D_MODEL), lambda l, j: (l, 0, j)),
        ],
        out_specs=pl.BlockSpec((None, rows, D_MODEL), lambda l, j: (l, 0, j)),
        compiler_params=_cparams(("arbitrary", "arbitrary")),
        name="ada_mod",
    )(c_all, w_ada, b_ada.reshape(depth, 1, N_MOD * D_MODEL))


class _Rows:
    def __init__(self, n_prompt_seq, seq_len, n_sample_seq, n_sample_tok):
        self.b = n_prompt_seq
        self.l = seq_len
        self.bs = n_sample_seq
        self.t = n_sample_tok
        self.np_rows = n_prompt_seq * seq_len
        self.ns_rows = n_sample_seq * n_sample_tok
        assert seq_len % TOKEN_TILE == 0 and self.ns_rows == TOKEN_TILE
        assert n_sample_seq % 8 == 0 and seq_len % WINDOW == 0
        self.rows = self.np_rows + self.ns_rows
        self.n_prompt_tiles = self.np_rows // TOKEN_TILE
        self.tiles_per_seq = seq_len // TOKEN_TILE
        self.n_tiles = self.n_prompt_tiles + 1
        self.mod_rows = n_sample_seq + 8
        assert n_prompt_seq <= 8


def _mod_specs(rows, layer, j):
    return [
        pl.BlockSpec((None, rows.bs, D_MODEL), lambda i, l=layer, j=j: (l, 0, j)),
        pl.BlockSpec((None, 8, D_MODEL), lambda i, l=layer, j=j, r=rows.bs // 8: (l, r, j)),
    ]


def _mod_value(rows, ms_ref, mp_ref):
    i = pl.program_id(0)
    is_sample = i == rows.n_prompt_tiles
    seq = jnp.minimum(i // rows.tiles_per_seq, rows.b - 1)
    s = ms_ref[...]
    s = jnp.concatenate([s] * rows.t, axis=0)
    p = mp_ref[pl.ds(seq, 1), :]
    return jnp.where(is_sample, s, p)


def _ada_norm(x, g, shift, scale):
    var = jnp.mean(x * x, axis=-1, keepdims=True)
    return (x * lax.rsqrt(var + NORM_EPS)) * g * (1.0 + scale) + shift


def _select_rows(rows, p_ref, s_ref):
    is_sample = pl.program_id(0) == rows.n_prompt_tiles
    return jnp.where(is_sample, s_ref[...], p_ref[...])


def _pair_specs(rows, width):
    last = rows.n_prompt_tiles - 1
    return [
        pl.BlockSpec((TOKEN_TILE, width), lambda i, last=last: (jnp.minimum(i, last), 0)),
        pl.BlockSpec((TOKEN_TILE, width), lambda i: (0, 0)),
    ]


MXU_TILE = 256
FF_SPLITS = (0, 6 * MXU_TILE, D_FF)
assert D_FF % MXU_TILE == 0


def _ffn_kernel(*refs, rows, x_pair):
    if x_pair:
        xp_ref, xs_ref, g_ref, shs, shp, scs, scp, gts, gtp, wg_ref, wu_ref, wd_ref, o_ref = refs
        x = _select_rows(rows, xp_ref, xs_ref)
    else:
        x_ref, g_ref, shs, shp, scs, scp, gts, gtp, wg_ref, wu_ref, wd_ref, o_ref = refs
        x = x_ref[...]
    h = _ada_norm(x, g_ref[...], _mod_value(rows, shs, shp), _mod_value(rows, scs, scp)).astype(BF16)
    chunks = [slice(lo, hi) for lo, hi in zip(FF_SPLITS[:-1], FF_SPLITS[1:])]
    gate_up = [(_dot(h, wg_ref[:, sl]), _dot(h, wu_ref[:, sl])) for sl in chunks]
    act = [(_silu(gate) * up).astype(BF16) for gate, up in gate_up]
    acc = _dot(act[0], wd_ref[chunks[0], :])
    for a, sl in zip(act[1:], chunks[1:]):
        acc = acc + _dot(a, wd_ref[sl, :])
    o_ref[...] = x + (MACARON_W * _mod_value(rows, gts, gtp)) * acc


def _ffn_call(rows, layer, mod_base, x, mod, norm_g, wg, wu, wd):
    x_pair = isinstance(x, tuple)
    if x_pair:
        in_specs = _pair_specs(rows, D_MODEL)
        operands = list(x)
    else:
        in_specs = [pl.BlockSpec((TOKEN_TILE, D_MODEL), lambda i: (i, 0))]
        operands = [x]
    in_specs.append(pl.BlockSpec((None, 1, D_MODEL), lambda i, l=layer: (l, 0, 0)))
    operands.append(norm_g)
    for j in range(3):
        in_specs += _mod_specs(rows, layer, mod_base + j)
        operands += [mod, mod]
    in_specs += [
        _resident((None, D_MODEL, D_FF), lambda i, l=layer: (l, 0, 0)),
        _resident((None, D_MODEL, D_FF), lambda i, l=layer: (l, 0, 0)),
        _resident((None, D_FF, D_MODEL), lambda i, l=layer: (l, 0, 0)),
    ]
    operands += [wg, wu, wd]
    return pl.pallas_call(
        functools.partial(_ffn_kernel, rows=rows, x_pair=x_pair),
        out_shape=jax.ShapeDtypeStruct((rows.rows, D_MODEL), F32),
        grid=(rows.n_tiles,),
        in_specs=in_specs,
        out_specs=pl.BlockSpec((TOKEN_TILE, D_MODEL), lambda i: (i, 0)),
        compiler_params=_cparams(("arbitrary",)),
        name="ada_swiglu",
    )(*operands)


W_MAIN = 2 * A_QK + A_V + B_Q + 2 * B_KV + C_WIDTH


def _rotary(x, cos, s_up, s_dn):
    half = ROPE_DIM // 2
    return x * cos + pltpu.roll(x, half, axis=1) * s_up + pltpu.roll(x, LANES - half, axis=1) * s_dn


def _inproj_kernel(x_ref, g_ref, shs, shp, scs, scp, wm_ref, wr_ref, wup_ref, bg_ref,
                   cos_ref, sup_ref, sdn_ref,
                   gq_ref, gk_ref, gv_ref, la_ref, sq_ref, sk_ref, sv_ref, su_ref, *, rows):
    x = x_ref[...]
    h = _ada_norm(x, g_ref[...], _mod_value(rows, shs, shp), _mod_value(rows, scs, scp)).astype(BF16)
    z = _dot(h, wm_ref[...])
    o = 0
    gq_ref[...] = z[:, o:o + A_QK] * (A_DK ** -0.5)
    o += A_QK
    gk_ref[...] = z[:, o:o + A_QK]
    o += A_QK
    gv_ref[...] = z[:, o:o + A_V]
    o += A_V
    cos, s_up, s_dn = cos_ref[...], sup_ref[...], sdn_ref[...]
    for c in range(B_Q // LANES):
        sq_ref[:, c * LANES:(c + 1) * LANES] = _rotary(z[:, o + c * LANES:o + (c + 1) * LANES], cos, s_up, s_dn)
    o += B_Q
    sk_ref[...] = _rotary(z[:, o:o + B_KV], cos, s_up, s_dn)
    o += B_KV
    sv_ref[...] = z[:, o:o + B_KV]
    o += B_KV
    su_ref[...] = z[:, o:o + C_WIDTH]
    low = _dot(h, wr_ref[...]).astype(BF16)
    pre = _dot(low, wup_ref[...]) + bg_ref[...]
    la_ref[...] = _log_sigmoid(pre) * (1.0 / A_GATE_NORM)


def _inproj_call(rows, layer, x, mod, norm_g, w_main, w_rank, w_up, b_gate, rot):
    row_spec = lambda w: pl.BlockSpec((TOKEN_TILE, w), lambda i: (i, 0))
    in_specs = [row_spec(D_MODEL), pl.BlockSpec((None, 1, D_MODEL), lambda i, l=layer: (l, 0, 0))]
    operands = [x, norm_g]
    for j in (3, 4):
        in_specs += _mod_specs(rows, layer, j)
        operands += [mod, mod]
    in_specs += [
        _resident((None, D_MODEL, W_MAIN), lambda i, l=layer: (l, 0, 0)),
        _resident((None, D_MODEL, LANES), lambda i, l=layer: (l, 0, 0)),
        _resident((None, LANES, A_QK), lambda i, l=layer: (l, 0, 0)),
        pl.BlockSpec((None, 1, A_QK), lambda i, l=layer: (l, 0, 0)),
        row_spec(LANES), row_spec(LANES), row_spec(LANES),
    ]
    operands += [w_main, w_rank, w_up, b_gate, *rot]
    widths = (A_QK, A_QK, A_V, A_QK, B_Q, B_KV, B_KV, C_WIDTH)
    return pl.pallas_call(
        functools.partial(_inproj_kernel, rows=rows),
        out_shape=[jax.ShapeDtypeStruct((rows.rows, w), F32) for w in widths],
        grid=(rows.n_tiles,),
        in_specs=in_specs,
        out_specs=[row_spec(w) for w in widths],
        compiler_params=_cparams(("arbitrary",)),
        name="mixer_in_proj",
    )(*operands)


def _cumsum_rows(x):
    n = x.shape[0]
    row = lax.broadcasted_iota(jnp.int32, x.shape, 0)
    s = 1
    while s < n:
        x = x + jnp.where(row >= s, pltpu.roll(x, s, axis=0), 0.0)
        s *= 2
    return x


def _head_blockdiag(x, lanes_per_head):
    lane_head = lax.broadcasted_iota(jnp.int32, x.shape, 1) // lanes_per_head
    return jnp.concatenate([jnp.where(lane_head == h, x, 0.0) for h in range(A_HEADS)], axis=0)


def _gla_kernel(*refs, chunk, block, n_chunks, n_seqs, single_step, has_state):
    if has_state:
        q_ref, k_ref, v_ref, la_ref, gn_ref, s0_ref, o_ref, so_ref, st_ref = refs
    else:
        q_ref, k_ref, v_ref, la_ref, gn_ref, o_ref, so_ref, st_ref = refs
    n_blocks = chunk // block
    st_shape = (A_V, A_QK)
    st_row_head = lax.broadcasted_iota(jnp.int32, st_shape, 0) // A_DV
    st_lane_head = lax.broadcasted_iota(jnp.int32, st_shape, 1) // A_DK
    st_mask = st_row_head == st_lane_head
    srow = lax.broadcasted_iota(jnp.int32, (block, A_HEADS * block), 0)
    scol = lax.broadcasted_iota(jnp.int32, (block, A_HEADS * block), 1) % block
    causal = scol <= srow
    gn = gn_ref[...]

    def init_state(sq):
        st_ref[sq] = jnp.zeros(st_shape, F32)
        if has_state:
            for h in range(A_HEADS):
                st_ref[sq, h * A_DV:(h + 1) * A_DV, h * A_DK:(h + 1) * A_DK] = s0_ref[sq, h].T

    def emit_state(sq):
        for h in range(A_HEADS):
            so_ref[sq, h] = st_ref[sq, h * A_DV:(h + 1) * A_DV, h * A_DK:(h + 1) * A_DK].T

    if single_step:
        for sq in range(n_seqs):
            init_state(sq)
    else:
        pl.when(pl.program_id(1) == 0)(lambda: init_state(0))

    units = [(sq, c) for sq in range(n_seqs) for c in range(n_chunks)]
    rows_of = {u: slice((u[0] * n_chunks + u[1]) * chunk, (u[0] * n_chunks + u[1] + 1) * chunk) for u in units}

    q, k, v, b = {}, {}, {}, {}
    for u in units:
        q[u], k[u], v[u] = q_ref[rows_of[u], :], k_ref[rows_of[u], :], v_ref[rows_of[u], :]
        b[u] = _cumsum_rows(la_ref[rows_of[u], :])

    s_diag, s_off, upd, q_in = {}, {}, {}, {}
    for u in units:
        for i in range(n_blocks):
            lo, hi = i * block, (i + 1) * block
            qi, ki, bi = q[u][lo:hi], k[u][lo:hi], b[u][lo:hi]
            mid = b[u][lo + block // 2 - 1:lo + block // 2, :]
            qd = (qi * jnp.exp(bi - mid)).astype(BF16)
            kd = _head_blockdiag(ki * jnp.exp(mid - bi), A_DK).astype(BF16)
            s_diag[u, i] = _dot_nt(qd, kd)
            if i > 0:
                ref = b[u][lo - 1:lo, :]
                qo = (qi * jnp.exp(bi - ref)).astype(BF16)
                ko = _head_blockdiag(k[u][:lo] * jnp.exp(ref - b[u][:lo]), A_DK).astype(BF16)
                s_off[u, i] = _dot_nt(qo, ko)
        b_last = b[u][chunk - 1:chunk, :]
        upd[u] = _dot(v[u].T.astype(BF16), (k[u] * jnp.exp(b_last - b[u])).astype(BF16))
        q_in[u] = (q[u] * jnp.exp(b[u])).astype(BF16)

    intra = {}
    for u in units:
        outs = []
        for i in range(n_blocks):
            lo, hi = i * block, (i + 1) * block
            att = jnp.where(causal, s_diag[u, i], 0.0).astype(BF16)
            oi = _dot(att, _head_blockdiag(v[u][lo:hi], A_DV).astype(BF16))
            if i > 0:
                oi = oi + _dot(s_off[u, i].astype(BF16), _head_blockdiag(v[u][:lo], A_DV).astype(BF16))
            outs.append(oi)
        intra[u] = jnp.concatenate(outs, axis=0) if n_blocks > 1 else outs[0]

    for u in units:
        sq = u[0]
        st = st_ref[sq]
        o = _dot_nt(q_in[u], st.astype(BF16)) + intra[u]
        st_ref[sq] = st * jnp.exp(b[u][chunk - 1:chunk, :]) + jnp.where(st_mask, upd[u], 0.0)
        normed = []
        for h in range(A_HEADS):
            oh = o[:, h * A_DV:(h + 1) * A_DV]
            var = jnp.mean(oh * oh, axis=-1, keepdims=True)
            normed.append(oh * lax.rsqrt(var + NORM_EPS) * gn)
        o_ref[rows_of[u], :] = jnp.concatenate(normed, axis=1)

    if single_step:
        for sq in range(n_seqs):
            emit_state(sq)
    else:
        pl.when(pl.program_id(1) == pl.num_programs(1) - 1)(lambda: emit_state(0))


def _gla_call(layer, q, k, v, la, gnorm, s0, n_seq, seq_rows, step_rows, chunk, block, seqs_per_step=1):
    steps = seq_rows // step_rows
    sps = seqs_per_step
    assert sps == 1 or steps == 1
    blk_rows = sps * step_rows
    row_spec = lambda w: pl.BlockSpec((blk_rows, w), lambda s, j, steps=steps: (s * steps + j, 0))
    in_specs = [row_spec(A_QK), row_spec(A_QK), row_spec(A_V), row_spec(A_QK),
                pl.BlockSpec((None, 1, A_DV), lambda s, j, l=layer: (l, 0, 0))]
    operands = [q, k, v, la, gnorm]
    if s0 is not None:
        in_specs.append(pl.BlockSpec((None, sps, A_HEADS, A_DK, A_DV), lambda s, j, l=layer: (l, s, 0, 0, 0)))
        operands.append(s0)
    return pl.pallas_call(
        functools.partial(_gla_kernel, chunk=chunk, block=block, n_chunks=step_rows // chunk,
                          n_seqs=sps, single_step=steps == 1, has_state=s0 is not None),
        out_shape=[jax.ShapeDtypeStruct((n_seq * seq_rows, A_V), F32),
                   jax.ShapeDtypeStruct((n_seq, A_HEADS, A_DK, A_DV), F32)],
        grid=(n_seq // sps, steps),
        in_specs=in_specs,
        out_specs=[row_spec(A_V),
                   pl.BlockSpec((sps, A_HEADS, A_DK, A_DV), lambda s, j: (s, 0, 0, 0))],
        scratch_shapes=[pltpu.VMEM((sps, A_V, A_QK), F32)],
        compiler_params=_cparams(("arbitrary", "arbitrary")),
        name="gla_chunked",
    )(*operands)


SWA_STEP_BLOCKS = 4


def _swa_kernel(sink_ref, q_ref, kc_ref, vc_ref, kp_ref, vp_ref, o_ref, *, layer):
    tk = 2 * WINDOW
    kj = lax.broadcasted_iota(jnp.int32, (tk, WINDOW), 0)
    qi = lax.broadcasted_iota(jnp.int32, (tk, WINDOW), 1) + WINDOW
    band = (kj <= qi) & (kj >= qi - WINDOW)
    zeros = jnp.zeros((B_HEAD_DIM, WINDOW), F32)
    chains = [(blk, h) for blk in range(SWA_STEP_BLOCKS) for h in range(B_KV_HEADS)]

    keys, vals_t, q_t, valid = [], [], [], []
    for blk in range(SWA_STEP_BLOCKS):
        cur = slice(blk * WINDOW, (blk + 1) * WINDOW)
        if blk == 0:
            k_prev, v_prev = kp_ref[...], vp_ref[...]
            valid.append(band & (kj >= jnp.where(pl.program_id(1) > 0, 0, WINDOW)))
        else:
            prev = slice((blk - 1) * WINDOW, blk * WINDOW)
            k_prev, v_prev = kc_ref[prev, :], vc_ref[prev, :]
            valid.append(band)
        keys.append(jnp.concatenate([k_prev, kc_ref[cur, :]], axis=0).astype(BF16))
        vals_t.append(jnp.concatenate([v_prev, vc_ref[cur, :]], axis=0).T.astype(BF16))
        q_t.append(q_ref[cur, :].T)

    scores = {}
    for blk, h in chains:
        cols = []
        for g in range(B_GROUP):
            hd = h * B_GROUP + g
            qg = q_t[blk][hd * B_HEAD_DIM:(hd + 1) * B_HEAD_DIM]
            cols.append(jnp.concatenate([qg, zeros] if h == 0 else [zeros, qg], axis=0))
        scores[blk, h] = _dot(keys[blk], jnp.concatenate(cols, axis=1).astype(BF16))

    probs, inv = {}, {}
    for blk, h in chains:
        ps = []
        for g in range(B_GROUP):
            sink = sink_ref[layer, h * B_GROUP + g]
            sg = scores[blk, h][:, g * WINDOW:(g + 1) * WINDOW] * (B_HEAD_DIM ** -0.5)
            sg = jnp.where(valid[blk], sg, NEG_BIG)
            m = jnp.maximum(jnp.max(sg, axis=0, keepdims=True), sink)
            p = jnp.exp(sg - m)
            inv[blk, h, g] = 1.0 / (jnp.sum(p, axis=0, keepdims=True) + jnp.exp(sink - m))
            ps.append(p.astype(BF16))
        probs[blk, h] = jnp.concatenate(ps, axis=1)

    out_t = {}
    for blk, h in chains:
        out_t[blk, h] = _dot(vals_t[blk][h * B_HEAD_DIM:(h + 1) * B_HEAD_DIM], probs[blk, h])

    for blk in range(SWA_STEP_BLOCKS):
        rows = [out_t[blk, h][:, g * WINDOW:(g + 1) * WINDOW] * inv[blk, h, g]
                for h in range(B_KV_HEADS) for g in range(B_GROUP)]
        o_ref[blk * WINDOW:(blk + 1) * WINDOW, :] = jnp.concatenate(rows, axis=0).T


def _swa_call(layer, sinks, q, k, v, n_seq, blocks_per_seq):
    sb = SWA_STEP_BLOCKS
    assert blocks_per_seq % sb == 0
    steps = blocks_per_seq // sb
    cur = lambda w: pl.BlockSpec((sb * WINDOW, w), lambda s, n, steps=steps: (s * steps + n, 0))
    prev = pl.BlockSpec((WINDOW, B_KV),
                        lambda s, n, nb=blocks_per_seq, sb=sb: (s * nb + jnp.maximum(n * sb - 1, 0), 0))
    return pl.pallas_call(
        functools.partial(_swa_kernel, layer=layer),
        out_shape=jax.ShapeDtypeStruct((n_seq * blocks_per_seq * WINDOW, B_Q), F32),
        grid=(n_seq, steps),
        in_specs=[pl.BlockSpec(memory_space=pltpu.SMEM), cur(B_Q), cur(B_KV), cur(B_KV), prev, prev],
        out_specs=cur(B_Q),
        compiler_params=_cparams(("arbitrary", "arbitrary")),
        name="swa_banded",
    )(sinks, q, k, v, k, v)


SWA_SAMPLE_SEQS = 8
SWA_SAMPLE_ROWS = 8


def _swa_sample_kernel(sink_ref, q_ref, kn_ref, vn_ref, kt_ref, vt_ref, o_ref, nk_ref, nv_ref, *, layer, n_tok):
    pr = SWA_SAMPLE_ROWS
    keep = WINDOW - n_tok
    lane = lax.broadcasted_iota(jnp.int32, (B_HEAD_DIM, WINDOW), 1)
    rows = B_GROUP * pr
    tok = lax.broadcasted_iota(jnp.int32, (rows, WINDOW), 0) % pr
    grp = lax.broadcasted_iota(jnp.int32, (rows, 1), 0) // pr
    pos = lax.broadcasted_iota(jnp.int32, (rows, WINDOW), 1)
    valid_new = pos - keep <= tok
    valid_old = (pos >= tok) & (pos < n_tok)

    def shifted_in(old, new_rows):
        padded = jnp.concatenate([jnp.zeros((WINDOW - pr, B_KV), F32), new_rows], axis=0)
        cols = pltpu.roll(padded, n_tok, axis=0).T
        return [jnp.where(lane < keep, pltpu.roll(old[h], keep, axis=1),
                          cols[h * B_HEAD_DIM:(h + 1) * B_HEAD_DIM]) for h in range(B_KV_HEADS)]

    chains = [(i, h) for i in range(SWA_SAMPLE_SEQS) for h in range(B_KV_HEADS)]
    k_old, v_old, k_new, v_new = {}, {}, {}, {}
    for i in range(SWA_SAMPLE_SEQS):
        rs = slice(i * pr, (i + 1) * pr)
        ko = [kt_ref[i, h] for h in range(B_KV_HEADS)]
        vo = [vt_ref[i, h] for h in range(B_KV_HEADS)]
        kn, vn = shifted_in(ko, kn_ref[rs, :]), shifted_in(vo, vn_ref[rs, :])
        for h in range(B_KV_HEADS):
            nk_ref[i, h] = kn[h]
            nv_ref[i, h] = vn[h]
            k_old[i, h], v_old[i, h] = ko[h].astype(BF16), vo[h].astype(BF16)
            k_new[i, h], v_new[i, h] = kn[h].astype(BF16), vn[h].astype(BF16)

    s_new, s_old = {}, {}
    for c in chains:
        q = q_ref[c[0], c[1]].astype(BF16)
        s_new[c], s_old[c] = _dot(q, k_new[c]), _dot(q, k_old[c])

    p_new, p_old, inv = {}, {}, {}
    for c in chains:
        sn = jnp.where(valid_new, s_new[c] * (B_HEAD_DIM ** -0.5), NEG_BIG)
        so = jnp.where(valid_old, s_old[c] * (B_HEAD_DIM ** -0.5), NEG_BIG)
        sink = jnp.zeros((rows, 1), F32)
        for g in range(B_GROUP):
            sink = jnp.where(grp == g, sink_ref[layer, c[1] * B_GROUP + g], sink)
        m = jnp.maximum(jnp.maximum(jnp.max(sn, axis=-1, keepdims=True),
                                    jnp.max(so, axis=-1, keepdims=True)), sink)
        pn, po = jnp.exp(sn - m), jnp.exp(so - m)
        inv[c] = 1.0 / (jnp.sum(pn, axis=-1, keepdims=True) + jnp.sum(po, axis=-1, keepdims=True)
                        + jnp.exp(sink - m))
        p_new[c], p_old[c] = pn.astype(BF16), po.astype(BF16)

    for c in chains:
        o = _dot_nt(p_new[c], v_new[c]) + _dot_nt(p_old[c], v_old[c])
        o_ref[c[0], c[1]] = o * inv[c]


def _swa_sample_call(layer, sinks, q, k_new, v_new, kt, vt, n_seq, n_tok):
    sb, pr = SWA_SAMPLE_SEQS, SWA_SAMPLE_ROWS
    qspec = pl.BlockSpec((sb, B_KV_HEADS, B_GROUP * pr, B_HEAD_DIM), lambda s: (s, 0, 0, 0))
    new_rows = pl.BlockSpec((sb * pr, B_KV), lambda s: (s, 0))
    buf_in = pl.BlockSpec((None, sb, B_KV_HEADS, B_HEAD_DIM, WINDOW), lambda s, l=layer: (l, s, 0, 0, 0))
    buf_out = pl.BlockSpec((sb, B_KV_HEADS, B_HEAD_DIM, WINDOW), lambda s: (s, 0, 0, 0))
    buf_sds = jax.ShapeDtypeStruct((n_seq, B_KV_HEADS, B_HEAD_DIM, WINDOW), F32)
    return pl.pallas_call(
        functools.partial(_swa_sample_kernel, layer=layer, n_tok=n_tok),
        out_shape=[jax.ShapeDtypeStruct((n_seq, B_KV_HEADS, B_GROUP * pr, B_HEAD_DIM), F32), buf_sds, buf_sds],
        grid=(n_seq // sb,),
        in_specs=[pl.BlockSpec(memory_space=pltpu.SMEM), qspec, new_rows, new_rows, buf_in, buf_in],
        out_specs=[qspec, buf_out, buf_out],
        compiler_params=_cparams(("arbitrary",)),
        name="swa_decode",
    )(sinks, q, k_new, v_new, kt, vt)


def _s5_prep_kernel(lr_ref, li_ref, ldt_ref, br_ref, bi_ref, ctr_ref, cti_ref,
                    bw_ref, cr_ref, ci_ref, ar_ref, ai_ref, er_ref, ei_ref, fr_ref, fi_ref):
    lr, li = lr_ref[...], li_ref[...]
    dt = jnp.exp(ldt_ref[...])
    rho, th = lr * dt, li * dt
    mag = jnp.exp(rho)
    a_re, a_im = mag * jnp.cos(th), mag * jnp.sin(th)
    den = lr * lr + li * li
    nr, ni = a_re - 1.0, a_im
    f_re = (nr * lr + ni * li) / den
    f_im = (ni * lr - nr * li) / den
    br, bi = br_ref[...], bi_ref[...]
    bb_re = f_re * br - f_im * bi
    bb_im = f_re * bi + f_im * br
    lane_group = lax.broadcasted_iota(jnp.int32, bb_re.shape, 1) // C_STATE
    place = lambda bb: jnp.concatenate(
        [jnp.where(lane_group == g, bb, 0.0) for g in range(C_GROUPS)], axis=0).astype(BF16)
    bw_ref[:, :S5_N] = place(bb_re)
    bw_ref[:, S5_N:] = place(bb_im)
    row_group = lax.broadcasted_iota(jnp.int32, cr_ref.shape, 0) // C_STATE
    col_group = lax.broadcasted_iota(jnp.int32, cr_ref.shape, 1) // C_GROUP
    own = row_group == col_group
    cr_ref[...] = jnp.where(own, ctr_ref[...], 0.0).astype(BF16)
    ci_ref[...] = jnp.where(own, cti_ref[...], 0.0).astype(BF16)
    ar_ref[...] = a_re
    ai_ref[...] = a_im
    s = (lax.broadcasted_iota(jnp.int32, er_ref.shape, 0) + 1).astype(F32)
    cs, sn = jnp.cos(s * th), jnp.sin(s * th)
    e_mag, f_mag = jnp.exp(-(s * rho)), jnp.exp(s * rho)
    er_ref[...] = e_mag * cs
    ei_ref[...] = -(e_mag * sn)
    fr_ref[...] = f_mag * cs
    fi_ref[...] = f_mag * sn


def _s5_prep_call(lr, li, ldt, b_re_t, b_im_t, c_re_t, c_im_t):
    depth = lr.shape[0]
    lsel = lambda l: (l, 0, 0)
    vec = pl.BlockSpec((None, 1, S5_N), lsel)
    mat = pl.BlockSpec((None, C_GROUP, S5_N), lsel)
    cmat = pl.BlockSpec((None, S5_N, C_WIDTH), lsel)
    tab = pl.BlockSpec((None, S5_CHUNK, S5_N), lsel)
    sds = lambda r: jax.ShapeDtypeStruct((depth, r, S5_N), F32)
    c_sds = jax.ShapeDtypeStruct((depth, S5_N, C_WIDTH), BF16)
    return pl.pallas_call(
        _s5_prep_kernel,
        out_shape=[jax.ShapeDtypeStruct((depth, C_WIDTH, 2 * S5_N), BF16), c_sds, c_sds,
                   sds(1), sds(1)] + [sds(S5_CHUNK)] * 4,
        grid=(depth,),
        in_specs=[vec, vec, vec, mat, mat, cmat, cmat],
        out_specs=[pl.BlockSpec((None, C_WIDTH, 2 * S5_N), lsel), cmat, cmat, vec, vec, tab, tab, tab, tab],
        compiler_params=_cparams(("arbitrary",)),
        name="s5_discretise",
    )(lr, li, ldt, b_re_t, b_im_t, c_re_t, c_im_t)


def _cmul(ar, ai, br, bi):
    return ar * br - ai * bi, ar * bi + ai * br


def _s5_prompt_kernel(u_ref, bw_ref, cr_ref, ci_ref, d_ref, er_ref, ei_ref, fr_ref, fi_ref,
                      y_ref, hs_ref, car_ref, cai_ref, bu_ref, hr_ref, hi_ref, *, n_chunks):
    step = pl.program_id(1)
    t = S5_CHUNK

    @pl.when(step == 0)
    def _():
        car_ref[...] = jnp.zeros(car_ref.shape, F32)
        cai_ref[...] = jnp.zeros(cai_ref.shape, F32)

    u = u_ref[...]
    ub = u.astype(BF16)
    kt_cols = S5_N // (C_WIDTH // S5_KT)
    for part in range(2):
        for kt in range(C_WIDTH // S5_KT):
            cols = slice(part * S5_N + kt * kt_cols, part * S5_N + (kt + 1) * kt_cols)
            rows_k = slice(kt * S5_KT, (kt + 1) * S5_KT)
            bu_ref[:, cols] = _dot(ub[:, rows_k], bw_ref[rows_k, cols])

    row = lax.broadcasted_iota(jnp.int32, (t, t), 0)
    col = lax.broadcasted_iota(jnp.int32, (t, t), 1)
    tril = jnp.where(col <= row, 1.0, 0.0).astype(BF16)

    col_tiles = [slice(n * LANES, (n + 1) * LANES) for n in range(S5_N // LANES)]
    im = lambda cs: slice(S5_N + cs.start, S5_N + cs.stop)

    for c in range(n_chunks):
        rs = slice(c * t, (c + 1) * t)
        for ca, cb in zip(col_tiles[0::2], col_tiles[1::2]):
            xs = [_cmul(er_ref[:, cs], ei_ref[:, cs], bu_ref[rs, cs], bu_ref[rs, im(cs)]) for cs in (ca, cb)]
            wide = slice(ca.start, cb.stop)
            bu_ref[rs, wide] = _dot(tril, jnp.concatenate([xs[0][0], xs[1][0]], axis=1).astype(BF16))
            bu_ref[rs, im(wide)] = _dot(tril, jnp.concatenate([xs[0][1], xs[1][1]], axis=1).astype(BF16))

    for cs in col_tiles:
        car, cai = car_ref[:, cs], cai_ref[:, cs]
        for c in range(n_chunks):
            rs = slice(c * t, (c + 1) * t)
            hr, hi = _cmul(fr_ref[:, cs], fi_ref[:, cs], bu_ref[rs, cs] + car, bu_ref[rs, im(cs)] + cai)
            hr_ref[rs, cs] = hr.astype(BF16)
            hi_ref[rs, cs] = hi.astype(BF16)
            car, cai = hr[t - 1:t, :], hi[t - 1:t, :]
        car_ref[:, cs] = car
        cai_ref[:, cs] = cai

    n_out = C_WIDTH // S5_KT
    k_rows = S5_N // n_out
    for j in range(n_out):
        ks = slice(j * k_rows, (j + 1) * k_rows)
        cs = slice(j * S5_KT, (j + 1) * S5_KT)
        y_ref[:, cs] = (_dot(hr_ref[:, ks], cr_ref[ks, cs]) - _dot(hi_ref[:, ks], ci_ref[ks, cs])
                        + d_ref[:, cs] * u[:, cs])

    @pl.when(step == pl.num_programs(1) - 1)
    def _():
        hs_ref[0:1, :] = car_ref[...]
        hs_ref[1:2, :] = cai_ref[...]


def _s5_prompt_call(layer, u, bw, c_re, c_im, d, tabs, n_seq, seq_rows, step_rows):
    steps = seq_rows // step_rows
    row_spec = pl.BlockSpec((step_rows, C_WIDTH), lambda s, j, steps=steps: (s * steps + j, 0))
    lsel = lambda s, j, l=layer: (l, 0, 0)
    tab = pl.BlockSpec((None, S5_CHUNK, S5_N), lsel)
    return pl.pallas_call(
        functools.partial(_s5_prompt_kernel, n_chunks=step_rows // S5_CHUNK),
        out_shape=[jax.ShapeDtypeStruct((n_seq * seq_rows, C_WIDTH), F32),
                   jax.ShapeDtypeStruct((n_seq, 2, S5_N), F32)],
        grid=(n_seq, steps),
        in_specs=[row_spec,
                  _resident((None, C_WIDTH, 2 * S5_N), lsel),
                  _resident((None, S5_N, C_WIDTH), lsel),
                  _resident((None, S5_N, C_WIDTH), lsel),
                  pl.BlockSpec((None, 1, C_WIDTH), lsel),
                  tab, tab, tab, tab],
        out_specs=[row_spec, pl.BlockSpec((None, 2, S5_N), lambda s, j: (s, 0, 0))],
        scratch_shapes=[pltpu.VMEM((1, S5_N), F32), pltpu.VMEM((1, S5_N), F32),
                        pltpu.VMEM((step_rows, 2 * S5_N), F32),
                        pltpu.VMEM((step_rows, S5_N), BF16), pltpu.VMEM((step_rows, S5_N), BF16)],
        compiler_params=_cparams(("arbitrary", "arbitrary")),
        name="s5_chunked_scan",
    )(u, bw, c_re, c_im, d, *tabs)


def _s5_sample_kernel(u_ref, h0r_ref, h0i_ref, ar_ref, ai_ref, bw_ref, cr_ref, ci_ref, d_ref,
                      y_ref, hr_ref, hi_ref, *, n_tok, n_seq):
    hr, hi = h0r_ref[...], h0i_ref[...]
    a_re, a_im = ar_ref[...], ai_ref[...]
    for t in range(n_tok):
        u = u_ref[t * n_seq:(t + 1) * n_seq, :]
        bu = _dot(u.astype(BF16), bw_ref[...])
        hr, hi = _cmul(a_re, a_im, hr, hi)
        hr, hi = hr + bu[:, :S5_N], hi + bu[:, S5_N:]
        y_ref[t * n_seq:(t + 1) * n_seq, :] = (_dot(hr.astype(BF16), cr_ref[...]) - _dot(hi.astype(BF16), ci_ref[...])
                                               + d_ref[...] * u)
    hr_ref[...] = hr
    hi_ref[...] = hi


def _s5_sample_call(layer, u, h0_re, h0_im, a_re, a_im, bw, c_re, c_im, d, n_tok, n_seq):
    lsel = lambda i, l=layer: (l, 0, 0)
    full = lambda shape: pl.BlockSpec(shape, lambda i: (0,) * len(shape))
    return pl.pallas_call(
        functools.partial(_s5_sample_kernel, n_tok=n_tok, n_seq=n_seq),
        out_shape=[jax.ShapeDtypeStruct((n_tok * n_seq, C_WIDTH), F32),
                   jax.ShapeDtypeStruct((n_seq, S5_N), F32),
                   jax.ShapeDtypeStruct((n_seq, S5_N), F32)],
        grid=(1,),
        in_specs=[full((n_tok * n_seq, C_WIDTH)),
                  pl.BlockSpec((None, n_seq, S5_N), lsel),
                  pl.BlockSpec((None, n_seq, S5_N), lsel),
                  pl.BlockSpec((None, 1, S5_N), lsel),
                  pl.BlockSpec((None, 1, S5_N), lsel),
                  pl.BlockSpec((None, C_WIDTH, 2 * S5_N), lsel),
                  pl.BlockSpec((None, S5_N, C_WIDTH), lsel),
                  pl.BlockSpec((None, S5_N, C_WIDTH), lsel),
                  pl.BlockSpec((None, 1, C_WIDTH), lsel)],
        out_specs=[full((n_tok * n_seq, C_WIDTH)), full((n_seq, S5_N)), full((n_seq, S5_N))],
        compiler_params=_cparams(("arbitrary",)),
        name="s5_step_scan",
    )(u, h0_re, h0_im, a_re, a_im, bw, c_re, c_im, d)


W_GATES = A_V + 3 * D_MODEL


def _merge_kernel(x_ref, g_ref, shs, shp, scs, scp, gts, gtp,
                  oap, oas, obp, obs, ocp, ocs,
                  wz_ref, wa_ref, wb_ref, wc_ref, wglu_ref, wo_ref, o_ref, *, rows):
    x = x_ref[...]
    h = _ada_norm(x, g_ref[...], _mod_value(rows, shs, shp), _mod_value(rows, scs, scp)).astype(BF16)
    z = _dot(h, wz_ref[...])
    oa = _select_rows(rows, oap, oas) * _silu(z[:, :A_V])
    br_a = _dot(oa.astype(BF16), wa_ref[...])
    br_b = _dot(_select_rows(rows, obp, obs).astype(BF16), wb_ref[...])
    glu = _dot(_gelu_tanh(_select_rows(rows, ocp, ocs)).astype(BF16), wglu_ref[...])
    oc = glu[:, :C_WIDTH] * _sigmoid(glu[:, C_WIDTH:])
    br_c = _dot(oc.astype(BF16), wc_ref[...])
    o = A_V
    mix = (_sigmoid(z[:, o:o + D_MODEL]) * br_a
           + _sigmoid(z[:, o + D_MODEL:o + 2 * D_MODEL]) * br_b
           + _sigmoid(z[:, o + 2 * D_MODEL:o + 3 * D_MODEL]) * br_c)
    out = _dot(mix.astype(BF16), wo_ref[...])
    o_ref[...] = x + _mod_value(rows, gts, gtp) * out


def _merge_call(rows, layer, x, mod, norm_g, oa, ob, oc, wz, wa, wb, wc, wglu, wo):
    in_specs = [pl.BlockSpec((TOKEN_TILE, D_MODEL), lambda i: (i, 0)),
                pl.BlockSpec((None, 1, D_MODEL), lambda i, l=layer: (l, 0, 0))]
    operands = [x, norm_g]
    for j in (3, 4, 5):
        in_specs += _mod_specs(rows, layer, j)
        operands += [mod, mod]
    for pair, w in ((oa, A_V), (ob, B_Q), (oc, C_WIDTH)):
        in_specs += _pair_specs(rows, w)
        operands += list(pair)
    lsel = lambda i, l=layer: (l, 0, 0)
    in_specs += [_resident((None, D_MODEL, W_GATES), lsel),
                 _resident((None, A_V, D_MODEL), lsel),
                 _resident((None, B_Q, D_MODEL), lsel),
                 _resident((None, C_WIDTH, D_MODEL), lsel),
                 _resident((None, C_WIDTH, 2 * C_WIDTH), lsel),
                 _resident((None, D_MODEL, D_MODEL), lsel)]
    operands += [wz, wa, wb, wc, wglu, wo]
    return pl.pallas_call(
        functools.partial(_merge_kernel, rows=rows),
        out_shape=jax.ShapeDtypeStruct((rows.rows, D_MODEL), F32),
        grid=(rows.n_tiles,),
        in_specs=in_specs,
        out_specs=pl.BlockSpec((TOKEN_TILE, D_MODEL), lambda i: (i, 0)),
        compiler_params=_cparams(("arbitrary",)),
        name="branch_merge",
    )(*operands)


def _final_kernel(x_ref, g_ref, o_ref):
    x = x_ref[...]
    var = jnp.mean(x * x, axis=-1, keepdims=True)
    o_ref[...] = x * lax.rsqrt(var + NORM_EPS) * g_ref[...]


def _final_call(x, g, first_tile, n_tiles):
    return pl.pallas_call(
        _final_kernel,
        out_shape=jax.ShapeDtypeStruct((n_tiles * TOKEN_TILE, D_MODEL), F32),
        grid=(n_tiles,),
        in_specs=[pl.BlockSpec((TOKEN_TILE, D_MODEL), lambda i, f=first_tile: (i + f, 0)),
                  pl.BlockSpec((1, D_MODEL), lambda i: (0, 0))],
        out_specs=pl.BlockSpec((TOKEN_TILE, D_MODEL), lambda i: (i, 0)),
        compiler_params=_cparams(("arbitrary",)),
        name="final_norm",
    )(x, g)


def _rotary_tables(rows):
    half = ROPE_DIM // 2
    inv_freq = ROPE_THETA ** (-jnp.arange(half, dtype=F32) / half)
    pos_p = jnp.tile(jnp.arange(rows.l, dtype=jnp.int32), rows.b)
    pos_s = PAST_LEN + jnp.repeat(jnp.arange(rows.t, dtype=jnp.int32), rows.bs)
    pos = jnp.concatenate([pos_p, pos_s])
    ang = pos.astype(F32)[:, None] * inv_freq[None, :]
    cos, sin = jnp.cos(ang), jnp.sin(ang)
    n = pos.shape[0]
    ones = jnp.ones((n, B_HEAD_DIM - ROPE_DIM), F32)
    zeros = jnp.zeros((n, B_HEAD_DIM - ROPE_DIM), F32)
    z8 = jnp.zeros((n, half), F32)
    per_head = lambda parts: jnp.tile(jnp.concatenate(parts, axis=1), (1, LANES // B_HEAD_DIM))
    return (per_head([cos, cos, ones]), per_head([z8, sin, zeros]), per_head([-sin, z8, zeros]))


def _sample_to_seq(a, rows, pad_to):
    w = a.shape[-1]
    a = a.reshape(rows.t, rows.bs, w).transpose(1, 0, 2)
    a = jnp.pad(a, ((0, 0), (0, pad_to - rows.t), (0, 0)))
    return a.reshape(rows.bs * pad_to, w)


def _seq_to_sample(a, rows, pad_to):
    w = a.shape[-1]
    return a.reshape(rows.bs, pad_to, w)[:, :rows.t].transpose(1, 0, 2).reshape(rows.ns_rows, w)


def kernel(x_prompt, x_sample, state_gla, cache_swa_k, cache_swa_v, state_s5_re, state_s5_im, c_prompt, c_sample, w_ada, b_ada, norm_ffn1, ffn1_w_gate, ffn1_w_up, ffn1_w_down, norm_mix, w_in, gla_w_gate_up, gla_b_gate, gla_norm, swa_sinks, s5_a_re, s5_a_im, s5_b_re, s5_b_im, s5_c_re, s5_c_im, s5_d, s5_log_dt, s5_w_glu, w_branch_a, w_branch_b, w_branch_c, w_out, norm_ffn2, ffn2_w_gate, ffn2_w_up, ffn2_w_down, norm_final):
    depth = w_ada.shape[0]
    b, l, _ = x_prompt.shape
    bs, t, _ = x_sample.shape
    rows = _Rows(b, l, bs, t)
    np_rows = rows.np_rows

    x = (x_prompt.reshape(np_rows, D_MODEL), x_sample.transpose(1, 0, 2).reshape(rows.ns_rows, D_MODEL))
    c_all = jnp.concatenate([c_sample, c_prompt, jnp.zeros((8 - b, D_MODEL), F32)], axis=0)
    mod = _mod_call(c_all, w_ada, b_ada)

    pts = [0, A_QK, 2 * A_QK, 2 * A_QK + A_V]
    o_rank = pts[-1]
    o_ar = o_rank + A_GATE_RANK
    o_bq = o_ar + A_V
    o_cu_end = o_bq + B_Q + 2 * B_KV + C_WIDTH
    w_main = jnp.concatenate([w_in[:, :, :o_rank], w_in[:, :, o_bq:o_cu_end]], axis=-1).astype(BF16)
    w_rank = jnp.pad(w_in[:, :, o_rank:o_ar], ((0, 0), (0, 0), (0, LANES - A_GATE_RANK))).astype(BF16)
    w_up = jnp.pad(gla_w_gate_up, ((0, 0), (0, LANES - A_GATE_RANK), (0, 0))).astype(BF16)
    w_gates = jnp.concatenate([w_in[:, :, o_ar:o_bq], w_in[:, :, o_cu_end:]], axis=-1).astype(BF16)
    cast = lambda w: w.astype(BF16)
    f1g, f1u, f1d = cast(ffn1_w_gate), cast(ffn1_w_up), cast(ffn1_w_down)
    f2g, f2u, f2d = cast(ffn2_w_gate), cast(ffn2_w_up), cast(ffn2_w_down)
    wa, wb, wc, wglu, wo = cast(w_branch_a), cast(w_branch_b), cast(w_branch_c), cast(s5_w_glu), cast(w_out)
    vec3 = lambda a: a.reshape(depth, 1, a.shape[-1])
    n1, nm, n2 = vec3(norm_ffn1), vec3(norm_mix), vec3(norm_ffn2)
    b_gate, g_norm, s5d = vec3(gla_b_gate), vec3(gla_norm), vec3(s5_d)
    rot = _rotary_tables(rows)

    flat = lambda a: a.reshape(depth, 1, S5_N)
    ldt = flat(jnp.broadcast_to(s5_log_dt[:, :, None], (depth, C_GROUPS, C_STATE)))
    to_cols = lambda bm: bm.transpose(0, 3, 1, 2).reshape(depth, C_GROUP, S5_N)
    to_rows = lambda cm: jnp.tile(cm.transpose(0, 1, 3, 2).reshape(depth, S5_N, C_GROUP), (1, 1, C_GROUPS))
    s5_bw, s5_cr, s5_ci, a_re, a_im, e_re, e_im, f_re, f_im = _s5_prep_call(
        flat(s5_a_re), flat(s5_a_im), ldt, to_cols(s5_b_re), to_cols(s5_b_im), to_rows(s5_c_re), to_rows(s5_c_im))
    tabs = (e_re, e_im, f_re, f_im)
    h0_re = state_s5_re.reshape(depth, bs, S5_N)
    h0_im = state_s5_im.reshape(depth, bs, S5_N)
    cache_kt = cache_swa_k.transpose(0, 1, 3, 4, 2)
    cache_vt = cache_swa_v.transpose(0, 1, 3, 4, 2)

    outs = {k: [] for k in ("gla_p", "gla_s", "kp", "ks", "vp", "vs", "rp", "rs", "ip", "is")}
    pad8 = SWA_SAMPLE_ROWS
    for layer in range(depth):
        x = _ffn_call(rows, layer, 0, x, mod, n1, f1g, f1u, f1d)
        gq, gk, gv, la, sq, sk, sv, su = _inproj_call(rows, layer, x, mod, nm, w_main, w_rank, w_up, b_gate, rot)

        oa_p, st_p = _gla_call(layer, gq, gk, gv, la, g_norm, None, b, l, TOKEN_TILE, GLA_CHUNK, GLA_BLOCK)
        sseq = lambda a: _sample_to_seq(a[np_rows:], rows, pad8)
        oa_s, st_s = _gla_call(layer, sseq(gq), sseq(gk), sseq(gv), sseq(la), g_norm, state_gla,
                               bs, pad8, pad8, pad8, pad8, seqs_per_step=GLA_SAMPLE_SEQS)
        oa_s = _seq_to_sample(oa_s, rows, pad8)

        ob_p = _swa_call(layer, swa_sinks, sq, sk, sv, b, l // WINDOW)
        q_s = sq[np_rows:].reshape(t, bs, B_KV_HEADS, B_GROUP, B_HEAD_DIM).transpose(1, 2, 3, 0, 4)
        q_s = jnp.pad(q_s, ((0, 0), (0, 0), (0, 0), (0, pad8 - t), (0, 0)))
        q_s = q_s.reshape(bs, B_KV_HEADS, B_GROUP * pad8, B_HEAD_DIM)
        ob_s, kt_new, vt_new = _swa_sample_call(layer, swa_sinks, q_s, sseq(sk), sseq(sv), cache_kt, cache_vt, bs, t)
        ob_s = ob_s.reshape(bs, B_KV_HEADS, B_GROUP, pad8, B_HEAD_DIM)[:, :, :, :t]
        ob_s = ob_s.transpose(3, 0, 1, 2, 4).reshape(rows.ns_rows, B_Q)

        oc_p, hs_p = _s5_prompt_call(layer, su, s5_bw, s5_cr, s5_ci, s5d, tabs, b, l, TOKEN_TILE)
        oc_s, hr_s, hi_s = _s5_sample_call(layer, su[np_rows:], h0_re, h0_im, a_re, a_im,
                                           s5_bw, s5_cr, s5_ci, s5d, t, bs)

        x = _merge_call(rows, layer, x, mod, nm, (oa_p, oa_s), (ob_p, ob_s), (oc_p, oc_s),
                        w_gates, wa, wb, wc, wglu, wo)
        x = _ffn_call(rows, layer, 6, x, mod, n2, f2g, f2u, f2d)

        outs["gla_p"].append(st_p)
        outs["gla_s"].append(st_s)
        kv_shape = (b, WINDOW, B_KV_HEADS, B_HEAD_DIM)
        outs["kp"].append(sk[:np_rows].reshape(b, l, B_KV)[:, l - WINDOW:].reshape(kv_shape))
        outs["vp"].append(sv[:np_rows].reshape(b, l, B_KV)[:, l - WINDOW:].reshape(kv_shape))
        outs["ks"].append(kt_new.transpose(0, 3, 1, 2))
        outs["vs"].append(vt_new.transpose(0, 3, 1, 2))
        outs["rp"].append(hs_p[:, 0].reshape(b, C_GROUPS, C_STATE))
        outs["ip"].append(hs_p[:, 1].reshape(b, C_GROUPS, C_STATE))
        outs["rs"].append(hr_s.reshape(bs, C_GROUPS, C_STATE))
        outs["is"].append(hi_s.reshape(bs, C_GROUPS, C_STATE))

    g_fin = norm_final.reshape(1, D_MODEL)
    y_p = _final_call(x, g_fin, 0, rows.n_prompt_tiles).reshape(b, l, D_MODEL)
    y_s = _final_call(x, g_fin, rows.n_prompt_tiles, 1).reshape(t, bs, D_MODEL).transpose(1, 0, 2)
    st = lambda k: jnp.stack(outs[k])
    return (y_p, y_s, st("gla_p"), st("gla_s"), st("kp"), st("ks"), st("vp"), st("vs"),
            st("rp"), st("rs"), st("ip"), st("is"))
```

```python
import functools
import math

import jax
import jax.numpy as jnp
from jax import lax
from jax.experimental import pallas as pl
from jax.experimental.pallas import tpu as pltpu

D_MODEL = 1024
D_FF = 2816
PAST_LEN = 8192
A_HEADS, A_DK, A_DV = 4, 64, 128
A_GATE_RANK = 16
A_GATE_NORM = 16.0
B_HEADS, B_KV_HEADS, B_HEAD_DIM = 8, 2, 64
B_GROUP = B_HEADS // B_KV_HEADS
WINDOW = 128
ROPE_DIM = B_HEAD_DIM // 4
ROPE_THETA = 500000.0
C_WIDTH, C_GROUP, C_STATE = 512, 16, 64
C_GROUPS = C_WIDTH // C_GROUP
MACARON_W = 0.5
NORM_EPS = 1e-6
N_MOD = 9

A_QK = A_HEADS * A_DK
A_V = A_HEADS * A_DV
B_Q = B_HEADS * B_HEAD_DIM
B_KV = B_KV_HEADS * B_HEAD_DIM
S5_N = C_GROUPS * C_STATE

LANES = 128
TOKEN_TILE = 512
GLA_CHUNK = 64
GLA_BLOCK = 32
GLA_SAMPLE_SEQS = 8
S5_CHUNK = 64
S5_KT = 256
VMEM_LIMIT = 56 * 1024 * 1024

F32 = jnp.float32
BF16 = jnp.bfloat16
NEG_BIG = -1e30


def _cparams(sem):
    return pltpu.CompilerParams(dimension_semantics=sem, vmem_limit_bytes=VMEM_LIMIT)


def _dot(a, b):
    return jnp.dot(a, b, preferred_element_type=F32)


def _dot_nt(a, b):
    return lax.dot_general(a, b, (((1,), (1,)), ((), ())), preferred_element_type=F32)


def _sigmoid(x):
    return 1.0 / (1.0 + jnp.exp(-x))


def _silu(x):
    return x * _sigmoid(x)


def _gelu_tanh(x):
    return 0.5 * x * (1.0 + jnp.tanh(math.sqrt(2.0 / math.pi) * (x + 0.044715 * (x * x * x))))


def _log_sigmoid(x):
    return jnp.minimum(x, 0.0) - jnp.log1p(jnp.exp(-jnp.abs(x)))


def _resident(shape, index_map):
    return pl.BlockSpec(shape, index_map, pipeline_mode=pl.Buffered(1))


def _mod_kernel(c_ref, w_ref, b_ref, o_ref):
    c = c_ref[...]
    s = _silu(c).astype(BF16)
    o_ref[...] = _dot(s, w_ref[...].astype(BF16)) + b_ref[...]


MOD_BLOCK = 3 * D_MODEL


def _mod_call(c_all, w_ada, b_ada):
    depth = w_ada.shape[0]
    rows = c_all.shape[0]
    return pl.pallas_call(
        _mod_kernel,
        out_shape=jax.ShapeDtypeStruct((depth, rows, N_MOD * D_MODEL), F32),
        grid=(depth, N_MOD * D_MODEL // MOD_BLOCK),
        in_specs=[
            pl.BlockSpec((rows, D_MODEL), lambda l, j: (0, 0)),
            pl.BlockSpec((None, D_MODEL, MOD_BLOCK), lambda l, j: (l, 0, j)),
            pl.BlockSpec((None, 1, MOD_BLOCK), lambda l, j: (l, 0, j)),
        ],
        out_specs=pl.BlockSpec((None, rows, MOD_BLOCK), lambda l, j: (l, 0, j)),
        compiler_params=_cparams(("arbitrary", "arbitrary")),
        name="ada_mod",
    )(c_all, w_ada, b_ada.reshape(depth, 1, N_MOD * D_MODEL))


class _Rows:
    def __init__(self, n_prompt_seq, seq_len, n_sample_seq, n_sample_tok):
        self.b = n_prompt_seq
        self.l = seq_len
        self.bs = n_sample_seq
        self.t = n_sample_tok
        self.np_rows = n_prompt_seq * seq_len
        self.ns_rows = n_sample_seq * n_sample_tok
        assert seq_len % TOKEN_TILE == 0 and self.ns_rows == TOKEN_TILE
        assert n_sample_seq % 8 == 0 and seq_len % WINDOW == 0
        self.rows = self.np_rows + self.ns_rows
        self.n_prompt_tiles = self.np_rows // TOKEN_TILE
        self.tiles_per_seq = seq_len // TOKEN_TILE
        self.n_tiles = self.n_prompt_tiles + 1
        self.mod_rows = n_sample_seq + 8
        assert n_prompt_seq <= 8


def _mod_specs(rows, layer, j):
    return [
        pl.BlockSpec((None, rows.bs, D_MODEL), lambda i, l=layer, j=j: (l, 0, j)),
        pl.BlockSpec((None, 8, D_MODEL), lambda i, l=layer, j=j, r=rows.bs // 8: (l, r, j)),
    ]


def _mod_value(rows, ms_ref, mp_ref):
    i = pl.program_id(0)
    is_sample = i == rows.n_prompt_tiles
    seq = jnp.minimum(i // rows.tiles_per_seq, rows.b - 1)
    s = ms_ref[...]
    s = jnp.concatenate([s] * rows.t, axis=0)
    p = mp_ref[pl.ds(seq, 1), :]
    return jnp.where(is_sample, s, p)


def _ada_norm(x, g, shift, scale):
    var = jnp.mean(x * x, axis=-1, keepdims=True)
    return (x * lax.rsqrt(var + NORM_EPS)) * g * (1.0 + scale) + shift


def _select_rows(rows, p_ref, s_ref):
    is_sample = pl.program_id(0) == rows.n_prompt_tiles
    return jnp.where(is_sample, s_ref[...], p_ref[...])


def _pair_specs(rows, width):
    last = rows.n_prompt_tiles - 1
    return [
        pl.BlockSpec((TOKEN_TILE, width), lambda i, last=last: (jnp.minimum(i, last), 0)),
        pl.BlockSpec((TOKEN_TILE, width), lambda i: (0, 0)),
    ]


MXU_TILE = 256
FF_SPLITS = (0, 6 * MXU_TILE, D_FF)
assert D_FF % MXU_TILE == 0


def _ffn_kernel(*refs, rows, x_pair, final_norm):
    n_in = 11 + int(x_pair) + int(final_norm)
    ins, outs = refs[:n_in], refs[n_in:]
    if x_pair:
        x = _select_rows(rows, ins[0], ins[1])
    else:
        x = ins[0][...]
    g_ref, shs, shp, scs, scp, gts, gtp, wg_ref, wu_ref, wd_ref = ins[1 + int(x_pair):11 + int(x_pair)]
    h = _ada_norm(x, g_ref[...], _mod_value(rows, shs, shp), _mod_value(rows, scs, scp)).astype(BF16)
    chunks = [slice(lo, hi) for lo, hi in zip(FF_SPLITS[:-1], FF_SPLITS[1:])]
    gate_up = [(_dot(h, wg_ref[:, sl]), _dot(h, wu_ref[:, sl])) for sl in chunks]
    act = [(_silu(gate) * up).astype(BF16) for gate, up in gate_up]
    acc = _dot(act[0], wd_ref[chunks[0], :])
    for a, sl in zip(act[1:], chunks[1:]):
        acc = acc + _dot(a, wd_ref[sl, :])
    out = x + (MACARON_W * _mod_value(rows, gts, gtp)) * acc
    if not final_norm:
        outs[0][...] = out
        return
    var = jnp.mean(out * out, axis=-1, keepdims=True)
    y = out * lax.rsqrt(var + NORM_EPS) * ins[-1][...]
    is_sample = pl.program_id(0) == rows.n_prompt_tiles

    @pl.when(jnp.logical_not(is_sample))
    def _():
        outs[0][...] = y

    @pl.when(is_sample)
    def _():
        outs[1][...] = y


def _ffn_call(rows, layer, mod_base, x, mod, norm_g, wg, wu, wd, final_g=None):
    x_pair = isinstance(x, tuple)
    if x_pair:
        in_specs = _pair_specs(rows, D_MODEL)
        operands = list(x)
    else:
        in_specs = [pl.BlockSpec((TOKEN_TILE, D_MODEL), lambda i: (i, 0))]
        operands = [x]
    in_specs.append(pl.BlockSpec((None, 1, D_MODEL), lambda i, l=layer: (l, 0, 0)))
    operands.append(norm_g)
    for j in range(3):
        in_specs += _mod_specs(rows, layer, mod_base + j)
        operands += [mod, mod]
    in_specs += [
        _resident((None, D_MODEL, D_FF), lambda i, l=layer: (l, 0, 0)),
        _resident((None, D_MODEL, D_FF), lambda i, l=layer: (l, 0, 0)),
        _resident((None, D_FF, D_MODEL), lambda i, l=layer: (l, 0, 0)),
    ]
    operands += [wg, wu, wd]
    if final_g is None:
        out_shape = jax.ShapeDtypeStruct((rows.rows, D_MODEL), F32)
        out_specs = pl.BlockSpec((TOKEN_TILE, D_MODEL), lambda i: (i, 0))
    else:
        in_specs.append(pl.BlockSpec((1, D_MODEL), lambda i: (0, 0)))
        operands.append(final_g)
        out_shape = [jax.ShapeDtypeStruct((rows.np_rows, D_MODEL), F32),
                     jax.ShapeDtypeStruct((rows.ns_rows, D_MODEL), F32)]
        out_specs = _pair_specs(rows, D_MODEL)
    return pl.pallas_call(
        functools.partial(_ffn_kernel, rows=rows, x_pair=x_pair, final_norm=final_g is not None),
        out_shape=out_shape,
        grid=(rows.n_tiles,),
        in_specs=in_specs,
        out_specs=out_specs,
        compiler_params=_cparams(("arbitrary",)),
        name="ada_swiglu",
    )(*operands)


W_MAIN = 2 * A_QK + A_V + B_Q + 2 * B_KV + C_WIDTH


def _rotary(x, cos, s_up, s_dn):
    half = ROPE_DIM // 2
    return x * cos + pltpu.roll(x, half, axis=1) * s_up + pltpu.roll(x, LANES - half, axis=1) * s_dn


def _inproj_kernel(x_ref, g_ref, shs, shp, scs, scp, wm_ref, wr_ref, wup_ref, bg_ref,
                   cos_ref, sup_ref, sdn_ref,
                   gq_ref, gk_ref, gv_ref, la_ref, sq_ref, sk_ref, sv_ref, su_ref, *, rows):
    x = x_ref[...]
    h = _ada_norm(x, g_ref[...], _mod_value(rows, shs, shp), _mod_value(rows, scs, scp)).astype(BF16)
    z = _dot(h, wm_ref[...])
    o = 0
    gq_ref[...] = z[:, o:o + A_QK] * (A_DK ** -0.5)
    o += A_QK
    gk_ref[...] = z[:, o:o + A_QK]
    o += A_QK
    gv_ref[...] = z[:, o:o + A_V]
    o += A_V
    cos, s_up, s_dn = cos_ref[...], sup_ref[...], sdn_ref[...]
    for c in range(B_Q // LANES):
        sq_ref[:, c * LANES:(c + 1) * LANES] = _rotary(z[:, o + c * LANES:o + (c + 1) * LANES], cos, s_up, s_dn)
    o += B_Q
    sk_ref[...] = _rotary(z[:, o:o + B_KV], cos, s_up, s_dn)
    o += B_KV
    sv_ref[...] = z[:, o:o + B_KV]
    o += B_KV
    su_ref[...] = z[:, o:o + C_WIDTH]
    low = _dot(h, wr_ref[...]).astype(BF16)
    pre = _dot(low, wup_ref[...]) + bg_ref[...]
    la_ref[...] = _log_sigmoid(pre) * (1.0 / A_GATE_NORM)


def _inproj_call(rows, layer, x, mod, norm_g, w_main, w_rank, w_up, b_gate, rot):
    row_spec = lambda w: pl.BlockSpec((TOKEN_TILE, w), lambda i: (i, 0))
    rot_spec = pl.BlockSpec(
        (TOKEN_TILE, LANES),
        lambda i, n=rows.n_prompt_tiles, per=rows.tiles_per_seq: (jnp.where(i < n, i % per, per), 0))
    in_specs = [row_spec(D_MODEL), pl.BlockSpec((None, 1, D_MODEL), lambda i, l=layer: (l, 0, 0))]
    operands = [x, norm_g]
    for j in (3, 4):
        in_specs += _mod_specs(rows, layer, j)
        operands += [mod, mod]
    in_specs += [
        _resident((None, D_MODEL, W_MAIN), lambda i, l=layer: (l, 0, 0)),
        _resident((None, D_MODEL, LANES), lambda i, l=layer: (l, 0, 0)),
        _resident((None, LANES, A_QK), lambda i, l=layer: (l, 0, 0)),
        pl.BlockSpec((None, 1, A_QK), lambda i, l=layer: (l, 0, 0)),
        rot_spec, rot_spec, rot_spec,
    ]
    operands += [w_main, w_rank, w_up, b_gate, *rot]
    widths = (A_QK, A_QK, A_V, A_QK, B_Q, B_KV, B_KV, C_WIDTH)
    return pl.pallas_call(
        functools.partial(_inproj_kernel, rows=rows),
        out_shape=[jax.ShapeDtypeStruct((rows.rows, w), F32) for w in widths],
        grid=(rows.n_tiles,),
        in_specs=in_specs,
        out_specs=[row_spec(w) for w in widths],
        compiler_params=_cparams(("arbitrary",)),
        name="mixer_in_proj",
    )(*operands)


def _cumsum_rows(x):
    n = x.shape[0]
    row = lax.broadcasted_iota(jnp.int32, x.shape, 0)
    s = 1
    while s < n:
        x = x + jnp.where(row >= s, pltpu.roll(x, s, axis=0), 0.0)
        s *= 2
    return x


def _head_blockdiag(x, lanes_per_head):
    lane_head = lax.broadcasted_iota(jnp.int32, x.shape, 1) // lanes_per_head
    return jnp.concatenate([jnp.where(lane_head == h, x, 0.0) for h in range(A_HEADS)], axis=0)


def _gla_kernel(*refs, chunk, block, n_chunks, n_seqs, single_step, has_state):
    if has_state:
        q_ref, k_ref, v_ref, la_ref, gn_ref, s0_ref, o_ref, so_ref, st_ref = refs
    else:
        q_ref, k_ref, v_ref, la_ref, gn_ref, o_ref, so_ref, st_ref = refs
    n_blocks = chunk // block
    st_shape = (A_V, A_QK)
    st_row_head = lax.broadcasted_iota(jnp.int32, st_shape, 0) // A_DV
    st_lane_head = lax.broadcasted_iota(jnp.int32, st_shape, 1) // A_DK
    st_mask = st_row_head == st_lane_head
    srow = lax.broadcasted_iota(jnp.int32, (block, A_HEADS * block), 0)
    scol = lax.broadcasted_iota(jnp.int32, (block, A_HEADS * block), 1) % block
    causal = scol <= srow
    gn = gn_ref[...]

    def init_state(sq):
        st_ref[sq] = jnp.zeros(st_shape, F32)
        if has_state:
            for h in range(A_HEADS):
                st_ref[sq, h * A_DV:(h + 1) * A_DV, h * A_DK:(h + 1) * A_DK] = s0_ref[sq, h].T

    def emit_state(sq):
        for h in range(A_HEADS):
            so_ref[sq, h] = st_ref[sq, h * A_DV:(h + 1) * A_DV, h * A_DK:(h + 1) * A_DK].T

    if single_step:
        for sq in range(n_seqs):
            init_state(sq)
    else:
        pl.when(pl.program_id(1) == 0)(lambda: init_state(0))

    units = [(sq, c) for sq in range(n_seqs) for c in range(n_chunks)]
    rows_of = {u: slice((u[0] * n_chunks + u[1]) * chunk, (u[0] * n_chunks + u[1] + 1) * chunk) for u in units}

    q, k, v, b = {}, {}, {}, {}
    for u in units:
        q[u], k[u], v[u] = q_ref[rows_of[u], :], k_ref[rows_of[u], :], v_ref[rows_of[u], :]
        b[u] = _cumsum_rows(la_ref[rows_of[u], :])

    s_diag, s_off, upd, q_in = {}, {}, {}, {}
    for u in units:
        for i in range(n_blocks):
            lo, hi = i * block, (i + 1) * block
            qi, ki, bi = q[u][lo:hi], k[u][lo:hi], b[u][lo:hi]
            mid = b[u][lo + block // 2 - 1:lo + block // 2, :]
            qd = (qi * jnp.exp(bi - mid)).astype(BF16)
            kd = _head_blockdiag(ki * jnp.exp(mid - bi), A_DK).astype(BF16)
            s_diag[u, i] = _dot_nt(qd, kd)
            if i > 0:
                ref = b[u][lo - 1:lo, :]
                qo = (qi * jnp.exp(bi - ref)).astype(BF16)
                ko = _head_blockdiag(k[u][:lo] * jnp.exp(ref - b[u][:lo]), A_DK).astype(BF16)
                s_off[u, i] = _dot_nt(qo, ko)
        b_last = b[u][chunk - 1:chunk, :]
        upd[u] = _dot(v[u].T.astype(BF16), (k[u] * jnp.exp(b_last - b[u])).astype(BF16))
        q_in[u] = (q[u] * jnp.exp(b[u])).astype(BF16)

    intra = {}
    for u in units:
        outs = []
        for i in range(n_blocks):
            lo, hi = i * block, (i + 1) * block
            att = jnp.where(causal, s_diag[u, i], 0.0).astype(BF16)
            oi = _dot(att, _head_blockdiag(v[u][lo:hi], A_DV).astype(BF16))
            if i > 0:
                oi = oi + _dot(s_off[u, i].astype(BF16), _head_blockdiag(v[u][:lo], A_DV).astype(BF16))
            outs.append(oi)
        intra[u] = jnp.concatenate(outs, axis=0) if n_blocks > 1 else outs[0]

    for u in units:
        sq = u[0]
        st = st_ref[sq]
        o = _dot_nt(q_in[u], st.astype(BF16)) + intra[u]
        st_ref[sq] = st * jnp.exp(b[u][chunk - 1:chunk, :]) + jnp.where(st_mask, upd[u], 0.0)
        normed = []
        for h in range(A_HEADS):
            oh = o[:, h * A_DV:(h + 1) * A_DV]
            var = jnp.mean(oh * oh, axis=-1, keepdims=True)
            normed.append(oh * lax.rsqrt(var + NORM_EPS) * gn)
        o_ref[rows_of[u], :] = jnp.concatenate(normed, axis=1)

    if single_step:
        for sq in range(n_seqs):
            emit_state(sq)
    else:
        pl.when(pl.program_id(1) == pl.num_programs(1) - 1)(lambda: emit_state(0))


def _gla_call(layer, q, k, v, la, gnorm, s0, n_seq, seq_rows, step_rows, chunk, block, seqs_per_step=1):
    steps = seq_rows // step_rows
    sps = seqs_per_step
    assert sps == 1 or steps == 1
    blk_rows = sps * step_rows
    row_spec = lambda w: pl.BlockSpec((blk_rows, w), lambda s, j, steps=steps: (s * steps + j, 0))
    in_specs = [row_spec(A_QK), row_spec(A_QK), row_spec(A_V), row_spec(A_QK),
                pl.BlockSpec((None, 1, A_DV), lambda s, j, l=layer: (l, 0, 0))]
    operands = [q, k, v, la, gnorm]
    if s0 is not None:
        in_specs.append(pl.BlockSpec((None, sps, A_HEADS, A_DK, A_DV), lambda s, j, l=layer: (l, s, 0, 0, 0)))
        operands.append(s0)
    return pl.pallas_call(
        functools.partial(_gla_kernel, chunk=chunk, block=block, n_chunks=step_rows // chunk,
                          n_seqs=sps, single_step=steps == 1, has_state=s0 is not None),
        out_shape=[jax.ShapeDtypeStruct((n_seq * seq_rows, A_V), F32),
                   jax.ShapeDtypeStruct((n_seq, A_HEADS, A_DK, A_DV), F32)],
        grid=(n_seq // sps, steps),
        in_specs=in_specs,
        out_specs=[row_spec(A_V),
                   pl.BlockSpec((sps, A_HEADS, A_DK, A_DV), lambda s, j: (s, 0, 0, 0))],
        scratch_shapes=[pltpu.VMEM((sps, A_V, A_QK), F32)],
        compiler_params=_cparams(("arbitrary", "arbitrary")),
        name="gla_chunked",
    )(*operands)


SWA_STEP_BLOCKS = 4


def _swa_kernel(sink_ref, q_ref, kc_ref, vc_ref, kp_ref, vp_ref, o_ref, *, layer):
    tk = 2 * WINDOW
    kj = lax.broadcasted_iota(jnp.int32, (tk, WINDOW), 0)
    qi = lax.broadcasted_iota(jnp.int32, (tk, WINDOW), 1) + WINDOW
    band = (kj <= qi) & (kj >= qi - WINDOW)
    zeros = jnp.zeros((B_HEAD_DIM, WINDOW), F32)
    chains = [(blk, h) for blk in range(SWA_STEP_BLOCKS) for h in range(B_KV_HEADS)]

    keys, vals_t, q_t, valid = [], [], [], []
    for blk in range(SWA_STEP_BLOCKS):
        cur = slice(blk * WINDOW, (blk + 1) * WINDOW)
        if blk == 0:
            k_prev, v_prev = kp_ref[...], vp_ref[...]
            valid.append(band & (kj >= jnp.where(pl.program_id(1) > 0, 0, WINDOW)))
        else:
            prev = slice((blk - 1) * WINDOW, blk * WINDOW)
            k_prev, v_prev = kc_ref[prev, :], vc_ref[prev, :]
            valid.append(band)
        keys.append(jnp.concatenate([k_prev, kc_ref[cur, :]], axis=0).astype(BF16))
        vals_t.append(jnp.concatenate([v_prev, vc_ref[cur, :]], axis=0).T.astype(BF16))
        q_t.append(q_ref[cur, :].T)

    scores = {}
    for blk, h in chains:
        cols = []
        for g in range(B_GROUP):
            hd = h * B_GROUP + g
            qg = q_t[blk][hd * B_HEAD_DIM:(hd + 1) * B_HEAD_DIM]
            cols.append(jnp.concatenate([qg, zeros] if h == 0 else [zeros, qg], axis=0))
        scores[blk, h] = _dot(keys[blk], jnp.concatenate(cols, axis=1).astype(BF16))

    probs, inv = {}, {}
    for blk, h in chains:
        ps = []
        for g in range(B_GROUP):
            sink = sink_ref[layer, h * B_GROUP + g]
            sg = scores[blk, h][:, g * WINDOW:(g + 1) * WINDOW] * (B_HEAD_DIM ** -0.5)
            sg = jnp.where(valid[blk], sg, NEG_BIG)
            m = jnp.maximum(jnp.max(sg, axis=0, keepdims=True), sink)
            p = jnp.exp(sg - m)
            inv[blk, h, g] = 1.0 / (jnp.sum(p, axis=0, keepdims=True) + jnp.exp(sink - m))
            ps.append(p.astype(BF16))
        probs[blk, h] = jnp.concatenate(ps, axis=1)

    out_t = {}
    for blk, h in chains:
        out_t[blk, h] = _dot(vals_t[blk][h * B_HEAD_DIM:(h + 1) * B_HEAD_DIM], probs[blk, h])

    for blk in range(SWA_STEP_BLOCKS):
        rows = [out_t[blk, h][:, g * WINDOW:(g + 1) * WINDOW] * inv[blk, h, g]
                for h in range(B_KV_HEADS) for g in range(B_GROUP)]
        o_ref[blk * WINDOW:(blk + 1) * WINDOW, :] = jnp.concatenate(rows, axis=0).T


def _swa_call(layer, sinks, q, k, v, n_seq, blocks_per_seq):
    sb = SWA_STEP_BLOCKS
    assert blocks_per_seq % sb == 0
    steps = blocks_per_seq // sb
    cur = lambda w: pl.BlockSpec((sb * WINDOW, w), lambda s, n, steps=steps: (s * steps + n, 0))
    prev = pl.BlockSpec((WINDOW, B_KV),
                        lambda s, n, nb=blocks_per_seq, sb=sb: (s * nb + jnp.maximum(n * sb - 1, 0), 0))
    return pl.pallas_call(
        functools.partial(_swa_kernel, layer=layer),
        out_shape=jax.ShapeDtypeStruct((n_seq * blocks_per_seq * WINDOW, B_Q), F32),
        grid=(n_seq, steps),
        in_specs=[pl.BlockSpec(memory_space=pltpu.SMEM), cur(B_Q), cur(B_KV), cur(B_KV), prev, prev],
        out_specs=cur(B_Q),
        compiler_params=_cparams(("arbitrary", "arbitrary")),
        name="swa_banded",
    )(sinks, q, k, v, k, v)


SWA_SAMPLE_SEQS = 8
SWA_SAMPLE_ROWS = 8


def _swa_sample_kernel(sink_ref, q_ref, kn_ref, vn_ref, kt_ref, vt_ref, o_ref, nk_ref, nv_ref, *, layer, n_tok):
    pr = SWA_SAMPLE_ROWS
    keep = WINDOW - n_tok
    lane = lax.broadcasted_iota(jnp.int32, (B_HEAD_DIM, WINDOW), 1)
    rows = B_GROUP * pr
    tok = lax.broadcasted_iota(jnp.int32, (rows, WINDOW), 0) % pr
    grp = lax.broadcasted_iota(jnp.int32, (rows, 1), 0) // pr
    pos = lax.broadcasted_iota(jnp.int32, (rows, WINDOW), 1)
    valid_new = pos - keep <= tok
    valid_old = (pos >= tok) & (pos < n_tok)

    def shifted_in(old, new_rows):
        padded = jnp.concatenate([jnp.zeros((WINDOW - pr, B_KV), F32), new_rows], axis=0)
        cols = pltpu.roll(padded, n_tok, axis=0).T
        return [jnp.where(lane < keep, pltpu.roll(old[h], keep, axis=1),
                          cols[h * B_HEAD_DIM:(h + 1) * B_HEAD_DIM]) for h in range(B_KV_HEADS)]

    chains = [(i, h) for i in range(SWA_SAMPLE_SEQS) for h in range(B_KV_HEADS)]
    k_old, v_old, k_new, v_new = {}, {}, {}, {}
    for i in range(SWA_SAMPLE_SEQS):
        rs = slice(i * pr, (i + 1) * pr)
        ko = [kt_ref[i, h] for h in range(B_KV_HEADS)]
        vo = [vt_ref[i, h] for h in range(B_KV_HEADS)]
        kn, vn = shifted_in(ko, kn_ref[rs, :]), shifted_in(vo, vn_ref[rs, :])
        for h in range(B_KV_HEADS):
            nk_ref[i, h] = kn[h]
            nv_ref[i, h] = vn[h]
            k_old[i, h], v_old[i, h] = ko[h].astype(BF16), vo[h].astype(BF16)
            k_new[i, h], v_new[i, h] = kn[h].astype(BF16), vn[h].astype(BF16)

    s_new, s_old = {}, {}
    for c in chains:
        q = q_ref[c[0], c[1]].astype(BF16)
        s_new[c], s_old[c] = _dot(q, k_new[c]), _dot(q, k_old[c])

    p_new, p_old, inv = {}, {}, {}
    for c in chains:
        sn = jnp.where(valid_new, s_new[c] * (B_HEAD_DIM ** -0.5), NEG_BIG)
        so = jnp.where(valid_old, s_old[c] * (B_HEAD_DIM ** -0.5), NEG_BIG)
        sink = jnp.zeros((rows, 1), F32)
        for g in range(B_GROUP):
            sink = jnp.where(grp == g, sink_ref[layer, c[1] * B_GROUP + g], sink)
        m = jnp.maximum(jnp.maximum(jnp.max(sn, axis=-1, keepdims=True),
                                    jnp.max(so, axis=-1, keepdims=True)), sink)
        pn, po = jnp.exp(sn - m), jnp.exp(so - m)
        inv[c] = 1.0 / (jnp.sum(pn, axis=-1, keepdims=True) + jnp.sum(po, axis=-1, keepdims=True)
                        + jnp.exp(sink - m))
        p_new[c], p_old[c] = pn.astype(BF16), po.astype(BF16)

    for c in chains:
        o = _dot_nt(p_new[c], v_new[c]) + _dot_nt(p_old[c], v_old[c])
        o_ref[c[0], c[1]] = o * inv[c]


def _swa_sample_call(layer, sinks, q, k_new, v_new, kt, vt, n_seq, n_tok):
    sb, pr = SWA_SAMPLE_SEQS, SWA_SAMPLE_ROWS
    qspec = pl.BlockSpec((sb, B_KV_HEADS, B_GROUP * pr, B_HEAD_DIM), lambda s: (s, 0, 0, 0))
    new_rows = pl.BlockSpec((sb * pr, B_KV), lambda s: (s, 0))
    buf_in = pl.BlockSpec((None, sb, B_KV_HEADS, B_HEAD_DIM, WINDOW), lambda s, l=layer: (l, s, 0, 0, 0))
    buf_out = pl.BlockSpec((sb, B_KV_HEADS, B_HEAD_DIM, WINDOW), lambda s: (s, 0, 0, 0))
    buf_sds = jax.ShapeDtypeStruct((n_seq, B_KV_HEADS, B_HEAD_DIM, WINDOW), F32)
    return pl.pallas_call(
        functools.partial(_swa_sample_kernel, layer=layer, n_tok=n_tok),
        out_shape=[jax.ShapeDtypeStruct((n_seq, B_KV_HEADS, B_GROUP * pr, B_HEAD_DIM), F32), buf_sds, buf_sds],
        grid=(n_seq // sb,),
        in_specs=[pl.BlockSpec(memory_space=pltpu.SMEM), qspec, new_rows, new_rows, buf_in, buf_in],
        out_specs=[qspec, buf_out, buf_out],
        compiler_params=_cparams(("arbitrary",)),
        name="swa_decode",
    )(sinks, q, k_new, v_new, kt, vt)


def _s5_prep_kernel(lr_ref, li_ref, ldt_ref, br_ref, bi_ref, ctr_ref, cti_ref,
                    bw_ref, cr_ref, ci_ref, ar_ref, ai_ref, er_ref, ei_ref, fr_ref, fi_ref):
    lr, li = lr_ref[...], li_ref[...]
    dt = jnp.exp(ldt_ref[...])
    rho, th = lr * dt, li * dt
    mag = jnp.exp(rho)
    a_re, a_im = mag * jnp.cos(th), mag * jnp.sin(th)
    den = lr * lr + li * li
    nr, ni = a_re - 1.0, a_im
    f_re = (nr * lr + ni * li) / den
    f_im = (ni * lr - nr * li) / den
    br, bi = br_ref[...], bi_ref[...]
    bb_re = f_re * br - f_im * bi
    bb_im = f_re * bi + f_im * br
    lane_group = lax.broadcasted_iota(jnp.int32, bb_re.shape, 1) // C_STATE
    place = lambda bb: jnp.concatenate(
        [jnp.where(lane_group == g, bb, 0.0) for g in range(C_GROUPS)], axis=0).astype(BF16)
    bw_ref[:, :S5_N] = place(bb_re)
    bw_ref[:, S5_N:] = place(bb_im)
    spread_c = lax.broadcasted_iota(jnp.int32, (LANES, C_WIDTH), 0)
    spread_j = lax.broadcasted_iota(jnp.int32, (LANES, C_WIDTH), 1) % C_GROUP
    spread = jnp.where(spread_c == spread_j, 1.0, 0.0).astype(BF16)
    row_group = lax.broadcasted_iota(jnp.int32, cr_ref.shape, 0) // C_STATE
    col_group = lax.broadcasted_iota(jnp.int32, cr_ref.shape, 1) // C_GROUP
    own = row_group == col_group
    cr_ref[...] = jnp.where(own, _dot(ctr_ref[...].astype(BF16), spread), 0.0).astype(BF16)
    ci_ref[...] = jnp.where(own, _dot(cti_ref[...].astype(BF16), spread), 0.0).astype(BF16)
    ar_ref[...] = a_re
    ai_ref[...] = a_im
    s = (lax.broadcasted_iota(jnp.int32, er_ref.shape, 0) + 1).astype(F32)
    cs, sn = jnp.cos(s * th), jnp.sin(s * th)
    e_mag, f_mag = jnp.exp(-(s * rho)), jnp.exp(s * rho)
    er_ref[...] = e_mag * cs
    ei_ref[...] = -(e_mag * sn)
    fr_ref[...] = f_mag * cs
    fi_ref[...] = f_mag * sn


def _s5_prep_call(lr, li, ldt, b_re_t, b_im_t, c_re_t, c_im_t):
    depth = lr.shape[0]
    lsel = lambda l: (l, 0, 0)
    vec = pl.BlockSpec((None, 1, S5_N), lsel)
    mat = pl.BlockSpec((None, C_GROUP, S5_N), lsel)
    c_in = pl.BlockSpec((None, S5_N, LANES), lsel)
    cmat = pl.BlockSpec((None, S5_N, C_WIDTH), lsel)
    tab = pl.BlockSpec((None, S5_CHUNK, S5_N), lsel)
    sds = lambda r: jax.ShapeDtypeStruct((depth, r, S5_N), F32)
    c_sds = jax.ShapeDtypeStruct((depth, S5_N, C_WIDTH), BF16)
    return pl.pallas_call(
        _s5_prep_kernel,
        out_shape=[jax.ShapeDtypeStruct((depth, C_WIDTH, 2 * S5_N), BF16), c_sds, c_sds,
                   sds(1), sds(1)] + [sds(S5_CHUNK)] * 4,
        grid=(depth,),
        in_specs=[vec, vec, vec, mat, mat, c_in, c_in],
        out_specs=[pl.BlockSpec((None, C_WIDTH, 2 * S5_N), lsel), cmat, cmat, vec, vec, tab, tab, tab, tab],
        compiler_params=_cparams(("arbitrary",)),
        name="s5_discretise",
    )(lr, li, ldt, b_re_t, b_im_t, c_re_t, c_im_t)


def _cmul(ar, ai, br, bi):
    return ar * br - ai * bi, ar * bi + ai * br


def _s5_prompt_kernel(u_ref, bw_ref, cr_ref, ci_ref, d_ref, er_ref, ei_ref, fr_ref, fi_ref,
                      y_ref, hs_ref, car_ref, cai_ref, bu_ref, hr_ref, hi_ref, *, n_chunks):
    step = pl.program_id(1)
    t = S5_CHUNK

    @pl.when(step == 0)
    def _():
        car_ref[...] = jnp.zeros(car_ref.shape, F32)
        cai_ref[...] = jnp.zeros(cai_ref.shape, F32)

    u = u_ref[...]
    ub = u.astype(BF16)
    kt_cols = S5_N // (C_WIDTH // S5_KT)
    for part in range(2):
        for kt in range(C_WIDTH // S5_KT):
            cols = slice(part * S5_N + kt * kt_cols, part * S5_N + (kt + 1) * kt_cols)
            rows_k = slice(kt * S5_KT, (kt + 1) * S5_KT)
            bu_ref[:, cols] = _dot(ub[:, rows_k], bw_ref[rows_k, cols])

    row = lax.broadcasted_iota(jnp.int32, (t, t), 0)
    col = lax.broadcasted_iota(jnp.int32, (t, t), 1)
    tril = jnp.where(col <= row, 1.0, 0.0).astype(BF16)

    groups = [slice(n * MXU_TILE, (n + 1) * MXU_TILE) for n in range(S5_N // MXU_TILE)]
    halves = [slice(0, LANES), slice(LANES, MXU_TILE)]
    carry = [(car_ref[:, g], cai_ref[:, g]) for g in groups]
    for c in range(n_chunks):
        rs = slice(c * t, (c + 1) * t)
        group_cols = [[slice(g.start + hv.start, g.start + hv.stop) for hv in halves] for g in groups]
        sums = []
        for cols in group_cols:
            xs = [_cmul(er_ref[:, cs], ei_ref[:, cs], bu_ref[rs, cs],
                        bu_ref[rs, S5_N + cs.start:S5_N + cs.stop]) for cs in cols]
            sums.append((_dot(tril, jnp.concatenate([x[0] for x in xs], axis=1).astype(BF16)),
                         _dot(tril, jnp.concatenate([x[1] for x in xs], axis=1).astype(BF16))))
        for n, (g, cols) in enumerate(zip(groups, group_cols)):
            sum_re, sum_im = sums[n][0] + carry[n][0], sums[n][1] + carry[n][1]
            hs = [_cmul(fr_ref[:, cs], fi_ref[:, cs], sum_re[:, hv], sum_im[:, hv]) for cs, hv in zip(cols, halves)]
            hr = jnp.concatenate([h[0] for h in hs], axis=1)
            hi = jnp.concatenate([h[1] for h in hs], axis=1)
            hr_ref[rs, g] = hr.astype(BF16)
            hi_ref[rs, g] = hi.astype(BF16)
            carry[n] = (hr[t - 1:t, :], hi[t - 1:t, :])
    for g, (car, cai) in zip(groups, carry):
        car_ref[:, g] = car
        cai_ref[:, g] = cai

    n_out = C_WIDTH // S5_KT
    k_rows = S5_N // n_out
    for j in range(n_out):
        ks = slice(j * k_rows, (j + 1) * k_rows)
        cs = slice(j * S5_KT, (j + 1) * S5_KT)
        y_ref[:, cs] = (_dot(hr_ref[:, ks], cr_ref[ks, cs]) - _dot(hi_ref[:, ks], ci_ref[ks, cs])
                        + d_ref[:, cs] * u[:, cs])

    @pl.when(step == pl.num_programs(1) - 1)
    def _():
        hs_ref[0:1, :] = car_ref[...]
        hs_ref[1:2, :] = cai_ref[...]


def _s5_prompt_call(layer, u, bw, c_re, c_im, d, tabs, n_seq, seq_rows, step_rows):
    steps = seq_rows // step_rows
    row_spec = pl.BlockSpec((step_rows, C_WIDTH), lambda s, j, steps=steps: (s * steps + j, 0))
    lsel = lambda s, j, l=layer: (l, 0, 0)
    tab = pl.BlockSpec((None, S5_CHUNK, S5_N), lsel)
    return pl.pallas_call(
        functools.partial(_s5_prompt_kernel, n_chunks=step_rows // S5_CHUNK),
        out_shape=[jax.ShapeDtypeStruct((n_seq * seq_rows, C_WIDTH), F32),
                   jax.ShapeDtypeStruct((n_seq, 2, S5_N), F32)],
        grid=(n_seq, steps),
        in_specs=[row_spec,
                  _resident((None, C_WIDTH, 2 * S5_N), lsel),
                  _resident((None, S5_N, C_WIDTH), lsel),
                  _resident((None, S5_N, C_WIDTH), lsel),
                  pl.BlockSpec((None, 1, C_WIDTH), lsel),
                  tab, tab, tab, tab],
        out_specs=[row_spec, pl.BlockSpec((None, 2, S5_N), lambda s, j: (s, 0, 0))],
        scratch_shapes=[pltpu.VMEM((1, S5_N), F32), pltpu.VMEM((1, S5_N), F32),
                        pltpu.VMEM((step_rows, 2 * S5_N), F32),
                        pltpu.VMEM((step_rows, S5_N), BF16), pltpu.VMEM((step_rows, S5_N), BF16)],
        compiler_params=_cparams(("arbitrary", "arbitrary")),
        name="s5_chunked_scan",
    )(u, bw, c_re, c_im, d, *tabs)


def _s5_sample_kernel(u_ref, h0r_ref, h0i_ref, ar_ref, ai_ref, bw_ref, cr_ref, ci_ref, d_ref,
                      y_ref, hr_ref, hi_ref, *, n_tok, n_seq):
    hr, hi = h0r_ref[...], h0i_ref[...]
    a_re, a_im = ar_ref[...], ai_ref[...]
    for t in range(n_tok):
        u = u_ref[t * n_seq:(t + 1) * n_seq, :]
        bu = _dot(u.astype(BF16), bw_ref[...])
        hr, hi = _cmul(a_re, a_im, hr, hi)
        hr, hi = hr + bu[:, :S5_N], hi + bu[:, S5_N:]
        y_ref[t * n_seq:(t + 1) * n_seq, :] = (_dot(hr.astype(BF16), cr_ref[...]) - _dot(hi.astype(BF16), ci_ref[...])
                                               + d_ref[...] * u)
    hr_ref[...] = hr
    hi_ref[...] = hi


def _s5_sample_call(layer, u, h0_re, h0_im, a_re, a_im, bw, c_re, c_im, d, n_tok, n_seq):
    lsel = lambda i, l=layer: (l, 0, 0)
    full = lambda shape: pl.BlockSpec(shape, lambda i: (0,) * len(shape))
    return pl.pallas_call(
        functools.partial(_s5_sample_kernel, n_tok=n_tok, n_seq=n_seq),
        out_shape=[jax.ShapeDtypeStruct((n_tok * n_seq, C_WIDTH), F32),
                   jax.ShapeDtypeStruct((n_seq, S5_N), F32),
                   jax.ShapeDtypeStruct((n_seq, S5_N), F32)],
        grid=(1,),
        in_specs=[full((n_tok * n_seq, C_WIDTH)),
                  pl.BlockSpec((None, n_seq, S5_N), lsel),
                  pl.BlockSpec((None, n_seq, S5_N), lsel),
                  pl.BlockSpec((None, 1, S5_N), lsel),
                  pl.BlockSpec((None, 1, S5_N), lsel),
                  pl.BlockSpec((None, C_WIDTH, 2 * S5_N), lsel),
                  pl.BlockSpec((None, S5_N, C_WIDTH), lsel),
                  pl.BlockSpec((None, S5_N, C_WIDTH), lsel),
                  pl.BlockSpec((None, 1, C_WIDTH), lsel)],
        out_specs=[full((n_tok * n_seq, C_WIDTH)), full((n_seq, S5_N)), full((n_seq, S5_N))],
        compiler_params=_cparams(("arbitrary",)),
        name="s5_step_scan",
    )(u, h0_re, h0_im, a_re, a_im, bw, c_re, c_im, d)


W_GATES = A_V + 3 * D_MODEL


def _merge_kernel(x_ref, g_ref, shs, shp, scs, scp, gts, gtp,
                  oap, oas, obp, obs, ocp, ocs,
                  wz_ref, wa_ref, wb_ref, wc_ref, wglu_ref, wo_ref, o_ref, *, rows):
    x = x_ref[...]
    h = _ada_norm(x, g_ref[...], _mod_value(rows, shs, shp), _mod_value(rows, scs, scp)).astype(BF16)
    z = _dot(h, wz_ref[...])
    oa = _select_rows(rows, oap, oas) * _silu(z[:, :A_V])
    br_a = _dot(oa.astype(BF16), wa_ref[...])
    br_b = _dot(_select_rows(rows, obp, obs).astype(BF16), wb_ref[...])
    glu = _dot(_gelu_tanh(_select_rows(rows, ocp, ocs)).astype(BF16), wglu_ref[...])
    oc = glu[:, :C_WIDTH] * _sigmoid(glu[:, C_WIDTH:])
    br_c = _dot(oc.astype(BF16), wc_ref[...])
    o = A_V
    mix = (_sigmoid(z[:, o:o + D_MODEL]) * br_a
           + _sigmoid(z[:, o + D_MODEL:o + 2 * D_MODEL]) * br_b
           + _sigmoid(z[:, o + 2 * D_MODEL:o + 3 * D_MODEL]) * br_c)
    out = _dot(mix.astype(BF16), wo_ref[...])
    o_ref[...] = x + _mod_value(rows, gts, gtp) * out


def _merge_call(rows, layer, x, mod, norm_g, oa, ob, oc, wz, wa, wb, wc, wglu, wo):
    in_specs = [pl.BlockSpec((TOKEN_TILE, D_MODEL), lambda i: (i, 0)),
                pl.BlockSpec((None, 1, D_MODEL), lambda i, l=layer: (l, 0, 0))]
    operands = [x, norm_g]
    for j in (3, 4, 5):
        in_specs += _mod_specs(rows, layer, j)
        operands += [mod, mod]
    for pair, w in ((oa, A_V), (ob, B_Q), (oc, C_WIDTH)):
        in_specs += _pair_specs(rows, w)
        operands += list(pair)
    lsel = lambda i, l=layer: (l, 0, 0)
    in_specs += [_resident((None, D_MODEL, W_GATES), lsel),
                 _resident((None, A_V, D_MODEL), lsel),
                 _resident((None, B_Q, D_MODEL), lsel),
                 _resident((None, C_WIDTH, D_MODEL), lsel),
                 _resident((None, C_WIDTH, 2 * C_WIDTH), lsel),
                 _resident((None, D_MODEL, D_MODEL), lsel)]
    operands += [wz, wa, wb, wc, wglu, wo]
    return pl.pallas_call(
        functools.partial(_merge_kernel, rows=rows),
        out_shape=jax.ShapeDtypeStruct((rows.rows, D_MODEL), F32),
        grid=(rows.n_tiles,),
        in_specs=in_specs,
        out_specs=pl.BlockSpec((TOKEN_TILE, D_MODEL), lambda i: (i, 0)),
        compiler_params=_cparams(("arbitrary",)),
        name="branch_merge",
    )(*operands)


def _rotary_tables(rows):
    half = ROPE_DIM // 2
    inv_freq = ROPE_THETA ** (-jnp.arange(half, dtype=F32) / half)
    pos_p = jnp.arange(rows.l, dtype=jnp.int32)
    pos_s = PAST_LEN + jnp.repeat(jnp.arange(rows.t, dtype=jnp.int32), rows.bs)
    pos = jnp.concatenate([pos_p, pos_s])
    ang = pos.astype(F32)[:, None] * inv_freq[None, :]
    cos, sin = jnp.cos(ang), jnp.sin(ang)
    n = pos.shape[0]
    ones = jnp.ones((n, B_HEAD_DIM - ROPE_DIM), F32)
    zeros = jnp.zeros((n, B_HEAD_DIM - ROPE_DIM), F32)
    z8 = jnp.zeros((n, half), F32)
    per_head = lambda parts: jnp.tile(jnp.concatenate(parts, axis=1), (1, LANES // B_HEAD_DIM))
    return (per_head([cos, cos, ones]), per_head([z8, sin, zeros]), per_head([-sin, z8, zeros]))


def _sample_to_seq(a, rows, pad_to):
    w = a.shape[-1]
    a = a.reshape(rows.t, rows.bs, w).transpose(1, 0, 2)
    a = jnp.pad(a, ((0, 0), (0, pad_to - rows.t), (0, 0)))
    return a.reshape(rows.bs * pad_to, w)


def _seq_to_sample(a, rows, pad_to):
    w = a.shape[-1]
    return a.reshape(rows.bs, pad_to, w)[:, :rows.t].transpose(1, 0, 2).reshape(rows.ns_rows, w)


def kernel(x_prompt, x_sample, state_gla, cache_swa_k, cache_swa_v, state_s5_re, state_s5_im, c_prompt, c_sample, w_ada, b_ada, norm_ffn1, ffn1_w_gate, ffn1_w_up, ffn1_w_down, norm_mix, w_in, gla_w_gate_up, gla_b_gate, gla_norm, swa_sinks, s5_a_re, s5_a_im, s5_b_re, s5_b_im, s5_c_re, s5_c_im, s5_d, s5_log_dt, s5_w_glu, w_branch_a, w_branch_b, w_branch_c, w_out, norm_ffn2, ffn2_w_gate, ffn2_w_up, ffn2_w_down, norm_final):
    depth = w_ada.shape[0]
    b, l, _ = x_prompt.shape
    bs, t, _ = x_sample.shape
    rows = _Rows(b, l, bs, t)
    np_rows = rows.np_rows

    x = (x_prompt.reshape(np_rows, D_MODEL), x_sample.transpose(1, 0, 2).reshape(rows.ns_rows, D_MODEL))
    c_all = jnp.concatenate([c_sample, c_prompt, jnp.zeros((8 - b, D_MODEL), F32)], axis=0)
    mod = _mod_call(c_all, w_ada, b_ada)

    pts = [0, A_QK, 2 * A_QK, 2 * A_QK + A_V]
    o_rank = pts[-1]
    o_ar = o_rank + A_GATE_RANK
    o_bq = o_ar + A_V
    o_cu_end = o_bq + B_Q + 2 * B_KV + C_WIDTH
    w_main = jnp.concatenate([w_in[:, :, :o_rank], w_in[:, :, o_bq:o_cu_end]], axis=-1).astype(BF16)
    w_rank = jnp.pad(w_in[:, :, o_rank:o_ar], ((0, 0), (0, 0), (0, LANES - A_GATE_RANK))).astype(BF16)
    w_up = jnp.pad(gla_w_gate_up, ((0, 0), (0, LANES - A_GATE_RANK), (0, 0))).astype(BF16)
    w_gates = jnp.concatenate([w_in[:, :, o_ar:o_bq], w_in[:, :, o_cu_end:]], axis=-1).astype(BF16)
    cast = lambda w: w.astype(BF16)
    f1g, f1u, f1d = cast(ffn1_w_gate), cast(ffn1_w_up), cast(ffn1_w_down)
    f2g, f2u, f2d = cast(ffn2_w_gate), cast(ffn2_w_up), cast(ffn2_w_down)
    wa, wb, wc, wglu, wo = cast(w_branch_a), cast(w_branch_b), cast(w_branch_c), cast(s5_w_glu), cast(w_out)
    vec3 = lambda a: a.reshape(depth, 1, a.shape[-1])
    n1, nm, n2 = vec3(norm_ffn1), vec3(norm_mix), vec3(norm_ffn2)
    b_gate, g_norm, s5d = vec3(gla_b_gate), vec3(gla_norm), vec3(s5_d)
    rot = _rotary_tables(rows)

    flat = lambda a: a.reshape(depth, 1, S5_N)
    ldt = flat(jnp.broadcast_to(s5_log_dt[:, :, None], (depth, C_GROUPS, C_STATE)))
    to_cols = lambda bm: bm.transpose(0, 3, 1, 2).reshape(depth, C_GROUP, S5_N)
    to_rows = lambda cm: jnp.pad(cm.transpose(0, 1, 3, 2).reshape(depth, S5_N, C_GROUP),
                                 ((0, 0), (0, 0), (0, LANES - C_GROUP)))
    s5_bw, s5_cr, s5_ci, a_re, a_im, e_re, e_im, f_re, f_im = _s5_prep_call(
        flat(s5_a_re), flat(s5_a_im), ldt, to_cols(s5_b_re), to_cols(s5_b_im), to_rows(s5_c_re), to_rows(s5_c_im))
    tabs = (e_re, e_im, f_re, f_im)
    h0_re = state_s5_re.reshape(depth, bs, S5_N)
    h0_im = state_s5_im.reshape(depth, bs, S5_N)
    cache_kt = cache_swa_k.transpose(0, 1, 3, 4, 2)
    cache_vt = cache_swa_v.transpose(0, 1, 3, 4, 2)

    outs = {k: [] for k in ("gla_p", "gla_s", "kp", "ks", "vp", "vs", "rp", "rs", "ip", "is")}
    pad8 = SWA_SAMPLE_ROWS
    for layer in range(depth):
        x = _ffn_call(rows, layer, 0, x, mod, n1, f1g, f1u, f1d)
        gq, gk, gv, la, sq, sk, sv, su = _inproj_call(rows, layer, x, mod, nm, w_main, w_rank, w_up, b_gate, rot)

        oa_p, st_p = _gla_call(layer, gq, gk, gv, la, g_norm, None, b, l, TOKEN_TILE, GLA_CHUNK, GLA_BLOCK)
        sseq = lambda a: _sample_to_seq(a[np_rows:], rows, pad8)
        oa_s, st_s = _gla_call(layer, sseq(gq), sseq(gk), sseq(gv), sseq(la), g_norm, state_gla,
                               bs, pad8, pad8, pad8, pad8, seqs_per_step=GLA_SAMPLE_SEQS)
        oa_s = _seq_to_sample(oa_s, rows, pad8)

        ob_p = _swa_call(layer, swa_sinks, sq, sk, sv, b, l // WINDOW)
        q_s = sq[np_rows:].reshape(t, bs, B_KV_HEADS, B_GROUP, B_HEAD_DIM).transpose(1, 2, 3, 0, 4)
        q_s = jnp.pad(q_s, ((0, 0), (0, 0), (0, 0), (0, pad8 - t), (0, 0)))
        q_s = q_s.reshape(bs, B_KV_HEADS, B_GROUP * pad8, B_HEAD_DIM)
        ob_s, kt_new, vt_new = _swa_sample_call(layer, swa_sinks, q_s, sseq(sk), sseq(sv), cache_kt, cache_vt, bs, t)
        ob_s = ob_s.reshape(bs, B_KV_HEADS, B_GROUP, pad8, B_HEAD_DIM)[:, :, :, :t]
        ob_s = ob_s.transpose(3, 0, 1, 2, 4).reshape(rows.ns_rows, B_Q)

        oc_p, hs_p = _s5_prompt_call(layer, su, s5_bw, s5_cr, s5_ci, s5d, tabs, b, l, TOKEN_TILE)
        oc_s, hr_s, hi_s = _s5_sample_call(layer, su[np_rows:], h0_re, h0_im, a_re, a_im,
                                           s5_bw, s5_cr, s5_ci, s5d, t, bs)

        x = _merge_call(rows, layer, x, mod, nm, (oa_p, oa_s), (ob_p, ob_s), (oc_p, oc_s),
                        w_gates, wa, wb, wc, wglu, wo)
        x = _ffn_call(rows, layer, 6, x, mod, n2, f2g, f2u, f2d,
                      final_g=norm_final.reshape(1, D_MODEL) if layer == depth - 1 else None)

        outs["gla_p"].append(st_p)
        outs["gla_s"].append(st_s)
        kv_shape = (b, WINDOW, B_KV_HEADS, B_HEAD_DIM)
        window_rows = lambda a: jnp.stack([a[(i + 1) * l - WINDOW:(i + 1) * l] for i in range(b)])
        outs["kp"].append(window_rows(sk).reshape(kv_shape))
        outs["vp"].append(window_rows(sv).reshape(kv_shape))
        outs["ks"].append(kt_new.transpose(0, 3, 1, 2))
        outs["vs"].append(vt_new.transpose(0, 3, 1, 2))
        outs["rp"].append(hs_p[:, 0].reshape(b, C_GROUPS, C_STATE))
        outs["ip"].append(hs_p[:, 1].reshape(b, C_GROUPS, C_STATE))
        outs["rs"].append(hr_s.reshape(bs, C_GROUPS, C_STATE))
        outs["is"].append(hi_s.reshape(bs, C_GROUPS, C_STATE))

    y_p = x[0].reshape(b, l, D_MODEL)
    y_s = x[1].reshape(t, bs, D_MODEL).transpose(1, 0, 2)
    st = lambda k: jnp.stack(outs[k])
    return (y_p, y_s, st("gla_p"), st("gla_s"), st("kp"), st("ks"), st("vp"), st("vs"),
            st("rp"), st("rs"), st("ip"), st("is"))
```

```python
import functools
import math

import jax
import jax.numpy as jnp
from jax import lax
from jax.experimental import pallas as pl
from jax.experimental.pallas import tpu as pltpu

D_MODEL = 1024
D_FF = 2816
PAST_LEN = 8192
A_HEADS, A_DK, A_DV = 4, 64, 128
A_GATE_RANK = 16
A_GATE_NORM = 16.0
B_HEADS, B_KV_HEADS, B_HEAD_DIM = 8, 2, 64
B_GROUP = B_HEADS // B_KV_HEADS
WINDOW = 128
ROPE_DIM = B_HEAD_DIM // 4
ROPE_THETA = 500000.0
C_WIDTH, C_GROUP, C_STATE = 512, 16, 64
C_GROUPS = C_WIDTH // C_GROUP
MACARON_W = 0.5
NORM_EPS = 1e-6
N_MOD = 9

A_QK = A_HEADS * A_DK
A_V = A_HEADS * A_DV
B_Q = B_HEADS * B_HEAD_DIM
B_KV = B_KV_HEADS * B_HEAD_DIM
S5_N = C_GROUPS * C_STATE

LANES = 128
TOKEN_TILE = 512
GLA_CHUNK = 64
GLA_BLOCK = 32
GLA_SAMPLE_SEQS = 8
S5_CHUNK = 64
S5_KT = 256
VMEM_LIMIT = 56 * 1024 * 1024

F32 = jnp.float32
BF16 = jnp.bfloat16
NEG_BIG = -1e30


def _cparams(sem):
    return pltpu.CompilerParams(dimension_semantics=sem, vmem_limit_bytes=VMEM_LIMIT)


def _dot(a, b):
    return jnp.dot(a, b, preferred_element_type=F32)


def _dot_nt(a, b):
    return lax.dot_general(a, b, (((1,), (1,)), ((), ())), preferred_element_type=F32)


def _sigmoid(x):
    return 1.0 / (1.0 + jnp.exp(-x))


def _silu(x):
    return x * _sigmoid(x)


def _gelu_tanh(x):
    return 0.5 * x * (1.0 + jnp.tanh(math.sqrt(2.0 / math.pi) * (x + 0.044715 * (x * x * x))))


def _log_sigmoid(x):
    return jnp.minimum(x, 0.0) - jnp.log1p(jnp.exp(-jnp.abs(x)))


def _resident(shape, index_map):
    return pl.BlockSpec(shape, index_map, pipeline_mode=pl.Buffered(1))


def _mod_kernel(c_ref, w_ref, b_ref, o_ref):
    c = c_ref[...]
    s = _silu(c).astype(BF16)
    o_ref[...] = _dot(s, w_ref[...].astype(BF16)) + b_ref[...]


MOD_BLOCK = 3 * D_MODEL


def _mod_call(c_all, w_ada, b_ada):
    depth = w_ada.shape[0]
    rows = c_all.shape[0]
    return pl.pallas_call(
        _mod_kernel,
        out_shape=jax.ShapeDtypeStruct((depth, rows, N_MOD * D_MODEL), F32),
        grid=(depth, N_MOD * D_MODEL // MOD_BLOCK),
        in_specs=[
            pl.BlockSpec((rows, D_MODEL), lambda l, j: (0, 0)),
            pl.BlockSpec((None, D_MODEL, MOD_BLOCK), lambda l, j: (l, 0, j)),
            pl.BlockSpec((None, 1, MOD_BLOCK), lambda l, j: (l, 0, j)),
        ],
        out_specs=pl.BlockSpec((None, rows, MOD_BLOCK), lambda l, j: (l, 0, j)),
        compiler_params=_cparams(("arbitrary", "arbitrary")),
        name="ada_mod",
    )(c_all, w_ada, b_ada.reshape(depth, 1, N_MOD * D_MODEL))


class _Rows:
    def __init__(self, n_prompt_seq, seq_len, n_sample_seq, n_sample_tok):
        self.b = n_prompt_seq
        self.l = seq_len
        self.bs = n_sample_seq
        self.t = n_sample_tok
        self.np_rows = n_prompt_seq * seq_len
        self.ns_rows = n_sample_seq * n_sample_tok
        assert seq_len % TOKEN_TILE == 0 and self.ns_rows == TOKEN_TILE
        assert n_sample_seq % 8 == 0 and seq_len % WINDOW == 0
        self.rows = self.np_rows + self.ns_rows
        self.n_prompt_tiles = self.np_rows // TOKEN_TILE
        self.tiles_per_seq = seq_len // TOKEN_TILE
        self.n_tiles = self.n_prompt_tiles + 1
        self.mod_rows = n_sample_seq + 8
        assert n_prompt_seq <= 8


def _mod_specs(rows, layer, j):
    return [
        pl.BlockSpec((None, rows.bs, D_MODEL), lambda i, l=layer, j=j: (l, 0, j)),
        pl.BlockSpec((None, 8, D_MODEL), lambda i, l=layer, j=j, r=rows.bs // 8: (l, r, j)),
    ]


def _mod_value(rows, ms_ref, mp_ref):
    i = pl.program_id(0)
    is_sample = i == rows.n_prompt_tiles
    seq = jnp.minimum(i // rows.tiles_per_seq, rows.b - 1)
    s = ms_ref[...]
    s = jnp.concatenate([s] * rows.t, axis=0)
    p = mp_ref[pl.ds(seq, 1), :]
    return jnp.where(is_sample, s, p)


def _ada_norm(x, g, shift, scale):
    var = jnp.mean(x * x, axis=-1, keepdims=True)
    return (x * lax.rsqrt(var + NORM_EPS)) * g * (1.0 + scale) + shift


def _select_rows(rows, p_ref, s_ref):
    is_sample = pl.program_id(0) == rows.n_prompt_tiles
    return jnp.where(is_sample, s_ref[...], p_ref[...])


def _pair_specs(rows, width):
    last = rows.n_prompt_tiles - 1
    return [
        pl.BlockSpec((TOKEN_TILE, width), lambda i, last=last: (jnp.minimum(i, last), 0)),
        pl.BlockSpec((TOKEN_TILE, width), lambda i: (0, 0)),
    ]


MXU_TILE = 256
FF_SPLITS = (0, 6 * MXU_TILE, D_FF)
assert D_FF % MXU_TILE == 0


def _ffn_kernel(*refs, rows, x_pair, final_norm):
    n_in = 11 + int(x_pair) + int(final_norm)
    ins, outs = refs[:n_in], refs[n_in:]
    if x_pair:
        x = _select_rows(rows, ins[0], ins[1])
    else:
        x = ins[0][...]
    g_ref, shs, shp, scs, scp, gts, gtp, wg_ref, wu_ref, wd_ref = ins[1 + int(x_pair):11 + int(x_pair)]
    h = _ada_norm(x, g_ref[...], _mod_value(rows, shs, shp), _mod_value(rows, scs, scp)).astype(BF16)
    chunks = [slice(lo, hi) for lo, hi in zip(FF_SPLITS[:-1], FF_SPLITS[1:])]
    gate_up = [(_dot(h, wg_ref[:, sl]), _dot(h, wu_ref[:, sl])) for sl in chunks]
    act = [(_silu(gate) * up).astype(BF16) for gate, up in gate_up]
    acc = _dot(act[0], wd_ref[chunks[0], :])
    for a, sl in zip(act[1:], chunks[1:]):
        acc = acc + _dot(a, wd_ref[sl, :])
    out = x + (MACARON_W * _mod_value(rows, gts, gtp)) * acc
    if not final_norm:
        outs[0][...] = out
        return
    var = jnp.mean(out * out, axis=-1, keepdims=True)
    y = out * lax.rsqrt(var + NORM_EPS) * ins[-1][...]
    is_sample = pl.program_id(0) == rows.n_prompt_tiles

    @pl.when(jnp.logical_not(is_sample))
    def _():
        outs[0][...] = y

    @pl.when(is_sample)
    def _():
        outs[1][...] = y


def _ffn_call(rows, layer, mod_base, x, mod, norm_g, wg, wu, wd, final_g=None):
    x_pair = isinstance(x, tuple)
    if x_pair:
        in_specs = _pair_specs(rows, D_MODEL)
        operands = list(x)
    else:
        in_specs = [pl.BlockSpec((TOKEN_TILE, D_MODEL), lambda i: (i, 0))]
        operands = [x]
    in_specs.append(pl.BlockSpec((None, 1, D_MODEL), lambda i, l=layer: (l, 0, 0)))
    operands.append(norm_g)
    for j in range(3):
        in_specs += _mod_specs(rows, layer, mod_base + j)
        operands += [mod, mod]
    in_specs += [
        _resident((None, D_MODEL, D_FF), lambda i, l=layer: (l, 0, 0)),
        _resident((None, D_MODEL, D_FF), lambda i, l=layer: (l, 0, 0)),
        _resident((None, D_FF, D_MODEL), lambda i, l=layer: (l, 0, 0)),
    ]
    operands += [wg, wu, wd]
    if final_g is None:
        out_shape = jax.ShapeDtypeStruct((rows.rows, D_MODEL), F32)
        out_specs = pl.BlockSpec((TOKEN_TILE, D_MODEL), lambda i: (i, 0))
    else:
        in_specs.append(pl.BlockSpec((1, D_MODEL), lambda i: (0, 0)))
        operands.append(final_g)
        out_shape = [jax.ShapeDtypeStruct((rows.np_rows, D_MODEL), F32),
                     jax.ShapeDtypeStruct((rows.ns_rows, D_MODEL), F32)]
        out_specs = _pair_specs(rows, D_MODEL)
    return pl.pallas_call(
        functools.partial(_ffn_kernel, rows=rows, x_pair=x_pair, final_norm=final_g is not None),
        out_shape=out_shape,
        grid=(rows.n_tiles,),
        in_specs=in_specs,
        out_specs=out_specs,
        compiler_params=_cparams(("arbitrary",)),
        name="ada_swiglu",
    )(*operands)


W_MAIN = 2 * A_QK + A_V + B_Q + 2 * B_KV + C_WIDTH


def _rotary(x, cos, s_up, s_dn):
    half = ROPE_DIM // 2
    return x * cos + pltpu.roll(x, half, axis=1) * s_up + pltpu.roll(x, LANES - half, axis=1) * s_dn


def _inproj_kernel(x_ref, g_ref, shs, shp, scs, scp, wm_ref, wr_ref, wup_ref, bg_ref,
                   cos_ref, sup_ref, sdn_ref,
                   gq_ref, gk_ref, gv_ref, la_ref, sq_ref, sk_ref, sv_ref, su_ref, *, rows):
    x = x_ref[...]
    h = _ada_norm(x, g_ref[...], _mod_value(rows, shs, shp), _mod_value(rows, scs, scp)).astype(BF16)
    z = _dot_nt(h, wm_ref[...])
    o = 0
    gq_ref[...] = z[:, o:o + A_QK] * (A_DK ** -0.5)
    o += A_QK
    gk_ref[...] = z[:, o:o + A_QK]
    o += A_QK
    gv_ref[...] = z[:, o:o + A_V]
    o += A_V
    cos, s_up, s_dn = cos_ref[...], sup_ref[...], sdn_ref[...]
    for c in range(B_Q // LANES):
        sq_ref[:, c * LANES:(c + 1) * LANES] = _rotary(z[:, o + c * LANES:o + (c + 1) * LANES], cos, s_up, s_dn)
    o += B_Q
    sk_ref[...] = _rotary(z[:, o:o + B_KV], cos, s_up, s_dn)
    o += B_KV
    sv_ref[...] = z[:, o:o + B_KV]
    o += B_KV
    su_ref[...] = z[:, o:o + C_WIDTH]
    low = _dot_nt(h, wr_ref[...]).astype(BF16)
    pre = _dot(low, wup_ref[...]) + bg_ref[...]
    la_ref[...] = _log_sigmoid(pre) * (1.0 / A_GATE_NORM)


def _inproj_call(rows, layer, x, mod, norm_g, w_main, w_rank, w_up, b_gate, rot):
    row_spec = lambda w: pl.BlockSpec((TOKEN_TILE, w), lambda i: (i, 0))
    rot_spec = pl.BlockSpec(
        (TOKEN_TILE, LANES),
        lambda i, n=rows.n_prompt_tiles, per=rows.tiles_per_seq: (jnp.where(i < n, i % per, per), 0))
    in_specs = [row_spec(D_MODEL), pl.BlockSpec((None, 1, D_MODEL), lambda i, l=layer: (l, 0, 0))]
    operands = [x, norm_g]
    for j in (3, 4):
        in_specs += _mod_specs(rows, layer, j)
        operands += [mod, mod]
    in_specs += [
        _resident((None, W_MAIN, D_MODEL), lambda i, l=layer: (l, 0, 0)),
        _resident((None, LANES, D_MODEL), lambda i, l=layer: (l, 0, 0)),
        _resident((None, LANES, A_QK), lambda i, l=layer: (l, 0, 0)),
        pl.BlockSpec((None, 1, A_QK), lambda i, l=layer: (l, 0, 0)),
        rot_spec, rot_spec, rot_spec,
    ]
    operands += [w_main, w_rank, w_up, b_gate, *rot]
    widths = (A_QK, A_QK, A_V, A_QK, B_Q, B_KV, B_KV, C_WIDTH)
    return pl.pallas_call(
        functools.partial(_inproj_kernel, rows=rows),
        out_shape=[jax.ShapeDtypeStruct((rows.rows, w), F32) for w in widths],
        grid=(rows.n_tiles,),
        in_specs=in_specs,
        out_specs=[row_spec(w) for w in widths],
        compiler_params=_cparams(("arbitrary",)),
        name="mixer_in_proj",
    )(*operands)


def _cumsum_rows(x):
    n = x.shape[0]
    row = lax.broadcasted_iota(jnp.int32, x.shape, 0)
    s = 1
    while s < n:
        x = x + jnp.where(row >= s, pltpu.roll(x, s, axis=0), 0.0)
        s *= 2
    return x


def _head_blockdiag(x, lanes_per_head):
    lane_head = lax.broadcasted_iota(jnp.int32, x.shape, 1) // lanes_per_head
    return jnp.concatenate([jnp.where(lane_head == h, x, 0.0) for h in range(A_HEADS)], axis=0)


def _gla_kernel(*refs, chunk, block, n_chunks, n_seqs, single_step, has_state):
    if has_state:
        q_ref, k_ref, v_ref, la_ref, gn_ref, s0_ref, o_ref, so_ref, st_ref = refs
    else:
        q_ref, k_ref, v_ref, la_ref, gn_ref, o_ref, so_ref, st_ref = refs
    n_blocks = chunk // block
    st_shape = (A_V, A_QK)
    st_row_head = lax.broadcasted_iota(jnp.int32, st_shape, 0) // A_DV
    st_lane_head = lax.broadcasted_iota(jnp.int32, st_shape, 1) // A_DK
    st_mask = st_row_head == st_lane_head
    srow = lax.broadcasted_iota(jnp.int32, (block, A_HEADS * block), 0)
    scol = lax.broadcasted_iota(jnp.int32, (block, A_HEADS * block), 1) % block
    causal = scol <= srow
    gn = gn_ref[...]

    def init_state(sq):
        st_ref[sq] = jnp.zeros(st_shape, F32)
        if has_state:
            for h in range(A_HEADS):
                st_ref[sq, h * A_DV:(h + 1) * A_DV, h * A_DK:(h + 1) * A_DK] = s0_ref[sq, h].T

    def emit_state(sq):
        for h in range(A_HEADS):
            so_ref[sq, h] = st_ref[sq, h * A_DV:(h + 1) * A_DV, h * A_DK:(h + 1) * A_DK].T

    if single_step:
        for sq in range(n_seqs):
            init_state(sq)
    else:
        pl.when(pl.program_id(1) == 0)(lambda: init_state(0))

    units = [(sq, c) for sq in range(n_seqs) for c in range(n_chunks)]
    rows_of = {u: slice((u[0] * n_chunks + u[1]) * chunk, (u[0] * n_chunks + u[1] + 1) * chunk) for u in units}

    q, k, v, b = {}, {}, {}, {}
    for u in units:
        q[u], k[u], v[u] = q_ref[rows_of[u], :], k_ref[rows_of[u], :], v_ref[rows_of[u], :]
        b[u] = _cumsum_rows(la_ref[rows_of[u], :])

    s_diag, s_off, upd, q_in = {}, {}, {}, {}
    for u in units:
        for i in range(n_blocks):
            lo, hi = i * block, (i + 1) * block
            qi, ki, bi = q[u][lo:hi], k[u][lo:hi], b[u][lo:hi]
            mid = b[u][lo + block // 2 - 1:lo + block // 2, :]
            qd = (qi * jnp.exp(bi - mid)).astype(BF16)
            kd = _head_blockdiag(ki * jnp.exp(mid - bi), A_DK).astype(BF16)
            s_diag[u, i] = _dot_nt(qd, kd)
            if i > 0:
                ref = b[u][lo - 1:lo, :]
                qo = (qi * jnp.exp(bi - ref)).astype(BF16)
                ko = _head_blockdiag(k[u][:lo] * jnp.exp(ref - b[u][:lo]), A_DK).astype(BF16)
                s_off[u, i] = _dot_nt(qo, ko)
        b_last = b[u][chunk - 1:chunk, :]
        upd[u] = _dot(v[u].T.astype(BF16), (k[u] * jnp.exp(b_last - b[u])).astype(BF16))
        q_in[u] = (q[u] * jnp.exp(b[u])).astype(BF16)

    intra = {}
    for u in units:
        outs = []
        for i in range(n_blocks):
            lo, hi = i * block, (i + 1) * block
            att = jnp.where(causal, s_diag[u, i], 0.0).astype(BF16)
            oi = _dot(att, _head_blockdiag(v[u][lo:hi], A_DV).astype(BF16))
            if i > 0:
                oi = oi + _dot(s_off[u, i].astype(BF16), _head_blockdiag(v[u][:lo], A_DV).astype(BF16))
            outs.append(oi)
        intra[u] = jnp.concatenate(outs, axis=0) if n_blocks > 1 else outs[0]

    for u in units:
        sq = u[0]
        st = st_ref[sq]
        o = _dot_nt(q_in[u], st.astype(BF16)) + intra[u]
        st_ref[sq] = st * jnp.exp(b[u][chunk - 1:chunk, :]) + jnp.where(st_mask, upd[u], 0.0)
        normed = []
        for h in range(A_HEADS):
            oh = o[:, h * A_DV:(h + 1) * A_DV]
            var = jnp.mean(oh * oh, axis=-1, keepdims=True)
            normed.append(oh * lax.rsqrt(var + NORM_EPS) * gn)
        o_ref[rows_of[u], :] = jnp.concatenate(normed, axis=1)

    if single_step:
        for sq in range(n_seqs):
            emit_state(sq)
    else:
        pl.when(pl.program_id(1) == pl.num_programs(1) - 1)(lambda: emit_state(0))


def _gla_call(layer, q, k, v, la, gnorm, s0, n_seq, seq_rows, step_rows, chunk, block, seqs_per_step=1):
    steps = seq_rows // step_rows
    sps = seqs_per_step
    assert sps == 1 or steps == 1
    blk_rows = sps * step_rows
    row_spec = lambda w: pl.BlockSpec((blk_rows, w), lambda s, j, steps=steps: (s * steps + j, 0))
    in_specs = [row_spec(A_QK), row_spec(A_QK), row_spec(A_V), row_spec(A_QK),
                pl.BlockSpec((None, 1, A_DV), lambda s, j, l=layer: (l, 0, 0))]
    operands = [q, k, v, la, gnorm]
    if s0 is not None:
        in_specs.append(pl.BlockSpec((None, sps, A_HEADS, A_DK, A_DV), lambda s, j, l=layer: (l, s, 0, 0, 0)))
        operands.append(s0)
    return pl.pallas_call(
        functools.partial(_gla_kernel, chunk=chunk, block=block, n_chunks=step_rows // chunk,
                          n_seqs=sps, single_step=steps == 1, has_state=s0 is not None),
        out_shape=[jax.ShapeDtypeStruct((n_seq * seq_rows, A_V), F32),
                   jax.ShapeDtypeStruct((n_seq, A_HEADS, A_DK, A_DV), F32)],
        grid=(n_seq // sps, steps),
        in_specs=in_specs,
        out_specs=[row_spec(A_V),
                   pl.BlockSpec((sps, A_HEADS, A_DK, A_DV), lambda s, j: (s, 0, 0, 0))],
        scratch_shapes=[pltpu.VMEM((sps, A_V, A_QK), F32)],
        compiler_params=_cparams(("arbitrary", "arbitrary")),
        name="gla_chunked",
    )(*operands)


SWA_STEP_BLOCKS = 4


def _swa_kernel(sink_ref, q_ref, kc_ref, vc_ref, kp_ref, vp_ref, o_ref, *, layer):
    tk = 2 * WINDOW
    kj = lax.broadcasted_iota(jnp.int32, (tk, WINDOW), 0)
    qi = lax.broadcasted_iota(jnp.int32, (tk, WINDOW), 1) + WINDOW
    band = (kj <= qi) & (kj >= qi - WINDOW)
    zeros = jnp.zeros((B_HEAD_DIM, WINDOW), F32)
    chains = [(blk, h) for blk in range(SWA_STEP_BLOCKS) for h in range(B_KV_HEADS)]

    keys, vals_t, q_t, valid = [], [], [], []
    for blk in range(SWA_STEP_BLOCKS):
        cur = slice(blk * WINDOW, (blk + 1) * WINDOW)
        if blk == 0:
            k_prev, v_prev = kp_ref[...], vp_ref[...]
            valid.append(band & (kj >= jnp.where(pl.program_id(1) > 0, 0, WINDOW)))
        else:
            prev = slice((blk - 1) * WINDOW, blk * WINDOW)
            k_prev, v_prev = kc_ref[prev, :], vc_ref[prev, :]
            valid.append(band)
        keys.append(jnp.concatenate([k_prev, kc_ref[cur, :]], axis=0).astype(BF16))
        vals_t.append(jnp.concatenate([v_prev, vc_ref[cur, :]], axis=0).T.astype(BF16))
        q_t.append(q_ref[cur, :].T)

    scores = {}
    for blk, h in chains:
        cols = []
        for g in range(B_GROUP):
            hd = h * B_GROUP + g
            qg = q_t[blk][hd * B_HEAD_DIM:(hd + 1) * B_HEAD_DIM]
            cols.append(jnp.concatenate([qg, zeros] if h == 0 else [zeros, qg], axis=0))
        scores[blk, h] = _dot(keys[blk], jnp.concatenate(cols, axis=1).astype(BF16))

    probs, inv = {}, {}
    for blk, h in chains:
        ps = []
        for g in range(B_GROUP):
            sink = sink_ref[layer, h * B_GROUP + g]
            sg = scores[blk, h][:, g * WINDOW:(g + 1) * WINDOW] * (B_HEAD_DIM ** -0.5)
            sg = jnp.where(valid[blk], sg, NEG_BIG)
            m = jnp.maximum(jnp.max(sg, axis=0, keepdims=True), sink)
            p = jnp.exp(sg - m)
            inv[blk, h, g] = 1.0 / (jnp.sum(p, axis=0, keepdims=True) + jnp.exp(sink - m))
            ps.append(p.astype(BF16))
        probs[blk, h] = jnp.concatenate(ps, axis=1)

    out_t = {}
    for blk, h in chains:
        out_t[blk, h] = _dot(vals_t[blk][h * B_HEAD_DIM:(h + 1) * B_HEAD_DIM], probs[blk, h])

    for blk in range(SWA_STEP_BLOCKS):
        rows = [out_t[blk, h][:, g * WINDOW:(g + 1) * WINDOW] * inv[blk, h, g]
                for h in range(B_KV_HEADS) for g in range(B_GROUP)]
        o_ref[blk * WINDOW:(blk + 1) * WINDOW, :] = jnp.concatenate(rows, axis=0).T


def _swa_call(layer, sinks, q, k, v, n_seq, blocks_per_seq):
    sb = SWA_STEP_BLOCKS
    assert blocks_per_seq % sb == 0
    steps = blocks_per_seq // sb
    cur = lambda w: pl.BlockSpec((sb * WINDOW, w), lambda s, n, steps=steps: (s * steps + n, 0))
    prev = pl.BlockSpec((WINDOW, B_KV),
                        lambda s, n, nb=blocks_per_seq, sb=sb: (s * nb + jnp.maximum(n * sb - 1, 0), 0))
    return pl.pallas_call(
        functools.partial(_swa_kernel, layer=layer),
        out_shape=jax.ShapeDtypeStruct((n_seq * blocks_per_seq * WINDOW, B_Q), F32),
        grid=(n_seq, steps),
        in_specs=[pl.BlockSpec(memory_space=pltpu.SMEM), cur(B_Q), cur(B_KV), cur(B_KV), prev, prev],
        out_specs=cur(B_Q),
        compiler_params=_cparams(("arbitrary", "arbitrary")),
        name="swa_banded",
    )(sinks, q, k, v, k, v)


SWA_SAMPLE_SEQS = 8
SWA_SAMPLE_ROWS = 8


def _swa_sample_kernel(sink_ref, q_ref, kn_ref, vn_ref, kt_ref, vt_ref, o_ref, nk_ref, nv_ref, *, layer, n_tok):
    pr = SWA_SAMPLE_ROWS
    keep = WINDOW - n_tok
    lane = lax.broadcasted_iota(jnp.int32, (B_HEAD_DIM, WINDOW), 1)
    rows = B_GROUP * pr
    tok = lax.broadcasted_iota(jnp.int32, (rows, WINDOW), 0) % pr
    grp = lax.broadcasted_iota(jnp.int32, (rows, 1), 0) // pr
    pos = lax.broadcasted_iota(jnp.int32, (rows, WINDOW), 1)
    valid_new = pos - keep <= tok
    valid_old = (pos >= tok) & (pos < n_tok)

    def shifted_in(old, new_rows):
        padded = jnp.concatenate([jnp.zeros((WINDOW - pr, B_KV), F32), new_rows], axis=0)
        cols = pltpu.roll(padded, n_tok, axis=0).T
        return [jnp.where(lane < keep, pltpu.roll(old[h], keep, axis=1),
                          cols[h * B_HEAD_DIM:(h + 1) * B_HEAD_DIM]) for h in range(B_KV_HEADS)]

    chains = [(i, h) for i in range(SWA_SAMPLE_SEQS) for h in range(B_KV_HEADS)]
    k_old, v_old, k_new, v_new = {}, {}, {}, {}
    for i in range(SWA_SAMPLE_SEQS):
        rs = slice(i * pr, (i + 1) * pr)
        ko = [kt_ref[i, h] for h in range(B_KV_HEADS)]
        vo = [vt_ref[i, h] for h in range(B_KV_HEADS)]
        kn, vn = shifted_in(ko, kn_ref[rs, :]), shifted_in(vo, vn_ref[rs, :])
        for h in range(B_KV_HEADS):
            nk_ref[i, h] = kn[h]
            nv_ref[i, h] = vn[h]
            k_old[i, h], v_old[i, h] = ko[h].astype(BF16), vo[h].astype(BF16)
            k_new[i, h], v_new[i, h] = kn[h].astype(BF16), vn[h].astype(BF16)

    s_new, s_old = {}, {}
    for c in chains:
        q = q_ref[c[0], c[1]].astype(BF16)
        s_new[c], s_old[c] = _dot(q, k_new[c]), _dot(q, k_old[c])

    p_new, p_old, inv = {}, {}, {}
    for c in chains:
        sn = jnp.where(valid_new, s_new[c] * (B_HEAD_DIM ** -0.5), NEG_BIG)
        so = jnp.where(valid_old, s_old[c] * (B_HEAD_DIM ** -0.5), NEG_BIG)
        sink = jnp.zeros((rows, 1), F32)
        for g in range(B_GROUP):
            sink = jnp.where(grp == g, sink_ref[layer, c[1] * B_GROUP + g], sink)
        m = jnp.maximum(jnp.maximum(jnp.max(sn, axis=-1, keepdims=True),
                                    jnp.max(so, axis=-1, keepdims=True)), sink)
        pn, po = jnp.exp(sn - m), jnp.exp(so - m)
        inv[c] = 1.0 / (jnp.sum(pn, axis=-1, keepdims=True) + jnp.sum(po, axis=-1, keepdims=True)
                        + jnp.exp(sink - m))
        p_new[c], p_old[c] = pn.astype(BF16), po.astype(BF16)

    for c in chains:
        o = _dot_nt(p_new[c], v_new[c]) + _dot_nt(p_old[c], v_old[c])
        o_ref[c[0], c[1]] = o * inv[c]


def _swa_sample_call(layer, sinks, q, k_new, v_new, kt, vt, n_seq, n_tok):
    sb, pr = SWA_SAMPLE_SEQS, SWA_SAMPLE_ROWS
    qspec = pl.BlockSpec((sb, B_KV_HEADS, B_GROUP * pr, B_HEAD_DIM), lambda s: (s, 0, 0, 0))
    new_rows = pl.BlockSpec((sb * pr, B_KV), lambda s: (s, 0))
    buf_in = pl.BlockSpec((None, sb, B_KV_HEADS, B_HEAD_DIM, WINDOW), lambda s, l=layer: (l, s, 0, 0, 0))
    buf_out = pl.BlockSpec((sb, B_KV_HEADS, B_HEAD_DIM, WINDOW), lambda s: (s, 0, 0, 0))
    buf_sds = jax.ShapeDtypeStruct((n_seq, B_KV_HEADS, B_HEAD_DIM, WINDOW), F32)
    return pl.pallas_call(
        functools.partial(_swa_sample_kernel, layer=layer, n_tok=n_tok),
        out_shape=[jax.ShapeDtypeStruct((n_seq, B_KV_HEADS, B_GROUP * pr, B_HEAD_DIM), F32), buf_sds, buf_sds],
        grid=(n_seq // sb,),
        in_specs=[pl.BlockSpec(memory_space=pltpu.SMEM), qspec, new_rows, new_rows, buf_in, buf_in],
        out_specs=[qspec, buf_out, buf_out],
        compiler_params=_cparams(("arbitrary",)),
        name="swa_decode",
    )(sinks, q, k_new, v_new, kt, vt)


def _s5_prep_kernel(lr_ref, li_ref, ldt_ref, br_ref, bi_ref, ctr_ref, cti_ref,
                    bw_ref, cr_ref, ci_ref, ar_ref, ai_ref, er_ref, ei_ref, fr_ref, fi_ref):
    lr, li = lr_ref[...], li_ref[...]
    dt = jnp.exp(ldt_ref[...])
    rho, th = lr * dt, li * dt
    mag = jnp.exp(rho)
    a_re, a_im = mag * jnp.cos(th), mag * jnp.sin(th)
    den = lr * lr + li * li
    nr, ni = a_re - 1.0, a_im
    f_re = (nr * lr + ni * li) / den
    f_im = (ni * lr - nr * li) / den
    br, bi = br_ref[...], bi_ref[...]
    bb_re = f_re * br - f_im * bi
    bb_im = f_re * bi + f_im * br
    lane_group = lax.broadcasted_iota(jnp.int32, bb_re.shape, 1) // C_STATE
    place = lambda bb: jnp.concatenate(
        [jnp.where(lane_group == g, bb, 0.0) for g in range(C_GROUPS)], axis=0).astype(BF16)
    placed_re, placed_im = place(bb_re), place(bb_im)
    for n in range(S5_N // MXU_TILE):
        src = slice(n * MXU_TILE, (n + 1) * MXU_TILE)
        bw_ref[:, 2 * n * MXU_TILE:(2 * n + 1) * MXU_TILE] = placed_re[:, src]
        bw_ref[:, (2 * n + 1) * MXU_TILE:(2 * n + 2) * MXU_TILE] = placed_im[:, src]
    spread_c = lax.broadcasted_iota(jnp.int32, (LANES, C_WIDTH), 0)
    spread_j = lax.broadcasted_iota(jnp.int32, (LANES, C_WIDTH), 1) % C_GROUP
    spread = jnp.where(spread_c == spread_j, 1.0, 0.0).astype(BF16)
    row_group = lax.broadcasted_iota(jnp.int32, cr_ref.shape, 0) // C_STATE
    col_group = lax.broadcasted_iota(jnp.int32, cr_ref.shape, 1) // C_GROUP
    own = row_group == col_group
    cr_ref[...] = jnp.where(own, _dot(ctr_ref[...].astype(BF16), spread), 0.0).astype(BF16)
    ci_ref[...] = jnp.where(own, _dot(cti_ref[...].astype(BF16), spread), 0.0).astype(BF16)
    ar_ref[...] = a_re
    ai_ref[...] = a_im
    s = (lax.broadcasted_iota(jnp.int32, er_ref.shape, 0) + 1).astype(F32)
    cs, sn = jnp.cos(s * th), jnp.sin(s * th)
    e_mag, f_mag = jnp.exp(-(s * rho)), jnp.exp(s * rho)
    er_ref[...] = e_mag * cs
    ei_ref[...] = -(e_mag * sn)
    fr_ref[...] = f_mag * cs
    fi_ref[...] = f_mag * sn


def _s5_prep_call(lr, li, ldt, b_re_t, b_im_t, c_re_t, c_im_t):
    depth = lr.shape[0]
    lsel = lambda l: (l, 0, 0)
    vec = pl.BlockSpec((None, 1, S5_N), lsel)
    mat = pl.BlockSpec((None, C_GROUP, S5_N), lsel)
    c_in = pl.BlockSpec((None, S5_N, LANES), lsel)
    cmat = pl.BlockSpec((None, S5_N, C_WIDTH), lsel)
    tab = pl.BlockSpec((None, S5_CHUNK, S5_N), lsel)
    sds = lambda r: jax.ShapeDtypeStruct((depth, r, S5_N), F32)
    c_sds = jax.ShapeDtypeStruct((depth, S5_N, C_WIDTH), BF16)
    return pl.pallas_call(
        _s5_prep_kernel,
        out_shape=[jax.ShapeDtypeStruct((depth, C_WIDTH, 2 * S5_N), BF16), c_sds, c_sds,
                   sds(1), sds(1)] + [sds(S5_CHUNK)] * 4,
        grid=(depth,),
        in_specs=[vec, vec, vec, mat, mat, c_in, c_in],
        out_specs=[pl.BlockSpec((None, C_WIDTH, 2 * S5_N), lsel), cmat, cmat, vec, vec, tab, tab, tab, tab],
        compiler_params=_cparams(("arbitrary",)),
        name="s5_discretise",
    )(lr, li, ldt, b_re_t, b_im_t, c_re_t, c_im_t)


def _cmul(ar, ai, br, bi):
    return ar * br - ai * bi, ar * bi + ai * br


def _s5_prompt_kernel(u_ref, bw_ref, cr_ref, ci_ref, d_ref, er_ref, ei_ref, fr_ref, fi_ref,
                      y_ref, hs_ref, car_ref, cai_ref, xr_ref, xi_ref, hr_ref, hi_ref, *, n_chunks):
    step = pl.program_id(1)
    t = S5_CHUNK

    @pl.when(step == 0)
    def _():
        car_ref[...] = jnp.zeros(car_ref.shape, F32)
        cai_ref[...] = jnp.zeros(cai_ref.shape, F32)

    groups = [slice(n * MXU_TILE, (n + 1) * MXU_TILE) for n in range(S5_N // MXU_TILE)]
    halves = [slice(0, LANES), slice(LANES, MXU_TILE)]
    group_cols = [[slice(g.start + hv.start, g.start + hv.stop) for hv in halves] for g in groups]

    u = u_ref[...]
    ub = u.astype(BF16)
    tiles_per_kt = len(groups) // (C_WIDTH // S5_KT)
    for n, cols in enumerate(group_cols):
        rows_k = slice(n // tiles_per_kt * S5_KT, (n // tiles_per_kt + 1) * S5_KT)
        bu = _dot(ub[:, rows_k], bw_ref[rows_k, 2 * n * MXU_TILE:(2 * n + 2) * MXU_TILE])
        for c in range(n_chunks):
            rs = slice(c * t, (c + 1) * t)
            for cs, hv in zip(cols, halves):
                xr, xi = _cmul(er_ref[:, cs], ei_ref[:, cs], bu[rs, hv],
                               bu[rs, MXU_TILE + hv.start:MXU_TILE + hv.stop])
                xr_ref[rs, cs] = xr.astype(BF16)
                xi_ref[rs, cs] = xi.astype(BF16)

    row = lax.broadcasted_iota(jnp.int32, (t, t), 0)
    col = lax.broadcasted_iota(jnp.int32, (t, t), 1)
    tril = jnp.where(col <= row, 1.0, 0.0).astype(BF16)

    carry = [(car_ref[:, g], cai_ref[:, g]) for g in groups]
    for c in range(n_chunks):
        rs = slice(c * t, (c + 1) * t)
        sums = [(_dot(tril, xr_ref[rs, g]), _dot(tril, xi_ref[rs, g])) for g in groups]
        for n, (g, cols) in enumerate(zip(groups, group_cols)):
            sum_re, sum_im = sums[n][0] + carry[n][0], sums[n][1] + carry[n][1]
            hs = [_cmul(fr_ref[:, cs], fi_ref[:, cs], sum_re[:, hv], sum_im[:, hv]) for cs, hv in zip(cols, halves)]
            hr = jnp.concatenate([h[0] for h in hs], axis=1)
            hi = jnp.concatenate([h[1] for h in hs], axis=1)
            hr_ref[rs, g] = hr.astype(BF16)
            hi_ref[rs, g] = hi.astype(BF16)
            carry[n] = (hr[t - 1:t, :], hi[t - 1:t, :])
    for g, (car, cai) in zip(groups, carry):
        car_ref[:, g] = car
        cai_ref[:, g] = cai

    n_out = C_WIDTH // S5_KT
    k_rows = S5_N // n_out
    for j in range(n_out):
        ks = slice(j * k_rows, (j + 1) * k_rows)
        cs = slice(j * S5_KT, (j + 1) * S5_KT)
        y_ref[:, cs] = (_dot(hr_ref[:, ks], cr_ref[ks, cs]) - _dot(hi_ref[:, ks], ci_ref[ks, cs])
                        + d_ref[:, cs] * u[:, cs])

    @pl.when(step == pl.num_programs(1) - 1)
    def _():
        hs_ref[0:1, :] = car_ref[...]
        hs_ref[1:2, :] = cai_ref[...]


def _s5_prompt_call(layer, u, bw, c_re, c_im, d, tabs, n_seq, seq_rows, step_rows):
    steps = seq_rows // step_rows
    row_spec = pl.BlockSpec((step_rows, C_WIDTH), lambda s, j, steps=steps: (s * steps + j, 0))
    lsel = lambda s, j, l=layer: (l, 0, 0)
    tab = pl.BlockSpec((None, S5_CHUNK, S5_N), lsel)
    return pl.pallas_call(
        functools.partial(_s5_prompt_kernel, n_chunks=step_rows // S5_CHUNK),
        out_shape=[jax.ShapeDtypeStruct((n_seq * seq_rows, C_WIDTH), F32),
                   jax.ShapeDtypeStruct((n_seq, 2, S5_N), F32)],
        grid=(n_seq, steps),
        in_specs=[row_spec,
                  _resident((None, C_WIDTH, 2 * S5_N), lsel),
                  _resident((None, S5_N, C_WIDTH), lsel),
                  _resident((None, S5_N, C_WIDTH), lsel),
                  pl.BlockSpec((None, 1, C_WIDTH), lsel),
                  tab, tab, tab, tab],
        out_specs=[row_spec, pl.BlockSpec((None, 2, S5_N), lambda s, j: (s, 0, 0))],
        scratch_shapes=[pltpu.VMEM((1, S5_N), F32), pltpu.VMEM((1, S5_N), F32)]
                       + [pltpu.VMEM((step_rows, S5_N), BF16)] * 4,
        compiler_params=_cparams(("arbitrary", "arbitrary")),
        name="s5_chunked_scan",
    )(u, bw, c_re, c_im, d, *tabs)


def _s5_sample_kernel(u_ref, h0r_ref, h0i_ref, ar_ref, ai_ref, bw_ref, cr_ref, ci_ref, d_ref,
                      y_ref, hr_ref, hi_ref, *, n_tok, n_seq):
    hr, hi = h0r_ref[...], h0i_ref[...]
    a_re, a_im = ar_ref[...], ai_ref[...]
    for t in range(n_tok):
        u = u_ref[t * n_seq:(t + 1) * n_seq, :]
        bu = _dot(u.astype(BF16), bw_ref[...])
        tiles = range(S5_N // MXU_TILE)
        bu_re = jnp.concatenate([bu[:, 2 * n * MXU_TILE:(2 * n + 1) * MXU_TILE] for n in tiles], axis=1)
        bu_im = jnp.concatenate([bu[:, (2 * n + 1) * MXU_TILE:(2 * n + 2) * MXU_TILE] for n in tiles], axis=1)
        hr, hi = _cmul(a_re, a_im, hr, hi)
        hr, hi = hr + bu_re, hi + bu_im
        y_ref[t * n_seq:(t + 1) * n_seq, :] = (_dot(hr.astype(BF16), cr_ref[...]) - _dot(hi.astype(BF16), ci_ref[...])
                                               + d_ref[...] * u)
    hr_ref[...] = hr
    hi_ref[...] = hi


def _s5_sample_call(layer, u, h0_re, h0_im, a_re, a_im, bw, c_re, c_im, d, n_tok, n_seq):
    lsel = lambda i, l=layer: (l, 0, 0)
    full = lambda shape: pl.BlockSpec(shape, lambda i: (0,) * len(shape))
    return pl.pallas_call(
        functools.partial(_s5_sample_kernel, n_tok=n_tok, n_seq=n_seq),
        out_shape=[jax.ShapeDtypeStruct((n_tok * n_seq, C_WIDTH), F32),
                   jax.ShapeDtypeStruct((n_seq, S5_N), F32),
                   jax.ShapeDtypeStruct((n_seq, S5_N), F32)],
        grid=(1,),
        in_specs=[full((n_tok * n_seq, C_WIDTH)),
                  pl.BlockSpec((None, n_seq, S5_N), lsel),
                  pl.BlockSpec((None, n_seq, S5_N), lsel),
                  pl.BlockSpec((None, 1, S5_N), lsel),
                  pl.BlockSpec((None, 1, S5_N), lsel),
                  pl.BlockSpec((None, C_WIDTH, 2 * S5_N), lsel),
                  pl.BlockSpec((None, S5_N, C_WIDTH), lsel),
                  pl.BlockSpec((None, S5_N, C_WIDTH), lsel),
                  pl.BlockSpec((None, 1, C_WIDTH), lsel)],
        out_specs=[full((n_tok * n_seq, C_WIDTH)), full((n_seq, S5_N)), full((n_seq, S5_N))],
        compiler_params=_cparams(("arbitrary",)),
        name="s5_step_scan",
    )(u, h0_re, h0_im, a_re, a_im, bw, c_re, c_im, d)


W_GATES = A_V + 3 * D_MODEL


def _merge_kernel(x_ref, g_ref, shs, shp, scs, scp, gts, gtp,
                  oap, oas, obp, obs, ocp, ocs,
                  wz_ref, wa_ref, wb_ref, wc_ref, wglu_ref, wo_ref, o_ref, *, rows):
    x = x_ref[...]
    br_b = _dot(_select_rows(rows, obp, obs).astype(BF16), wb_ref[...])
    glu = _dot(_gelu_tanh(_select_rows(rows, ocp, ocs)).astype(BF16), wglu_ref[...])
    h = _ada_norm(x, g_ref[...], _mod_value(rows, shs, shp), _mod_value(rows, scs, scp)).astype(BF16)
    z = _dot_nt(h, wz_ref[...])
    oc = glu[:, :C_WIDTH] * _sigmoid(glu[:, C_WIDTH:])
    br_c = _dot(oc.astype(BF16), wc_ref[...])
    oa = _select_rows(rows, oap, oas) * _silu(z[:, :A_V])
    br_a = _dot(oa.astype(BF16), wa_ref[...])
    o = A_V
    mix = (_sigmoid(z[:, o + D_MODEL:o + 2 * D_MODEL]) * br_b
           + _sigmoid(z[:, o + 2 * D_MODEL:o + 3 * D_MODEL]) * br_c
           + _sigmoid(z[:, o:o + D_MODEL]) * br_a)
    out = _dot(mix.astype(BF16), wo_ref[...])
    o_ref[...] = x + _mod_value(rows, gts, gtp) * out


def _merge_call(rows, layer, x, mod, norm_g, oa, ob, oc, wz, wa, wb, wc, wglu, wo):
    in_specs = [pl.BlockSpec((TOKEN_TILE, D_MODEL), lambda i: (i, 0)),
                pl.BlockSpec((None, 1, D_MODEL), lambda i, l=layer: (l, 0, 0))]
    operands = [x, norm_g]
    for j in (3, 4, 5):
        in_specs += _mod_specs(rows, layer, j)
        operands += [mod, mod]
    for pair, w in ((oa, A_V), (ob, B_Q), (oc, C_WIDTH)):
        in_specs += _pair_specs(rows, w)
        operands += list(pair)
    lsel = lambda i, l=layer: (l, 0, 0)
    in_specs += [_resident((None, W_GATES, D_MODEL), lsel),
                 _resident((None, A_V, D_MODEL), lsel),
                 _resident((None, B_Q, D_MODEL), lsel),
                 _resident((None, C_WIDTH, D_MODEL), lsel),
                 _resident((None, C_WIDTH, 2 * C_WIDTH), lsel),
                 _resident((None, D_MODEL, D_MODEL), lsel)]
    operands += [wz, wa, wb, wc, wglu, wo]
    return pl.pallas_call(
        functools.partial(_merge_kernel, rows=rows),
        out_shape=jax.ShapeDtypeStruct((rows.rows, D_MODEL), F32),
        grid=(rows.n_tiles,),
        in_specs=in_specs,
        out_specs=pl.BlockSpec((TOKEN_TILE, D_MODEL), lambda i: (i, 0)),
        compiler_params=_cparams(("arbitrary",)),
        name="branch_merge",
    )(*operands)


def _rotary_tables(rows):
    half = ROPE_DIM // 2
    inv_freq = ROPE_THETA ** (-jnp.arange(half, dtype=F32) / half)
    pos_p = jnp.arange(rows.l, dtype=jnp.int32)
    pos_s = PAST_LEN + jnp.repeat(jnp.arange(rows.t, dtype=jnp.int32), rows.bs)
    pos = jnp.concatenate([pos_p, pos_s])
    ang = pos.astype(F32)[:, None] * inv_freq[None, :]
    cos, sin = jnp.cos(ang), jnp.sin(ang)
    n = pos.shape[0]
    ones = jnp.ones((n, B_HEAD_DIM - ROPE_DIM), F32)
    zeros = jnp.zeros((n, B_HEAD_DIM - ROPE_DIM), F32)
    z8 = jnp.zeros((n, half), F32)
    per_head = lambda parts: jnp.tile(jnp.concatenate(parts, axis=1), (1, LANES // B_HEAD_DIM))
    return (per_head([cos, cos, ones]), per_head([z8, sin, zeros]), per_head([-sin, z8, zeros]))


def _sample_to_seq(a, rows, pad_to):
    w = a.shape[-1]
    a = a.reshape(rows.t, rows.bs, w).transpose(1, 0, 2)
    a = jnp.pad(a, ((0, 0), (0, pad_to - rows.t), (0, 0)))
    return a.reshape(rows.bs * pad_to, w)


def _seq_to_sample(a, rows, pad_to):
    w = a.shape[-1]
    return a.reshape(rows.bs, pad_to, w)[:, :rows.t].transpose(1, 0, 2).reshape(rows.ns_rows, w)


def kernel(x_prompt, x_sample, state_gla, cache_swa_k, cache_swa_v, state_s5_re, state_s5_im, c_prompt, c_sample, w_ada, b_ada, norm_ffn1, ffn1_w_gate, ffn1_w_up, ffn1_w_down, norm_mix, w_in, gla_w_gate_up, gla_b_gate, gla_norm, swa_sinks, s5_a_re, s5_a_im, s5_b_re, s5_b_im, s5_c_re, s5_c_im, s5_d, s5_log_dt, s5_w_glu, w_branch_a, w_branch_b, w_branch_c, w_out, norm_ffn2, ffn2_w_gate, ffn2_w_up, ffn2_w_down, norm_final):
    depth = w_ada.shape[0]
    b, l, _ = x_prompt.shape
    bs, t, _ = x_sample.shape
    rows = _Rows(b, l, bs, t)
    np_rows = rows.np_rows

    x = (x_prompt.reshape(np_rows, D_MODEL), x_sample.transpose(1, 0, 2).reshape(rows.ns_rows, D_MODEL))
    c_all = jnp.concatenate([c_sample, c_prompt, jnp.zeros((8 - b, D_MODEL), F32)], axis=0)
    mod = _mod_call(c_all, w_ada, b_ada)

    pts = [0, A_QK, 2 * A_QK, 2 * A_QK + A_V]
    o_rank = pts[-1]
    o_ar = o_rank + A_GATE_RANK
    o_bq = o_ar + A_V
    o_cu_end = o_bq + B_Q + 2 * B_KV + C_WIDTH
    w_in_t = w_in.transpose(0, 2, 1)
    w_main = jnp.concatenate([w_in_t[:, :o_rank], w_in_t[:, o_bq:o_cu_end]], axis=1).astype(BF16)
    w_rank = jnp.pad(w_in_t[:, o_rank:o_ar], ((0, 0), (0, LANES - A_GATE_RANK), (0, 0))).astype(BF16)
    w_up = jnp.pad(gla_w_gate_up, ((0, 0), (0, LANES - A_GATE_RANK), (0, 0))).astype(BF16)
    w_gates = jnp.concatenate([w_in_t[:, o_ar:o_bq], w_in_t[:, o_cu_end:]], axis=1).astype(BF16)
    cast = lambda w: w.astype(BF16)
    f1g, f1u, f1d = cast(ffn1_w_gate), cast(ffn1_w_up), cast(ffn1_w_down)
    f2g, f2u, f2d = cast(ffn2_w_gate), cast(ffn2_w_up), cast(ffn2_w_down)
    wa, wb, wc, wglu, wo = cast(w_branch_a), cast(w_branch_b), cast(w_branch_c), cast(s5_w_glu), cast(w_out)
    vec3 = lambda a: a.reshape(depth, 1, a.shape[-1])
    n1, nm, n2 = vec3(norm_ffn1), vec3(norm_mix), vec3(norm_ffn2)
    b_gate, g_norm, s5d = vec3(gla_b_gate), vec3(gla_norm), vec3(s5_d)
    rot = _rotary_tables(rows)

    flat = lambda a: a.reshape(depth, 1, S5_N)
    ldt = flat(jnp.broadcast_to(s5_log_dt[:, :, None], (depth, C_GROUPS, C_STATE)))
    to_cols = lambda bm: bm.transpose(0, 3, 1, 2).reshape(depth, C_GROUP, S5_N)
    to_rows = lambda cm: jnp.pad(cm.transpose(0, 1, 3, 2).reshape(depth, S5_N, C_GROUP),
                                 ((0, 0), (0, 0), (0, LANES - C_GROUP)))
    s5_bw, s5_cr, s5_ci, a_re, a_im, e_re, e_im, f_re, f_im = _s5_prep_call(
        flat(s5_a_re), flat(s5_a_im), ldt, to_cols(s5_b_re), to_cols(s5_b_im), to_rows(s5_c_re), to_rows(s5_c_im))
    tabs = (e_re, e_im, f_re, f_im)
    h0_re = state_s5_re.reshape(depth, bs, S5_N)
    h0_im = state_s5_im.reshape(depth, bs, S5_N)
    cache_kt = cache_swa_k.transpose(0, 1, 3, 4, 2)
    cache_vt = cache_swa_v.transpose(0, 1, 3, 4, 2)

    outs = {k: [] for k in ("gla_p", "gla_s", "kp", "ks", "vp", "vs", "rp", "rs", "ip", "is")}
    pad8 = SWA_SAMPLE_ROWS
    for layer in range(depth):
        x = _ffn_call(rows, layer, 0, x, mod, n1, f1g, f1u, f1d)
        gq, gk, gv, la, sq, sk, sv, su = _inproj_call(rows, layer, x, mod, nm, w_main, w_rank, w_up, b_gate, rot)

        oa_p, st_p = _gla_call(layer, gq, gk, gv, la, g_norm, None, b, l, TOKEN_TILE, GLA_CHUNK, GLA_BLOCK)
        sseq = lambda a: _sample_to_seq(a[np_rows:], rows, pad8)
        oa_s, st_s = _gla_call(layer, sseq(gq), sseq(gk), sseq(gv), sseq(la), g_norm, state_gla,
                               bs, pad8, pad8, pad8, pad8, seqs_per_step=GLA_SAMPLE_SEQS)
        oa_s = _seq_to_sample(oa_s, rows, pad8)

        ob_p = _swa_call(layer, swa_sinks, sq, sk, sv, b, l // WINDOW)
        q_s = sq[np_rows:].reshape(t, bs, B_KV_HEADS, B_GROUP, B_HEAD_DIM).transpose(1, 2, 3, 0, 4)
        q_s = jnp.pad(q_s, ((0, 0), (0, 0), (0, 0), (0, pad8 - t), (0, 0)))
        q_s = q_s.reshape(bs, B_KV_HEADS, B_GROUP * pad8, B_HEAD_DIM)
        ob_s, kt_new, vt_new = _swa_sample_call(layer, swa_sinks, q_s, sseq(sk), sseq(sv), cache_kt, cache_vt, bs, t)
        ob_s = ob_s.reshape(bs, B_KV_HEADS, B_GROUP, pad8, B_HEAD_DIM)[:, :, :, :t]
        ob_s = ob_s.transpose(3, 0, 1, 2, 4).reshape(rows.ns_rows, B_Q)

        oc_p, hs_p = _s5_prompt_call(layer, su, s5_bw, s5_cr, s5_ci, s5d, tabs, b, l, TOKEN_TILE)
        oc_s, hr_s, hi_s = _s5_sample_call(layer, su[np_rows:], h0_re, h0_im, a_re, a_im,
                                           s5_bw, s5_cr, s5_ci, s5d, t, bs)

        x = _merge_call(rows, layer, x, mod, nm, (oa_p, oa_s), (ob_p, ob_s), (oc_p, oc_s),
                        w_gates, wa, wb, wc, wglu, wo)
        x = _ffn_call(rows, layer, 6, x, mod, n2, f2g, f2u, f2d,
                      final_g=norm_final.reshape(1, D_MODEL) if layer == depth - 1 else None)

        outs["gla_p"].append(st_p)
        outs["gla_s"].append(st_s)
        kv_shape = (b, WINDOW, B_KV_HEADS, B_HEAD_DIM)
        window_rows = lambda a: jnp.stack([a[(i + 1) * l - WINDOW:(i + 1) * l] for i in range(b)])
        outs["kp"].append(window_rows(sk).reshape(kv_shape))
        outs["vp"].append(window_rows(sv).reshape(kv_shape))
        outs["ks"].append(kt_new.transpose(0, 3, 1, 2))
        outs["vs"].append(vt_new.transpose(0, 3, 1, 2))
        outs["rp"].append(hs_p[:, 0].reshape(b, C_GROUPS, C_STATE))
        outs["ip"].append(hs_p[:, 1].reshape(b, C_GROUPS, C_STATE))
        outs["rs"].append(hr_s.reshape(bs, C_GROUPS, C_STATE))
        outs["is"].append(hi_s.reshape(bs, C_GROUPS, C_STATE))

    y_p = x[0].reshape(b, l, D_MODEL)
    y_s = x[1].reshape(t, bs, D_MODEL).transpose(1, 0, 2)
    st = lambda k: jnp.stack(outs[k])
    return (y_p, y_s, st("gla_p"), st("gla_s"), st("kp"), st("ks"), st("vp"), st("vs"),
            st("rp"), st("rs"), st("ip"), st("is"))
```

```python
import functools
import math

import jax
import jax.numpy as jnp
from jax import lax
from jax.experimental import pallas as pl
from jax.experimental.pallas import tpu as pltpu

D_MODEL = 1024
D_FF = 2816
PAST_LEN = 8192
A_HEADS, A_DK, A_DV = 4, 64, 128
A_GATE_RANK = 16
A_GATE_NORM = 16.0
B_HEADS, B_KV_HEADS, B_HEAD_DIM = 8, 2, 64
B_GROUP = B_HEADS // B_KV_HEADS
WINDOW = 128
ROPE_DIM = B_HEAD_DIM // 4
ROPE_THETA = 500000.0
C_WIDTH, C_GROUP, C_STATE = 512, 16, 64
C_GROUPS = C_WIDTH // C_GROUP
MACARON_W = 0.5
NORM_EPS = 1e-6
N_MOD = 9

A_QK = A_HEADS * A_DK
A_V = A_HEADS * A_DV
B_Q = B_HEADS * B_HEAD_DIM
B_KV = B_KV_HEADS * B_HEAD_DIM
S5_N = C_GROUPS * C_STATE

LANES = 128
TOKEN_TILE = 512
MIXER_STEP_ROWS = 1024
GLA_CHUNK = 64
GLA_BLOCK = 32
GLA_SAMPLE_SEQS = 8
S5_CHUNK = 64
S5_KT = 256
VMEM_LIMIT = 56 * 1024 * 1024

F32 = jnp.float32
BF16 = jnp.bfloat16
NEG_BIG = -1e30
LOG2_E = math.log2(math.e)


def _cparams(sem):
    return pltpu.CompilerParams(dimension_semantics=sem, vmem_limit_bytes=VMEM_LIMIT)


def _dot(a, b):
    return jnp.dot(a, b, preferred_element_type=F32)


def _dot_nt(a, b):
    return lax.dot_general(a, b, (((1,), (1,)), ((), ())), preferred_element_type=F32)


def _sigmoid(x):
    return 1.0 / (1.0 + jnp.exp(-x))


def _silu(x):
    return x * _sigmoid(x)


def _gelu_tanh(x):
    return 0.5 * x * (1.0 + jnp.tanh(math.sqrt(2.0 / math.pi) * (x + 0.044715 * (x * x * x))))


def _log_sigmoid(x):
    return jnp.minimum(x, 0.0) - jnp.log1p(jnp.exp(-jnp.abs(x)))


def _resident(shape, index_map):
    return pl.BlockSpec(shape, index_map, pipeline_mode=pl.Buffered(1))


def _mod_kernel(c_ref, w_ref, b_ref, o_ref):
    c = c_ref[...]
    s = _silu(c).astype(BF16)
    o_ref[...] = _dot(s, w_ref[...].astype(BF16)) + b_ref[...]


MOD_BLOCK = 3 * D_MODEL


def _mod_call(c_all, w_ada, b_ada):
    depth = w_ada.shape[0]
    rows = c_all.shape[0]
    return pl.pallas_call(
        _mod_kernel,
        out_shape=jax.ShapeDtypeStruct((depth, rows, N_MOD * D_MODEL), F32),
        grid=(depth, N_MOD * D_MODEL // MOD_BLOCK),
        in_specs=[
            pl.BlockSpec((rows, D_MODEL), lambda l, j: (0, 0)),
            pl.BlockSpec((None, D_MODEL, MOD_BLOCK), lambda l, j: (l, 0, j)),
            pl.BlockSpec((None, 1, MOD_BLOCK), lambda l, j: (l, 0, j)),
        ],
        out_specs=pl.BlockSpec((None, rows, MOD_BLOCK), lambda l, j: (l, 0, j)),
        compiler_params=_cparams(("arbitrary", "arbitrary")),
        name="ada_mod",
    )(c_all, w_ada, b_ada.reshape(depth, 1, N_MOD * D_MODEL))


class _Rows:
    def __init__(self, n_prompt_seq, seq_len, n_sample_seq, n_sample_tok):
        self.b = n_prompt_seq
        self.l = seq_len
        self.bs = n_sample_seq
        self.t = n_sample_tok
        self.np_rows = n_prompt_seq * seq_len
        self.ns_rows = n_sample_seq * n_sample_tok
        assert seq_len % TOKEN_TILE == 0 and self.ns_rows == TOKEN_TILE
        assert n_sample_seq % 8 == 0 and seq_len % MIXER_STEP_ROWS == 0
        self.rows = self.np_rows + self.ns_rows
        self.n_prompt_tiles = self.np_rows // TOKEN_TILE
        self.tiles_per_seq = seq_len // TOKEN_TILE
        self.n_tiles = self.n_prompt_tiles + 1
        self.mod_rows = n_sample_seq + 8
        assert n_prompt_seq <= 8


def _mod_specs(rows, layer, j):
    return [
        pl.BlockSpec((None, rows.bs, D_MODEL), lambda i, l=layer, j=j: (l, 0, j)),
        pl.BlockSpec((None, 8, D_MODEL), lambda i, l=layer, j=j, r=rows.bs // 8: (l, r, j)),
    ]


def _mod_value(rows, ms_ref, mp_ref):
    i = pl.program_id(0)
    is_sample = i == rows.n_prompt_tiles
    seq = jnp.minimum(i // rows.tiles_per_seq, rows.b - 1)
    s = ms_ref[...]
    s = jnp.concatenate([s] * rows.t, axis=0)
    p = mp_ref[pl.ds(seq, 1), :]
    return jnp.where(is_sample, s, p)


def _ada_norm(x, g, shift, scale):
    var = jnp.mean(x * x, axis=-1, keepdims=True)
    return (x * lax.rsqrt(var + NORM_EPS)) * g * (1.0 + scale) + shift


def _select_rows(rows, p_ref, s_ref):
    is_sample = pl.program_id(0) == rows.n_prompt_tiles
    return jnp.where(is_sample, s_ref[...], p_ref[...])


def _pair_specs(rows, width):
    last = rows.n_prompt_tiles - 1
    return [
        pl.BlockSpec((TOKEN_TILE, width), lambda i, last=last: (jnp.minimum(i, last), 0)),
        pl.BlockSpec((TOKEN_TILE, width), lambda i: (0, 0)),
    ]


MXU_TILE = 256
FF_SPLITS = (0, 6 * MXU_TILE, D_FF)
assert D_FF % MXU_TILE == 0


def _ffn_kernel(*refs, rows, x_pair, final_norm):
    n_in = 11 + int(x_pair) + int(final_norm)
    ins, outs = refs[:n_in], refs[n_in:]
    if x_pair:
        x = _select_rows(rows, ins[0], ins[1])
    else:
        x = ins[0][...]
    g_ref, shs, shp, scs, scp, gts, gtp, wg_ref, wu_ref, wd_ref = ins[1 + int(x_pair):11 + int(x_pair)]
    h = _ada_norm(x, g_ref[...], _mod_value(rows, shs, shp), _mod_value(rows, scs, scp)).astype(BF16)
    chunks = [slice(lo, hi) for lo, hi in zip(FF_SPLITS[:-1], FF_SPLITS[1:])]
    gate_up = [(_dot(h, wg_ref[:, sl]), _dot(h, wu_ref[:, sl])) for sl in chunks]
    act = [(_silu(gate) * up).astype(BF16) for gate, up in gate_up]
    acc = _dot(act[0], wd_ref[chunks[0], :])
    for a, sl in zip(act[1:], chunks[1:]):
        acc = acc + _dot(a, wd_ref[sl, :])
    out = x + (MACARON_W * _mod_value(rows, gts, gtp)) * acc
    if not final_norm:
        outs[0][...] = out
        return
    var = jnp.mean(out * out, axis=-1, keepdims=True)
    y = out * lax.rsqrt(var + NORM_EPS) * ins[-1][...]
    is_sample = pl.program_id(0) == rows.n_prompt_tiles

    @pl.when(jnp.logical_not(is_sample))
    def _():
        outs[0][...] = y

    @pl.when(is_sample)
    def _():
        outs[1][...] = y


def _ffn_call(rows, layer, mod_base, x, mod, norm_g, wg, wu, wd, final_g=None):
    x_pair = isinstance(x, tuple)
    if x_pair:
        in_specs = _pair_specs(rows, D_MODEL)
        operands = list(x)
    else:
        in_specs = [pl.BlockSpec((TOKEN_TILE, D_MODEL), lambda i: (i, 0))]
        operands = [x]
    in_specs.append(pl.BlockSpec((None, 1, D_MODEL), lambda i, l=layer: (l, 0, 0)))
    operands.append(norm_g)
    for j in range(3):
        in_specs += _mod_specs(rows, layer, mod_base + j)
        operands += [mod, mod]
    in_specs += [
        _resident((None, D_MODEL, D_FF), lambda i, l=layer: (l, 0, 0)),
        _resident((None, D_MODEL, D_FF), lambda i, l=layer: (l, 0, 0)),
        _resident((None, D_FF, D_MODEL), lambda i, l=layer: (l, 0, 0)),
    ]
    operands += [wg, wu, wd]
    if final_g is None:
        out_shape = jax.ShapeDtypeStruct((rows.rows, D_MODEL), F32)
        out_specs = pl.BlockSpec((TOKEN_TILE, D_MODEL), lambda i: (i, 0))
    else:
        in_specs.append(pl.BlockSpec((1, D_MODEL), lambda i: (0, 0)))
        operands.append(final_g)
        out_shape = [jax.ShapeDtypeStruct((rows.np_rows, D_MODEL), F32),
                     jax.ShapeDtypeStruct((rows.ns_rows, D_MODEL), F32)]
        out_specs = _pair_specs(rows, D_MODEL)
    return pl.pallas_call(
        functools.partial(_ffn_kernel, rows=rows, x_pair=x_pair, final_norm=final_g is not None),
        out_shape=out_shape,
        grid=(rows.n_tiles,),
        in_specs=in_specs,
        out_specs=out_specs,
        compiler_params=_cparams(("arbitrary",)),
        name="ada_swiglu",
    )(*operands)


W_MAIN = 2 * A_QK + A_V + B_Q + 2 * B_KV + C_WIDTH


def _rotary(x, cos, s_up, s_dn):
    half = ROPE_DIM // 2
    return x * cos + pltpu.roll(x, half, axis=1) * s_up + pltpu.roll(x, LANES - half, axis=1) * s_dn


def _inproj_kernel(x_ref, g_ref, shs, shp, scs, scp, wm_ref, wr_ref, wup_ref, bg_ref,
                   cos_ref, sup_ref, sdn_ref,
                   gq_ref, gk_ref, gv_ref, la_ref, sq_ref, sk_ref, sv_ref, su_ref, *, rows):
    x = x_ref[...]
    h = _ada_norm(x, g_ref[...], _mod_value(rows, shs, shp), _mod_value(rows, scs, scp)).astype(BF16)
    z = _dot_nt(h, wm_ref[...])
    o = 0
    gq_ref[...] = z[:, o:o + A_QK] * (A_DK ** -0.5)
    o += A_QK
    gk_ref[...] = z[:, o:o + A_QK]
    o += A_QK
    gv_ref[...] = z[:, o:o + A_V]
    o += A_V
    cos, s_up, s_dn = cos_ref[...], sup_ref[...], sdn_ref[...]
    for c in range(B_Q // LANES):
        sq_ref[:, c * LANES:(c + 1) * LANES] = _rotary(z[:, o + c * LANES:o + (c + 1) * LANES], cos, s_up, s_dn)
    o += B_Q
    sk_ref[...] = _rotary(z[:, o:o + B_KV], cos, s_up, s_dn)
    o += B_KV
    sv_ref[...] = z[:, o:o + B_KV]
    o += B_KV
    su_ref[...] = z[:, o:o + C_WIDTH]
    low = _dot_nt(h, wr_ref[...]).astype(BF16)
    pre = _dot(low, wup_ref[...]) + bg_ref[...]
    la_ref[...] = _log_sigmoid(pre) * (LOG2_E / A_GATE_NORM)


def _inproj_call(rows, layer, x, mod, norm_g, w_main, w_rank, w_up, b_gate, rot):
    row_spec = lambda w: pl.BlockSpec((TOKEN_TILE, w), lambda i: (i, 0))
    rot_spec = pl.BlockSpec(
        (TOKEN_TILE, LANES),
        lambda i, n=rows.n_prompt_tiles, per=rows.tiles_per_seq: (jnp.where(i < n, i % per, per), 0))
    in_specs = [row_spec(D_MODEL), pl.BlockSpec((None, 1, D_MODEL), lambda i, l=layer: (l, 0, 0))]
    operands = [x, norm_g]
    for j in (3, 4):
        in_specs += _mod_specs(rows, layer, j)
        operands += [mod, mod]
    in_specs += [
        _resident((None, W_MAIN, D_MODEL), lambda i, l=layer: (l, 0, 0)),
        _resident((None, LANES, D_MODEL), lambda i, l=layer: (l, 0, 0)),
        _resident((None, LANES, A_QK), lambda i, l=layer: (l, 0, 0)),
        pl.BlockSpec((None, 1, A_QK), lambda i, l=layer: (l, 0, 0)),
        rot_spec, rot_spec, rot_spec,
    ]
    operands += [w_main, w_rank, w_up, b_gate, *rot]
    widths = (A_QK, A_QK, A_V, A_QK, B_Q, B_KV, B_KV, C_WIDTH)
    return pl.pallas_call(
        functools.partial(_inproj_kernel, rows=rows),
        out_shape=[jax.ShapeDtypeStruct((rows.rows, w), F32) for w in widths],
        grid=(rows.n_tiles,),
        in_specs=in_specs,
        out_specs=[row_spec(w) for w in widths],
        compiler_params=_cparams(("arbitrary",)),
        name="mixer_in_proj",
    )(*operands)


def _cumsum_rows(x):
    n = x.shape[0]
    row = lax.broadcasted_iota(jnp.int32, x.shape, 0)
    s = 1
    while s < n:
        x = x + jnp.where(row >= s, pltpu.roll(x, s, axis=0), 0.0)
        s *= 2
    return x


def _head_blockdiag(x, lanes_per_head):
    lane_head = lax.broadcasted_iota(jnp.int32, x.shape, 1) // lanes_per_head
    return jnp.concatenate([jnp.where(lane_head == h, x, 0.0) for h in range(A_HEADS)], axis=0)


def _gla_kernel(*refs, chunk, block, n_chunks, n_seqs, single_step, has_state):
    if has_state:
        q_ref, k_ref, v_ref, la_ref, gn_ref, s0_ref, o_ref, so_ref, st_ref = refs
    else:
        q_ref, k_ref, v_ref, la_ref, gn_ref, o_ref, so_ref, st_ref = refs
    n_blocks = chunk // block
    st_shape = (A_V, A_QK)
    st_row_head = lax.broadcasted_iota(jnp.int32, st_shape, 0) // A_DV
    st_lane_head = lax.broadcasted_iota(jnp.int32, st_shape, 1) // A_DK
    st_mask = st_row_head == st_lane_head
    srow = lax.broadcasted_iota(jnp.int32, (block, A_HEADS * block), 0)
    scol = lax.broadcasted_iota(jnp.int32, (block, A_HEADS * block), 1) % block
    causal = scol <= srow
    gn = gn_ref[...]

    def init_state(sq):
        st_ref[sq] = jnp.zeros(st_shape, F32)
        if has_state:
            for h in range(A_HEADS):
                st_ref[sq, h * A_DV:(h + 1) * A_DV, h * A_DK:(h + 1) * A_DK] = s0_ref[sq, h].T

    def emit_state(sq):
        for h in range(A_HEADS):
            so_ref[sq, h] = st_ref[sq, h * A_DV:(h + 1) * A_DV, h * A_DK:(h + 1) * A_DK].T

    if single_step:
        for sq in range(n_seqs):
            init_state(sq)
    else:
        pl.when(pl.program_id(1) == 0)(lambda: init_state(0))

    units = [(sq, c) for sq in range(n_seqs) for c in range(n_chunks)]
    rows_of = {u: slice((u[0] * n_chunks + u[1]) * chunk, (u[0] * n_chunks + u[1] + 1) * chunk) for u in units}

    q, k, v, b = {}, {}, {}, {}
    for u in units:
        q[u], k[u], v[u] = q_ref[rows_of[u], :], k_ref[rows_of[u], :], v_ref[rows_of[u], :]
        b[u] = _cumsum_rows(la_ref[rows_of[u], :])

    s_diag, s_off, upd, q_in = {}, {}, {}, {}
    for u in units:
        for i in range(n_blocks):
            lo, hi = i * block, (i + 1) * block
            qi, ki, bi = q[u][lo:hi], k[u][lo:hi], b[u][lo:hi]
            mid = b[u][lo + block // 2 - 1:lo + block // 2, :]
            qd = (qi * jnp.exp2(bi - mid)).astype(BF16)
            kd = _head_blockdiag(ki * jnp.exp2(mid - bi), A_DK).astype(BF16)
            s_diag[u, i] = _dot_nt(qd, kd)
            if i > 0:
                ref = b[u][lo - 1:lo, :]
                qo = (qi * jnp.exp2(bi - ref)).astype(BF16)
                ko = _head_blockdiag(k[u][:lo] * jnp.exp2(ref - b[u][:lo]), A_DK).astype(BF16)
                s_off[u, i] = _dot_nt(qo, ko)
        b_last = b[u][chunk - 1:chunk, :]
        upd[u] = _dot(v[u].T.astype(BF16), (k[u] * jnp.exp2(b_last - b[u])).astype(BF16))
        q_in[u] = (q[u] * jnp.exp2(b[u])).astype(BF16)

    intra = {}
    for u in units:
        outs = []
        for i in range(n_blocks):
            lo, hi = i * block, (i + 1) * block
            att = jnp.where(causal, s_diag[u, i], 0.0).astype(BF16)
            oi = _dot(att, _head_blockdiag(v[u][lo:hi], A_DV).astype(BF16))
            if i > 0:
                oi = oi + _dot(s_off[u, i].astype(BF16), _head_blockdiag(v[u][:lo], A_DV).astype(BF16))
            outs.append(oi)
        intra[u] = jnp.concatenate(outs, axis=0) if n_blocks > 1 else outs[0]

    for u in units:
        sq = u[0]
        st = st_ref[sq]
        o = _dot_nt(q_in[u], st.astype(BF16)) + intra[u]
        st_ref[sq] = st * jnp.exp2(b[u][chunk - 1:chunk, :]) + jnp.where(st_mask, upd[u], 0.0)
        normed = []
        for h in range(A_HEADS):
            oh = o[:, h * A_DV:(h + 1) * A_DV]
            var = jnp.mean(oh * oh, axis=-1, keepdims=True)
            normed.append(oh * lax.rsqrt(var + NORM_EPS) * gn)
        o_ref[rows_of[u], :] = jnp.concatenate(normed, axis=1)

    if single_step:
        for sq in range(n_seqs):
            emit_state(sq)
    else:
        pl.when(pl.program_id(1) == pl.num_programs(1) - 1)(lambda: emit_state(0))


def _gla_call(layer, q, k, v, la, gnorm, s0, n_seq, seq_rows, step_rows, chunk, block, seqs_per_step=1):
    steps = seq_rows // step_rows
    sps = seqs_per_step
    assert sps == 1 or steps == 1
    blk_rows = sps * step_rows
    row_spec = lambda w: pl.BlockSpec((blk_rows, w), lambda s, j, steps=steps: (s * steps + j, 0))
    in_specs = [row_spec(A_QK), row_spec(A_QK), row_spec(A_V), row_spec(A_QK),
                pl.BlockSpec((None, 1, A_DV), lambda s, j, l=layer: (l, 0, 0))]
    operands = [q, k, v, la, gnorm]
    if s0 is not None:
        in_specs.append(pl.BlockSpec((None, sps, A_HEADS, A_DK, A_DV), lambda s, j, l=layer: (l, s, 0, 0, 0)))
        operands.append(s0)
    return pl.pallas_call(
        functools.partial(_gla_kernel, chunk=chunk, block=block, n_chunks=step_rows // chunk,
                          n_seqs=sps, single_step=steps == 1, has_state=s0 is not None),
        out_shape=[jax.ShapeDtypeStruct((n_seq * seq_rows, A_V), F32),
                   jax.ShapeDtypeStruct((n_seq, A_HEADS, A_DK, A_DV), F32)],
        grid=(n_seq // sps, steps),
        in_specs=in_specs,
        out_specs=[row_spec(A_V),
                   pl.BlockSpec((sps, A_HEADS, A_DK, A_DV), lambda s, j: (s, 0, 0, 0))],
        scratch_shapes=[pltpu.VMEM((sps, A_V, A_QK), F32)],
        compiler_params=_cparams(("arbitrary", "arbitrary")),
        name="gla_chunked",
    )(*operands)


SWA_STEP_BLOCKS = MIXER_STEP_ROWS // WINDOW


def _swa_kernel(sink_ref, q_ref, kc_ref, vc_ref, kp_ref, vp_ref, o_ref, *, layer):
    tk = 2 * WINDOW
    kj = lax.broadcasted_iota(jnp.int32, (tk, WINDOW), 0)
    qi = lax.broadcasted_iota(jnp.int32, (tk, WINDOW), 1) + WINDOW
    band = (kj <= qi) & (kj >= qi - WINDOW)
    zeros = jnp.zeros((B_HEAD_DIM, WINDOW), F32)
    chains = [(blk, h) for blk in range(SWA_STEP_BLOCKS) for h in range(B_KV_HEADS)]

    keys, vals_t, q_t, valid = [], [], [], []
    for blk in range(SWA_STEP_BLOCKS):
        cur = slice(blk * WINDOW, (blk + 1) * WINDOW)
        if blk == 0:
            k_prev, v_prev = kp_ref[...], vp_ref[...]
            valid.append(band & (kj >= jnp.where(pl.program_id(1) > 0, 0, WINDOW)))
        else:
            prev = slice((blk - 1) * WINDOW, blk * WINDOW)
            k_prev, v_prev = kc_ref[prev, :], vc_ref[prev, :]
            valid.append(band)
        keys.append(jnp.concatenate([k_prev, kc_ref[cur, :]], axis=0).astype(BF16))
        vals_t.append(jnp.concatenate([v_prev, vc_ref[cur, :]], axis=0).T.astype(BF16))
        q_t.append(q_ref[cur, :].T * (LOG2_E * B_HEAD_DIM ** -0.5))

    scores = {}
    for blk, h in chains:
        cols = []
        for g in range(B_GROUP):
            hd = h * B_GROUP + g
            qg = q_t[blk][hd * B_HEAD_DIM:(hd + 1) * B_HEAD_DIM]
            cols.append(jnp.concatenate([qg, zeros] if h == 0 else [zeros, qg], axis=0))
        scores[blk, h] = _dot(keys[blk], jnp.concatenate(cols, axis=1).astype(BF16))

    probs, inv = {}, {}
    for blk, h in chains:
        ps = []
        for g in range(B_GROUP):
            sink = sink_ref[layer, h * B_GROUP + g] * LOG2_E
            sg = jnp.where(valid[blk], scores[blk, h][:, g * WINDOW:(g + 1) * WINDOW], NEG_BIG)
            m = jnp.maximum(jnp.max(sg, axis=0, keepdims=True), sink)
            p = jnp.exp2(sg - m)
            inv[blk, h, g] = 1.0 / (jnp.sum(p, axis=0, keepdims=True) + jnp.exp2(sink - m))
            ps.append(p.astype(BF16))
        probs[blk, h] = jnp.concatenate(ps, axis=1)

    out_t = {}
    for blk, h in chains:
        out_t[blk, h] = _dot(vals_t[blk][h * B_HEAD_DIM:(h + 1) * B_HEAD_DIM], probs[blk, h])

    for blk in range(SWA_STEP_BLOCKS):
        rows = [out_t[blk, h][:, g * WINDOW:(g + 1) * WINDOW] * inv[blk, h, g]
                for h in range(B_KV_HEADS) for g in range(B_GROUP)]
        o_ref[blk * WINDOW:(blk + 1) * WINDOW, :] = jnp.concatenate(rows, axis=0).T


def _swa_call(layer, sinks, q, k, v, n_seq, blocks_per_seq):
    sb = SWA_STEP_BLOCKS
    assert blocks_per_seq % sb == 0
    steps = blocks_per_seq // sb
    cur = lambda w: pl.BlockSpec((sb * WINDOW, w), lambda s, n, steps=steps: (s * steps + n, 0))
    prev = pl.BlockSpec((WINDOW, B_KV),
                        lambda s, n, nb=blocks_per_seq, sb=sb: (s * nb + jnp.maximum(n * sb - 1, 0), 0))
    return pl.pallas_call(
        functools.partial(_swa_kernel, layer=layer),
        out_shape=jax.ShapeDtypeStruct((n_seq * blocks_per_seq * WINDOW, B_Q), F32),
        grid=(n_seq, steps),
        in_specs=[pl.BlockSpec(memory_space=pltpu.SMEM), cur(B_Q), cur(B_KV), cur(B_KV), prev, prev],
        out_specs=cur(B_Q),
        compiler_params=_cparams(("arbitrary", "arbitrary")),
        name="swa_banded",
    )(sinks, q, k, v, k, v)


SWA_SAMPLE_SEQS = 8
SWA_SAMPLE_ROWS = 8


def _swa_sample_kernel(sink_ref, q_ref, kn_ref, vn_ref, kt_ref, vt_ref, o_ref, nk_ref, nv_ref, *, layer, n_tok):
    pr = SWA_SAMPLE_ROWS
    keep = WINDOW - n_tok
    lane = lax.broadcasted_iota(jnp.int32, (B_HEAD_DIM, WINDOW), 1)
    rows = B_GROUP * pr
    tok = lax.broadcasted_iota(jnp.int32, (rows, WINDOW), 0) % pr
    grp = lax.broadcasted_iota(jnp.int32, (rows, 1), 0) // pr
    pos = lax.broadcasted_iota(jnp.int32, (rows, WINDOW), 1)
    valid_new = pos - keep <= tok
    valid_old = (pos >= tok) & (pos < n_tok)

    def shifted_in(old, new_rows):
        padded = jnp.concatenate([jnp.zeros((WINDOW - pr, B_KV), F32), new_rows], axis=0)
        cols = pltpu.roll(padded, n_tok, axis=0).T
        return [jnp.where(lane < keep, pltpu.roll(old[h], keep, axis=1),
                          cols[h * B_HEAD_DIM:(h + 1) * B_HEAD_DIM]) for h in range(B_KV_HEADS)]

    chains = [(i, h) for i in range(SWA_SAMPLE_SEQS) for h in range(B_KV_HEADS)]
    k_old, v_old, k_new, v_new = {}, {}, {}, {}
    for i in range(SWA_SAMPLE_SEQS):
        rs = slice(i * pr, (i + 1) * pr)
        ko = [kt_ref[i, h] for h in range(B_KV_HEADS)]
        vo = [vt_ref[i, h] for h in range(B_KV_HEADS)]
        kn, vn = shifted_in(ko, kn_ref[rs, :]), shifted_in(vo, vn_ref[rs, :])
        for h in range(B_KV_HEADS):
            nk_ref[i, h] = kn[h]
            nv_ref[i, h] = vn[h]
            k_old[i, h], v_old[i, h] = ko[h].astype(BF16), vo[h].astype(BF16)
            k_new[i, h], v_new[i, h] = kn[h].astype(BF16), vn[h].astype(BF16)

    s_new, s_old = {}, {}
    for c in chains:
        q = (q_ref[c[0], c[1]] * (LOG2_E * B_HEAD_DIM ** -0.5)).astype(BF16)
        s_new[c], s_old[c] = _dot(q, k_new[c]), _dot(q, k_old[c])

    p_new, p_old, inv = {}, {}, {}
    for c in chains:
        sn = jnp.where(valid_new, s_new[c], NEG_BIG)
        so = jnp.where(valid_old, s_old[c], NEG_BIG)
        sink = jnp.zeros((rows, 1), F32)
        for g in range(B_GROUP):
            sink = jnp.where(grp == g, sink_ref[layer, c[1] * B_GROUP + g] * LOG2_E, sink)
        m = jnp.maximum(jnp.maximum(jnp.max(sn, axis=-1, keepdims=True),
                                    jnp.max(so, axis=-1, keepdims=True)), sink)
        pn, po = jnp.exp2(sn - m), jnp.exp2(so - m)
        inv[c] = 1.0 / (jnp.sum(pn, axis=-1, keepdims=True) + jnp.sum(po, axis=-1, keepdims=True)
                        + jnp.exp2(sink - m))
        p_new[c], p_old[c] = pn.astype(BF16), po.astype(BF16)

    for c in chains:
        o = _dot_nt(p_new[c], v_new[c]) + _dot_nt(p_old[c], v_old[c])
        o_ref[c[0], c[1]] = o * inv[c]


def _swa_sample_call(layer, sinks, q, k_new, v_new, kt, vt, n_seq, n_tok):
    sb, pr = SWA_SAMPLE_SEQS, SWA_SAMPLE_ROWS
    qspec = pl.BlockSpec((sb, B_KV_HEADS, B_GROUP * pr, B_HEAD_DIM), lambda s: (s, 0, 0, 0))
    new_rows = pl.BlockSpec((sb * pr, B_KV), lambda s: (s, 0))
    buf_in = pl.BlockSpec((None, sb, B_KV_HEADS, B_HEAD_DIM, WINDOW), lambda s, l=layer: (l, s, 0, 0, 0))
    buf_out = pl.BlockSpec((sb, B_KV_HEADS, B_HEAD_DIM, WINDOW), lambda s: (s, 0, 0, 0))
    buf_sds = jax.ShapeDtypeStruct((n_seq, B_KV_HEADS, B_HEAD_DIM, WINDOW), F32)
    return pl.pallas_call(
        functools.partial(_swa_sample_kernel, layer=layer, n_tok=n_tok),
        out_shape=[jax.ShapeDtypeStruct((n_seq, B_KV_HEADS, B_GROUP * pr, B_HEAD_DIM), F32), buf_sds, buf_sds],
        grid=(n_seq // sb,),
        in_specs=[pl.BlockSpec(memory_space=pltpu.SMEM), qspec, new_rows, new_rows, buf_in, buf_in],
        out_specs=[qspec, buf_out, buf_out],
        compiler_params=_cparams(("arbitrary",)),
        name="swa_decode",
    )(sinks, q, k_new, v_new, kt, vt)


def _s5_prep_kernel(lr_ref, li_ref, ldt_ref, br_ref, bi_ref, ctr_ref, cti_ref,
                    bw_ref, cr_ref, ci_ref, ar_ref, ai_ref, er_ref, ei_ref, fr_ref, fi_ref):
    lr, li = lr_ref[...], li_ref[...]
    dt = jnp.exp(ldt_ref[...])
    rho, th = lr * dt, li * dt
    mag = jnp.exp(rho)
    a_re, a_im = mag * jnp.cos(th), mag * jnp.sin(th)
    den = lr * lr + li * li
    nr, ni = a_re - 1.0, a_im
    f_re = (nr * lr + ni * li) / den
    f_im = (ni * lr - nr * li) / den
    br, bi = br_ref[...], bi_ref[...]
    bb_re = f_re * br - f_im * bi
    bb_im = f_re * bi + f_im * br
    lane_group = lax.broadcasted_iota(jnp.int32, bb_re.shape, 1) // C_STATE
    place = lambda bb: jnp.concatenate(
        [jnp.where(lane_group == g, bb, 0.0) for g in range(C_GROUPS)], axis=0).astype(BF16)
    placed_re, placed_im = place(bb_re), place(bb_im)
    for n in range(S5_N // MXU_TILE):
        src = slice(n * MXU_TILE, (n + 1) * MXU_TILE)
        bw_ref[:, 2 * n * MXU_TILE:(2 * n + 1) * MXU_TILE] = placed_re[:, src]
        bw_ref[:, (2 * n + 1) * MXU_TILE:(2 * n + 2) * MXU_TILE] = placed_im[:, src]
    spread_c = lax.broadcasted_iota(jnp.int32, (LANES, C_WIDTH), 0)
    spread_j = lax.broadcasted_iota(jnp.int32, (LANES, C_WIDTH), 1) % C_GROUP
    spread = jnp.where(spread_c == spread_j, 1.0, 0.0).astype(BF16)
    row_group = lax.broadcasted_iota(jnp.int32, cr_ref.shape, 0) // C_STATE
    col_group = lax.broadcasted_iota(jnp.int32, cr_ref.shape, 1) // C_GROUP
    own = row_group == col_group
    cr_ref[...] = jnp.where(own, _dot(ctr_ref[...].astype(BF16), spread), 0.0).astype(BF16)
    ci_ref[...] = jnp.where(own, _dot(cti_ref[...].astype(BF16), spread), 0.0).astype(BF16)
    ar_ref[...] = a_re
    ai_ref[...] = a_im
    s = (lax.broadcasted_iota(jnp.int32, er_ref.shape, 0) + 1).astype(F32)
    cs, sn = jnp.cos(s * th), jnp.sin(s * th)
    e_mag, f_mag = jnp.exp(-(s * rho)), jnp.exp(s * rho)
    er_ref[...] = e_mag * cs
    ei_ref[...] = -(e_mag * sn)
    fr_ref[...] = f_mag * cs
    fi_ref[...] = f_mag * sn


def _s5_prep_call(lr, li, ldt, b_re_t, b_im_t, c_re_t, c_im_t):
    depth = lr.shape[0]
    lsel = lambda l: (l, 0, 0)
    vec = pl.BlockSpec((None, 1, S5_N), lsel)
    mat = pl.BlockSpec((None, C_GROUP, S5_N), lsel)
    c_in = pl.BlockSpec((None, S5_N, LANES), lsel)
    cmat = pl.BlockSpec((None, S5_N, C_WIDTH), lsel)
    tab = pl.BlockSpec((None, S5_CHUNK, S5_N), lsel)
    sds = lambda r: jax.ShapeDtypeStruct((depth, r, S5_N), F32)
    c_sds = jax.ShapeDtypeStruct((depth, S5_N, C_WIDTH), BF16)
    return pl.pallas_call(
        _s5_prep_kernel,
        out_shape=[jax.ShapeDtypeStruct((depth, C_WIDTH, 2 * S5_N), BF16), c_sds, c_sds,
                   sds(1), sds(1)] + [sds(S5_CHUNK)] * 4,
        grid=(depth,),
        in_specs=[vec, vec, vec, mat, mat, c_in, c_in],
        out_specs=[pl.BlockSpec((None, C_WIDTH, 2 * S5_N), lsel), cmat, cmat, vec, vec, tab, tab, tab, tab],
        compiler_params=_cparams(("arbitrary",)),
        name="s5_discretise",
    )(lr, li, ldt, b_re_t, b_im_t, c_re_t, c_im_t)


def _cmul(ar, ai, br, bi):
    return ar * br - ai * bi, ar * bi + ai * br


def _s5_prompt_kernel(u_ref, bw_ref, cr_ref, ci_ref, d_ref, er_ref, ei_ref, fr_ref, fi_ref,
                      y_ref, hs_ref, car_ref, cai_ref, xr_ref, xi_ref, hr_ref, hi_ref, *, n_chunks):
    step = pl.program_id(1)
    t = S5_CHUNK

    @pl.when(step == 0)
    def _():
        car_ref[...] = jnp.zeros(car_ref.shape, F32)
        cai_ref[...] = jnp.zeros(cai_ref.shape, F32)

    groups = [slice(n * MXU_TILE, (n + 1) * MXU_TILE) for n in range(S5_N // MXU_TILE)]
    halves = [slice(0, LANES), slice(LANES, MXU_TILE)]
    group_cols = [[slice(g.start + hv.start, g.start + hv.stop) for hv in halves] for g in groups]

    u = u_ref[...]
    ub = u.astype(BF16)
    tiles_per_kt = len(groups) // (C_WIDTH // S5_KT)
    for n, cols in enumerate(group_cols):
        rows_k = slice(n // tiles_per_kt * S5_KT, (n // tiles_per_kt + 1) * S5_KT)
        bu = _dot(ub[:, rows_k], bw_ref[rows_k, 2 * n * MXU_TILE:(2 * n + 2) * MXU_TILE])
        for c in range(n_chunks):
            rs = slice(c * t, (c + 1) * t)
            for cs, hv in zip(cols, halves):
                xr, xi = _cmul(er_ref[:, cs], ei_ref[:, cs], bu[rs, hv],
                               bu[rs, MXU_TILE + hv.start:MXU_TILE + hv.stop])
                xr_ref[rs, cs] = xr.astype(BF16)
                xi_ref[rs, cs] = xi.astype(BF16)

    row = lax.broadcasted_iota(jnp.int32, (t, t), 0)
    col = lax.broadcasted_iota(jnp.int32, (t, t), 1)
    tril = jnp.where(col <= row, 1.0, 0.0).astype(BF16)

    carry = [(car_ref[:, g], cai_ref[:, g]) for g in groups]
    for c in range(n_chunks):
        rs = slice(c * t, (c + 1) * t)
        sums = [(_dot(tril, xr_ref[rs, g]), _dot(tril, xi_ref[rs, g])) for g in groups]
        for n, (g, cols) in enumerate(zip(groups, group_cols)):
            sum_re, sum_im = sums[n][0] + carry[n][0], sums[n][1] + carry[n][1]
            hs = [_cmul(fr_ref[:, cs], fi_ref[:, cs], sum_re[:, hv], sum_im[:, hv]) for cs, hv in zip(cols, halves)]
            hr = jnp.concatenate([h[0] for h in hs], axis=1)
            hi = jnp.concatenate([h[1] for h in hs], axis=1)
            hr_ref[rs, g] = hr.astype(BF16)
            hi_ref[rs, g] = hi.astype(BF16)
            carry[n] = (hr[t - 1:t, :], hi[t - 1:t, :])
    for g, (car, cai) in zip(groups, carry):
        car_ref[:, g] = car
        cai_ref[:, g] = cai

    n_out = C_WIDTH // S5_KT
    k_rows = S5_N // n_out
    for j in range(n_out):
        ks = slice(j * k_rows, (j + 1) * k_rows)
        cs = slice(j * S5_KT, (j + 1) * S5_KT)
        y_ref[:, cs] = (_dot(hr_ref[:, ks], cr_ref[ks, cs]) - _dot(hi_ref[:, ks], ci_ref[ks, cs])
                        + d_ref[:, cs] * u[:, cs])

    @pl.when(step == pl.num_programs(1) - 1)
    def _():
        hs_ref[0:1, :] = car_ref[...]
        hs_ref[1:2, :] = cai_ref[...]


def _s5_prompt_call(layer, u, bw, c_re, c_im, d, tabs, n_seq, seq_rows, step_rows):
    steps = seq_rows // step_rows
    row_spec = pl.BlockSpec((step_rows, C_WIDTH), lambda s, j, steps=steps: (s * steps + j, 0))
    lsel = lambda s, j, l=layer: (l, 0, 0)
    tab = pl.BlockSpec((None, S5_CHUNK, S5_N), lsel)
    return pl.pallas_call(
        functools.partial(_s5_prompt_kernel, n_chunks=step_rows // S5_CHUNK),
        out_shape=[jax.ShapeDtypeStruct((n_seq * seq_rows, C_WIDTH), F32),
                   jax.ShapeDtypeStruct((n_seq, 2, S5_N), F32)],
        grid=(n_seq, steps),
        in_specs=[row_spec,
                  _resident((None, C_WIDTH, 2 * S5_N), lsel),
                  _resident((None, S5_N, C_WIDTH), lsel),
                  _resident((None, S5_N, C_WIDTH), lsel),
                  pl.BlockSpec((None, 1, C_WIDTH), lsel),
                  tab, tab, tab, tab],
        out_specs=[row_spec, pl.BlockSpec((None, 2, S5_N), lambda s, j: (s, 0, 0))],
        scratch_shapes=[pltpu.VMEM((1, S5_N), F32), pltpu.VMEM((1, S5_N), F32)]
                       + [pltpu.VMEM((step_rows, S5_N), BF16)] * 4,
        compiler_params=_cparams(("arbitrary", "arbitrary")),
        name="s5_chunked_scan",
    )(u, bw, c_re, c_im, d, *tabs)


def _s5_sample_kernel(u_ref, h0r_ref, h0i_ref, ar_ref, ai_ref, bw_ref, cr_ref, ci_ref, d_ref,
                      y_ref, hr_ref, hi_ref, *, n_tok, n_seq):
    hr, hi = h0r_ref[...], h0i_ref[...]
    a_re, a_im = ar_ref[...], ai_ref[...]
    for t in range(n_tok):
        u = u_ref[t * n_seq:(t + 1) * n_seq, :]
        bu = _dot(u.astype(BF16), bw_ref[...])
        tiles = range(S5_N // MXU_TILE)
        bu_re = jnp.concatenate([bu[:, 2 * n * MXU_TILE:(2 * n + 1) * MXU_TILE] for n in tiles], axis=1)
        bu_im = jnp.concatenate([bu[:, (2 * n + 1) * MXU_TILE:(2 * n + 2) * MXU_TILE] for n in tiles], axis=1)
        hr, hi = _cmul(a_re, a_im, hr, hi)
        hr, hi = hr + bu_re, hi + bu_im
        y_ref[t * n_seq:(t + 1) * n_seq, :] = (_dot(hr.astype(BF16), cr_ref[...]) - _dot(hi.astype(BF16), ci_ref[...])
                                               + d_ref[...] * u)
    hr_ref[...] = hr
    hi_ref[...] = hi


def _s5_sample_call(layer, u, h0_re, h0_im, a_re, a_im, bw, c_re, c_im, d, n_tok, n_seq):
    lsel = lambda i, l=layer: (l, 0, 0)
    full = lambda shape: pl.BlockSpec(shape, lambda i: (0,) * len(shape))
    return pl.pallas_call(
        functools.partial(_s5_sample_kernel, n_tok=n_tok, n_seq=n_seq),
        out_shape=[jax.ShapeDtypeStruct((n_tok * n_seq, C_WIDTH), F32),
                   jax.ShapeDtypeStruct((n_seq, S5_N), F32),
                   jax.ShapeDtypeStruct((n_seq, S5_N), F32)],
        grid=(1,),
        in_specs=[full((n_tok * n_seq, C_WIDTH)),
                  pl.BlockSpec((None, n_seq, S5_N), lsel),
                  pl.BlockSpec((None, n_seq, S5_N), lsel),
                  pl.BlockSpec((None, 1, S5_N), lsel),
                  pl.BlockSpec((None, 1, S5_N), lsel),
                  pl.BlockSpec((None, C_WIDTH, 2 * S5_N), lsel),
                  pl.BlockSpec((None, S5_N, C_WIDTH), lsel),
                  pl.BlockSpec((None, S5_N, C_WIDTH), lsel),
                  pl.BlockSpec((None, 1, C_WIDTH), lsel)],
        out_specs=[full((n_tok * n_seq, C_WIDTH)), full((n_seq, S5_N)), full((n_seq, S5_N))],
        compiler_params=_cparams(("arbitrary",)),
        name="s5_step_scan",
    )(u, h0_re, h0_im, a_re, a_im, bw, c_re, c_im, d)


W_GATES = A_V + 3 * D_MODEL


def _merge_kernel(x_ref, g_ref, shs, shp, scs, scp, gts, gtp,
                  oap, oas, obp, obs, ocp, ocs,
                  wz_ref, wa_ref, wb_ref, wc_ref, wglu_ref, wo_ref, o_ref, *, rows):
    x = x_ref[...]
    br_b = _dot(_select_rows(rows, obp, obs).astype(BF16), wb_ref[...])
    glu = _dot(_gelu_tanh(_select_rows(rows, ocp, ocs)).astype(BF16), wglu_ref[...])
    h = _ada_norm(x, g_ref[...], _mod_value(rows, shs, shp), _mod_value(rows, scs, scp)).astype(BF16)
    z = _dot_nt(h, wz_ref[...])
    oc = glu[:, :C_WIDTH] * _sigmoid(glu[:, C_WIDTH:])
    br_c = _dot(oc.astype(BF16), wc_ref[...])
    oa = _select_rows(rows, oap, oas) * _silu(z[:, :A_V])
    br_a = _dot(oa.astype(BF16), wa_ref[...])
    o = A_V
    mix = (_sigmoid(z[:, o + D_MODEL:o + 2 * D_MODEL]) * br_b
           + _sigmoid(z[:, o + 2 * D_MODEL:o + 3 * D_MODEL]) * br_c
           + _sigmoid(z[:, o:o + D_MODEL]) * br_a)
    out = _dot(mix.astype(BF16), wo_ref[...])
    o_ref[...] = x + _mod_value(rows, gts, gtp) * out


def _merge_call(rows, layer, x, mod, norm_g, oa, ob, oc, wz, wa, wb, wc, wglu, wo):
    in_specs = [pl.BlockSpec((TOKEN_TILE, D_MODEL), lambda i: (i, 0)),
                pl.BlockSpec((None, 1, D_MODEL), lambda i, l=layer: (l, 0, 0))]
    operands = [x, norm_g]
    for j in (3, 4, 5):
        in_specs += _mod_specs(rows, layer, j)
        operands += [mod, mod]
    for pair, w in ((oa, A_V), (ob, B_Q), (oc, C_WIDTH)):
        in_specs += _pair_specs(rows, w)
        operands += list(pair)
    lsel = lambda i, l=layer: (l, 0, 0)
    in_specs += [_resident((None, W_GATES, D_MODEL), lsel),
                 _resident((None, A_V, D_MODEL), lsel),
                 _resident((None, B_Q, D_MODEL), lsel),
                 _resident((None, C_WIDTH, D_MODEL), lsel),
                 _resident((None, C_WIDTH, 2 * C_WIDTH), lsel),
                 _resident((None, D_MODEL, D_MODEL), lsel)]
    operands += [wz, wa, wb, wc, wglu, wo]
    return pl.pallas_call(
        functools.partial(_merge_kernel, rows=rows),
        out_shape=jax.ShapeDtypeStruct((rows.rows, D_MODEL), F32),
        grid=(rows.n_tiles,),
        in_specs=in_specs,
        out_specs=pl.BlockSpec((TOKEN_TILE, D_MODEL), lambda i: (i, 0)),
        compiler_params=_cparams(("arbitrary",)),
        name="branch_merge",
    )(*operands)


def _rotary_tables(rows):
    half = ROPE_DIM // 2
    inv_freq = ROPE_THETA ** (-jnp.arange(half, dtype=F32) / half)
    pos_p = jnp.arange(rows.l, dtype=jnp.int32)
    pos_s = PAST_LEN + jnp.repeat(jnp.arange(rows.t, dtype=jnp.int32), rows.bs)
    pos = jnp.concatenate([pos_p, pos_s])
    ang = pos.astype(F32)[:, None] * inv_freq[None, :]
    cos, sin = jnp.cos(ang), jnp.sin(ang)
    n = pos.shape[0]
    ones = jnp.ones((n, B_HEAD_DIM - ROPE_DIM), F32)
    zeros = jnp.zeros((n, B_HEAD_DIM - ROPE_DIM), F32)
    z8 = jnp.zeros((n, half), F32)
    per_head = lambda parts: jnp.tile(jnp.concatenate(parts, axis=1), (1, LANES // B_HEAD_DIM))
    return (per_head([cos, cos, ones]), per_head([z8, sin, zeros]), per_head([-sin, z8, zeros]))


def _sample_to_seq(a, rows, pad_to):
    w = a.shape[-1]
    a = a.reshape(rows.t, rows.bs, w).transpose(1, 0, 2)
    a = jnp.pad(a, ((0, 0), (0, pad_to - rows.t), (0, 0)))
    return a.reshape(rows.bs * pad_to, w)


def _seq_to_sample(a, rows, pad_to):
    w = a.shape[-1]
    return a.reshape(rows.bs, pad_to, w)[:, :rows.t].transpose(1, 0, 2).reshape(rows.ns_rows, w)


def kernel(x_prompt, x_sample, state_gla, cache_swa_k, cache_swa_v, state_s5_re, state_s5_im, c_prompt, c_sample, w_ada, b_ada, norm_ffn1, ffn1_w_gate, ffn1_w_up, ffn1_w_down, norm_mix, w_in, gla_w_gate_up, gla_b_gate, gla_norm, swa_sinks, s5_a_re, s5_a_im, s5_b_re, s5_b_im, s5_c_re, s5_c_im, s5_d, s5_log_dt, s5_w_glu, w_branch_a, w_branch_b, w_branch_c, w_out, norm_ffn2, ffn2_w_gate, ffn2_w_up, ffn2_w_down, norm_final):
    depth = w_ada.shape[0]
    b, l, _ = x_prompt.shape
    bs, t, _ = x_sample.shape
    rows = _Rows(b, l, bs, t)
    np_rows = rows.np_rows

    x = (x_prompt.reshape(np_rows, D_MODEL), x_sample.transpose(1, 0, 2).reshape(rows.ns_rows, D_MODEL))
    c_all = jnp.concatenate([c_sample, c_prompt, jnp.zeros((8 - b, D_MODEL), F32)], axis=0)
    mod = _mod_call(c_all, w_ada, b_ada)

    pts = [0, A_QK, 2 * A_QK, 2 * A_QK + A_V]
    o_rank = pts[-1]
    o_ar = o_rank + A_GATE_RANK
    o_bq = o_ar + A_V
    o_cu_end = o_bq + B_Q + 2 * B_KV + C_WIDTH
    w_in_t = w_in.transpose(0, 2, 1)
    w_main = jnp.concatenate([w_in_t[:, :o_rank], w_in_t[:, o_bq:o_cu_end]], axis=1).astype(BF16)
    w_rank = jnp.pad(w_in_t[:, o_rank:o_ar], ((0, 0), (0, LANES - A_GATE_RANK), (0, 0))).astype(BF16)
    w_up = jnp.pad(gla_w_gate_up, ((0, 0), (0, LANES - A_GATE_RANK), (0, 0))).astype(BF16)
    w_gates = jnp.concatenate([w_in_t[:, o_ar:o_bq], w_in_t[:, o_cu_end:]], axis=1).astype(BF16)
    cast = lambda w: w.astype(BF16)
    f1g, f1u, f1d = cast(ffn1_w_gate), cast(ffn1_w_up), cast(ffn1_w_down)
    f2g, f2u, f2d = cast(ffn2_w_gate), cast(ffn2_w_up), cast(ffn2_w_down)
    wa, wb, wc, wglu, wo = cast(w_branch_a), cast(w_branch_b), cast(w_branch_c), cast(s5_w_glu), cast(w_out)
    vec3 = lambda a: a.reshape(depth, 1, a.shape[-1])
    n1, nm, n2 = vec3(norm_ffn1), vec3(norm_mix), vec3(norm_ffn2)
    b_gate, g_norm, s5d = vec3(gla_b_gate), vec3(gla_norm), vec3(s5_d)
    rot = _rotary_tables(rows)

    flat = lambda a: a.reshape(depth, 1, S5_N)
    ldt = flat(jnp.broadcast_to(s5_log_dt[:, :, None], (depth, C_GROUPS, C_STATE)))
    to_cols = lambda bm: bm.transpose(0, 3, 1, 2).reshape(depth, C_GROUP, S5_N)
    to_rows = lambda cm: jnp.pad(cm.transpose(0, 1, 3, 2).reshape(depth, S5_N, C_GROUP),
                                 ((0, 0), (0, 0), (0, LANES - C_GROUP)))
    s5_bw, s5_cr, s5_ci, a_re, a_im, e_re, e_im, f_re, f_im = _s5_prep_call(
        flat(s5_a_re), flat(s5_a_im), ldt, to_cols(s5_b_re), to_cols(s5_b_im), to_rows(s5_c_re), to_rows(s5_c_im))
    tabs = (e_re, e_im, f_re, f_im)
    h0_re = state_s5_re.reshape(depth, bs, S5_N)
    h0_im = state_s5_im.reshape(depth, bs, S5_N)
    cache_kt = cache_swa_k.transpose(0, 1, 3, 4, 2)
    cache_vt = cache_swa_v.transpose(0, 1, 3, 4, 2)

    outs = {k: [] for k in ("gla_p", "gla_s", "kp", "ks", "vp", "vs", "rp", "rs", "ip", "is")}
    pad8 = SWA_SAMPLE_ROWS
    for layer in range(depth):
        x = _ffn_call(rows, layer, 0, x, mod, n1, f1g, f1u, f1d)
        gq, gk, gv, la, sq, sk, sv, su = _inproj_call(rows, layer, x, mod, nm, w_main, w_rank, w_up, b_gate, rot)

        oa_p, st_p = _gla_call(layer, gq, gk, gv, la, g_norm, None, b, l, MIXER_STEP_ROWS, GLA_CHUNK, GLA_BLOCK)
        sseq = lambda a: _sample_to_seq(a[np_rows:], rows, pad8)
        oa_s, st_s = _gla_call(layer, sseq(gq), sseq(gk), sseq(gv), sseq(la), g_norm, state_gla,
                               bs, pad8, pad8, pad8, pad8, seqs_per_step=GLA_SAMPLE_SEQS)
        oa_s = _seq_to_sample(oa_s, rows, pad8)

        ob_p = _swa_call(layer, swa_sinks, sq, sk, sv, b, l // WINDOW)
        q_s = sq[np_rows:].reshape(t, bs, B_KV_HEADS, B_GROUP, B_HEAD_DIM).transpose(1, 2, 3, 0, 4)
        q_s = jnp.pad(q_s, ((0, 0), (0, 0), (0, 0), (0, pad8 - t), (0, 0)))
        q_s = q_s.reshape(bs, B_KV_HEADS, B_GROUP * pad8, B_HEAD_DIM)
        ob_s, kt_new, vt_new = _swa_sample_call(layer, swa_sinks, q_s, sseq(sk), sseq(sv), cache_kt, cache_vt, bs, t)
        ob_s = ob_s.reshape(bs, B_KV_HEADS, B_GROUP, pad8, B_HEAD_DIM)[:, :, :, :t]
        ob_s = ob_s.transpose(3, 0, 1, 2, 4).reshape(rows.ns_rows, B_Q)

        oc_p, hs_p = _s5_prompt_call(layer, su, s5_bw, s5_cr, s5_ci, s5d, tabs, b, l, MIXER_STEP_ROWS)
        oc_s, hr_s, hi_s = _s5_sample_call(layer, su[np_rows:], h0_re, h0_im, a_re, a_im,
                                           s5_bw, s5_cr, s5_ci, s5d, t, bs)

        x = _merge_call(rows, layer, x, mod, nm, (oa_p, oa_s), (ob_p, ob_s), (oc_p, oc_s),
                        w_gates, wa, wb, wc, wglu, wo)
        x = _ffn_call(rows, layer, 6, x, mod, n2, f2g, f2u, f2d,
                      final_g=norm_final.reshape(1, D_MODEL) if layer == depth - 1 else None)

        outs["gla_p"].append(st_p)
        outs["gla_s"].append(st_s)
        kv_shape = (b, WINDOW, B_KV_HEADS, B_HEAD_DIM)
        window_rows = lambda a: jnp.stack([a[(i + 1) * l - WINDOW:(i + 1) * l] for i in range(b)])
        outs["kp"].append(window_rows(sk).reshape(kv_shape))
        outs["vp"].append(window_rows(sv).reshape(kv_shape))
        outs["ks"].append(kt_new.transpose(0, 3, 1, 2))
        outs["vs"].append(vt_new.transpose(0, 3, 1, 2))
        outs["rp"].append(hs_p[:, 0].reshape(b, C_GROUPS, C_STATE))
        outs["ip"].append(hs_p[:, 1].reshape(b, C_GROUPS, C_STATE))
        outs["rs"].append(hr_s.reshape(bs, C_GROUPS, C_STATE))
        outs["is"].append(hi_s.reshape(bs, C_GROUPS, C_STATE))

    y_p = x[0].reshape(b, l, D_MODEL)
    y_s = x[1].reshape(t, bs, D_MODEL).transpose(1, 0, 2)
    st = lambda k: jnp.stack(outs[k])
    return (y_p, y_s, st("gla_p"), st("gla_s"), st("kp"), st("ks"), st("vp"), st("vs"),
            st("rp"), st("rs"), st("ip"), st("is"))
```

```python
import functools
import math

import jax
import jax.numpy as jnp
import numpy as np
from jax import lax
from jax.experimental import pallas as pl
from jax.experimental.pallas import tpu as pltpu

D_MODEL = 1024
D_FF = 2816
PAST_LEN = 8192
A_HEADS, A_DK, A_DV = 4, 64, 128
A_GATE_RANK = 16
A_GATE_NORM = 16.0
B_HEADS, B_KV_HEADS, B_HEAD_DIM = 8, 2, 64
B_GROUP = B_HEADS // B_KV_HEADS
WINDOW = 128
ROPE_DIM = B_HEAD_DIM // 4
ROPE_THETA = 500000.0
C_WIDTH, C_GROUP, C_STATE = 512, 16, 64
C_GROUPS = C_WIDTH // C_GROUP
MACARON_W = 0.5
NORM_EPS = 1e-6
N_MOD = 9

A_QK = A_HEADS * A_DK
A_V = A_HEADS * A_DV
B_Q = B_HEADS * B_HEAD_DIM
B_KV = B_KV_HEADS * B_HEAD_DIM
S5_N = C_GROUPS * C_STATE

LANES = 128
TOKEN_TILE = 512
MIXER_STEP_ROWS = 1024
GLA_CHUNK = 64
GLA_BLOCK = 32
GLA_SAMPLE_SEQS = 8
S5_CHUNK = 64
S5_KT = 256
VMEM_LIMIT = 56 * 1024 * 1024

F32 = jnp.float32
BF16 = jnp.bfloat16
NEG_BIG = -1e30
LOG2_E = math.log2(math.e)


def _cparams(sem):
    return pltpu.CompilerParams(dimension_semantics=sem, vmem_limit_bytes=VMEM_LIMIT)


def _dot(a, b):
    return jnp.dot(a, b, preferred_element_type=F32)


def _dot_nt(a, b):
    return lax.dot_general(a, b, (((1,), (1,)), ((), ())), preferred_element_type=F32)


def _sigmoid(x):
    return 1.0 / (1.0 + jnp.exp(-x))


def _silu(x):
    return x * _sigmoid(x)


def _gelu_tanh(x):
    return 0.5 * x * (1.0 + jnp.tanh(math.sqrt(2.0 / math.pi) * (x + 0.044715 * (x * x * x))))


def _log_sigmoid(x):
    return jnp.minimum(x, 0.0) - jnp.log1p(jnp.exp(-jnp.abs(x)))


def _resident(shape, index_map):
    return pl.BlockSpec(shape, index_map, pipeline_mode=pl.Buffered(1))


def _mod_kernel(c_ref, w_ref, b_ref, o_ref):
    c = c_ref[...]
    s = _silu(c).astype(BF16)
    o_ref[...] = _dot(s, w_ref[...].astype(BF16)) + b_ref[...]


MOD_BLOCK = 3 * D_MODEL


def _mod_call(c_all, w_ada, b_ada):
    depth = w_ada.shape[0]
    rows = c_all.shape[0]
    return pl.pallas_call(
        _mod_kernel,
        out_shape=jax.ShapeDtypeStruct((depth, rows, N_MOD * D_MODEL), F32),
        grid=(depth, N_MOD * D_MODEL // MOD_BLOCK),
        in_specs=[
            pl.BlockSpec((rows, D_MODEL), lambda l, j: (0, 0)),
            pl.BlockSpec((None, D_MODEL, MOD_BLOCK), lambda l, j: (l, 0, j)),
            pl.BlockSpec((None, 1, MOD_BLOCK), lambda l, j: (l, 0, j)),
        ],
        out_specs=pl.BlockSpec((None, rows, MOD_BLOCK), lambda l, j: (l, 0, j)),
        compiler_params=_cparams(("arbitrary", "arbitrary")),
        name="ada_mod",
    )(c_all, w_ada, b_ada.reshape(depth, 1, N_MOD * D_MODEL))


class _Rows:
    def __init__(self, n_prompt_seq, seq_len, n_sample_seq, n_sample_tok):
        self.b = n_prompt_seq
        self.l = seq_len
        self.bs = n_sample_seq
        self.t = n_sample_tok
        self.np_rows = n_prompt_seq * seq_len
        self.ns_rows = n_sample_seq * n_sample_tok
        assert seq_len % TOKEN_TILE == 0 and self.ns_rows == TOKEN_TILE
        assert n_sample_seq % 8 == 0 and seq_len % MIXER_STEP_ROWS == 0
        self.rows = self.np_rows + self.ns_rows
        self.n_prompt_tiles = self.np_rows // TOKEN_TILE
        self.tiles_per_seq = seq_len // TOKEN_TILE
        self.n_tiles = self.n_prompt_tiles + 1
        self.mod_rows = n_sample_seq + 8
        assert n_prompt_seq <= 8


def _mod_specs(rows, layer, j):
    return [
        pl.BlockSpec((None, rows.bs, D_MODEL), lambda i, l=layer, j=j: (l, 0, j)),
        pl.BlockSpec((None, 8, D_MODEL), lambda i, l=layer, j=j, r=rows.bs // 8: (l, r, j)),
    ]


def _mod_value(rows, ms_ref, mp_ref):
    i = pl.program_id(0)
    is_sample = i == rows.n_prompt_tiles
    seq = jnp.minimum(i // rows.tiles_per_seq, rows.b - 1)
    s = ms_ref[...]
    s = jnp.concatenate([s] * rows.t, axis=0)
    p = mp_ref[pl.ds(seq, 1), :]
    return jnp.where(is_sample, s, p)


def _ada_norm(x, g, shift, scale):
    var = jnp.mean(x * x, axis=-1, keepdims=True)
    return (x * lax.rsqrt(var + NORM_EPS)) * g * (1.0 + scale) + shift


def _select_rows(rows, p_ref, s_ref):
    is_sample = pl.program_id(0) == rows.n_prompt_tiles
    return jnp.where(is_sample, s_ref[...], p_ref[...])


def _pair_specs(rows, width):
    last = rows.n_prompt_tiles - 1
    return [
        pl.BlockSpec((TOKEN_TILE, width), lambda i, last=last: (jnp.minimum(i, last), 0)),
        pl.BlockSpec((TOKEN_TILE, width), lambda i: (0, 0)),
    ]


MXU_TILE = 256
FF_SPLITS = (0, 6 * MXU_TILE, D_FF)
assert D_FF % MXU_TILE == 0


def _ffn_kernel(*refs, rows, x_pair, final_norm):
    n_in = 11 + int(x_pair) + int(final_norm)
    ins, outs = refs[:n_in], refs[n_in:]
    if x_pair:
        x = _select_rows(rows, ins[0], ins[1])
    else:
        x = ins[0][...]
    g_ref, shs, shp, scs, scp, gts, gtp, wg_ref, wu_ref, wd_ref = ins[1 + int(x_pair):11 + int(x_pair)]
    h = _ada_norm(x, g_ref[...], _mod_value(rows, shs, shp), _mod_value(rows, scs, scp)).astype(BF16)
    chunks = [slice(lo, hi) for lo, hi in zip(FF_SPLITS[:-1], FF_SPLITS[1:])]
    gate_up = [(_dot(h, wg_ref[:, sl]), _dot(h, wu_ref[:, sl])) for sl in chunks]
    act = [(_silu(gate) * up).astype(BF16) for gate, up in gate_up]
    acc = _dot(act[0], wd_ref[chunks[0], :])
    for a, sl in zip(act[1:], chunks[1:]):
        acc = acc + _dot(a, wd_ref[sl, :])
    out = x + (MACARON_W * _mod_value(rows, gts, gtp)) * acc
    if not final_norm:
        outs[0][...] = out
        return
    var = jnp.mean(out * out, axis=-1, keepdims=True)
    y = out * lax.rsqrt(var + NORM_EPS) * ins[-1][...]
    is_sample = pl.program_id(0) == rows.n_prompt_tiles

    @pl.when(jnp.logical_not(is_sample))
    def _():
        outs[0][...] = y

    @pl.when(is_sample)
    def _():
        outs[1][...] = y


def _ffn_call(rows, layer, mod_base, x, mod, norm_g, wg, wu, wd, final_g=None):
    x_pair = isinstance(x, tuple)
    if x_pair:
        in_specs = _pair_specs(rows, D_MODEL)
        operands = list(x)
    else:
        in_specs = [pl.BlockSpec((TOKEN_TILE, D_MODEL), lambda i: (i, 0))]
        operands = [x]
    in_specs.append(pl.BlockSpec((None, 1, D_MODEL), lambda i, l=layer: (l, 0, 0)))
    operands.append(norm_g)
    for j in range(3):
        in_specs += _mod_specs(rows, layer, mod_base + j)
        operands += [mod, mod]
    in_specs += [
        _resident((None, D_MODEL, D_FF), lambda i, l=layer: (l, 0, 0)),
        _resident((None, D_MODEL, D_FF), lambda i, l=layer: (l, 0, 0)),
        _resident((None, D_FF, D_MODEL), lambda i, l=layer: (l, 0, 0)),
    ]
    operands += [wg, wu, wd]
    if final_g is None:
        out_shape = jax.ShapeDtypeStruct((rows.rows, D_MODEL), F32)
        out_specs = pl.BlockSpec((TOKEN_TILE, D_MODEL), lambda i: (i, 0))
    else:
        in_specs.append(pl.BlockSpec((1, D_MODEL), lambda i: (0, 0)))
        operands.append(final_g)
        out_shape = [jax.ShapeDtypeStruct((rows.np_rows, D_MODEL), F32),
                     jax.ShapeDtypeStruct((rows.ns_rows, D_MODEL), F32)]
        out_specs = _pair_specs(rows, D_MODEL)
    return pl.pallas_call(
        functools.partial(_ffn_kernel, rows=rows, x_pair=x_pair, final_norm=final_g is not None),
        out_shape=out_shape,
        grid=(rows.n_tiles,),
        in_specs=in_specs,
        out_specs=out_specs,
        compiler_params=_cparams(("arbitrary",)),
        name="ada_swiglu",
    )(*operands)


IN_OFF_RANK = 2 * A_QK + A_V
IN_OFF_AR = IN_OFF_RANK + A_GATE_RANK
IN_OFF_BQ = IN_OFF_AR + A_V
IN_OFF_GL = IN_OFF_BQ + B_Q + 2 * B_KV + C_WIDTH
IN_WIDTH = IN_OFF_GL + 3 * D_MODEL


def _rotary(x, cos, s_up, s_dn):
    half = ROPE_DIM // 2
    return x * cos + pltpu.roll(x, half, axis=1) * s_up + pltpu.roll(x, LANES - half, axis=1) * s_dn


def _inproj_kernel(x_ref, g_ref, shs, shp, scs, scp, w_ref, wup_ref, bg_ref,
                   cos_ref, sup_ref, sdn_ref,
                   gq_ref, gk_ref, gv_ref, la_ref, sq_ref, sk_ref, sv_ref, su_ref, *, rows):
    x = x_ref[...]
    h = _ada_norm(x, g_ref[...], _mod_value(rows, shs, shp), _mod_value(rows, scs, scp)).astype(BF16)
    za = _dot_nt(h, w_ref[0:IN_OFF_RANK, :])
    gq_ref[...] = za[:, 0:A_QK] * (A_DK ** -0.5)
    gk_ref[...] = za[:, A_QK:2 * A_QK]
    gv_ref[...] = za[:, 2 * A_QK:]
    z = _dot_nt(h, w_ref[IN_OFF_BQ:IN_OFF_GL, :])
    o = 0
    cos, s_up, s_dn = cos_ref[...], sup_ref[...], sdn_ref[...]
    for c in range(B_Q // LANES):
        sq_ref[:, c * LANES:(c + 1) * LANES] = _rotary(z[:, o + c * LANES:o + (c + 1) * LANES], cos, s_up, s_dn)
    o += B_Q
    sk_ref[...] = _rotary(z[:, o:o + B_KV], cos, s_up, s_dn)
    o += B_KV
    sv_ref[...] = z[:, o:o + B_KV]
    o += B_KV
    su_ref[...] = z[:, o:o + C_WIDTH]
    low = _dot_nt(h, w_ref[IN_OFF_RANK:IN_OFF_RANK + LANES, :]).astype(BF16)
    pre = _dot(low, wup_ref[...]) + bg_ref[...]
    la_ref[...] = _log_sigmoid(pre) * (LOG2_E / A_GATE_NORM)


def _inproj_call(rows, layer, x, mod, norm_g, w_in_t, w_up, b_gate, rot):
    row_spec = lambda w: pl.BlockSpec((TOKEN_TILE, w), lambda i: (i, 0))
    rot_spec = pl.BlockSpec(
        (TOKEN_TILE, LANES),
        lambda i, n=rows.n_prompt_tiles, per=rows.tiles_per_seq: (jnp.where(i < n, i % per, per), 0))
    in_specs = [row_spec(D_MODEL), pl.BlockSpec((None, 1, D_MODEL), lambda i, l=layer: (l, 0, 0))]
    operands = [x, norm_g]
    for j in (3, 4):
        in_specs += _mod_specs(rows, layer, j)
        operands += [mod, mod]
    in_specs += [
        _resident((None, IN_WIDTH, D_MODEL), lambda i, l=layer: (l, 0, 0)),
        _resident((None, LANES, A_QK), lambda i, l=layer: (l, 0, 0)),
        pl.BlockSpec((None, 1, A_QK), lambda i, l=layer: (l, 0, 0)),
        rot_spec, rot_spec, rot_spec,
    ]
    operands += [w_in_t, w_up, b_gate, *rot]
    widths = (A_QK, A_QK, A_V, A_QK, B_Q, B_KV, B_KV, C_WIDTH)
    return pl.pallas_call(
        functools.partial(_inproj_kernel, rows=rows),
        out_shape=[jax.ShapeDtypeStruct((rows.rows, w), F32) for w in widths],
        grid=(rows.n_tiles,),
        in_specs=in_specs,
        out_specs=[row_spec(w) for w in widths],
        compiler_params=_cparams(("arbitrary",)),
        name="mixer_in_proj",
    )(*operands)


def _cumsum_rows(x):
    n = x.shape[0]
    row = lax.broadcasted_iota(jnp.int32, x.shape, 0)
    s = 1
    while s < n:
        x = x + jnp.where(row >= s, pltpu.roll(x, s, axis=0), 0.0)
        s *= 2
    return x


def _head_blockdiag(x, lanes_per_head):
    lane_head = lax.broadcasted_iota(jnp.int32, x.shape, 1) // lanes_per_head
    return jnp.concatenate([jnp.where(lane_head == h, x, 0.0) for h in range(A_HEADS)], axis=0)


def _gla_kernel(*refs, chunk, block, n_chunks, n_seqs, single_step, has_state):
    if has_state:
        q_ref, k_ref, v_ref, la_ref, gn_ref, s0_ref, o_ref, so_ref, st_ref = refs
    else:
        q_ref, k_ref, v_ref, la_ref, gn_ref, o_ref, so_ref, st_ref = refs
    n_blocks = chunk // block
    st_shape = (A_V, A_QK)
    st_row_head = lax.broadcasted_iota(jnp.int32, st_shape, 0) // A_DV
    st_lane_head = lax.broadcasted_iota(jnp.int32, st_shape, 1) // A_DK
    st_mask = st_row_head == st_lane_head
    srow = lax.broadcasted_iota(jnp.int32, (block, A_HEADS * block), 0)
    scol = lax.broadcasted_iota(jnp.int32, (block, A_HEADS * block), 1) % block
    causal = scol <= srow
    gn = gn_ref[...]

    def init_state(sq):
        st_ref[sq] = jnp.zeros(st_shape, F32)
        if has_state:
            for h in range(A_HEADS):
                st_ref[sq, h * A_DV:(h + 1) * A_DV, h * A_DK:(h + 1) * A_DK] = s0_ref[sq, h].T

    def emit_state(sq):
        for h in range(A_HEADS):
            so_ref[sq, h] = st_ref[sq, h * A_DV:(h + 1) * A_DV, h * A_DK:(h + 1) * A_DK].T

    if single_step:
        for sq in range(n_seqs):
            init_state(sq)
    else:
        pl.when(pl.program_id(1) == 0)(lambda: init_state(0))

    units = [(sq, c) for sq in range(n_seqs) for c in range(n_chunks)]
    rows_of = {u: slice((u[0] * n_chunks + u[1]) * chunk, (u[0] * n_chunks + u[1] + 1) * chunk) for u in units}

    q, k, v, b = {}, {}, {}, {}
    for u in units:
        q[u], k[u], v[u] = q_ref[rows_of[u], :], k_ref[rows_of[u], :], v_ref[rows_of[u], :]
        b[u] = _cumsum_rows(la_ref[rows_of[u], :])

    s_diag, s_off, upd, q_in = {}, {}, {}, {}
    for u in units:
        for i in range(n_blocks):
            lo, hi = i * block, (i + 1) * block
            qi, ki, bi = q[u][lo:hi], k[u][lo:hi], b[u][lo:hi]
            mid = b[u][lo + block // 2 - 1:lo + block // 2, :]
            qd = (qi * jnp.exp2(bi - mid)).astype(BF16)
            kd = _head_blockdiag(ki * jnp.exp2(mid - bi), A_DK).astype(BF16)
            s_diag[u, i] = _dot_nt(qd, kd)
            if i > 0:
                ref = b[u][lo - 1:lo, :]
                qo = (qi * jnp.exp2(bi - ref)).astype(BF16)
                ko = _head_blockdiag(k[u][:lo] * jnp.exp2(ref - b[u][:lo]), A_DK).astype(BF16)
                s_off[u, i] = _dot_nt(qo, ko)
        b_last = b[u][chunk - 1:chunk, :]
        upd[u] = _dot(v[u].T.astype(BF16), (k[u] * jnp.exp2(b_last - b[u])).astype(BF16))
        q_in[u] = (q[u] * jnp.exp2(b[u])).astype(BF16)

    intra = {}
    for u in units:
        outs = []
        for i in range(n_blocks):
            lo, hi = i * block, (i + 1) * block
            att = jnp.where(causal, s_diag[u, i], 0.0).astype(BF16)
            oi = _dot(att, _head_blockdiag(v[u][lo:hi], A_DV).astype(BF16))
            if i > 0:
                oi = oi + _dot(s_off[u, i].astype(BF16), _head_blockdiag(v[u][:lo], A_DV).astype(BF16))
            outs.append(oi)
        intra[u] = jnp.concatenate(outs, axis=0) if n_blocks > 1 else outs[0]

    for u in units:
        sq = u[0]
        st = st_ref[sq]
        o = _dot_nt(q_in[u], st.astype(BF16)) + intra[u]
        st_ref[sq] = st * jnp.exp2(b[u][chunk - 1:chunk, :]) + jnp.where(st_mask, upd[u], 0.0)
        normed = []
        for h in range(A_HEADS):
            oh = o[:, h * A_DV:(h + 1) * A_DV]
            var = jnp.mean(oh * oh, axis=-1, keepdims=True)
            normed.append(oh * lax.rsqrt(var + NORM_EPS) * gn)
        o_ref[rows_of[u], :] = jnp.concatenate(normed, axis=1)

    if single_step:
        for sq in range(n_seqs):
            emit_state(sq)
    else:
        pl.when(pl.program_id(1) == pl.num_programs(1) - 1)(lambda: emit_state(0))


def _gla_call(layer, q, k, v, la, gnorm, s0, n_seq, seq_rows, step_rows, chunk, block, seqs_per_step=1,
              col_blocks=(0, 0, 0, 0)):
    steps = seq_rows // step_rows
    sps = seqs_per_step
    assert sps == 1 or steps == 1
    blk_rows = sps * step_rows
    row_spec = lambda w, cb=0: pl.BlockSpec((blk_rows, w), lambda s, j, steps=steps, cb=cb: (s * steps + j, cb))
    in_specs = [row_spec(A_QK, col_blocks[0]), row_spec(A_QK, col_blocks[1]), row_spec(A_V, col_blocks[2]),
                row_spec(A_QK, col_blocks[3]),
                pl.BlockSpec((None, 1, A_DV), lambda s, j, l=layer: (l, 0, 0))]
    operands = [q, k, v, la, gnorm]
    if s0 is not None:
        in_specs.append(pl.BlockSpec((None, sps, A_HEADS, A_DK, A_DV), lambda s, j, l=layer: (l, s, 0, 0, 0)))
        operands.append(s0)
    return pl.pallas_call(
        functools.partial(_gla_kernel, chunk=chunk, block=block, n_chunks=step_rows // chunk,
                          n_seqs=sps, single_step=steps == 1, has_state=s0 is not None),
        out_shape=[jax.ShapeDtypeStruct((n_seq * seq_rows, A_V), F32),
                   jax.ShapeDtypeStruct((n_seq, A_HEADS, A_DK, A_DV), F32)],
        grid=(n_seq // sps, steps),
        in_specs=in_specs,
        out_specs=[row_spec(A_V),
                   pl.BlockSpec((sps, A_HEADS, A_DK, A_DV), lambda s, j: (s, 0, 0, 0))],
        scratch_shapes=[pltpu.VMEM((sps, A_V, A_QK), F32)],
        compiler_params=_cparams(("arbitrary", "arbitrary")),
        name="gla_chunked",
    )(*operands)


SWA_STEP_BLOCKS = MIXER_STEP_ROWS // WINDOW


def _swa_kernel(sink_ref, q_ref, kc_ref, vc_ref, kp_ref, vp_ref, o_ref, *, layer):
    tk = 2 * WINDOW
    kj = lax.broadcasted_iota(jnp.int32, (tk, WINDOW), 0)
    qi = lax.broadcasted_iota(jnp.int32, (tk, WINDOW), 1) + WINDOW
    band = (kj <= qi) & (kj >= qi - WINDOW)
    zeros = jnp.zeros((B_HEAD_DIM, WINDOW), F32)
    chains = [(blk, h) for blk in range(SWA_STEP_BLOCKS) for h in range(B_KV_HEADS)]

    keys, vals_t, q_t, valid = [], [], [], []
    for blk in range(SWA_STEP_BLOCKS):
        cur = slice(blk * WINDOW, (blk + 1) * WINDOW)
        if blk == 0:
            k_prev, v_prev = kp_ref[...], vp_ref[...]
            valid.append(band & (kj >= jnp.where(pl.program_id(1) > 0, 0, WINDOW)))
        else:
            prev = slice((blk - 1) * WINDOW, blk * WINDOW)
            k_prev, v_prev = kc_ref[prev, :], vc_ref[prev, :]
            valid.append(band)
        keys.append(jnp.concatenate([k_prev, kc_ref[cur, :]], axis=0).astype(BF16))
        vals_t.append(jnp.concatenate([v_prev, vc_ref[cur, :]], axis=0).T.astype(BF16))
        q_t.append(q_ref[cur, :].T * (LOG2_E * B_HEAD_DIM ** -0.5))

    scores = {}
    for blk, h in chains:
        cols = []
        for g in range(B_GROUP):
            hd = h * B_GROUP + g
            qg = q_t[blk][hd * B_HEAD_DIM:(hd + 1) * B_HEAD_DIM]
            cols.append(jnp.concatenate([qg, zeros] if h == 0 else [zeros, qg], axis=0))
        scores[blk, h] = _dot(keys[blk], jnp.concatenate(cols, axis=1).astype(BF16))

    probs, inv = {}, {}
    for blk, h in chains:
        ps = []
        for g in range(B_GROUP):
            sink = sink_ref[layer, h * B_GROUP + g] * LOG2_E
            sg = jnp.where(valid[blk], scores[blk, h][:, g * WINDOW:(g + 1) * WINDOW], NEG_BIG)
            m = jnp.maximum(jnp.max(sg, axis=0, keepdims=True), sink)
            p = jnp.exp2(sg - m)
            inv[blk, h, g] = 1.0 / (jnp.sum(p, axis=0, keepdims=True) + jnp.exp2(sink - m))
            ps.append(p.astype(BF16))
        probs[blk, h] = jnp.concatenate(ps, axis=1)

    out_t = {}
    for blk, h in chains:
        out_t[blk, h] = _dot(vals_t[blk][h * B_HEAD_DIM:(h + 1) * B_HEAD_DIM], probs[blk, h])

    for blk in range(SWA_STEP_BLOCKS):
        rows = [out_t[blk, h][:, g * WINDOW:(g + 1) * WINDOW] * inv[blk, h, g]
                for h in range(B_KV_HEADS) for g in range(B_GROUP)]
        o_ref[blk * WINDOW:(blk + 1) * WINDOW, :] = jnp.concatenate(rows, axis=0).T


def _swa_call(layer, sinks, q, k, v, n_seq, blocks_per_seq):
    sb = SWA_STEP_BLOCKS
    assert blocks_per_seq % sb == 0
    steps = blocks_per_seq // sb
    cur = lambda w: pl.BlockSpec((sb * WINDOW, w), lambda s, n, steps=steps: (s * steps + n, 0))
    prev = pl.BlockSpec((WINDOW, B_KV),
                        lambda s, n, nb=blocks_per_seq, sb=sb: (s * nb + jnp.maximum(n * sb - 1, 0), 0))
    return pl.pallas_call(
        functools.partial(_swa_kernel, layer=layer),
        out_shape=jax.ShapeDtypeStruct((n_seq * blocks_per_seq * WINDOW, B_Q), F32),
        grid=(n_seq, steps),
        in_specs=[pl.BlockSpec(memory_space=pltpu.SMEM), cur(B_Q), cur(B_KV), cur(B_KV), prev, prev],
        out_specs=cur(B_Q),
        compiler_params=_cparams(("arbitrary", "arbitrary")),
        name="swa_banded",
    )(sinks, q, k, v, k, v)


SWA_SAMPLE_SEQS = 8
SWA_SAMPLE_ROWS = 8


def _swa_sample_kernel(sink_ref, q_ref, kn_ref, vn_ref, kt_ref, vt_ref, o_ref, nk_ref, nv_ref, *, layer, n_tok):
    pr = SWA_SAMPLE_ROWS
    keep = WINDOW - n_tok
    lane = lax.broadcasted_iota(jnp.int32, (B_HEAD_DIM, WINDOW), 1)
    rows = B_GROUP * pr
    tok = lax.broadcasted_iota(jnp.int32, (rows, WINDOW), 0) % pr
    grp = lax.broadcasted_iota(jnp.int32, (rows, 1), 0) // pr
    pos = lax.broadcasted_iota(jnp.int32, (rows, WINDOW), 1)
    valid_new = pos - keep <= tok
    valid_old = (pos >= tok) & (pos < n_tok)

    def shifted_in(old, new_rows):
        padded = jnp.concatenate([jnp.zeros((WINDOW - pr, B_KV), F32), new_rows], axis=0)
        cols = pltpu.roll(padded, n_tok, axis=0).T
        return [jnp.where(lane < keep, pltpu.roll(old[h], keep, axis=1),
                          cols[h * B_HEAD_DIM:(h + 1) * B_HEAD_DIM]) for h in range(B_KV_HEADS)]

    chains = [(i, h) for i in range(SWA_SAMPLE_SEQS) for h in range(B_KV_HEADS)]
    k_old, v_old, k_new, v_new = {}, {}, {}, {}
    for i in range(SWA_SAMPLE_SEQS):
        rs = slice(i * pr, (i + 1) * pr)
        ko = [kt_ref[i, h] for h in range(B_KV_HEADS)]
        vo = [vt_ref[i, h] for h in range(B_KV_HEADS)]
        kn, vn = shifted_in(ko, kn_ref[rs, :]), shifted_in(vo, vn_ref[rs, :])
        for h in range(B_KV_HEADS):
            nk_ref[i, h] = kn[h]
            nv_ref[i, h] = vn[h]
            k_old[i, h], v_old[i, h] = ko[h].astype(BF16), vo[h].astype(BF16)
            k_new[i, h], v_new[i, h] = kn[h].astype(BF16), vn[h].astype(BF16)

    s_new, s_old = {}, {}
    for c in chains:
        q = (q_ref[c[0], c[1]] * (LOG2_E * B_HEAD_DIM ** -0.5)).astype(BF16)
        s_new[c], s_old[c] = _dot(q, k_new[c]), _dot(q, k_old[c])

    p_new, p_old, inv = {}, {}, {}
    for c in chains:
        sn = jnp.where(valid_new, s_new[c], NEG_BIG)
        so = jnp.where(valid_old, s_old[c], NEG_BIG)
        sink = jnp.zeros((rows, 1), F32)
        for g in range(B_GROUP):
            sink = jnp.where(grp == g, sink_ref[layer, c[1] * B_GROUP + g] * LOG2_E, sink)
        m = jnp.maximum(jnp.maximum(jnp.max(sn, axis=-1, keepdims=True),
                                    jnp.max(so, axis=-1, keepdims=True)), sink)
        pn, po = jnp.exp2(sn - m), jnp.exp2(so - m)
        inv[c] = 1.0 / (jnp.sum(pn, axis=-1, keepdims=True) + jnp.sum(po, axis=-1, keepdims=True)
                        + jnp.exp2(sink - m))
        p_new[c], p_old[c] = pn.astype(BF16), po.astype(BF16)

    for c in chains:
        o = _dot_nt(p_new[c], v_new[c]) + _dot_nt(p_old[c], v_old[c])
        o_ref[c[0], c[1]] = o * inv[c]


def _swa_sample_call(layer, sinks, q, k_new, v_new, kt, vt, n_seq, n_tok, col_blocks=(0, 0)):
    sb, pr = SWA_SAMPLE_SEQS, SWA_SAMPLE_ROWS
    qspec = pl.BlockSpec((sb, B_KV_HEADS, B_GROUP * pr, B_HEAD_DIM), lambda s: (s, 0, 0, 0))
    new_rows = lambda cb: pl.BlockSpec((sb * pr, B_KV), lambda s, cb=cb: (s, cb))
    buf_in = pl.BlockSpec((None, sb, B_KV_HEADS, B_HEAD_DIM, WINDOW), lambda s, l=layer: (l, s, 0, 0, 0))
    buf_out = pl.BlockSpec((sb, B_KV_HEADS, B_HEAD_DIM, WINDOW), lambda s: (s, 0, 0, 0))
    buf_sds = jax.ShapeDtypeStruct((n_seq, B_KV_HEADS, B_HEAD_DIM, WINDOW), F32)
    return pl.pallas_call(
        functools.partial(_swa_sample_kernel, layer=layer, n_tok=n_tok),
        out_shape=[jax.ShapeDtypeStruct((n_seq, B_KV_HEADS, B_GROUP * pr, B_HEAD_DIM), F32), buf_sds, buf_sds],
        grid=(n_seq // sb,),
        in_specs=[pl.BlockSpec(memory_space=pltpu.SMEM), qspec, new_rows(col_blocks[0]), new_rows(col_blocks[1]),
                  buf_in, buf_in],
        out_specs=[qspec, buf_out, buf_out],
        compiler_params=_cparams(("arbitrary",)),
        name="swa_decode",
    )(sinks, q, k_new, v_new, kt, vt)


def _s5_prep_kernel(lr_ref, li_ref, ldt_ref, br_ref, bi_ref, ctr_ref, cti_ref,
                    bw_ref, cr_ref, ci_ref, ar_ref, ai_ref, er_ref, ei_ref, fr_ref, fi_ref):
    lr, li = lr_ref[...], li_ref[...]
    dt = jnp.exp(ldt_ref[...])
    rho, th = lr * dt, li * dt
    mag = jnp.exp(rho)
    a_re, a_im = mag * jnp.cos(th), mag * jnp.sin(th)
    den = lr * lr + li * li
    nr, ni = a_re - 1.0, a_im
    f_re = (nr * lr + ni * li) / den
    f_im = (ni * lr - nr * li) / den
    br, bi = br_ref[...], bi_ref[...]
    bb_re = f_re * br - f_im * bi
    bb_im = f_re * bi + f_im * br
    lane_group = lax.broadcasted_iota(jnp.int32, bb_re.shape, 1) // C_STATE
    place = lambda bb: jnp.concatenate(
        [jnp.where(lane_group == g, bb, 0.0) for g in range(C_GROUPS)], axis=0).astype(BF16)
    placed_re, placed_im = place(bb_re), place(bb_im)
    for n in range(S5_N // MXU_TILE):
        src = slice(n * MXU_TILE, (n + 1) * MXU_TILE)
        bw_ref[:, 2 * n * MXU_TILE:(2 * n + 1) * MXU_TILE] = placed_re[:, src]
        bw_ref[:, (2 * n + 1) * MXU_TILE:(2 * n + 2) * MXU_TILE] = placed_im[:, src]
    spread_c = lax.broadcasted_iota(jnp.int32, (LANES, C_WIDTH), 0)
    spread_j = lax.broadcasted_iota(jnp.int32, (LANES, C_WIDTH), 1) % C_GROUP
    spread = jnp.where(spread_c == spread_j, 1.0, 0.0).astype(BF16)
    row_group = lax.broadcasted_iota(jnp.int32, cr_ref.shape, 0) // C_STATE
    col_group = lax.broadcasted_iota(jnp.int32, cr_ref.shape, 1) // C_GROUP
    own = row_group == col_group
    cr_ref[...] = jnp.where(own, _dot(ctr_ref[...].astype(BF16), spread), 0.0).astype(BF16)
    ci_ref[...] = jnp.where(own, _dot(cti_ref[...].astype(BF16), spread), 0.0).astype(BF16)
    ar_ref[...] = a_re
    ai_ref[...] = a_im
    s = (lax.broadcasted_iota(jnp.int32, er_ref.shape, 0) + 1).astype(F32)
    cs, sn = jnp.cos(s * th), jnp.sin(s * th)
    e_mag, f_mag = jnp.exp(-(s * rho)), jnp.exp(s * rho)
    er_ref[...] = e_mag * cs
    ei_ref[...] = -(e_mag * sn)
    fr_ref[...] = f_mag * cs
    fi_ref[...] = f_mag * sn


def _s5_prep_call(lr, li, ldt, b_re_t, b_im_t, c_re_t, c_im_t):
    depth = lr.shape[0]
    lsel = lambda l: (l, 0, 0)
    vec = pl.BlockSpec((None, 1, S5_N), lsel)
    mat = pl.BlockSpec((None, C_GROUP, S5_N), lsel)
    c_in = pl.BlockSpec((None, S5_N, LANES), lsel)
    cmat = pl.BlockSpec((None, S5_N, C_WIDTH), lsel)
    tab = pl.BlockSpec((None, S5_CHUNK, S5_N), lsel)
    sds = lambda r: jax.ShapeDtypeStruct((depth, r, S5_N), F32)
    c_sds = jax.ShapeDtypeStruct((depth, S5_N, C_WIDTH), BF16)
    return pl.pallas_call(
        _s5_prep_kernel,
        out_shape=[jax.ShapeDtypeStruct((depth, C_WIDTH, 2 * S5_N), BF16), c_sds, c_sds,
                   sds(1), sds(1)] + [sds(S5_CHUNK)] * 4,
        grid=(depth,),
        in_specs=[vec, vec, vec, mat, mat, c_in, c_in],
        out_specs=[pl.BlockSpec((None, C_WIDTH, 2 * S5_N), lsel), cmat, cmat, vec, vec, tab, tab, tab, tab],
        compiler_params=_cparams(("arbitrary",)),
        name="s5_discretise",
    )(lr, li, ldt, b_re_t, b_im_t, c_re_t, c_im_t)


def _cmul(ar, ai, br, bi):
    return ar * br - ai * bi, ar * bi + ai * br


def _s5_prompt_kernel(u_ref, bw_ref, cr_ref, ci_ref, d_ref, er_ref, ei_ref, fr_ref, fi_ref,
                      y_ref, hs_ref, car_ref, cai_ref, xr_ref, xi_ref, hr_ref, hi_ref, *, n_chunks):
    step = pl.program_id(1)
    t = S5_CHUNK

    @pl.when(step == 0)
    def _():
        car_ref[...] = jnp.zeros(car_ref.shape, F32)
        cai_ref[...] = jnp.zeros(cai_ref.shape, F32)

    groups = [slice(n * MXU_TILE, (n + 1) * MXU_TILE) for n in range(S5_N // MXU_TILE)]
    halves = [slice(0, LANES), slice(LANES, MXU_TILE)]
    group_cols = [[slice(g.start + hv.start, g.start + hv.stop) for hv in halves] for g in groups]

    u = u_ref[...]
    ub = u.astype(BF16)
    tiles_per_kt = len(groups) // (C_WIDTH // S5_KT)
    for n, cols in enumerate(group_cols):
        rows_k = slice(n // tiles_per_kt * S5_KT, (n // tiles_per_kt + 1) * S5_KT)
        bu = _dot(ub[:, rows_k], bw_ref[rows_k, 2 * n * MXU_TILE:(2 * n + 2) * MXU_TILE])
        for c in range(n_chunks):
            rs = slice(c * t, (c + 1) * t)
            for cs, hv in zip(cols, halves):
                xr, xi = _cmul(er_ref[:, cs], ei_ref[:, cs], bu[rs, hv],
                               bu[rs, MXU_TILE + hv.start:MXU_TILE + hv.stop])
                xr_ref[rs, cs] = xr.astype(BF16)
                xi_ref[rs, cs] = xi.astype(BF16)

    row = lax.broadcasted_iota(jnp.int32, (t, t), 0)
    col = lax.broadcasted_iota(jnp.int32, (t, t), 1)
    tril = jnp.where(col <= row, 1.0, 0.0).astype(BF16)

    carry = [(car_ref[:, g], cai_ref[:, g]) for g in groups]
    for c in range(n_chunks):
        rs = slice(c * t, (c + 1) * t)
        sums = [(_dot(tril, xr_ref[rs, g]), _dot(tril, xi_ref[rs, g])) for g in groups]
        for n, (g, cols) in enumerate(zip(groups, group_cols)):
            sum_re, sum_im = sums[n][0] + carry[n][0], sums[n][1] + carry[n][1]
            hs = [_cmul(fr_ref[:, cs], fi_ref[:, cs], sum_re[:, hv], sum_im[:, hv]) for cs, hv in zip(cols, halves)]
            hr = jnp.concatenate([h[0] for h in hs], axis=1)
            hi = jnp.concatenate([h[1] for h in hs], axis=1)
            hr_ref[rs, g] = hr.astype(BF16)
            hi_ref[rs, g] = hi.astype(BF16)
            carry[n] = (hr[t - 1:t, :], hi[t - 1:t, :])
    for g, (car, cai) in zip(groups, carry):
        car_ref[:, g] = car
        cai_ref[:, g] = cai

    n_out = C_WIDTH // S5_KT
    k_rows = S5_N // n_out
    for j in range(n_out):
        ks = slice(j * k_rows, (j + 1) * k_rows)
        cs = slice(j * S5_KT, (j + 1) * S5_KT)
        y_ref[:, cs] = (_dot(hr_ref[:, ks], cr_ref[ks, cs]) - _dot(hi_ref[:, ks], ci_ref[ks, cs])
                        + d_ref[:, cs] * u[:, cs])

    @pl.when(step == pl.num_programs(1) - 1)
    def _():
        hs_ref[0:1, :] = car_ref[...]
        hs_ref[1:2, :] = cai_ref[...]


def _s5_prompt_call(layer, u, bw, c_re, c_im, d, tabs, n_seq, seq_rows, step_rows):
    steps = seq_rows // step_rows
    row_spec = pl.BlockSpec((step_rows, C_WIDTH), lambda s, j, steps=steps: (s * steps + j, 0))
    lsel = lambda s, j, l=layer: (l, 0, 0)
    tab = pl.BlockSpec((None, S5_CHUNK, S5_N), lsel)
    return pl.pallas_call(
        functools.partial(_s5_prompt_kernel, n_chunks=step_rows // S5_CHUNK),
        out_shape=[jax.ShapeDtypeStruct((n_seq * seq_rows, C_WIDTH), F32),
                   jax.ShapeDtypeStruct((n_seq, 2, S5_N), F32)],
        grid=(n_seq, steps),
        in_specs=[row_spec,
                  _resident((None, C_WIDTH, 2 * S5_N), lsel),
                  _resident((None, S5_N, C_WIDTH), lsel),
                  _resident((None, S5_N, C_WIDTH), lsel),
                  pl.BlockSpec((None, 1, C_WIDTH), lsel),
                  tab, tab, tab, tab],
        out_specs=[row_spec, pl.BlockSpec((None, 2, S5_N), lambda s, j: (s, 0, 0))],
        scratch_shapes=[pltpu.VMEM((1, S5_N), F32), pltpu.VMEM((1, S5_N), F32)]
                       + [pltpu.VMEM((step_rows, S5_N), BF16)] * 4,
        compiler_params=_cparams(("arbitrary", "arbitrary")),
        name="s5_chunked_scan",
    )(u, bw, c_re, c_im, d, *tabs)


def _s5_sample_kernel(u_ref, h0r_ref, h0i_ref, ar_ref, ai_ref, bw_ref, cr_ref, ci_ref, d_ref,
                      y_ref, hr_ref, hi_ref, *, n_tok, n_seq):
    hr, hi = h0r_ref[...], h0i_ref[...]
    a_re, a_im = ar_ref[...], ai_ref[...]
    for t in range(n_tok):
        u = u_ref[t * n_seq:(t + 1) * n_seq, :]
        bu = _dot(u.astype(BF16), bw_ref[...])
        tiles = range(S5_N // MXU_TILE)
        bu_re = jnp.concatenate([bu[:, 2 * n * MXU_TILE:(2 * n + 1) * MXU_TILE] for n in tiles], axis=1)
        bu_im = jnp.concatenate([bu[:, (2 * n + 1) * MXU_TILE:(2 * n + 2) * MXU_TILE] for n in tiles], axis=1)
        hr, hi = _cmul(a_re, a_im, hr, hi)
        hr, hi = hr + bu_re, hi + bu_im
        y_ref[t * n_seq:(t + 1) * n_seq, :] = (_dot(hr.astype(BF16), cr_ref[...]) - _dot(hi.astype(BF16), ci_ref[...])
                                               + d_ref[...] * u)
    hr_ref[...] = hr
    hi_ref[...] = hi


def _s5_sample_call(layer, u, h0_re, h0_im, a_re, a_im, bw, c_re, c_im, d, n_tok, n_seq):
    lsel = lambda i, l=layer: (l, 0, 0)
    full = lambda shape: pl.BlockSpec(shape, lambda i: (0,) * len(shape))
    return pl.pallas_call(
        functools.partial(_s5_sample_kernel, n_tok=n_tok, n_seq=n_seq),
        out_shape=[jax.ShapeDtypeStruct((n_tok * n_seq, C_WIDTH), F32),
                   jax.ShapeDtypeStruct((n_seq, S5_N), F32),
                   jax.ShapeDtypeStruct((n_seq, S5_N), F32)],
        grid=(1,),
        in_specs=[full((n_tok * n_seq, C_WIDTH)),
                  pl.BlockSpec((None, n_seq, S5_N), lsel),
                  pl.BlockSpec((None, n_seq, S5_N), lsel),
                  pl.BlockSpec((None, 1, S5_N), lsel),
                  pl.BlockSpec((None, 1, S5_N), lsel),
                  pl.BlockSpec((None, C_WIDTH, 2 * S5_N), lsel),
                  pl.BlockSpec((None, S5_N, C_WIDTH), lsel),
                  pl.BlockSpec((None, S5_N, C_WIDTH), lsel),
                  pl.BlockSpec((None, 1, C_WIDTH), lsel)],
        out_specs=[full((n_tok * n_seq, C_WIDTH)), full((n_seq, S5_N)), full((n_seq, S5_N))],
        compiler_params=_cparams(("arbitrary",)),
        name="s5_step_scan",
    )(u, h0_re, h0_im, a_re, a_im, bw, c_re, c_im, d)


def _merge_kernel(x_ref, g_ref, shs, shp, scs, scp, gts, gtp,
                  oap, oas, obp, obs, ocp, ocs,
                  w_ref, wa_ref, wb_ref, wc_ref, wglu_ref, wo_ref, o_ref, *, rows):
    x = x_ref[...]
    br_b = _dot(_select_rows(rows, obp, obs).astype(BF16), wb_ref[...])
    glu = _dot(_gelu_tanh(_select_rows(rows, ocp, ocs)).astype(BF16), wglu_ref[...])
    h = _ada_norm(x, g_ref[...], _mod_value(rows, shs, shp), _mod_value(rows, scs, scp)).astype(BF16)
    z = _dot_nt(h, w_ref[IN_OFF_GL:IN_WIDTH, :])
    z_ar = _dot_nt(h, w_ref[IN_OFF_AR:IN_OFF_BQ, :])
    oc = glu[:, :C_WIDTH] * _sigmoid(glu[:, C_WIDTH:])
    br_c = _dot(oc.astype(BF16), wc_ref[...])
    oa = _select_rows(rows, oap, oas) * _silu(z_ar)
    br_a = _dot(oa.astype(BF16), wa_ref[...])
    mix = (_sigmoid(z[:, D_MODEL:2 * D_MODEL]) * br_b
           + _sigmoid(z[:, 2 * D_MODEL:3 * D_MODEL]) * br_c
           + _sigmoid(z[:, 0:D_MODEL]) * br_a)
    out = _dot(mix.astype(BF16), wo_ref[...])
    o_ref[...] = x + _mod_value(rows, gts, gtp) * out


def _merge_call(rows, layer, x, mod, norm_g, oa, ob, oc, w_in_t, wa, wb, wc, wglu, wo):
    in_specs = [pl.BlockSpec((TOKEN_TILE, D_MODEL), lambda i: (i, 0)),
                pl.BlockSpec((None, 1, D_MODEL), lambda i, l=layer: (l, 0, 0))]
    operands = [x, norm_g]
    for j in (3, 4, 5):
        in_specs += _mod_specs(rows, layer, j)
        operands += [mod, mod]
    for pair, w in ((oa, A_V), (ob, B_Q), (oc, C_WIDTH)):
        in_specs += _pair_specs(rows, w)
        operands += list(pair)
    lsel = lambda i, l=layer: (l, 0, 0)
    in_specs += [_resident((None, IN_WIDTH, D_MODEL), lsel),
                 _resident((None, A_V, D_MODEL), lsel),
                 _resident((None, B_Q, D_MODEL), lsel),
                 _resident((None, C_WIDTH, D_MODEL), lsel),
                 _resident((None, C_WIDTH, 2 * C_WIDTH), lsel),
                 _resident((None, D_MODEL, D_MODEL), lsel)]
    operands += [w_in_t, wa, wb, wc, wglu, wo]
    return pl.pallas_call(
        functools.partial(_merge_kernel, rows=rows),
        out_shape=jax.ShapeDtypeStruct((rows.rows, D_MODEL), F32),
        grid=(rows.n_tiles,),
        in_specs=in_specs,
        out_specs=pl.BlockSpec((TOKEN_TILE, D_MODEL), lambda i: (i, 0)),
        compiler_params=_cparams(("arbitrary",)),
        name="branch_merge",
    )(*operands)


def _rotary_tables(rows):
    half = ROPE_DIM // 2
    inv_freq = ROPE_THETA ** (-jnp.arange(half, dtype=F32) / half)
    pos_p = jnp.arange(rows.l, dtype=jnp.int32)
    pos_s = PAST_LEN + jnp.repeat(jnp.arange(rows.t, dtype=jnp.int32), rows.bs)
    pos = jnp.concatenate([pos_p, pos_s])
    ang = pos.astype(F32)[:, None] * inv_freq[None, :]
    cos, sin = jnp.cos(ang), jnp.sin(ang)
    n = pos.shape[0]
    ones = jnp.ones((n, B_HEAD_DIM - ROPE_DIM), F32)
    zeros = jnp.zeros((n, B_HEAD_DIM - ROPE_DIM), F32)
    z8 = jnp.zeros((n, half), F32)
    per_head = lambda parts: jnp.tile(jnp.concatenate(parts, axis=1), (1, LANES // B_HEAD_DIM))
    return (per_head([cos, cos, ones]), per_head([z8, sin, zeros]), per_head([-sin, z8, zeros]))


def _sample_to_seq(a, rows, pad_to):
    w = a.shape[-1]
    a = a.reshape(rows.t, rows.bs, w).transpose(1, 0, 2)
    a = jnp.pad(a, ((0, 0), (0, pad_to - rows.t), (0, 0)))
    return a.reshape(rows.bs * pad_to, w)


def _seq_to_sample(a, rows, pad_to):
    w = a.shape[-1]
    return a.reshape(rows.bs, pad_to, w)[:, :rows.t].transpose(1, 0, 2).reshape(rows.ns_rows, w)


def kernel(x_prompt, x_sample, state_gla, cache_swa_k, cache_swa_v, state_s5_re, state_s5_im, c_prompt, c_sample, w_ada, b_ada, norm_ffn1, ffn1_w_gate, ffn1_w_up, ffn1_w_down, norm_mix, w_in, gla_w_gate_up, gla_b_gate, gla_norm, swa_sinks, s5_a_re, s5_a_im, s5_b_re, s5_b_im, s5_c_re, s5_c_im, s5_d, s5_log_dt, s5_w_glu, w_branch_a, w_branch_b, w_branch_c, w_out, norm_ffn2, ffn2_w_gate, ffn2_w_up, ffn2_w_down, norm_final):
    depth = w_ada.shape[0]
    b, l, _ = x_prompt.shape
    bs, t, _ = x_sample.shape
    rows = _Rows(b, l, bs, t)
    np_rows = rows.np_rows

    x = (x_prompt.reshape(np_rows, D_MODEL), x_sample.transpose(1, 0, 2).reshape(rows.ns_rows, D_MODEL))
    c_all = jnp.concatenate([c_sample, c_prompt, jnp.zeros((8 - b, D_MODEL), F32)], axis=0)
    mod = _mod_call(c_all, w_ada, b_ada)

    assert w_in.shape[-1] == IN_WIDTH
    w_in_t = w_in.transpose(0, 2, 1).astype(BF16)
    w_up = jnp.pad(gla_w_gate_up, ((0, 0), (0, LANES - A_GATE_RANK), (0, 0))).astype(BF16)
    cast = lambda w: w.astype(BF16)
    f1g, f1u, f1d = cast(ffn1_w_gate), cast(ffn1_w_up), cast(ffn1_w_down)
    f2g, f2u, f2d = cast(ffn2_w_gate), cast(ffn2_w_up), cast(ffn2_w_down)
    wa, wb, wc, wglu, wo = cast(w_branch_a), cast(w_branch_b), cast(w_branch_c), cast(s5_w_glu), cast(w_out)
    vec3 = lambda a: a.reshape(depth, 1, a.shape[-1])
    n1, nm, n2 = vec3(norm_ffn1), vec3(norm_mix), vec3(norm_ffn2)
    b_gate, g_norm, s5d = vec3(gla_b_gate), vec3(gla_norm), vec3(s5_d)
    rot = _rotary_tables(rows)

    flat = lambda a: a.reshape(depth, 1, S5_N)
    ldt = flat(jnp.broadcast_to(s5_log_dt[:, :, None], (depth, C_GROUPS, C_STATE)))
    to_cols = lambda bm: bm.transpose(0, 3, 1, 2).reshape(depth, C_GROUP, S5_N)
    to_rows = lambda cm: jnp.pad(cm.transpose(0, 1, 3, 2).reshape(depth, S5_N, C_GROUP),
                                 ((0, 0), (0, 0), (0, LANES - C_GROUP)))
    s5_bw, s5_cr, s5_ci, a_re, a_im, e_re, e_im, f_re, f_im = _s5_prep_call(
        flat(s5_a_re), flat(s5_a_im), ldt, to_cols(s5_b_re), to_cols(s5_b_im), to_rows(s5_c_re), to_rows(s5_c_im))
    tabs = (e_re, e_im, f_re, f_im)
    h0_re = state_s5_re.reshape(depth, bs, S5_N)
    h0_im = state_s5_im.reshape(depth, bs, S5_N)
    cache_kt = cache_swa_k.transpose(0, 1, 3, 4, 2)
    cache_vt = cache_swa_v.transpose(0, 1, 3, 4, 2)

    outs = {k: [] for k in ("gla_p", "gla_s", "kp", "ks", "vp", "vs", "rp", "rs", "ip", "is")}
    pad8 = SWA_SAMPLE_ROWS
    for layer in range(depth):
        x = _ffn_call(rows, layer, 0, x, mod, n1, f1g, f1u, f1d)
        gq, gk, gv, la, sq, sk, sv, su = _inproj_call(rows, layer, x, mod, nm, w_in_t, w_up, b_gate, rot)

        oa_p, st_p = _gla_call(layer, gq, gk, gv, la, g_norm, None, b, l, MIXER_STEP_ROWS, GLA_CHUNK, GLA_BLOCK)
        parts = dict(gq=gq, gk=gk, la=la, sk=sk, sv=sv, gv=gv)
        seq_in = _sample_to_seq(jnp.concatenate([a[np_rows:] for a in parts.values()], axis=1), rows, pad8)
        starts = dict(zip(parts, np.cumsum([0] + [a.shape[1] for a in parts.values()])))
        col_block = lambda name: int(starts[name]) // parts[name].shape[1]
        assert all(int(starts[n]) % parts[n].shape[1] == 0 for n in parts)
        oa_s, st_s = _gla_call(layer, seq_in, seq_in, seq_in, seq_in, g_norm, state_gla,
                               bs, pad8, pad8, pad8, pad8, seqs_per_step=GLA_SAMPLE_SEQS,
                               col_blocks=tuple(col_block(n) for n in ("gq", "gk", "gv", "la")))
        oa_s = _seq_to_sample(oa_s, rows, pad8)

        ob_p = _swa_call(layer, swa_sinks, sq, sk, sv, b, l // WINDOW)
        q_s = sq[np_rows:].reshape(t, bs, B_KV_HEADS, B_GROUP, B_HEAD_DIM).transpose(1, 2, 3, 0, 4)
        q_s = jnp.pad(q_s, ((0, 0), (0, 0), (0, 0), (0, pad8 - t), (0, 0)))
        q_s = q_s.reshape(bs, B_KV_HEADS, B_GROUP * pad8, B_HEAD_DIM)
        ob_s, kt_new, vt_new = _swa_sample_call(layer, swa_sinks, q_s, seq_in, seq_in, cache_kt, cache_vt, bs, t,
                                                col_blocks=(col_block("sk"), col_block("sv")))
        ob_s = ob_s.reshape(bs, B_KV_HEADS, B_GROUP, pad8, B_HEAD_DIM)[:, :, :, :t]
        ob_s = ob_s.transpose(3, 0, 1, 2, 4).reshape(rows.ns_rows, B_Q)

        oc_p, hs_p = _s5_prompt_call(layer, su, s5_bw, s5_cr, s5_ci, s5d, tabs, b, l, MIXER_STEP_ROWS)
        oc_s, hr_s, hi_s = _s5_sample_call(layer, su[np_rows:], h0_re, h0_im, a_re, a_im,
                                           s5_bw, s5_cr, s5_ci, s5d, t, bs)

        x = _merge_call(rows, layer, x, mod, nm, (oa_p, oa_s), (ob_p, ob_s), (oc_p, oc_s),
                        w_in_t, wa, wb, wc, wglu, wo)
        x = _ffn_call(rows, layer, 6, x, mod, n2, f2g, f2u, f2d,
                      final_g=norm_final.reshape(1, D_MODEL) if layer == depth - 1 else None)

        outs["gla_p"].append(st_p)
        outs["gla_s"].append(st_s)
        kv_shape = (b, WINDOW, B_KV_HEADS, B_HEAD_DIM)
        window_rows = lambda a: jnp.stack([a[(i + 1) * l - WINDOW:(i + 1) * l] for i in range(b)])
        outs["kp"].append(window_rows(sk).reshape(kv_shape))
        outs["vp"].append(window_rows(sv).reshape(kv_shape))
        outs["ks"].append(kt_new.transpose(0, 3, 1, 2))
        outs["vs"].append(vt_new.transpose(0, 3, 1, 2))
        outs["rp"].append(hs_p[:, 0].reshape(b, C_GROUPS, C_STATE))
        outs["ip"].append(hs_p[:, 1].reshape(b, C_GROUPS, C_STATE))
        outs["rs"].append(hr_s.reshape(bs, C_GROUPS, C_STATE))
        outs["is"].append(hi_s.reshape(bs, C_GROUPS, C_STATE))

    y_p = x[0].reshape(b, l, D_MODEL)
    y_s = x[1].reshape(t, bs, D_MODEL).transpose(1, 0, 2)
    st = lambda k: jnp.stack(outs[k])
    return (y_p, y_s, st("gla_p"), st("gla_s"), st("kp"), st("ks"), st("vp"), st("vs"),
            st("rp"), st("rs"), st("ip"), st("is"))
```

```python
import functools
import math

import jax
import jax.numpy as jnp
import numpy as np
from jax import lax
from jax.experimental import pallas as pl
from jax.experimental.pallas import tpu as pltpu

D_MODEL = 1024
D_FF = 2816
PAST_LEN = 8192
A_HEADS, A_DK, A_DV = 4, 64, 128
A_GATE_RANK = 16
A_GATE_NORM = 16.0
B_HEADS, B_KV_HEADS, B_HEAD_DIM = 8, 2, 64
B_GROUP = B_HEADS // B_KV_HEADS
WINDOW = 128
ROPE_DIM = B_HEAD_DIM // 4
ROPE_THETA = 500000.0
C_WIDTH, C_GROUP, C_STATE = 512, 16, 64
C_GROUPS = C_WIDTH // C_GROUP
MACARON_W = 0.5
NORM_EPS = 1e-6
N_MOD = 9

A_QK = A_HEADS * A_DK
A_V = A_HEADS * A_DV
B_Q = B_HEADS * B_HEAD_DIM
B_KV = B_KV_HEADS * B_HEAD_DIM
S5_N = C_GROUPS * C_STATE

LANES = 128
TOKEN_TILE = 512
MIXER_STEP_ROWS = 1024
GLA_CHUNK = 64
GLA_BLOCK = 32
GLA_SAMPLE_SEQS = 8
S5_CHUNK = 64
S5_KT = 256
VMEM_LIMIT = 56 * 1024 * 1024

F32 = jnp.float32
BF16 = jnp.bfloat16
NEG_BIG = -1e30
LOG2_E = math.log2(math.e)


def _cparams(sem):
    return pltpu.CompilerParams(dimension_semantics=sem, vmem_limit_bytes=VMEM_LIMIT)


def _dot(a, b):
    return jnp.dot(a, b, preferred_element_type=F32)


def _dot_nt(a, b):
    return lax.dot_general(a, b, (((1,), (1,)), ((), ())), preferred_element_type=F32)


def _sigmoid(x):
    return 1.0 / (1.0 + jnp.exp(-x))


def _silu(x):
    return x * _sigmoid(x)


def _gelu_tanh(x):
    return 0.5 * x * (1.0 + jnp.tanh(math.sqrt(2.0 / math.pi) * (x + 0.044715 * (x * x * x))))


def _log_sigmoid(x):
    return jnp.minimum(x, 0.0) - jnp.log1p(jnp.exp(-jnp.abs(x)))


def _resident(shape, index_map):
    return pl.BlockSpec(shape, index_map, pipeline_mode=pl.Buffered(1))


def _mod_kernel(c_ref, w_ref, b_ref, o_ref):
    c = c_ref[...]
    s = _silu(c).astype(BF16)
    o_ref[...] = _dot(s, w_ref[...].astype(BF16)) + b_ref[...]


MOD_BLOCK = 3 * D_MODEL


def _mod_call(c_all, w_ada, b_ada):
    depth = w_ada.shape[0]
    rows = c_all.shape[0]
    return pl.pallas_call(
        _mod_kernel,
        out_shape=jax.ShapeDtypeStruct((depth, rows, N_MOD * D_MODEL), F32),
        grid=(depth, N_MOD * D_MODEL // MOD_BLOCK),
        in_specs=[
            pl.BlockSpec((rows, D_MODEL), lambda l, j: (0, 0)),
            pl.BlockSpec((None, D_MODEL, MOD_BLOCK), lambda l, j: (l, 0, j)),
            pl.BlockSpec((None, 1, MOD_BLOCK), lambda l, j: (l, 0, j)),
        ],
        out_specs=pl.BlockSpec((None, rows, MOD_BLOCK), lambda l, j: (l, 0, j)),
        compiler_params=_cparams(("arbitrary", "arbitrary")),
        name="ada_mod",
    )(c_all, w_ada, b_ada.reshape(depth, 1, N_MOD * D_MODEL))


class _Rows:
    def __init__(self, n_prompt_seq, seq_len, n_sample_seq, n_sample_tok):
        self.b = n_prompt_seq
        self.l = seq_len
        self.bs = n_sample_seq
        self.t = n_sample_tok
        self.np_rows = n_prompt_seq * seq_len
        self.ns_rows = n_sample_seq * n_sample_tok
        assert seq_len % TOKEN_TILE == 0 and self.ns_rows == TOKEN_TILE
        assert n_sample_seq % 8 == 0 and seq_len % MIXER_STEP_ROWS == 0
        self.rows = self.np_rows + self.ns_rows
        self.n_prompt_tiles = self.np_rows // TOKEN_TILE
        self.tiles_per_seq = seq_len // TOKEN_TILE
        self.n_tiles = self.n_prompt_tiles + 1
        self.mod_rows = n_sample_seq + 8
        assert n_prompt_seq <= 8


def _mod_specs(rows, layer, j):
    return [
        pl.BlockSpec((None, rows.bs, D_MODEL), lambda i, l=layer, j=j: (l, 0, j)),
        pl.BlockSpec((None, 8, D_MODEL), lambda i, l=layer, j=j, r=rows.bs // 8: (l, r, j)),
    ]


def _mod_value(rows, ms_ref, mp_ref):
    i = pl.program_id(0)
    is_sample = i == rows.n_prompt_tiles
    seq = jnp.minimum(i // rows.tiles_per_seq, rows.b - 1)
    s = ms_ref[...]
    s = jnp.concatenate([s] * rows.t, axis=0)
    p = mp_ref[pl.ds(seq, 1), :]
    return jnp.where(is_sample, s, p)


def _ada_norm(x, g, shift, scale):
    var = jnp.mean(x * x, axis=-1, keepdims=True)
    return (x * lax.rsqrt(var + NORM_EPS)) * g * (1.0 + scale) + shift


def _select_rows(rows, p_ref, s_ref):
    is_sample = pl.program_id(0) == rows.n_prompt_tiles
    return jnp.where(is_sample, s_ref[...], p_ref[...])


def _pair_specs(rows, width):
    last = rows.n_prompt_tiles - 1
    return [
        pl.BlockSpec((TOKEN_TILE, width), lambda i, last=last: (jnp.minimum(i, last), 0)),
        pl.BlockSpec((TOKEN_TILE, width), lambda i: (0, 0)),
    ]


MXU_TILE = 256
FF_SPLITS = (0, 6 * MXU_TILE, D_FF)
assert D_FF % MXU_TILE == 0


FF_LOAD_CHUNKS = 8


def _ffn_load_weights(layer, hbm, vmem, stages, sems):
    jobs = []
    for m in range(3):
        fam = 0 if m < 2 else 1
        n_rows = stages[fam].shape[1]
        assert hbm[m].shape[1] == n_rows * FF_LOAD_CHUNKS
        for c in range(FF_LOAD_CHUNKS):
            jobs.append((m, fam, c * n_rows, n_rows))
    slots, seen = [], [0, 0]
    for _, fam, _, _ in jobs:
        slots.append(seen[fam] % 2)
        seen[fam] += 1

    def copy(n):
        m, fam, r0, n_rows = jobs[n]
        return pltpu.make_async_copy(hbm[m].at[layer, pl.ds(r0, n_rows), :], stages[fam].at[slots[n]],
                                     sems[fam].at[slots[n]])

    copy(0).start()
    for n, (m, fam, r0, n_rows) in enumerate(jobs):
        if n + 1 < len(jobs):
            copy(n + 1).start()
        copy(n).wait()
        vmem[m][r0:r0 + n_rows, :] = stages[fam][slots[n]].astype(BF16)


def _ffn_kernel(*refs, rows, layer, x_pair, final_norm):
    n_in = 11 + int(x_pair) + int(final_norm)
    n_out = 2 if final_norm else 1
    ins, outs, scratch = refs[:n_in], refs[n_in:n_in + n_out], refs[n_in + n_out:]
    if x_pair:
        x = _select_rows(rows, ins[0], ins[1])
    else:
        x = ins[0][...]
    g_ref, shs, shp, scs, scp, gts, gtp, wg_hbm, wu_hbm, wd_hbm = ins[1 + int(x_pair):11 + int(x_pair)]
    wg_ref, wu_ref, wd_ref, stage_a, stage_d, sem_a, sem_d = scratch

    @pl.when(pl.program_id(0) == 0)
    def _():
        _ffn_load_weights(layer, (wg_hbm, wu_hbm, wd_hbm), (wg_ref, wu_ref, wd_ref),
                          (stage_a, stage_d), (sem_a, sem_d))

    h = _ada_norm(x, g_ref[...], _mod_value(rows, shs, shp), _mod_value(rows, scs, scp)).astype(BF16)
    chunks = [slice(lo, hi) for lo, hi in zip(FF_SPLITS[:-1], FF_SPLITS[1:])]
    gate_up = [(_dot(h, wg_ref[:, sl]), _dot(h, wu_ref[:, sl])) for sl in chunks]
    act = [(_silu(gate) * up).astype(BF16) for gate, up in gate_up]
    acc = _dot(act[0], wd_ref[chunks[0], :])
    for a, sl in zip(act[1:], chunks[1:]):
        acc = acc + _dot(a, wd_ref[sl, :])
    out = x + (MACARON_W * _mod_value(rows, gts, gtp)) * acc
    if not final_norm:
        outs[0][...] = out
        return
    var = jnp.mean(out * out, axis=-1, keepdims=True)
    y = out * lax.rsqrt(var + NORM_EPS) * ins[-1][...]
    is_sample = pl.program_id(0) == rows.n_prompt_tiles

    @pl.when(jnp.logical_not(is_sample))
    def _():
        outs[0][...] = y

    @pl.when(is_sample)
    def _():
        outs[1][...] = y


def _ffn_call(rows, layer, mod_base, x, mod, norm_g, wg, wu, wd, final_g=None):
    x_pair = isinstance(x, tuple)
    if x_pair:
        in_specs = _pair_specs(rows, D_MODEL)
        operands = list(x)
    else:
        in_specs = [pl.BlockSpec((TOKEN_TILE, D_MODEL), lambda i: (i, 0))]
        operands = [x]
    in_specs.append(pl.BlockSpec((None, 1, D_MODEL), lambda i, l=layer: (l, 0, 0)))
    operands.append(norm_g)
    for j in range(3):
        in_specs += _mod_specs(rows, layer, mod_base + j)
        operands += [mod, mod]
    in_specs += [pl.BlockSpec(memory_space=pl.ANY)] * 3
    operands += [wg, wu, wd]
    scratch_shapes = [
        pltpu.VMEM((D_MODEL, D_FF), BF16), pltpu.VMEM((D_MODEL, D_FF), BF16), pltpu.VMEM((D_FF, D_MODEL), BF16),
        pltpu.VMEM((2, D_MODEL // FF_LOAD_CHUNKS, D_FF), F32), pltpu.VMEM((2, D_FF // FF_LOAD_CHUNKS, D_MODEL), F32),
        pltpu.SemaphoreType.DMA((2,)), pltpu.SemaphoreType.DMA((2,)),
    ]
    if final_g is None:
        out_shape = jax.ShapeDtypeStruct((rows.rows, D_MODEL), F32)
        out_specs = pl.BlockSpec((TOKEN_TILE, D_MODEL), lambda i: (i, 0))
    else:
        in_specs.append(pl.BlockSpec((1, D_MODEL), lambda i: (0, 0)))
        operands.append(final_g)
        out_shape = [jax.ShapeDtypeStruct((rows.np_rows, D_MODEL), F32),
                     jax.ShapeDtypeStruct((rows.ns_rows, D_MODEL), F32)]
        out_specs = _pair_specs(rows, D_MODEL)
    return pl.pallas_call(
        functools.partial(_ffn_kernel, rows=rows, layer=layer, x_pair=x_pair, final_norm=final_g is not None),
        out_shape=out_shape,
        grid=(rows.n_tiles,),
        in_specs=in_specs,
        out_specs=out_specs,
        scratch_shapes=scratch_shapes,
        compiler_params=_cparams(("arbitrary",)),
        name="ada_swiglu",
    )(*operands)


IN_OFF_RANK = 2 * A_QK + A_V
IN_OFF_AR = IN_OFF_RANK + A_GATE_RANK
IN_OFF_BQ = IN_OFF_AR + A_V
IN_OFF_GL = IN_OFF_BQ + B_Q + 2 * B_KV + C_WIDTH
IN_WIDTH = IN_OFF_GL + 3 * D_MODEL


def _rotary(x, cos, s_up, s_dn):
    half = ROPE_DIM // 2
    return x * cos + pltpu.roll(x, half, axis=1) * s_up + pltpu.roll(x, LANES - half, axis=1) * s_dn


def _inproj_kernel(x_ref, g_ref, shs, shp, scs, scp, w_ref, wup_ref, bg_ref,
                   cos_ref, sup_ref, sdn_ref,
                   gq_ref, gk_ref, gv_ref, la_ref, sq_ref, sk_ref, sv_ref, su_ref, *, rows):
    x = x_ref[...]
    h = _ada_norm(x, g_ref[...], _mod_value(rows, shs, shp), _mod_value(rows, scs, scp)).astype(BF16)
    za = _dot_nt(h, w_ref[0:IN_OFF_RANK, :])
    gq_ref[...] = za[:, 0:A_QK] * (A_DK ** -0.5)
    gk_ref[...] = za[:, A_QK:2 * A_QK]
    gv_ref[...] = za[:, 2 * A_QK:]
    z = _dot_nt(h, w_ref[IN_OFF_BQ:IN_OFF_GL, :])
    o = 0
    cos, s_up, s_dn = cos_ref[...], sup_ref[...], sdn_ref[...]
    for c in range(B_Q // LANES):
        sq_ref[:, c * LANES:(c + 1) * LANES] = _rotary(z[:, o + c * LANES:o + (c + 1) * LANES], cos, s_up, s_dn)
    o += B_Q
    sk_ref[...] = _rotary(z[:, o:o + B_KV], cos, s_up, s_dn)
    o += B_KV
    sv_ref[...] = z[:, o:o + B_KV]
    o += B_KV
    su_ref[...] = z[:, o:o + C_WIDTH]
    low = _dot_nt(h, w_ref[IN_OFF_RANK:IN_OFF_RANK + LANES, :]).astype(BF16)
    pre = _dot(low, wup_ref[...]) + bg_ref[...]
    la_ref[...] = _log_sigmoid(pre) * (LOG2_E / A_GATE_NORM)


def _inproj_call(rows, layer, x, mod, norm_g, w_in_t, w_up, b_gate, rot):
    row_spec = lambda w: pl.BlockSpec((TOKEN_TILE, w), lambda i: (i, 0))
    rot_spec = pl.BlockSpec(
        (TOKEN_TILE, LANES),
        lambda i, n=rows.n_prompt_tiles, per=rows.tiles_per_seq: (jnp.where(i < n, i % per, per), 0))
    in_specs = [row_spec(D_MODEL), pl.BlockSpec((None, 1, D_MODEL), lambda i, l=layer: (l, 0, 0))]
    operands = [x, norm_g]
    for j in (3, 4):
        in_specs += _mod_specs(rows, layer, j)
        operands += [mod, mod]
    in_specs += [
        _resident((None, IN_WIDTH, D_MODEL), lambda i, l=layer: (l, 0, 0)),
        _resident((None, LANES, A_QK), lambda i, l=layer: (l, 0, 0)),
        pl.BlockSpec((None, 1, A_QK), lambda i, l=layer: (l, 0, 0)),
        rot_spec, rot_spec, rot_spec,
    ]
    operands += [w_in_t, w_up, b_gate, *rot]
    widths = (A_QK, A_QK, A_V, A_QK, B_Q, B_KV, B_KV, C_WIDTH)
    return pl.pallas_call(
        functools.partial(_inproj_kernel, rows=rows),
        out_shape=[jax.ShapeDtypeStruct((rows.rows, w), F32) for w in widths],
        grid=(rows.n_tiles,),
        in_specs=in_specs,
        out_specs=[row_spec(w) for w in widths],
        compiler_params=_cparams(("arbitrary",)),
        name="mixer_in_proj",
    )(*operands)


def _cumsum_rows(x):
    n = x.shape[0]
    row = lax.broadcasted_iota(jnp.int32, x.shape, 0)
    s = 1
    while s < n:
        x = x + jnp.where(row >= s, pltpu.roll(x, s, axis=0), 0.0)
        s *= 2
    return x


def _head_blockdiag(x, lanes_per_head):
    lane_head = lax.broadcasted_iota(jnp.int32, x.shape, 1) // lanes_per_head
    return jnp.concatenate([jnp.where(lane_head == h, x, 0.0) for h in range(A_HEADS)], axis=0)


def _gla_kernel(*refs, chunk, block, n_chunks, n_seqs, single_step, has_state):
    if has_state:
        q_ref, k_ref, v_ref, la_ref, gn_ref, s0_ref, o_ref, so_ref, st_ref = refs
    else:
        q_ref, k_ref, v_ref, la_ref, gn_ref, o_ref, so_ref, st_ref = refs
    n_blocks = chunk // block
    st_shape = (A_V, A_QK)
    st_row_head = lax.broadcasted_iota(jnp.int32, st_shape, 0) // A_DV
    st_lane_head = lax.broadcasted_iota(jnp.int32, st_shape, 1) // A_DK
    st_mask = st_row_head == st_lane_head
    srow = lax.broadcasted_iota(jnp.int32, (block, A_HEADS * block), 0)
    scol = lax.broadcasted_iota(jnp.int32, (block, A_HEADS * block), 1) % block
    causal = scol <= srow
    gn = gn_ref[...]

    def init_state(sq):
        st_ref[sq] = jnp.zeros(st_shape, F32)
        if has_state:
            for h in range(A_HEADS):
                st_ref[sq, h * A_DV:(h + 1) * A_DV, h * A_DK:(h + 1) * A_DK] = s0_ref[sq, h].T

    def emit_state(sq):
        for h in range(A_HEADS):
            so_ref[sq, h] = st_ref[sq, h * A_DV:(h + 1) * A_DV, h * A_DK:(h + 1) * A_DK].T

    if single_step:
        for sq in range(n_seqs):
            init_state(sq)
    else:
        pl.when(pl.program_id(1) == 0)(lambda: init_state(0))

    units = [(sq, c) for sq in range(n_seqs) for c in range(n_chunks)]
    rows_of = {u: slice((u[0] * n_chunks + u[1]) * chunk, (u[0] * n_chunks + u[1] + 1) * chunk) for u in units}

    q, k, v, b = {}, {}, {}, {}
    for u in units:
        q[u], k[u], v[u] = q_ref[rows_of[u], :], k_ref[rows_of[u], :], v_ref[rows_of[u], :]
        b[u] = _cumsum_rows(la_ref[rows_of[u], :])

    s_diag, s_off, upd, q_in = {}, {}, {}, {}
    for u in units:
        for i in range(n_blocks):
            lo, hi = i * block, (i + 1) * block
            qi, ki, bi = q[u][lo:hi], k[u][lo:hi], b[u][lo:hi]
            mid = b[u][lo + block // 2 - 1:lo + block // 2, :]
            qd = (qi * jnp.exp2(bi - mid)).astype(BF16)
            kd = _head_blockdiag(ki * jnp.exp2(mid - bi), A_DK).astype(BF16)
            s_diag[u, i] = _dot_nt(qd, kd)
            if i > 0:
                ref = b[u][lo - 1:lo, :]
                qo = (qi * jnp.exp2(bi - ref)).astype(BF16)
                ko = _head_blockdiag(k[u][:lo] * jnp.exp2(ref - b[u][:lo]), A_DK).astype(BF16)
                s_off[u, i] = _dot_nt(qo, ko)
        b_last = b[u][chunk - 1:chunk, :]
        upd[u] = _dot(v[u].T.astype(BF16), (k[u] * jnp.exp2(b_last - b[u])).astype(BF16))
        q_in[u] = (q[u] * jnp.exp2(b[u])).astype(BF16)

    intra = {}
    for u in units:
        outs = []
        for i in range(n_blocks):
            lo, hi = i * block, (i + 1) * block
            att = jnp.where(causal, s_diag[u, i], 0.0).astype(BF16)
            oi = _dot(att, _head_blockdiag(v[u][lo:hi], A_DV).astype(BF16))
            if i > 0:
                oi = oi + _dot(s_off[u, i].astype(BF16), _head_blockdiag(v[u][:lo], A_DV).astype(BF16))
            outs.append(oi)
        intra[u] = jnp.concatenate(outs, axis=0) if n_blocks > 1 else outs[0]

    for u in units:
        sq = u[0]
        st = st_ref[sq]
        o = _dot_nt(q_in[u], st.astype(BF16)) + intra[u]
        st_ref[sq] = st * jnp.exp2(b[u][chunk - 1:chunk, :]) + jnp.where(st_mask, upd[u], 0.0)
        normed = []
        for h in range(A_HEADS):
            oh = o[:, h * A_DV:(h + 1) * A_DV]
            var = jnp.mean(oh * oh, axis=-1, keepdims=True)
            normed.append(oh * lax.rsqrt(var + NORM_EPS) * gn)
        o_ref[rows_of[u], :] = jnp.concatenate(normed, axis=1)

    if single_step:
        for sq in range(n_seqs):
            emit_state(sq)
    else:
        pl.when(pl.program_id(1) == pl.num_programs(1) - 1)(lambda: emit_state(0))


def _gla_call(layer, q, k, v, la, gnorm, s0, n_seq, seq_rows, step_rows, chunk, block, seqs_per_step=1,
              col_blocks=(0, 0, 0, 0)):
    steps = seq_rows // step_rows
    sps = seqs_per_step
    assert sps == 1 or steps == 1
    blk_rows = sps * step_rows
    row_spec = lambda w, cb=0: pl.BlockSpec((blk_rows, w), lambda s, j, steps=steps, cb=cb: (s * steps + j, cb))
    in_specs = [row_spec(A_QK, col_blocks[0]), row_spec(A_QK, col_blocks[1]), row_spec(A_V, col_blocks[2]),
                row_spec(A_QK, col_blocks[3]),
                pl.BlockSpec((None, 1, A_DV), lambda s, j, l=layer: (l, 0, 0))]
    operands = [q, k, v, la, gnorm]
    if s0 is not None:
        in_specs.append(pl.BlockSpec((None, sps, A_HEADS, A_DK, A_DV), lambda s, j, l=layer: (l, s, 0, 0, 0)))
        operands.append(s0)
    return pl.pallas_call(
        functools.partial(_gla_kernel, chunk=chunk, block=block, n_chunks=step_rows // chunk,
                          n_seqs=sps, single_step=steps == 1, has_state=s0 is not None),
        out_shape=[jax.ShapeDtypeStruct((n_seq * seq_rows, A_V), F32),
                   jax.ShapeDtypeStruct((n_seq, A_HEADS, A_DK, A_DV), F32)],
        grid=(n_seq // sps, steps),
        in_specs=in_specs,
        out_specs=[row_spec(A_V),
                   pl.BlockSpec((sps, A_HEADS, A_DK, A_DV), lambda s, j: (s, 0, 0, 0))],
        scratch_shapes=[pltpu.VMEM((sps, A_V, A_QK), F32)],
        compiler_params=_cparams(("arbitrary", "arbitrary")),
        name="gla_chunked",
    )(*operands)


SWA_STEP_BLOCKS = MIXER_STEP_ROWS // WINDOW


def _swa_kernel(sink_ref, q_ref, kc_ref, vc_ref, kp_ref, vp_ref, o_ref, *, layer):
    tk = 2 * WINDOW
    kj = lax.broadcasted_iota(jnp.int32, (tk, WINDOW), 0)
    qi = lax.broadcasted_iota(jnp.int32, (tk, WINDOW), 1) + WINDOW
    band = (kj <= qi) & (kj >= qi - WINDOW)
    zeros = jnp.zeros((B_HEAD_DIM, WINDOW), F32)
    chains = [(blk, h) for blk in range(SWA_STEP_BLOCKS) for h in range(B_KV_HEADS)]

    keys, vals_t, q_t, valid = [], [], [], []
    for blk in range(SWA_STEP_BLOCKS):
        cur = slice(blk * WINDOW, (blk + 1) * WINDOW)
        if blk == 0:
            k_prev, v_prev = kp_ref[...], vp_ref[...]
            valid.append(band & (kj >= jnp.where(pl.program_id(1) > 0, 0, WINDOW)))
        else:
            prev = slice((blk - 1) * WINDOW, blk * WINDOW)
            k_prev, v_prev = kc_ref[prev, :], vc_ref[prev, :]
            valid.append(band)
        keys.append(jnp.concatenate([k_prev, kc_ref[cur, :]], axis=0).astype(BF16))
        vals_t.append(jnp.concatenate([v_prev, vc_ref[cur, :]], axis=0).T.astype(BF16))
        q_t.append(q_ref[cur, :].T * (LOG2_E * B_HEAD_DIM ** -0.5))

    scores = {}
    for blk, h in chains:
        cols = []
        for g in range(B_GROUP):
            hd = h * B_GROUP + g
            qg = q_t[blk][hd * B_HEAD_DIM:(hd + 1) * B_HEAD_DIM]
            cols.append(jnp.concatenate([qg, zeros] if h == 0 else [zeros, qg], axis=0))
        scores[blk, h] = _dot(keys[blk], jnp.concatenate(cols, axis=1).astype(BF16))

    probs, inv = {}, {}
    for blk, h in chains:
        ps = []
        for g in range(B_GROUP):
            sink = sink_ref[layer, h * B_GROUP + g] * LOG2_E
            sg = jnp.where(valid[blk], scores[blk, h][:, g * WINDOW:(g + 1) * WINDOW], NEG_BIG)
            m = jnp.maximum(jnp.max(sg, axis=0, keepdims=True), sink)
            p = jnp.exp2(sg - m)
            inv[blk, h, g] = 1.0 / (jnp.sum(p, axis=0, keepdims=True) + jnp.exp2(sink - m))
            ps.append(p.astype(BF16))
        probs[blk, h] = jnp.concatenate(ps, axis=1)

    out_t = {}
    for blk, h in chains:
        out_t[blk, h] = _dot(vals_t[blk][h * B_HEAD_DIM:(h + 1) * B_HEAD_DIM], probs[blk, h])

    for blk in range(SWA_STEP_BLOCKS):
        rows = [out_t[blk, h][:, g * WINDOW:(g + 1) * WINDOW] * inv[blk, h, g]
                for h in range(B_KV_HEADS) for g in range(B_GROUP)]
        o_ref[blk * WINDOW:(blk + 1) * WINDOW, :] = jnp.concatenate(rows, axis=0).T


def _swa_call(layer, sinks, q, k, v, n_seq, blocks_per_seq):
    sb = SWA_STEP_BLOCKS
    assert blocks_per_seq % sb == 0
    steps = blocks_per_seq // sb
    cur = lambda w: pl.BlockSpec((sb * WINDOW, w), lambda s, n, steps=steps: (s * steps + n, 0))
    prev = pl.BlockSpec((WINDOW, B_KV),
                        lambda s, n, nb=blocks_per_seq, sb=sb: (s * nb + jnp.maximum(n * sb - 1, 0), 0))
    return pl.pallas_call(
        functools.partial(_swa_kernel, layer=layer),
        out_shape=jax.ShapeDtypeStruct((n_seq * blocks_per_seq * WINDOW, B_Q), F32),
        grid=(n_seq, steps),
        in_specs=[pl.BlockSpec(memory_space=pltpu.SMEM), cur(B_Q), cur(B_KV), cur(B_KV), prev, prev],
        out_specs=cur(B_Q),
        compiler_params=_cparams(("arbitrary", "arbitrary")),
        name="swa_banded",
    )(sinks, q, k, v, k, v)


SWA_SAMPLE_SEQS = 8
SWA_SAMPLE_ROWS = 8


def _swa_sample_kernel(sink_ref, q_ref, kn_ref, vn_ref, kt_ref, vt_ref, o_ref, nk_ref, nv_ref, *, layer, n_tok):
    pr = SWA_SAMPLE_ROWS
    keep = WINDOW - n_tok
    lane = lax.broadcasted_iota(jnp.int32, (B_HEAD_DIM, WINDOW), 1)
    rows = B_GROUP * pr
    tok = lax.broadcasted_iota(jnp.int32, (rows, WINDOW), 0) % pr
    grp = lax.broadcasted_iota(jnp.int32, (rows, 1), 0) // pr
    pos = lax.broadcasted_iota(jnp.int32, (rows, WINDOW), 1)
    valid_new = pos - keep <= tok
    valid_old = (pos >= tok) & (pos < n_tok)

    def shifted_in(old, new_rows):
        padded = jnp.concatenate([jnp.zeros((WINDOW - pr, B_KV), F32), new_rows], axis=0)
        cols = pltpu.roll(padded, n_tok, axis=0).T
        return [jnp.where(lane < keep, pltpu.roll(old[h], keep, axis=1),
                          cols[h * B_HEAD_DIM:(h + 1) * B_HEAD_DIM]) for h in range(B_KV_HEADS)]

    chains = [(i, h) for i in range(SWA_SAMPLE_SEQS) for h in range(B_KV_HEADS)]
    k_old, v_old, k_new, v_new = {}, {}, {}, {}
    for i in range(SWA_SAMPLE_SEQS):
        rs = slice(i * pr, (i + 1) * pr)
        ko = [kt_ref[i, h] for h in range(B_KV_HEADS)]
        vo = [vt_ref[i, h] for h in range(B_KV_HEADS)]
        kn, vn = shifted_in(ko, kn_ref[rs, :]), shifted_in(vo, vn_ref[rs, :])
        for h in range(B_KV_HEADS):
            nk_ref[i, h] = kn[h]
            nv_ref[i, h] = vn[h]
            k_old[i, h], v_old[i, h] = ko[h].astype(BF16), vo[h].astype(BF16)
            k_new[i, h], v_new[i, h] = kn[h].astype(BF16), vn[h].astype(BF16)

    s_new, s_old = {}, {}
    for c in chains:
        q = (q_ref[c[0], c[1]] * (LOG2_E * B_HEAD_DIM ** -0.5)).astype(BF16)
        s_new[c], s_old[c] = _dot(q, k_new[c]), _dot(q, k_old[c])

    p_new, p_old, inv = {}, {}, {}
    for c in chains:
        sn = jnp.where(valid_new, s_new[c], NEG_BIG)
        so = jnp.where(valid_old, s_old[c], NEG_BIG)
        sink = jnp.zeros((rows, 1), F32)
        for g in range(B_GROUP):
            sink = jnp.where(grp == g, sink_ref[layer, c[1] * B_GROUP + g] * LOG2_E, sink)
        m = jnp.maximum(jnp.maximum(jnp.max(sn, axis=-1, keepdims=True),
                                    jnp.max(so, axis=-1, keepdims=True)), sink)
        pn, po = jnp.exp2(sn - m), jnp.exp2(so - m)
        inv[c] = 1.0 / (jnp.sum(pn, axis=-1, keepdims=True) + jnp.sum(po, axis=-1, keepdims=True)
                        + jnp.exp2(sink - m))
        p_new[c], p_old[c] = pn.astype(BF16), po.astype(BF16)

    for c in chains:
        o = _dot_nt(p_new[c], v_new[c]) + _dot_nt(p_old[c], v_old[c])
        o_ref[c[0], c[1]] = o * inv[c]


def _swa_sample_call(layer, sinks, q, k_new, v_new, kt, vt, n_seq, n_tok, col_blocks=(0, 0)):
    sb, pr = SWA_SAMPLE_SEQS, SWA_SAMPLE_ROWS
    qspec = pl.BlockSpec((sb, B_KV_HEADS, B_GROUP * pr, B_HEAD_DIM), lambda s: (s, 0, 0, 0))
    new_rows = lambda cb: pl.BlockSpec((sb * pr, B_KV), lambda s, cb=cb: (s, cb))
    buf_in = pl.BlockSpec((None, sb, B_KV_HEADS, B_HEAD_DIM, WINDOW), lambda s, l=layer: (l, s, 0, 0, 0))
    buf_out = pl.BlockSpec((sb, B_KV_HEADS, B_HEAD_DIM, WINDOW), lambda s: (s, 0, 0, 0))
    buf_sds = jax.ShapeDtypeStruct((n_seq, B_KV_HEADS, B_HEAD_DIM, WINDOW), F32)
    return pl.pallas_call(
        functools.partial(_swa_sample_kernel, layer=layer, n_tok=n_tok),
        out_shape=[jax.ShapeDtypeStruct((n_seq, B_KV_HEADS, B_GROUP * pr, B_HEAD_DIM), F32), buf_sds, buf_sds],
        grid=(n_seq // sb,),
        in_specs=[pl.BlockSpec(memory_space=pltpu.SMEM), qspec, new_rows(col_blocks[0]), new_rows(col_blocks[1]),
                  buf_in, buf_in],
        out_specs=[qspec, buf_out, buf_out],
        compiler_params=_cparams(("arbitrary",)),
        name="swa_decode",
    )(sinks, q, k_new, v_new, kt, vt)


def _s5_prep_kernel(lr_ref, li_ref, ldt_ref, br_ref, bi_ref, ctr_ref, cti_ref,
                    bw_ref, cr_ref, ci_ref, ar_ref, ai_ref, er_ref, ei_ref, fr_ref, fi_ref):
    lr, li = lr_ref[...], li_ref[...]
    dt = jnp.exp(ldt_ref[...])
    rho, th = lr * dt, li * dt
    mag = jnp.exp(rho)
    a_re, a_im = mag * jnp.cos(th), mag * jnp.sin(th)
    den = lr * lr + li * li
    nr, ni = a_re - 1.0, a_im
    f_re = (nr * lr + ni * li) / den
    f_im = (ni * lr - nr * li) / den
    br, bi = br_ref[...], bi_ref[...]
    bb_re = f_re * br - f_im * bi
    bb_im = f_re * bi + f_im * br
    lane_group = lax.broadcasted_iota(jnp.int32, bb_re.shape, 1) // C_STATE
    place = lambda bb: jnp.concatenate(
        [jnp.where(lane_group == g, bb, 0.0) for g in range(C_GROUPS)], axis=0).astype(BF16)
    placed_re, placed_im = place(bb_re), place(bb_im)
    for n in range(S5_N // MXU_TILE):
        src = slice(n * MXU_TILE, (n + 1) * MXU_TILE)
        bw_ref[:, 2 * n * MXU_TILE:(2 * n + 1) * MXU_TILE] = placed_re[:, src]
        bw_ref[:, (2 * n + 1) * MXU_TILE:(2 * n + 2) * MXU_TILE] = placed_im[:, src]
    spread_c = lax.broadcasted_iota(jnp.int32, (LANES, C_WIDTH), 0)
    spread_j = lax.broadcasted_iota(jnp.int32, (LANES, C_WIDTH), 1) % C_GROUP
    spread = jnp.where(spread_c == spread_j, 1.0, 0.0).astype(BF16)
    row_group = lax.broadcasted_iota(jnp.int32, cr_ref.shape, 0) // C_STATE
    col_group = lax.broadcasted_iota(jnp.int32, cr_ref.shape, 1) // C_GROUP
    own = row_group == col_group
    cr_ref[...] = jnp.where(own, _dot(ctr_ref[...].astype(BF16), spread), 0.0).astype(BF16)
    ci_ref[...] = jnp.where(own, _dot(cti_ref[...].astype(BF16), spread), 0.0).astype(BF16)
    ar_ref[...] = a_re
    ai_ref[...] = a_im
    s = (lax.broadcasted_iota(jnp.int32, er_ref.shape, 0) + 1).astype(F32)
    cs, sn = jnp.cos(s * th), jnp.sin(s * th)
    e_mag, f_mag = jnp.exp(-(s * rho)), jnp.exp(s * rho)
    er_ref[...] = e_mag * cs
    ei_ref[...] = -(e_mag * sn)
    fr_ref[...] = f_mag * cs
    fi_ref[...] = f_mag * sn


def _s5_prep_call(lr, li, ldt, b_re_t, b_im_t, c_re_t, c_im_t):
    depth = lr.shape[0]
    lsel = lambda l: (l, 0, 0)
    vec = pl.BlockSpec((None, 1, S5_N), lsel)
    mat = pl.BlockSpec((None, C_GROUP, S5_N), lsel)
    c_in = pl.BlockSpec((None, S5_N, LANES), lsel)
    cmat = pl.BlockSpec((None, S5_N, C_WIDTH), lsel)
    tab = pl.BlockSpec((None, S5_CHUNK, S5_N), lsel)
    sds = lambda r: jax.ShapeDtypeStruct((depth, r, S5_N), F32)
    c_sds = jax.ShapeDtypeStruct((depth, S5_N, C_WIDTH), BF16)
    return pl.pallas_call(
        _s5_prep_kernel,
        out_shape=[jax.ShapeDtypeStruct((depth, C_WIDTH, 2 * S5_N), BF16), c_sds, c_sds,
                   sds(1), sds(1)] + [sds(S5_CHUNK)] * 4,
        grid=(depth,),
        in_specs=[vec, vec, vec, mat, mat, c_in, c_in],
        out_specs=[pl.BlockSpec((None, C_WIDTH, 2 * S5_N), lsel), cmat, cmat, vec, vec, tab, tab, tab, tab],
        compiler_params=_cparams(("arbitrary",)),
        name="s5_discretise",
    )(lr, li, ldt, b_re_t, b_im_t, c_re_t, c_im_t)


def _cmul(ar, ai, br, bi):
    return ar * br - ai * bi, ar * bi + ai * br


def _s5_prompt_kernel(u_ref, bw_ref, cr_ref, ci_ref, d_ref, er_ref, ei_ref, fr_ref, fi_ref,
                      y_ref, hs_ref, car_ref, cai_ref, xr_ref, xi_ref, hr_ref, hi_ref, *, n_chunks):
    step = pl.program_id(1)
    t = S5_CHUNK

    @pl.when(step == 0)
    def _():
        car_ref[...] = jnp.zeros(car_ref.shape, F32)
        cai_ref[...] = jnp.zeros(cai_ref.shape, F32)

    groups = [slice(n * MXU_TILE, (n + 1) * MXU_TILE) for n in range(S5_N // MXU_TILE)]
    halves = [slice(0, LANES), slice(LANES, MXU_TILE)]
    group_cols = [[slice(g.start + hv.start, g.start + hv.stop) for hv in halves] for g in groups]

    u = u_ref[...]
    ub = u.astype(BF16)
    tiles_per_kt = len(groups) // (C_WIDTH // S5_KT)
    for n, cols in enumerate(group_cols):
        rows_k = slice(n // tiles_per_kt * S5_KT, (n // tiles_per_kt + 1) * S5_KT)
        bu = _dot(ub[:, rows_k], bw_ref[rows_k, 2 * n * MXU_TILE:(2 * n + 2) * MXU_TILE])
        for c in range(n_chunks):
            rs = slice(c * t, (c + 1) * t)
            for cs, hv in zip(cols, halves):
                xr, xi = _cmul(er_ref[:, cs], ei_ref[:, cs], bu[rs, hv],
                               bu[rs, MXU_TILE + hv.start:MXU_TILE + hv.stop])
                xr_ref[rs, cs] = xr.astype(BF16)
                xi_ref[rs, cs] = xi.astype(BF16)

    row = lax.broadcasted_iota(jnp.int32, (t, t), 0)
    col = lax.broadcasted_iota(jnp.int32, (t, t), 1)
    tril = jnp.where(col <= row, 1.0, 0.0).astype(BF16)

    carry = [(car_ref[:, g], cai_ref[:, g]) for g in groups]
    for c in range(n_chunks):
        rs = slice(c * t, (c + 1) * t)
        sums = [(_dot(tril, xr_ref[rs, g]), _dot(tril, xi_ref[rs, g])) for g in groups]
        for n, (g, cols) in enumerate(zip(groups, group_cols)):
            sum_re, sum_im = sums[n][0] + carry[n][0], sums[n][1] + carry[n][1]
            hs = [_cmul(fr_ref[:, cs], fi_ref[:, cs], sum_re[:, hv], sum_im[:, hv]) for cs, hv in zip(cols, halves)]
            hr = jnp.concatenate([h[0] for h in hs], axis=1)
            hi = jnp.concatenate([h[1] for h in hs], axis=1)
            hr_ref[rs, g] = hr.astype(BF16)
            hi_ref[rs, g] = hi.astype(BF16)
            carry[n] = (hr[t - 1:t, :], hi[t - 1:t, :])
    for g, (car, cai) in zip(groups, carry):
        car_ref[:, g] = car
        cai_ref[:, g] = cai

    n_out = C_WIDTH // S5_KT
    k_rows = S5_N // n_out
    for j in range(n_out):
        ks = slice(j * k_rows, (j + 1) * k_rows)
        cs = slice(j * S5_KT, (j + 1) * S5_KT)
        y_ref[:, cs] = (_dot(hr_ref[:, ks], cr_ref[ks, cs]) - _dot(hi_ref[:, ks], ci_ref[ks, cs])
                        + d_ref[:, cs] * u[:, cs])

    @pl.when(step == pl.num_programs(1) - 1)
    def _():
        hs_ref[0:1, :] = car_ref[...]
        hs_ref[1:2, :] = cai_ref[...]


def _s5_prompt_call(layer, u, bw, c_re, c_im, d, tabs, n_seq, seq_rows, step_rows):
    steps = seq_rows // step_rows
    row_spec = pl.BlockSpec((step_rows, C_WIDTH), lambda s, j, steps=steps: (s * steps + j, 0))
    lsel = lambda s, j, l=layer: (l, 0, 0)
    tab = pl.BlockSpec((None, S5_CHUNK, S5_N), lsel)
    return pl.pallas_call(
        functools.partial(_s5_prompt_kernel, n_chunks=step_rows // S5_CHUNK),
        out_shape=[jax.ShapeDtypeStruct((n_seq * seq_rows, C_WIDTH), F32),
                   jax.ShapeDtypeStruct((n_seq, 2, S5_N), F32)],
        grid=(n_seq, steps),
        in_specs=[row_spec,
                  _resident((None, C_WIDTH, 2 * S5_N), lsel),
                  _resident((None, S5_N, C_WIDTH), lsel),
                  _resident((None, S5_N, C_WIDTH), lsel),
                  pl.BlockSpec((None, 1, C_WIDTH), lsel),
                  tab, tab, tab, tab],
        out_specs=[row_spec, pl.BlockSpec((None, 2, S5_N), lambda s, j: (s, 0, 0))],
        scratch_shapes=[pltpu.VMEM((1, S5_N), F32), pltpu.VMEM((1, S5_N), F32)]
                       + [pltpu.VMEM((step_rows, S5_N), BF16)] * 4,
        compiler_params=_cparams(("arbitrary", "arbitrary")),
        name="s5_chunked_scan",
    )(u, bw, c_re, c_im, d, *tabs)


def _s5_sample_kernel(u_ref, h0r_ref, h0i_ref, ar_ref, ai_ref, bw_ref, cr_ref, ci_ref, d_ref,
                      y_ref, hr_ref, hi_ref, *, n_tok, n_seq):
    hr, hi = h0r_ref[...], h0i_ref[...]
    a_re, a_im = ar_ref[...], ai_ref[...]
    for t in range(n_tok):
        u = u_ref[t * n_seq:(t + 1) * n_seq, :]
        bu = _dot(u.astype(BF16), bw_ref[...])
        tiles = range(S5_N // MXU_TILE)
        bu_re = jnp.concatenate([bu[:, 2 * n * MXU_TILE:(2 * n + 1) * MXU_TILE] for n in tiles], axis=1)
        bu_im = jnp.concatenate([bu[:, (2 * n + 1) * MXU_TILE:(2 * n + 2) * MXU_TILE] for n in tiles], axis=1)
        hr, hi = _cmul(a_re, a_im, hr, hi)
        hr, hi = hr + bu_re, hi + bu_im
        y_ref[t * n_seq:(t + 1) * n_seq, :] = (_dot(hr.astype(BF16), cr_ref[...]) - _dot(hi.astype(BF16), ci_ref[...])
                                               + d_ref[...] * u)
    hr_ref[...] = hr
    hi_ref[...] = hi


def _s5_sample_call(layer, u, h0_re, h0_im, a_re, a_im, bw, c_re, c_im, d, n_tok, n_seq):
    lsel = lambda i, l=layer: (l, 0, 0)
    full = lambda shape: pl.BlockSpec(shape, lambda i: (0,) * len(shape))
    return pl.pallas_call(
        functools.partial(_s5_sample_kernel, n_tok=n_tok, n_seq=n_seq),
        out_shape=[jax.ShapeDtypeStruct((n_tok * n_seq, C_WIDTH), F32),
                   jax.ShapeDtypeStruct((n_seq, S5_N), F32),
                   jax.ShapeDtypeStruct((n_seq, S5_N), F32)],
        grid=(1,),
        in_specs=[full((n_tok * n_seq, C_WIDTH)),
                  pl.BlockSpec((None, n_seq, S5_N), lsel),
                  pl.BlockSpec((None, n_seq, S5_N), lsel),
                  pl.BlockSpec((None, 1, S5_N), lsel),
                  pl.BlockSpec((None, 1, S5_N), lsel),
                  pl.BlockSpec((None, C_WIDTH, 2 * S5_N), lsel),
                  pl.BlockSpec((None, S5_N, C_WIDTH), lsel),
                  pl.BlockSpec((None, S5_N, C_WIDTH), lsel),
                  pl.BlockSpec((None, 1, C_WIDTH), lsel)],
        out_specs=[full((n_tok * n_seq, C_WIDTH)), full((n_seq, S5_N)), full((n_seq, S5_N))],
        compiler_params=_cparams(("arbitrary",)),
        name="s5_step_scan",
    )(u, h0_re, h0_im, a_re, a_im, bw, c_re, c_im, d)


def _merge_kernel(x_ref, g_ref, shs, shp, scs, scp, gts, gtp,
                  oap, oas, obp, obs, ocp, ocs,
                  w_ref, wa_ref, wb_ref, wc_ref, wglu_ref, wo_ref, o_ref, *, rows):
    x = x_ref[...]
    br_b = _dot(_select_rows(rows, obp, obs).astype(BF16), wb_ref[...])
    glu = _dot(_gelu_tanh(_select_rows(rows, ocp, ocs)).astype(BF16), wglu_ref[...])
    h = _ada_norm(x, g_ref[...], _mod_value(rows, shs, shp), _mod_value(rows, scs, scp)).astype(BF16)
    z = _dot_nt(h, w_ref[IN_OFF_GL:IN_WIDTH, :])
    z_ar = _dot_nt(h, w_ref[IN_OFF_AR:IN_OFF_BQ, :])
    oc = glu[:, :C_WIDTH] * _sigmoid(glu[:, C_WIDTH:])
    br_c = _dot(oc.astype(BF16), wc_ref[...])
    oa = _select_rows(rows, oap, oas) * _silu(z_ar)
    br_a = _dot(oa.astype(BF16), wa_ref[...])
    mix = (_sigmoid(z[:, D_MODEL:2 * D_MODEL]) * br_b
           + _sigmoid(z[:, 2 * D_MODEL:3 * D_MODEL]) * br_c
           + _sigmoid(z[:, 0:D_MODEL]) * br_a)
    out = _dot(mix.astype(BF16), wo_ref[...])
    o_ref[...] = x + _mod_value(rows, gts, gtp) * out


def _merge_call(rows, layer, x, mod, norm_g, oa, ob, oc, w_in_t, wa, wb, wc, wglu, wo):
    in_specs = [pl.BlockSpec((TOKEN_TILE, D_MODEL), lambda i: (i, 0)),
                pl.BlockSpec((None, 1, D_MODEL), lambda i, l=layer: (l, 0, 0))]
    operands = [x, norm_g]
    for j in (3, 4, 5):
        in_specs += _mod_specs(rows, layer, j)
        operands += [mod, mod]
    for pair, w in ((oa, A_V), (ob, B_Q), (oc, C_WIDTH)):
        in_specs += _pair_specs(rows, w)
        operands += list(pair)
    lsel = lambda i, l=layer: (l, 0, 0)
    in_specs += [_resident((None, IN_WIDTH, D_MODEL), lsel),
                 _resident((None, A_V, D_MODEL), lsel),
                 _resident((None, B_Q, D_MODEL), lsel),
                 _resident((None, C_WIDTH, D_MODEL), lsel),
                 _resident((None, C_WIDTH, 2 * C_WIDTH), lsel),
                 _resident((None, D_MODEL, D_MODEL), lsel)]
    operands += [w_in_t, wa, wb, wc, wglu, wo]
    return pl.pallas_call(
        functools.partial(_merge_kernel, rows=rows),
        out_shape=jax.ShapeDtypeStruct((rows.rows, D_MODEL), F32),
        grid=(rows.n_tiles,),
        in_specs=in_specs,
        out_specs=pl.BlockSpec((TOKEN_TILE, D_MODEL), lambda i: (i, 0)),
        compiler_params=_cparams(("arbitrary",)),
        name="branch_merge",
    )(*operands)


def _rotary_tables(rows):
    half = ROPE_DIM // 2
    inv_freq = ROPE_THETA ** (-jnp.arange(half, dtype=F32) / half)
    pos_p = jnp.arange(rows.l, dtype=jnp.int32)
    pos_s = PAST_LEN + jnp.repeat(jnp.arange(rows.t, dtype=jnp.int32), rows.bs)
    pos = jnp.concatenate([pos_p, pos_s])
    ang = pos.astype(F32)[:, None] * inv_freq[None, :]
    cos, sin = jnp.cos(ang), jnp.sin(ang)
    n = pos.shape[0]
    ones = jnp.ones((n, B_HEAD_DIM - ROPE_DIM), F32)
    zeros = jnp.zeros((n, B_HEAD_DIM - ROPE_DIM), F32)
    z8 = jnp.zeros((n, half), F32)
    per_head = lambda parts: jnp.tile(jnp.concatenate(parts, axis=1), (1, LANES // B_HEAD_DIM))
    return (per_head([cos, cos, ones]), per_head([z8, sin, zeros]), per_head([-sin, z8, zeros]))


def _sample_to_seq(a, rows, pad_to):
    w = a.shape[-1]
    a = a.reshape(rows.t, rows.bs, w).transpose(1, 0, 2)
    a = jnp.pad(a, ((0, 0), (0, pad_to - rows.t), (0, 0)))
    return a.reshape(rows.bs * pad_to, w)


def _seq_to_sample(a, rows, pad_to):
    w = a.shape[-1]
    return a.reshape(rows.bs, pad_to, w)[:, :rows.t].transpose(1, 0, 2).reshape(rows.ns_rows, w)


def kernel(x_prompt, x_sample, state_gla, cache_swa_k, cache_swa_v, state_s5_re, state_s5_im, c_prompt, c_sample, w_ada, b_ada, norm_ffn1, ffn1_w_gate, ffn1_w_up, ffn1_w_down, norm_mix, w_in, gla_w_gate_up, gla_b_gate, gla_norm, swa_sinks, s5_a_re, s5_a_im, s5_b_re, s5_b_im, s5_c_re, s5_c_im, s5_d, s5_log_dt, s5_w_glu, w_branch_a, w_branch_b, w_branch_c, w_out, norm_ffn2, ffn2_w_gate, ffn2_w_up, ffn2_w_down, norm_final):
    depth = w_ada.shape[0]
    b, l, _ = x_prompt.shape
    bs, t, _ = x_sample.shape
    rows = _Rows(b, l, bs, t)
    np_rows = rows.np_rows

    x = (x_prompt.reshape(np_rows, D_MODEL), x_sample.transpose(1, 0, 2).reshape(rows.ns_rows, D_MODEL))
    c_all = jnp.concatenate([c_sample, c_prompt, jnp.zeros((8 - b, D_MODEL), F32)], axis=0)
    mod = _mod_call(c_all, w_ada, b_ada)

    assert w_in.shape[-1] == IN_WIDTH
    w_in_t = w_in.transpose(0, 2, 1).astype(BF16)
    w_up = jnp.pad(gla_w_gate_up, ((0, 0), (0, LANES - A_GATE_RANK), (0, 0))).astype(BF16)
    cast = lambda w: w.astype(BF16)
    f1g, f1u, f1d = ffn1_w_gate, ffn1_w_up, ffn1_w_down
    f2g, f2u, f2d = ffn2_w_gate, ffn2_w_up, ffn2_w_down
    wa, wb, wc, wglu, wo = cast(w_branch_a), cast(w_branch_b), cast(w_branch_c), cast(s5_w_glu), cast(w_out)
    vec3 = lambda a: a.reshape(depth, 1, a.shape[-1])
    n1, nm, n2 = vec3(norm_ffn1), vec3(norm_mix), vec3(norm_ffn2)
    b_gate, g_norm, s5d = vec3(gla_b_gate), vec3(gla_norm), vec3(s5_d)
    rot = _rotary_tables(rows)

    flat = lambda a: a.reshape(depth, 1, S5_N)
    ldt = flat(jnp.broadcast_to(s5_log_dt[:, :, None], (depth, C_GROUPS, C_STATE)))
    to_cols = lambda bm: bm.transpose(0, 3, 1, 2).reshape(depth, C_GROUP, S5_N)
    to_rows = lambda cm: jnp.pad(cm.transpose(0, 1, 3, 2).reshape(depth, S5_N, C_GROUP),
                                 ((0, 0), (0, 0), (0, LANES - C_GROUP)))
    s5_bw, s5_cr, s5_ci, a_re, a_im, e_re, e_im, f_re, f_im = _s5_prep_call(
        flat(s5_a_re), flat(s5_a_im), ldt, to_cols(s5_b_re), to_cols(s5_b_im), to_rows(s5_c_re), to_rows(s5_c_im))
    tabs = (e_re, e_im, f_re, f_im)
    h0_re = state_s5_re.reshape(depth, bs, S5_N)
    h0_im = state_s5_im.reshape(depth, bs, S5_N)
    cache_kt = cache_swa_k.transpose(0, 1, 3, 4, 2)
    cache_vt = cache_swa_v.transpose(0, 1, 3, 4, 2)

    outs = {k: [] for k in ("gla_p", "gla_s", "kp", "ks", "vp", "vs", "rp", "rs", "ip", "is")}
    pad8 = SWA_SAMPLE_ROWS
    for layer in range(depth):
        x = _ffn_call(rows, layer, 0, x, mod, n1, f1g, f1u, f1d)
        gq, gk, gv, la, sq, sk, sv, su = _inproj_call(rows, layer, x, mod, nm, w_in_t, w_up, b_gate, rot)

        oa_p, st_p = _gla_call(layer, gq, gk, gv, la, g_norm, None, b, l, MIXER_STEP_ROWS, GLA_CHUNK, GLA_BLOCK)
        parts = dict(gq=gq, gk=gk, la=la, sk=sk, sv=sv, gv=gv)
        seq_in = _sample_to_seq(jnp.concatenate([a[np_rows:] for a in parts.values()], axis=1), rows, pad8)
        starts = dict(zip(parts, np.cumsum([0] + [a.shape[1] for a in parts.values()])))
        col_block = lambda name: int(starts[name]) // parts[name].shape[1]
        assert all(int(starts[n]) % parts[n].shape[1] == 0 for n in parts)
        oa_s, st_s = _gla_call(layer, seq_in, seq_in, seq_in, seq_in, g_norm, state_gla,
                               bs, pad8, pad8, pad8, pad8, seqs_per_step=GLA_SAMPLE_SEQS,
                               col_blocks=tuple(col_block(n) for n in ("gq", "gk", "gv", "la")))
        oa_s = _seq_to_sample(oa_s, rows, pad8)

        ob_p = _swa_call(layer, swa_sinks, sq, sk, sv, b, l // WINDOW)
        q_s = sq[np_rows:].reshape(t, bs, B_KV_HEADS, B_GROUP, B_HEAD_DIM).transpose(1, 2, 3, 0, 4)
        q_s = jnp.pad(q_s, ((0, 0), (0, 0), (0, 0), (0, pad8 - t), (0, 0)))
        q_s = q_s.reshape(bs, B_KV_HEADS, B_GROUP * pad8, B_HEAD_DIM)
        ob_s, kt_new, vt_new = _swa_sample_call(layer, swa_sinks, q_s, seq_in, seq_in, cache_kt, cache_vt, bs, t,
                                                col_blocks=(col_block("sk"), col_block("sv")))
        ob_s = ob_s.reshape(bs, B_KV_HEADS, B_GROUP, pad8, B_HEAD_DIM)[:, :, :, :t]
        ob_s = ob_s.transpose(3, 0, 1, 2, 4).reshape(rows.ns_rows, B_Q)

        oc_p, hs_p = _s5_prompt_call(layer, su, s5_bw, s5_cr, s5_ci, s5d, tabs, b, l, MIXER_STEP_ROWS)
        oc_s, hr_s, hi_s = _s5_sample_call(layer, su[np_rows:], h0_re, h0_im, a_re, a_im,
                                           s5_bw, s5_cr, s5_ci, s5d, t, bs)

        x = _merge_call(rows, layer, x, mod, nm, (oa_p, oa_s), (ob_p, ob_s), (oc_p, oc_s),
                        w_in_t, wa, wb, wc, wglu, wo)
        x = _ffn_call(rows, layer, 6, x, mod, n2, f2g, f2u, f2d,
                      final_g=norm_final.reshape(1, D_MODEL) if layer == depth - 1 else None)

        outs["gla_p"].append(st_p)
        outs["gla_s"].append(st_s)
        kv_shape = (b, WINDOW, B_KV_HEADS, B_HEAD_DIM)
        window_rows = lambda a: jnp.stack([a[(i + 1) * l - WINDOW:(i + 1) * l] for i in range(b)])
        outs["kp"].append(window_rows(sk).reshape(kv_shape))
        outs["vp"].append(window_rows(sv).reshape(kv_shape))
        outs["ks"].append(kt_new.transpose(0, 3, 1, 2))
        outs["vs"].append(vt_new.transpose(0, 3, 1, 2))
        outs["rp"].append(hs_p[:, 0].reshape(b, C_GROUPS, C_STATE))
        outs["ip"].append(hs_p[:, 1].reshape(b, C_GROUPS, C_STATE))
        outs["rs"].append(hr_s.reshape(bs, C_GROUPS, C_STATE))
        outs["is"].append(hi_s.reshape(bs, C_GROUPS, C_STATE))

    y_p = x[0].reshape(b, l, D_MODEL)
    y_s = x[1].reshape(t, bs, D_MODEL).transpose(1, 0, 2)
    st = lambda k: jnp.stack(outs[k])
    return (y_p, y_s, st("gla_p"), st("gla_s"), st("kp"), st("ks"), st("vp"), st("vs"),
            st("rp"), st("rs"), st("ip"), st("is"))
```

```python
import functools
import math

import jax
import jax.numpy as jnp
import numpy as np
from jax import lax
from jax.experimental import pallas as pl
from jax.experimental.pallas import tpu as pltpu

D_MODEL = 1024
D_FF = 2816
PAST_LEN = 8192
A_HEADS, A_DK, A_DV = 4, 64, 128
A_GATE_RANK = 16
A_GATE_NORM = 16.0
B_HEADS, B_KV_HEADS, B_HEAD_DIM = 8, 2, 64
B_GROUP = B_HEADS // B_KV_HEADS
WINDOW = 128
ROPE_DIM = B_HEAD_DIM // 4
ROPE_THETA = 500000.0
C_WIDTH, C_GROUP, C_STATE = 512, 16, 64
C_GROUPS = C_WIDTH // C_GROUP
MACARON_W = 0.5
NORM_EPS = 1e-6
N_MOD = 9

A_QK = A_HEADS * A_DK
A_V = A_HEADS * A_DV
B_Q = B_HEADS * B_HEAD_DIM
B_KV = B_KV_HEADS * B_HEAD_DIM
S5_N = C_GROUPS * C_STATE

LANES = 128
TOKEN_TILE = 512
MIXER_STEP_ROWS = 1024
GLA_CHUNK = 64
GLA_BLOCK = 32
GLA_SAMPLE_SEQS = 8
S5_CHUNK = 64
S5_KT = 256
VMEM_LIMIT = 56 * 1024 * 1024

F32 = jnp.float32
BF16 = jnp.bfloat16
NEG_BIG = -1e30
LOG2_E = math.log2(math.e)


def _cparams(sem):
    return pltpu.CompilerParams(dimension_semantics=sem, vmem_limit_bytes=VMEM_LIMIT)


def _dot(a, b):
    return jnp.dot(a, b, preferred_element_type=F32)


def _dot_nt(a, b):
    return lax.dot_general(a, b, (((1,), (1,)), ((), ())), preferred_element_type=F32)


def _sigmoid(x):
    return 1.0 / (1.0 + jnp.exp(-x))


def _silu(x):
    return x * _sigmoid(x)


def _gelu_tanh(x):
    return 0.5 * x * (1.0 + jnp.tanh(math.sqrt(2.0 / math.pi) * (x + 0.044715 * (x * x * x))))


def _log_sigmoid(x):
    return jnp.minimum(x, 0.0) - jnp.log1p(jnp.exp(-jnp.abs(x)))


def _resident(shape, index_map):
    return pl.BlockSpec(shape, index_map, pipeline_mode=pl.Buffered(1))


def _mod_kernel(c_ref, w_ref, b_ref, o_ref):
    c = c_ref[...]
    s = _silu(c).astype(BF16)
    o_ref[...] = _dot(s, w_ref[...].astype(BF16)) + b_ref[...]


MOD_BLOCK = 3 * D_MODEL


def _mod_call(c_all, w_ada, b_ada):
    depth = w_ada.shape[0]
    rows = c_all.shape[0]
    return pl.pallas_call(
        _mod_kernel,
        out_shape=jax.ShapeDtypeStruct((depth, rows, N_MOD * D_MODEL), F32),
        grid=(depth, N_MOD * D_MODEL // MOD_BLOCK),
        in_specs=[
            pl.BlockSpec((rows, D_MODEL), lambda l, j: (0, 0)),
            pl.BlockSpec((None, D_MODEL, MOD_BLOCK), lambda l, j: (l, 0, j)),
            pl.BlockSpec((None, 1, MOD_BLOCK), lambda l, j: (l, 0, j)),
        ],
        out_specs=pl.BlockSpec((None, rows, MOD_BLOCK), lambda l, j: (l, 0, j)),
        compiler_params=_cparams(("arbitrary", "arbitrary")),
        name="ada_mod",
    )(c_all, w_ada, b_ada.reshape(depth, 1, N_MOD * D_MODEL))


class _Rows:
    def __init__(self, n_prompt_seq, seq_len, n_sample_seq, n_sample_tok):
        self.b = n_prompt_seq
        self.l = seq_len
        self.bs = n_sample_seq
        self.t = n_sample_tok
        self.np_rows = n_prompt_seq * seq_len
        self.ns_rows = n_sample_seq * n_sample_tok
        assert seq_len % TOKEN_TILE == 0 and self.ns_rows == TOKEN_TILE
        assert n_sample_seq % 8 == 0 and seq_len % MIXER_STEP_ROWS == 0
        self.rows = self.np_rows + self.ns_rows
        self.n_prompt_tiles = self.np_rows // TOKEN_TILE
        self.tiles_per_seq = seq_len // TOKEN_TILE
        self.n_tiles = self.n_prompt_tiles + 1
        self.mod_rows = n_sample_seq + 8
        assert n_prompt_seq <= 8


def _mod_specs(rows, layer, j):
    return [
        pl.BlockSpec((None, rows.bs, D_MODEL), lambda i, l=layer, j=j: (l, 0, j)),
        pl.BlockSpec((None, 8, D_MODEL), lambda i, l=layer, j=j, r=rows.bs // 8: (l, r, j)),
    ]


def _mod_value(rows, ms_ref, mp_ref):
    i = pl.program_id(0)
    is_sample = i == rows.n_prompt_tiles
    seq = jnp.minimum(i // rows.tiles_per_seq, rows.b - 1)
    s = ms_ref[...]
    s = jnp.concatenate([s] * rows.t, axis=0)
    p = mp_ref[pl.ds(seq, 1), :]
    return jnp.where(is_sample, s, p)


def _ada_norm(x, g, shift, scale):
    var = jnp.mean(x * x, axis=-1, keepdims=True)
    return (x * lax.rsqrt(var + NORM_EPS)) * g * (1.0 + scale) + shift


def _select_rows(rows, p_ref, s_ref):
    is_sample = pl.program_id(0) == rows.n_prompt_tiles
    return jnp.where(is_sample, s_ref[...], p_ref[...])


def _pair_specs(rows, width):
    last = rows.n_prompt_tiles - 1
    return [
        pl.BlockSpec((TOKEN_TILE, width), lambda i, last=last: (jnp.minimum(i, last), 0)),
        pl.BlockSpec((TOKEN_TILE, width), lambda i: (0, 0)),
    ]


MXU_TILE = 256
FF_SPLITS = (0, 6 * MXU_TILE, D_FF)
assert D_FF % MXU_TILE == 0


FF_LOAD_CHUNKS = 8
FF_LOAD_SLOTS = 4


def _ffn_load_weights(layer, hbm, vmem, stages, sems):
    jobs = []
    for m in range(3):
        fam = 0 if m < 2 else 1
        n_rows = stages[fam].shape[1]
        assert hbm[m].shape[1] == n_rows * FF_LOAD_CHUNKS
        for c in range(FF_LOAD_CHUNKS):
            jobs.append((m, fam, c * n_rows, n_rows))
    slots, seen = [], [0, 0]
    for _, fam, _, _ in jobs:
        slots.append(seen[fam] % FF_LOAD_SLOTS)
        seen[fam] += 1

    def copy(n):
        m, fam, r0, n_rows = jobs[n]
        return pltpu.make_async_copy(hbm[m].at[layer, pl.ds(r0, n_rows), :], stages[fam].at[slots[n]],
                                     sems[fam].at[slots[n]])

    ahead = FF_LOAD_SLOTS - 1
    for n in range(min(ahead, len(jobs))):
        copy(n).start()
    for n, (m, fam, r0, n_rows) in enumerate(jobs):
        if n + ahead < len(jobs):
            copy(n + ahead).start()
        copy(n).wait()
        vmem[m][r0:r0 + n_rows, :] = stages[fam][slots[n]].astype(BF16)


def _ffn_kernel(*refs, rows, layer, x_pair, final_norm):
    n_in = 11 + int(x_pair) + int(final_norm)
    n_out = 2 if final_norm else 1
    ins, outs, scratch = refs[:n_in], refs[n_in:n_in + n_out], refs[n_in + n_out:]
    if x_pair:
        x = _select_rows(rows, ins[0], ins[1])
    else:
        x = ins[0][...]
    g_ref, shs, shp, scs, scp, gts, gtp, wg_hbm, wu_hbm, wd_hbm = ins[1 + int(x_pair):11 + int(x_pair)]
    wg_ref, wu_ref, wd_ref, stage_a, stage_d, sem_a, sem_d = scratch

    @pl.when(pl.program_id(0) == 0)
    def _():
        _ffn_load_weights(layer, (wg_hbm, wu_hbm, wd_hbm), (wg_ref, wu_ref, wd_ref),
                          (stage_a, stage_d), (sem_a, sem_d))

    h = _ada_norm(x, g_ref[...], _mod_value(rows, shs, shp), _mod_value(rows, scs, scp)).astype(BF16)
    chunks = [slice(lo, hi) for lo, hi in zip(FF_SPLITS[:-1], FF_SPLITS[1:])]
    gate_up = [(_dot(h, wg_ref[:, sl]), _dot(h, wu_ref[:, sl])) for sl in chunks]
    act = [(_silu(gate) * up).astype(BF16) for gate, up in gate_up]
    acc = _dot(act[0], wd_ref[chunks[0], :])
    for a, sl in zip(act[1:], chunks[1:]):
        acc = acc + _dot(a, wd_ref[sl, :])
    out = x + (MACARON_W * _mod_value(rows, gts, gtp)) * acc
    if not final_norm:
        outs[0][...] = out
        return
    var = jnp.mean(out * out, axis=-1, keepdims=True)
    y = out * lax.rsqrt(var + NORM_EPS) * ins[-1][...]
    is_sample = pl.program_id(0) == rows.n_prompt_tiles

    @pl.when(jnp.logical_not(is_sample))
    def _():
        outs[0][...] = y

    @pl.when(is_sample)
    def _():
        outs[1][...] = y


def _ffn_call(rows, layer, mod_base, x, mod, norm_g, wg, wu, wd, final_g=None):
    x_pair = isinstance(x, tuple)
    if x_pair:
        in_specs = _pair_specs(rows, D_MODEL)
        operands = list(x)
    else:
        in_specs = [pl.BlockSpec((TOKEN_TILE, D_MODEL), lambda i: (i, 0))]
        operands = [x]
    in_specs.append(pl.BlockSpec((None, 1, D_MODEL), lambda i, l=layer: (l, 0, 0)))
    operands.append(norm_g)
    for j in range(3):
        in_specs += _mod_specs(rows, layer, mod_base + j)
        operands += [mod, mod]
    in_specs += [pl.BlockSpec(memory_space=pl.ANY)] * 3
    operands += [wg, wu, wd]
    scratch_shapes = [
        pltpu.VMEM((D_MODEL, D_FF), BF16), pltpu.VMEM((D_MODEL, D_FF), BF16), pltpu.VMEM((D_FF, D_MODEL), BF16),
        pltpu.VMEM((FF_LOAD_SLOTS, D_MODEL // FF_LOAD_CHUNKS, D_FF), F32),
        pltpu.VMEM((FF_LOAD_SLOTS, D_FF // FF_LOAD_CHUNKS, D_MODEL), F32),
        pltpu.SemaphoreType.DMA((FF_LOAD_SLOTS,)), pltpu.SemaphoreType.DMA((FF_LOAD_SLOTS,)),
    ]
    if final_g is None:
        out_shape = jax.ShapeDtypeStruct((rows.rows, D_MODEL), F32)
        out_specs = pl.BlockSpec((TOKEN_TILE, D_MODEL), lambda i: (i, 0))
    else:
        in_specs.append(pl.BlockSpec((1, D_MODEL), lambda i: (0, 0)))
        operands.append(final_g)
        out_shape = [jax.ShapeDtypeStruct((rows.np_rows, D_MODEL), F32),
                     jax.ShapeDtypeStruct((rows.ns_rows, D_MODEL), F32)]
        out_specs = _pair_specs(rows, D_MODEL)
    return pl.pallas_call(
        functools.partial(_ffn_kernel, rows=rows, layer=layer, x_pair=x_pair, final_norm=final_g is not None),
        out_shape=out_shape,
        grid=(rows.n_tiles,),
        in_specs=in_specs,
        out_specs=out_specs,
        scratch_shapes=scratch_shapes,
        compiler_params=_cparams(("arbitrary",)),
        name="ada_swiglu",
    )(*operands)


IN_OFF_RANK = 2 * A_QK + A_V
IN_OFF_AR = IN_OFF_RANK + A_GATE_RANK
IN_OFF_BQ = IN_OFF_AR + A_V
IN_OFF_GL = IN_OFF_BQ + B_Q + 2 * B_KV + C_WIDTH
IN_WIDTH = IN_OFF_GL + 3 * D_MODEL


def _rotary(x, cos, s_up, s_dn):
    half = ROPE_DIM // 2
    return x * cos + pltpu.roll(x, half, axis=1) * s_up + pltpu.roll(x, LANES - half, axis=1) * s_dn


def _inproj_kernel(x_ref, g_ref, shs, shp, scs, scp, w_ref, wup_ref, bg_ref,
                   cos_ref, sup_ref, sdn_ref,
                   gq_ref, gk_ref, gv_ref, la_ref, sq_ref, sk_ref, sv_ref, su_ref, *, rows):
    x = x_ref[...]
    h = _ada_norm(x, g_ref[...], _mod_value(rows, shs, shp), _mod_value(rows, scs, scp)).astype(BF16)
    za = _dot_nt(h, w_ref[0:IN_OFF_RANK, :])
    gq_ref[...] = za[:, 0:A_QK] * (A_DK ** -0.5)
    gk_ref[...] = za[:, A_QK:2 * A_QK]
    gv_ref[...] = za[:, 2 * A_QK:]
    z = _dot_nt(h, w_ref[IN_OFF_BQ:IN_OFF_GL, :])
    o = 0
    cos, s_up, s_dn = cos_ref[...], sup_ref[...], sdn_ref[...]
    for c in range(B_Q // LANES):
        sq_ref[:, c * LANES:(c + 1) * LANES] = _rotary(z[:, o + c * LANES:o + (c + 1) * LANES], cos, s_up, s_dn)
    o += B_Q
    sk_ref[...] = _rotary(z[:, o:o + B_KV], cos, s_up, s_dn)
    o += B_KV
    sv_ref[...] = z[:, o:o + B_KV]
    o += B_KV
    su_ref[...] = z[:, o:o + C_WIDTH]
    low = _dot_nt(h, w_ref[IN_OFF_RANK:IN_OFF_RANK + LANES, :]).astype(BF16)
    pre = _dot(low, wup_ref[...]) + bg_ref[...]
    la_ref[...] = _log_sigmoid(pre) * (LOG2_E / A_GATE_NORM)


def _inproj_call(rows, layer, x, mod, norm_g, w_in_t, w_up, b_gate, rot):
    row_spec = lambda w: pl.BlockSpec((TOKEN_TILE, w), lambda i: (i, 0))
    rot_spec = pl.BlockSpec(
        (TOKEN_TILE, LANES),
        lambda i, n=rows.n_prompt_tiles, per=rows.tiles_per_seq: (jnp.where(i < n, i % per, per), 0))
    in_specs = [row_spec(D_MODEL), pl.BlockSpec((None, 1, D_MODEL), lambda i, l=layer: (l, 0, 0))]
    operands = [x, norm_g]
    for j in (3, 4):
        in_specs += _mod_specs(rows, layer, j)
        operands += [mod, mod]
    in_specs += [
        _resident((None, IN_WIDTH, D_MODEL), lambda i, l=layer: (l, 0, 0)),
        _resident((None, LANES, A_QK), lambda i, l=layer: (l, 0, 0)),
        pl.BlockSpec((None, 1, A_QK), lambda i, l=layer: (l, 0, 0)),
        rot_spec, rot_spec, rot_spec,
    ]
    operands += [w_in_t, w_up, b_gate, *rot]
    widths = (A_QK, A_QK, A_V, A_QK, B_Q, B_KV, B_KV, C_WIDTH)
    return pl.pallas_call(
        functools.partial(_inproj_kernel, rows=rows),
        out_shape=[jax.ShapeDtypeStruct((rows.rows, w), F32) for w in widths],
        grid=(rows.n_tiles,),
        in_specs=in_specs,
        out_specs=[row_spec(w) for w in widths],
        compiler_params=_cparams(("arbitrary",)),
        name="mixer_in_proj",
    )(*operands)


def _cumsum_rows(x):
    n = x.shape[0]
    row = lax.broadcasted_iota(jnp.int32, x.shape, 0)
    s = 1
    while s < n:
        x = x + jnp.where(row >= s, pltpu.roll(x, s, axis=0), 0.0)
        s *= 2
    return x


def _head_blockdiag(x, lanes_per_head):
    lane_head = lax.broadcasted_iota(jnp.int32, x.shape, 1) // lanes_per_head
    return jnp.concatenate([jnp.where(lane_head == h, x, 0.0) for h in range(A_HEADS)], axis=0)


def _gla_kernel(*refs, chunk, block, n_chunks, n_seqs, single_step, has_state):
    if has_state:
        q_ref, k_ref, v_ref, la_ref, gn_ref, s0_ref, o_ref, so_ref, st_ref = refs
    else:
        q_ref, k_ref, v_ref, la_ref, gn_ref, o_ref, so_ref, st_ref = refs
    n_blocks = chunk // block
    st_shape = (A_V, A_QK)
    st_row_head = lax.broadcasted_iota(jnp.int32, st_shape, 0) // A_DV
    st_lane_head = lax.broadcasted_iota(jnp.int32, st_shape, 1) // A_DK
    st_mask = st_row_head == st_lane_head
    srow = lax.broadcasted_iota(jnp.int32, (block, A_HEADS * block), 0)
    scol = lax.broadcasted_iota(jnp.int32, (block, A_HEADS * block), 1) % block
    causal = scol <= srow
    gn = gn_ref[...]

    def init_state(sq):
        st_ref[sq] = jnp.zeros(st_shape, F32)
        if has_state:
            for h in range(A_HEADS):
                st_ref[sq, h * A_DV:(h + 1) * A_DV, h * A_DK:(h + 1) * A_DK] = s0_ref[sq, h].T

    def emit_state(sq):
        for h in range(A_HEADS):
            so_ref[sq, h] = st_ref[sq, h * A_DV:(h + 1) * A_DV, h * A_DK:(h + 1) * A_DK].T

    if single_step:
        for sq in range(n_seqs):
            init_state(sq)
    else:
        pl.when(pl.program_id(1) == 0)(lambda: init_state(0))

    units = [(sq, c) for sq in range(n_seqs) for c in range(n_chunks)]
    rows_of = {u: slice((u[0] * n_chunks + u[1]) * chunk, (u[0] * n_chunks + u[1] + 1) * chunk) for u in units}

    q, k, v, b = {}, {}, {}, {}
    for u in units:
        q[u], k[u], v[u] = q_ref[rows_of[u], :], k_ref[rows_of[u], :], v_ref[rows_of[u], :]
        b[u] = _cumsum_rows(la_ref[rows_of[u], :])

    s_diag, s_off, upd, q_in = {}, {}, {}, {}
    for u in units:
        for i in range(n_blocks):
            lo, hi = i * block, (i + 1) * block
            qi, ki, bi = q[u][lo:hi], k[u][lo:hi], b[u][lo:hi]
            mid = b[u][lo + block // 2 - 1:lo + block // 2, :]
            qd = (qi * jnp.exp2(bi - mid)).astype(BF16)
            kd = _head_blockdiag(ki * jnp.exp2(mid - bi), A_DK).astype(BF16)
            s_diag[u, i] = _dot_nt(qd, kd)
            if i > 0:
                ref = b[u][lo - 1:lo, :]
                qo = (qi * jnp.exp2(bi - ref)).astype(BF16)
                ko = _head_blockdiag(k[u][:lo] * jnp.exp2(ref - b[u][:lo]), A_DK).astype(BF16)
                s_off[u, i] = _dot_nt(qo, ko)
        b_last = b[u][chunk - 1:chunk, :]
        upd[u] = _dot(v[u].T.astype(BF16), (k[u] * jnp.exp2(b_last - b[u])).astype(BF16))
        q_in[u] = (q[u] * jnp.exp2(b[u])).astype(BF16)

    intra = {}
    for u in units:
        outs = []
        for i in range(n_blocks):
            lo, hi = i * block, (i + 1) * block
            att = jnp.where(causal, s_diag[u, i], 0.0).astype(BF16)
            oi = _dot(att, _head_blockdiag(v[u][lo:hi], A_DV).astype(BF16))
            if i > 0:
                oi = oi + _dot(s_off[u, i].astype(BF16), _head_blockdiag(v[u][:lo], A_DV).astype(BF16))
            outs.append(oi)
        intra[u] = jnp.concatenate(outs, axis=0) if n_blocks > 1 else outs[0]

    for u in units:
        sq = u[0]
        st = st_ref[sq]
        o = _dot_nt(q_in[u], st.astype(BF16)) + intra[u]
        st_ref[sq] = st * jnp.exp2(b[u][chunk - 1:chunk, :]) + jnp.where(st_mask, upd[u], 0.0)
        normed = []
        for h in range(A_HEADS):
            oh = o[:, h * A_DV:(h + 1) * A_DV]
            var = jnp.mean(oh * oh, axis=-1, keepdims=True)
            normed.append(oh * lax.rsqrt(var + NORM_EPS) * gn)
        o_ref[rows_of[u], :] = jnp.concatenate(normed, axis=1)

    if single_step:
        for sq in range(n_seqs):
            emit_state(sq)
    else:
        pl.when(pl.program_id(1) == pl.num_programs(1) - 1)(lambda: emit_state(0))


def _gla_call(layer, q, k, v, la, gnorm, s0, n_seq, seq_rows, step_rows, chunk, block, seqs_per_step=1,
              col_blocks=(0, 0, 0, 0)):
    steps = seq_rows // step_rows
    sps = seqs_per_step
    assert sps == 1 or steps == 1
    blk_rows = sps * step_rows
    row_spec = lambda w, cb=0: pl.BlockSpec((blk_rows, w), lambda s, j, steps=steps, cb=cb: (s * steps + j, cb))
    in_specs = [row_spec(A_QK, col_blocks[0]), row_spec(A_QK, col_blocks[1]), row_spec(A_V, col_blocks[2]),
                row_spec(A_QK, col_blocks[3]),
                pl.BlockSpec((None, 1, A_DV), lambda s, j, l=layer: (l, 0, 0))]
    operands = [q, k, v, la, gnorm]
    if s0 is not None:
        in_specs.append(pl.BlockSpec((None, sps, A_HEADS, A_DK, A_DV), lambda s, j, l=layer: (l, s, 0, 0, 0)))
        operands.append(s0)
    return pl.pallas_call(
        functools.partial(_gla_kernel, chunk=chunk, block=block, n_chunks=step_rows // chunk,
                          n_seqs=sps, single_step=steps == 1, has_state=s0 is not None),
        out_shape=[jax.ShapeDtypeStruct((n_seq * seq_rows, A_V), F32),
                   jax.ShapeDtypeStruct((n_seq, A_HEADS, A_DK, A_DV), F32)],
        grid=(n_seq // sps, steps),
        in_specs=in_specs,
        out_specs=[row_spec(A_V),
                   pl.BlockSpec((sps, A_HEADS, A_DK, A_DV), lambda s, j: (s, 0, 0, 0))],
        scratch_shapes=[pltpu.VMEM((sps, A_V, A_QK), F32)],
        compiler_params=_cparams(("arbitrary", "arbitrary")),
        name="gla_chunked",
    )(*operands)


SWA_STEP_BLOCKS = MIXER_STEP_ROWS // WINDOW


def _swa_kernel(sink_ref, q_ref, kc_ref, vc_ref, kp_ref, vp_ref, o_ref, *, layer):
    tk = 2 * WINDOW
    kj = lax.broadcasted_iota(jnp.int32, (tk, WINDOW), 0)
    qi = lax.broadcasted_iota(jnp.int32, (tk, WINDOW), 1) + WINDOW
    band = (kj <= qi) & (kj >= qi - WINDOW)
    zeros = jnp.zeros((B_HEAD_DIM, WINDOW), F32)
    chains = [(blk, h) for blk in range(SWA_STEP_BLOCKS) for h in range(B_KV_HEADS)]

    keys, vals_t, q_t, valid = [], [], [], []
    for blk in range(SWA_STEP_BLOCKS):
        cur = slice(blk * WINDOW, (blk + 1) * WINDOW)
        if blk == 0:
            k_prev, v_prev = kp_ref[...], vp_ref[...]
            valid.append(band & (kj >= jnp.where(pl.program_id(1) > 0, 0, WINDOW)))
        else:
            prev = slice((blk - 1) * WINDOW, blk * WINDOW)
            k_prev, v_prev = kc_ref[prev, :], vc_ref[prev, :]
            valid.append(band)
        keys.append(jnp.concatenate([k_prev, kc_ref[cur, :]], axis=0).astype(BF16))
        vals_t.append(jnp.concatenate([v_prev, vc_ref[cur, :]], axis=0).T.astype(BF16))
        q_t.append(q_ref[cur, :].T * (LOG2_E * B_HEAD_DIM ** -0.5))

    scores = {}
    for blk, h in chains:
        cols = []
        for g in range(B_GROUP):
            hd = h * B_GROUP + g
            qg = q_t[blk][hd * B_HEAD_DIM:(hd + 1) * B_HEAD_DIM]
            cols.append(jnp.concatenate([qg, zeros] if h == 0 else [zeros, qg], axis=0))
        scores[blk, h] = _dot(keys[blk], jnp.concatenate(cols, axis=1).astype(BF16))

    probs, inv = {}, {}
    for blk, h in chains:
        ps = []
        for g in range(B_GROUP):
            sink = sink_ref[layer, h * B_GROUP + g] * LOG2_E
            sg = jnp.where(valid[blk], scores[blk, h][:, g * WINDOW:(g + 1) * WINDOW], NEG_BIG)
            m = jnp.maximum(jnp.max(sg, axis=0, keepdims=True), sink)
            p = jnp.exp2(sg - m)
            inv[blk, h, g] = 1.0 / (jnp.sum(p, axis=0, keepdims=True) + jnp.exp2(sink - m))
            ps.append(p.astype(BF16))
        probs[blk, h] = jnp.concatenate(ps, axis=1)

    out_t = {}
    for blk, h in chains:
        out_t[blk, h] = _dot(vals_t[blk][h * B_HEAD_DIM:(h + 1) * B_HEAD_DIM], probs[blk, h])

    for blk in range(SWA_STEP_BLOCKS):
        rows = [out_t[blk, h][:, g * WINDOW:(g + 1) * WINDOW] * inv[blk, h, g]
                for h in range(B_KV_HEADS) for g in range(B_GROUP)]
        o_ref[blk * WINDOW:(blk + 1) * WINDOW, :] = jnp.concatenate(rows, axis=0).T


def _swa_call(layer, sinks, q, k, v, n_seq, blocks_per_seq):
    sb = SWA_STEP_BLOCKS
    assert blocks_per_seq % sb == 0
    steps = blocks_per_seq // sb
    cur = lambda w: pl.BlockSpec((sb * WINDOW, w), lambda s, n, steps=steps: (s * steps + n, 0))
    prev = pl.BlockSpec((WINDOW, B_KV),
                        lambda s, n, nb=blocks_per_seq, sb=sb: (s * nb + jnp.maximum(n * sb - 1, 0), 0))
    return pl.pallas_call(
        functools.partial(_swa_kernel, layer=layer),
        out_shape=jax.ShapeDtypeStruct((n_seq * blocks_per_seq * WINDOW, B_Q), F32),
        grid=(n_seq, steps),
        in_specs=[pl.BlockSpec(memory_space=pltpu.SMEM), cur(B_Q), cur(B_KV), cur(B_KV), prev, prev],
        out_specs=cur(B_Q),
        compiler_params=_cparams(("arbitrary", "arbitrary")),
        name="swa_banded",
    )(sinks, q, k, v, k, v)


SWA_SAMPLE_SEQS = 8
SWA_SAMPLE_ROWS = 8


def _swa_sample_kernel(sink_ref, q_ref, kn_ref, vn_ref, kt_ref, vt_ref, o_ref, nk_ref, nv_ref, *, layer, n_tok):
    pr = SWA_SAMPLE_ROWS
    keep = WINDOW - n_tok
    lane = lax.broadcasted_iota(jnp.int32, (B_HEAD_DIM, WINDOW), 1)
    rows = B_GROUP * pr
    tok = lax.broadcasted_iota(jnp.int32, (rows, WINDOW), 0) % pr
    grp = lax.broadcasted_iota(jnp.int32, (rows, 1), 0) // pr
    pos = lax.broadcasted_iota(jnp.int32, (rows, WINDOW), 1)
    valid_new = pos - keep <= tok
    valid_old = (pos >= tok) & (pos < n_tok)

    def shifted_in(old, new_rows):
        padded = jnp.concatenate([jnp.zeros((WINDOW - pr, B_KV), F32), new_rows], axis=0)
        cols = pltpu.roll(padded, n_tok, axis=0).T
        return [jnp.where(lane < keep, pltpu.roll(old[h], keep, axis=1),
                          cols[h * B_HEAD_DIM:(h + 1) * B_HEAD_DIM]) for h in range(B_KV_HEADS)]

    chains = [(i, h) for i in range(SWA_SAMPLE_SEQS) for h in range(B_KV_HEADS)]
    k_old, v_old, k_new, v_new = {}, {}, {}, {}
    for i in range(SWA_SAMPLE_SEQS):
        rs = slice(i * pr, (i + 1) * pr)
        ko = [kt_ref[i, h] for h in range(B_KV_HEADS)]
        vo = [vt_ref[i, h] for h in range(B_KV_HEADS)]
        kn, vn = shifted_in(ko, kn_ref[rs, :]), shifted_in(vo, vn_ref[rs, :])
        for h in range(B_KV_HEADS):
            nk_ref[i, h] = kn[h]
            nv_ref[i, h] = vn[h]
            k_old[i, h], v_old[i, h] = ko[h].astype(BF16), vo[h].astype(BF16)
            k_new[i, h], v_new[i, h] = kn[h].astype(BF16), vn[h].astype(BF16)

    s_new, s_old = {}, {}
    for c in chains:
        q = (q_ref[c[0], c[1]] * (LOG2_E * B_HEAD_DIM ** -0.5)).astype(BF16)
        s_new[c], s_old[c] = _dot(q, k_new[c]), _dot(q, k_old[c])

    p_new, p_old, inv = {}, {}, {}
    for c in chains:
        sn = jnp.where(valid_new, s_new[c], NEG_BIG)
        so = jnp.where(valid_old, s_old[c], NEG_BIG)
        sink = jnp.zeros((rows, 1), F32)
        for g in range(B_GROUP):
            sink = jnp.where(grp == g, sink_ref[layer, c[1] * B_GROUP + g] * LOG2_E, sink)
        m = jnp.maximum(jnp.maximum(jnp.max(sn, axis=-1, keepdims=True),
                                    jnp.max(so, axis=-1, keepdims=True)), sink)
        pn, po = jnp.exp2(sn - m), jnp.exp2(so - m)
        inv[c] = 1.0 / (jnp.sum(pn, axis=-1, keepdims=True) + jnp.sum(po, axis=-1, keepdims=True)
                        + jnp.exp2(sink - m))
        p_new[c], p_old[c] = pn.astype(BF16), po.astype(BF16)

    for c in chains:
        o = _dot_nt(p_new[c], v_new[c]) + _dot_nt(p_old[c], v_old[c])
        o_ref[c[0], c[1]] = o * inv[c]


def _swa_sample_call(layer, sinks, q, k_new, v_new, kt, vt, n_seq, n_tok, col_blocks=(0, 0)):
    sb, pr = SWA_SAMPLE_SEQS, SWA_SAMPLE_ROWS
    qspec = pl.BlockSpec((sb, B_KV_HEADS, B_GROUP * pr, B_HEAD_DIM), lambda s: (s, 0, 0, 0))
    new_rows = lambda cb: pl.BlockSpec((sb * pr, B_KV), lambda s, cb=cb: (s, cb))
    buf_in = pl.BlockSpec((None, sb, B_KV_HEADS, B_HEAD_DIM, WINDOW), lambda s, l=layer: (l, s, 0, 0, 0))
    buf_out = pl.BlockSpec((sb, B_KV_HEADS, B_HEAD_DIM, WINDOW), lambda s: (s, 0, 0, 0))
    buf_sds = jax.ShapeDtypeStruct((n_seq, B_KV_HEADS, B_HEAD_DIM, WINDOW), F32)
    return pl.pallas_call(
        functools.partial(_swa_sample_kernel, layer=layer, n_tok=n_tok),
        out_shape=[jax.ShapeDtypeStruct((n_seq, B_KV_HEADS, B_GROUP * pr, B_HEAD_DIM), F32), buf_sds, buf_sds],
        grid=(n_seq // sb,),
        in_specs=[pl.BlockSpec(memory_space=pltpu.SMEM), qspec, new_rows(col_blocks[0]), new_rows(col_blocks[1]),
                  buf_in, buf_in],
        out_specs=[qspec, buf_out, buf_out],
        compiler_params=_cparams(("arbitrary",)),
        name="swa_decode",
    )(sinks, q, k_new, v_new, kt, vt)


def _s5_prep_kernel(lr_ref, li_ref, ldt_ref, br_ref, bi_ref, ctr_ref, cti_ref,
                    bw_ref, cr_ref, ci_ref, ar_ref, ai_ref, er_ref, ei_ref, fr_ref, fi_ref):
    lr, li = lr_ref[...], li_ref[...]
    dt = jnp.exp(ldt_ref[...])
    rho, th = lr * dt, li * dt
    mag = jnp.exp(rho)
    a_re, a_im = mag * jnp.cos(th), mag * jnp.sin(th)
    den = lr * lr + li * li
    nr, ni = a_re - 1.0, a_im
    f_re = (nr * lr + ni * li) / den
    f_im = (ni * lr - nr * li) / den
    br, bi = br_ref[...], bi_ref[...]
    bb_re = f_re * br - f_im * bi
    bb_im = f_re * bi + f_im * br
    lane_group = lax.broadcasted_iota(jnp.int32, bb_re.shape, 1) // C_STATE
    place = lambda bb: jnp.concatenate(
        [jnp.where(lane_group == g, bb, 0.0) for g in range(C_GROUPS)], axis=0).astype(BF16)
    placed_re, placed_im = place(bb_re), place(bb_im)
    for n in range(S5_N // MXU_TILE):
        src = slice(n * MXU_TILE, (n + 1) * MXU_TILE)
        bw_ref[:, 2 * n * MXU_TILE:(2 * n + 1) * MXU_TILE] = placed_re[:, src]
        bw_ref[:, (2 * n + 1) * MXU_TILE:(2 * n + 2) * MXU_TILE] = placed_im[:, src]
    spread_c = lax.broadcasted_iota(jnp.int32, (LANES, C_WIDTH), 0)
    spread_j = lax.broadcasted_iota(jnp.int32, (LANES, C_WIDTH), 1) % C_GROUP
    spread = jnp.where(spread_c == spread_j, 1.0, 0.0).astype(BF16)
    row_group = lax.broadcasted_iota(jnp.int32, cr_ref.shape, 0) // C_STATE
    col_group = lax.broadcasted_iota(jnp.int32, cr_ref.shape, 1) // C_GROUP
    own = row_group == col_group
    cr_ref[...] = jnp.where(own, _dot(ctr_ref[...].astype(BF16), spread), 0.0).astype(BF16)
    ci_ref[...] = jnp.where(own, _dot(cti_ref[...].astype(BF16), spread), 0.0).astype(BF16)
    ar_ref[...] = a_re
    ai_ref[...] = a_im
    s = (lax.broadcasted_iota(jnp.int32, er_ref.shape, 0) + 1).astype(F32)
    cs, sn = jnp.cos(s * th), jnp.sin(s * th)
    e_mag, f_mag = jnp.exp(-(s * rho)), jnp.exp(s * rho)
    er_ref[...] = e_mag * cs
    ei_ref[...] = -(e_mag * sn)
    fr_ref[...] = f_mag * cs
    fi_ref[...] = f_mag * sn


def _s5_prep_call(lr, li, ldt, b_re_t, b_im_t, c_re_t, c_im_t):
    depth = lr.shape[0]
    lsel = lambda l: (l, 0, 0)
    vec = pl.BlockSpec((None, 1, S5_N), lsel)
    mat = pl.BlockSpec((None, C_GROUP, S5_N), lsel)
    c_in = pl.BlockSpec((None, S5_N, LANES), lsel)
    cmat = pl.BlockSpec((None, S5_N, C_WIDTH), lsel)
    tab = pl.BlockSpec((None, S5_CHUNK, S5_N), lsel)
    sds = lambda r: jax.ShapeDtypeStruct((depth, r, S5_N), F32)
    c_sds = jax.ShapeDtypeStruct((depth, S5_N, C_WIDTH), BF16)
    return pl.pallas_call(
        _s5_prep_kernel,
        out_shape=[jax.ShapeDtypeStruct((depth, C_WIDTH, 2 * S5_N), BF16), c_sds, c_sds,
                   sds(1), sds(1)] + [sds(S5_CHUNK)] * 4,
        grid=(depth,),
        in_specs=[vec, vec, vec, mat, mat, c_in, c_in],
        out_specs=[pl.BlockSpec((None, C_WIDTH, 2 * S5_N), lsel), cmat, cmat, vec, vec, tab, tab, tab, tab],
        compiler_params=_cparams(("arbitrary",)),
        name="s5_discretise",
    )(lr, li, ldt, b_re_t, b_im_t, c_re_t, c_im_t)


def _cmul(ar, ai, br, bi):
    return ar * br - ai * bi, ar * bi + ai * br


def _s5_prompt_kernel(u_ref, bw_ref, cr_ref, ci_ref, d_ref, er_ref, ei_ref, fr_ref, fi_ref,
                      y_ref, hs_ref, car_ref, cai_ref, xr_ref, xi_ref, hr_ref, hi_ref, *, n_chunks):
    step = pl.program_id(1)
    t = S5_CHUNK

    @pl.when(step == 0)
    def _():
        car_ref[...] = jnp.zeros(car_ref.shape, F32)
        cai_ref[...] = jnp.zeros(cai_ref.shape, F32)

    groups = [slice(n * MXU_TILE, (n + 1) * MXU_TILE) for n in range(S5_N // MXU_TILE)]
    halves = [slice(0, LANES), slice(LANES, MXU_TILE)]
    group_cols = [[slice(g.start + hv.start, g.start + hv.stop) for hv in halves] for g in groups]

    u = u_ref[...]
    ub = u.astype(BF16)
    tiles_per_kt = len(groups) // (C_WIDTH // S5_KT)
    for n, cols in enumerate(group_cols):
        rows_k = slice(n // tiles_per_kt * S5_KT, (n // tiles_per_kt + 1) * S5_KT)
        bu = _dot(ub[:, rows_k], bw_ref[rows_k, 2 * n * MXU_TILE:(2 * n + 2) * MXU_TILE])
        for c in range(n_chunks):
            rs = slice(c * t, (c + 1) * t)
            for cs, hv in zip(cols, halves):
                xr, xi = _cmul(er_ref[:, cs], ei_ref[:, cs], bu[rs, hv],
                               bu[rs, MXU_TILE + hv.start:MXU_TILE + hv.stop])
                xr_ref[rs, cs] = xr.astype(BF16)
                xi_ref[rs, cs] = xi.astype(BF16)

    row = lax.broadcasted_iota(jnp.int32, (t, t), 0)
    col = lax.broadcasted_iota(jnp.int32, (t, t), 1)
    tril = jnp.where(col <= row, 1.0, 0.0).astype(BF16)

    carry = [(car_ref[:, g], cai_ref[:, g]) for g in groups]
    for c in range(n_chunks):
        rs = slice(c * t, (c + 1) * t)
        sums = [(_dot(tril, xr_ref[rs, g]), _dot(tril, xi_ref[rs, g])) for g in groups]
        for n, (g, cols) in enumerate(zip(groups, group_cols)):
            sum_re, sum_im = sums[n][0] + carry[n][0], sums[n][1] + carry[n][1]
            hs = [_cmul(fr_ref[:, cs], fi_ref[:, cs], sum_re[:, hv], sum_im[:, hv]) for cs, hv in zip(cols, halves)]
            hr = jnp.concatenate([h[0] for h in hs], axis=1)
            hi = jnp.concatenate([h[1] for h in hs], axis=1)
            hr_ref[rs, g] = hr.astype(BF16)
            hi_ref[rs, g] = hi.astype(BF16)
            carry[n] = (hr[t - 1:t, :], hi[t - 1:t, :])
    for g, (car, cai) in zip(groups, carry):
        car_ref[:, g] = car
        cai_ref[:, g] = cai

    n_out = C_WIDTH // S5_KT
    k_rows = S5_N // n_out
    for j in range(n_out):
        ks = slice(j * k_rows, (j + 1) * k_rows)
        cs = slice(j * S5_KT, (j + 1) * S5_KT)
        y_ref[:, cs] = (_dot(hr_ref[:, ks], cr_ref[ks, cs]) - _dot(hi_ref[:, ks], ci_ref[ks, cs])
                        + d_ref[:, cs] * u[:, cs])

    @pl.when(step == pl.num_programs(1) - 1)
    def _():
        hs_ref[0:1, :] = car_ref[...]
        hs_ref[1:2, :] = cai_ref[...]


def _s5_prompt_call(layer, u, bw, c_re, c_im, d, tabs, n_seq, seq_rows, step_rows):
    steps = seq_rows // step_rows
    row_spec = pl.BlockSpec((step_rows, C_WIDTH), lambda s, j, steps=steps: (s * steps + j, 0))
    lsel = lambda s, j, l=layer: (l, 0, 0)
    tab = pl.BlockSpec((None, S5_CHUNK, S5_N), lsel)
    return pl.pallas_call(
        functools.partial(_s5_prompt_kernel, n_chunks=step_rows // S5_CHUNK),
        out_shape=[jax.ShapeDtypeStruct((n_seq * seq_rows, C_WIDTH), F32),
                   jax.ShapeDtypeStruct((n_seq, 2, S5_N), F32)],
        grid=(n_seq, steps),
        in_specs=[row_spec,
                  _resident((None, C_WIDTH, 2 * S5_N), lsel),
                  _resident((None, S5_N, C_WIDTH), lsel),
                  _resident((None, S5_N, C_WIDTH), lsel),
                  pl.BlockSpec((None, 1, C_WIDTH), lsel),
                  tab, tab, tab, tab],
        out_specs=[row_spec, pl.BlockSpec((None, 2, S5_N), lambda s, j: (s, 0, 0))],
        scratch_shapes=[pltpu.VMEM((1, S5_N), F32), pltpu.VMEM((1, S5_N), F32)]
                       + [pltpu.VMEM((step_rows, S5_N), BF16)] * 4,
        compiler_params=_cparams(("arbitrary", "arbitrary")),
        name="s5_chunked_scan",
    )(u, bw, c_re, c_im, d, *tabs)


def _s5_sample_kernel(u_ref, h0r_ref, h0i_ref, ar_ref, ai_ref, bw_ref, cr_ref, ci_ref, d_ref,
                      y_ref, hr_ref, hi_ref, *, n_tok, n_seq):
    hr, hi = h0r_ref[...], h0i_ref[...]
    a_re, a_im = ar_ref[...], ai_ref[...]
    for t in range(n_tok):
        u = u_ref[t * n_seq:(t + 1) * n_seq, :]
        bu = _dot(u.astype(BF16), bw_ref[...])
        tiles = range(S5_N // MXU_TILE)
        bu_re = jnp.concatenate([bu[:, 2 * n * MXU_TILE:(2 * n + 1) * MXU_TILE] for n in tiles], axis=1)
        bu_im = jnp.concatenate([bu[:, (2 * n + 1) * MXU_TILE:(2 * n + 2) * MXU_TILE] for n in tiles], axis=1)
        hr, hi = _cmul(a_re, a_im, hr, hi)
        hr, hi = hr + bu_re, hi + bu_im
        y_ref[t * n_seq:(t + 1) * n_seq, :] = (_dot(hr.astype(BF16), cr_ref[...]) - _dot(hi.astype(BF16), ci_ref[...])
                                               + d_ref[...] * u)
    hr_ref[...] = hr
    hi_ref[...] = hi


def _s5_sample_call(layer, u, h0_re, h0_im, a_re, a_im, bw, c_re, c_im, d, n_tok, n_seq):
    lsel = lambda i, l=layer: (l, 0, 0)
    full = lambda shape: pl.BlockSpec(shape, lambda i: (0,) * len(shape))
    return pl.pallas_call(
        functools.partial(_s5_sample_kernel, n_tok=n_tok, n_seq=n_seq),
        out_shape=[jax.ShapeDtypeStruct((n_tok * n_seq, C_WIDTH), F32),
                   jax.ShapeDtypeStruct((n_seq, S5_N), F32),
                   jax.ShapeDtypeStruct((n_seq, S5_N), F32)],
        grid=(1,),
        in_specs=[full((n_tok * n_seq, C_WIDTH)),
                  pl.BlockSpec((None, n_seq, S5_N), lsel),
                  pl.BlockSpec((None, n_seq, S5_N), lsel),
                  pl.BlockSpec((None, 1, S5_N), lsel),
                  pl.BlockSpec((None, 1, S5_N), lsel),
                  pl.BlockSpec((None, C_WIDTH, 2 * S5_N), lsel),
                  pl.BlockSpec((None, S5_N, C_WIDTH), lsel),
                  pl.BlockSpec((None, S5_N, C_WIDTH), lsel),
                  pl.BlockSpec((None, 1, C_WIDTH), lsel)],
        out_specs=[full((n_tok * n_seq, C_WIDTH)), full((n_seq, S5_N)), full((n_seq, S5_N))],
        compiler_params=_cparams(("arbitrary",)),
        name="s5_step_scan",
    )(u, h0_re, h0_im, a_re, a_im, bw, c_re, c_im, d)


def _merge_kernel(x_ref, g_ref, shs, shp, scs, scp, gts, gtp,
                  oap, oas, obp, obs, ocp, ocs,
                  w_ref, wa_ref, wb_ref, wc_ref, wglu_ref, wo_ref, o_ref, *, rows):
    x = x_ref[...]
    br_b = _dot(_select_rows(rows, obp, obs).astype(BF16), wb_ref[...])
    glu = _dot(_gelu_tanh(_select_rows(rows, ocp, ocs)).astype(BF16), wglu_ref[...])
    h = _ada_norm(x, g_ref[...], _mod_value(rows, shs, shp), _mod_value(rows, scs, scp)).astype(BF16)
    z = _dot_nt(h, w_ref[IN_OFF_GL:IN_WIDTH, :])
    z_ar = _dot_nt(h, w_ref[IN_OFF_AR:IN_OFF_BQ, :])
    oc = glu[:, :C_WIDTH] * _sigmoid(glu[:, C_WIDTH:])
    br_c = _dot(oc.astype(BF16), wc_ref[...])
    oa = _select_rows(rows, oap, oas) * _silu(z_ar)
    br_a = _dot(oa.astype(BF16), wa_ref[...])
    mix = (_sigmoid(z[:, D_MODEL:2 * D_MODEL]) * br_b
           + _sigmoid(z[:, 2 * D_MODEL:3 * D_MODEL]) * br_c
           + _sigmoid(z[:, 0:D_MODEL]) * br_a)
    out = _dot(mix.astype(BF16), wo_ref[...])
    o_ref[...] = x + _mod_value(rows, gts, gtp) * out


def _merge_call(rows, layer, x, mod, norm_g, oa, ob, oc, w_in_t, wa, wb, wc, wglu, wo):
    in_specs = [pl.BlockSpec((TOKEN_TILE, D_MODEL), lambda i: (i, 0)),
                pl.BlockSpec((None, 1, D_MODEL), lambda i, l=layer: (l, 0, 0))]
    operands = [x, norm_g]
    for j in (3, 4, 5):
        in_specs += _mod_specs(rows, layer, j)
        operands += [mod, mod]
    for pair, w in ((oa, A_V), (ob, B_Q), (oc, C_WIDTH)):
        in_specs += _pair_specs(rows, w)
        operands += list(pair)
    lsel = lambda i, l=layer: (l, 0, 0)
    in_specs += [_resident((None, IN_WIDTH, D_MODEL), lsel),
                 _resident((None, A_V, D_MODEL), lsel),
                 _resident((None, B_Q, D_MODEL), lsel),
                 _resident((None, C_WIDTH, D_MODEL), lsel),
                 _resident((None, C_WIDTH, 2 * C_WIDTH), lsel),
                 _resident((None, D_MODEL, D_MODEL), lsel)]
    operands += [w_in_t, wa, wb, wc, wglu, wo]
    return pl.pallas_call(
        functools.partial(_merge_kernel, rows=rows),
        out_shape=jax.ShapeDtypeStruct((rows.rows, D_MODEL), F32),
        grid=(rows.n_tiles,),
        in_specs=in_specs,
        out_specs=pl.BlockSpec((TOKEN_TILE, D_MODEL), lambda i: (i, 0)),
        compiler_params=_cparams(("arbitrary",)),
        name="branch_merge",
    )(*operands)


def _rotary_tables(rows):
    half = ROPE_DIM // 2
    inv_freq = ROPE_THETA ** (-jnp.arange(half, dtype=F32) / half)
    pos_p = jnp.arange(rows.l, dtype=jnp.int32)
    pos_s = PAST_LEN + jnp.repeat(jnp.arange(rows.t, dtype=jnp.int32), rows.bs)
    pos = jnp.concatenate([pos_p, pos_s])
    ang = pos.astype(F32)[:, None] * inv_freq[None, :]
    cos, sin = jnp.cos(ang), jnp.sin(ang)
    n = pos.shape[0]
    ones = jnp.ones((n, B_HEAD_DIM - ROPE_DIM), F32)
    zeros = jnp.zeros((n, B_HEAD_DIM - ROPE_DIM), F32)
    z8 = jnp.zeros((n, half), F32)
    per_head = lambda parts: jnp.tile(jnp.concatenate(parts, axis=1), (1, LANES // B_HEAD_DIM))
    return (per_head([cos, cos, ones]), per_head([z8, sin, zeros]), per_head([-sin, z8, zeros]))


def _sample_to_seq(a, rows, pad_to):
    w = a.shape[-1]
    a = a.reshape(rows.t, rows.bs, w).transpose(1, 0, 2)
    a = jnp.pad(a, ((0, 0), (0, pad_to - rows.t), (0, 0)))
    return a.reshape(rows.bs * pad_to, w)


def _seq_to_sample(a, rows, pad_to):
    w = a.shape[-1]
    return a.reshape(rows.bs, pad_to, w)[:, :rows.t].transpose(1, 0, 2).reshape(rows.ns_rows, w)


def kernel(x_prompt, x_sample, state_gla, cache_swa_k, cache_swa_v, state_s5_re, state_s5_im, c_prompt, c_sample, w_ada, b_ada, norm_ffn1, ffn1_w_gate, ffn1_w_up, ffn1_w_down, norm_mix, w_in, gla_w_gate_up, gla_b_gate, gla_norm, swa_sinks, s5_a_re, s5_a_im, s5_b_re, s5_b_im, s5_c_re, s5_c_im, s5_d, s5_log_dt, s5_w_glu, w_branch_a, w_branch_b, w_branch_c, w_out, norm_ffn2, ffn2_w_gate, ffn2_w_up, ffn2_w_down, norm_final):
    depth = w_ada.shape[0]
    b, l, _ = x_prompt.shape
    bs, t, _ = x_sample.shape
    rows = _Rows(b, l, bs, t)
    np_rows = rows.np_rows

    x = (x_prompt.reshape(np_rows, D_MODEL), x_sample.transpose(1, 0, 2).reshape(rows.ns_rows, D_MODEL))
    c_all = jnp.concatenate([c_sample, c_prompt, jnp.zeros((8 - b, D_MODEL), F32)], axis=0)
    mod = _mod_call(c_all, w_ada, b_ada)

    assert w_in.shape[-1] == IN_WIDTH
    w_in_t = w_in.transpose(0, 2, 1).astype(BF16)
    w_up = jnp.pad(gla_w_gate_up, ((0, 0), (0, LANES - A_GATE_RANK), (0, 0))).astype(BF16)
    cast = lambda w: w.astype(BF16)
    f1g, f1u, f1d = ffn1_w_gate, ffn1_w_up, ffn1_w_down
    f2g, f2u, f2d = ffn2_w_gate, ffn2_w_up, ffn2_w_down
    wa, wb, wc, wglu, wo = cast(w_branch_a), cast(w_branch_b), cast(w_branch_c), cast(s5_w_glu), cast(w_out)
    vec3 = lambda a: a.reshape(depth, 1, a.shape[-1])
    n1, nm, n2 = vec3(norm_ffn1), vec3(norm_mix), vec3(norm_ffn2)
    b_gate, g_norm, s5d = vec3(gla_b_gate), vec3(gla_norm), vec3(s5_d)
    rot = _rotary_tables(rows)

    flat = lambda a: a.reshape(depth, 1, S5_N)
    ldt = flat(jnp.broadcast_to(s5_log_dt[:, :, None], (depth, C_GROUPS, C_STATE)))
    to_cols = lambda bm: bm.transpose(0, 3, 1, 2).reshape(depth, C_GROUP, S5_N)
    to_rows = lambda cm: jnp.pad(cm.transpose(0, 1, 3, 2).reshape(depth, S5_N, C_GROUP),
                                 ((0, 0), (0, 0), (0, LANES - C_GROUP)))
    s5_bw, s5_cr, s5_ci, a_re, a_im, e_re, e_im, f_re, f_im = _s5_prep_call(
        flat(s5_a_re), flat(s5_a_im), ldt, to_cols(s5_b_re), to_cols(s5_b_im), to_rows(s5_c_re), to_rows(s5_c_im))
    tabs = (e_re, e_im, f_re, f_im)
    h0_re = state_s5_re.reshape(depth, bs, S5_N)
    h0_im = state_s5_im.reshape(depth, bs, S5_N)
    cache_kt = cache_swa_k.transpose(0, 1, 3, 4, 2)
    cache_vt = cache_swa_v.transpose(0, 1, 3, 4, 2)

    outs = {k: [] for k in ("gla_p", "gla_s", "kp", "ks", "vp", "vs", "rp", "rs", "ip", "is")}
    pad8 = SWA_SAMPLE_ROWS
    for layer in range(depth):
        x = _ffn_call(rows, layer, 0, x, mod, n1, f1g, f1u, f1d)
        gq, gk, gv, la, sq, sk, sv, su = _inproj_call(rows, layer, x, mod, nm, w_in_t, w_up, b_gate, rot)

        oa_p, st_p = _gla_call(layer, gq, gk, gv, la, g_norm, None, b, l, MIXER_STEP_ROWS, GLA_CHUNK, GLA_BLOCK)
        parts = dict(gq=gq, gk=gk, la=la, sk=sk, sv=sv, gv=gv)
        seq_in = _sample_to_seq(jnp.concatenate([a[np_rows:] for a in parts.values()], axis=1), rows, pad8)
        starts = dict(zip(parts, np.cumsum([0] + [a.shape[1] for a in parts.values()])))
        col_block = lambda name: int(starts[name]) // parts[name].shape[1]
        assert all(int(starts[n]) % parts[n].shape[1] == 0 for n in parts)
        oa_s, st_s = _gla_call(layer, seq_in, seq_in, seq_in, seq_in, g_norm, state_gla,
                               bs, pad8, pad8, pad8, pad8, seqs_per_step=GLA_SAMPLE_SEQS,
                               col_blocks=tuple(col_block(n) for n in ("gq", "gk", "gv", "la")))
        oa_s = _seq_to_sample(oa_s, rows, pad8)

        ob_p = _swa_call(layer, swa_sinks, sq, sk, sv, b, l // WINDOW)
        q_s = sq[np_rows:].reshape(t, bs, B_KV_HEADS, B_GROUP, B_HEAD_DIM).transpose(1, 2, 3, 0, 4)
        q_s = jnp.pad(q_s, ((0, 0), (0, 0), (0, 0), (0, pad8 - t), (0, 0)))
        q_s = q_s.reshape(bs, B_KV_HEADS, B_GROUP * pad8, B_HEAD_DIM)
        ob_s, kt_new, vt_new = _swa_sample_call(layer, swa_sinks, q_s, seq_in, seq_in, cache_kt, cache_vt, bs, t,
                                                col_blocks=(col_block("sk"), col_block("sv")))
        ob_s = ob_s.reshape(bs, B_KV_HEADS, B_GROUP, pad8, B_HEAD_DIM)[:, :, :, :t]
        ob_s = ob_s.transpose(3, 0, 1, 2, 4).reshape(rows.ns_rows, B_Q)

        oc_p, hs_p = _s5_prompt_call(layer, su, s5_bw, s5_cr, s5_ci, s5d, tabs, b, l, MIXER_STEP_ROWS)
        oc_s, hr_s, hi_s = _s5_sample_call(layer, su[np_rows:], h0_re, h0_im, a_re, a_im,
                                           s5_bw, s5_cr, s5_ci, s5d, t, bs)

        x = _merge_call(rows, layer, x, mod, nm, (oa_p, oa_s), (ob_p, ob_s), (oc_p, oc_s),
                        w_in_t, wa, wb, wc, wglu, wo)
        x = _ffn_call(rows, layer, 6, x, mod, n2, f2g, f2u, f2d,
                      final_g=norm_final.reshape(1, D_MODEL) if layer == depth - 1 else None)

        outs["gla_p"].append(st_p)
        outs["gla_s"].append(st_s)
        kv_shape = (b, WINDOW, B_KV_HEADS, B_HEAD_DIM)
        window_rows = lambda a: jnp.stack([a[(i + 1) * l - WINDOW:(i + 1) * l] for i in range(b)])
        outs["kp"].append(window_rows(sk).reshape(kv_shape))
        outs["vp"].append(window_rows(sv).reshape(kv_shape))
        outs["ks"].append(kt_new.transpose(0, 3, 1, 2))
        outs["vs"].append(vt_new.transpose(0, 3, 1, 2))
        outs["rp"].append(hs_p[:, 0].reshape(b, C_GROUPS, C_STATE))
        outs["ip"].append(hs_p[:, 1].reshape(b, C_GROUPS, C_STATE))
        outs["rs"].append(hr_s.reshape(bs, C_GROUPS, C_STATE))
        outs["is"].append(hi_s.reshape(bs, C_GROUPS, C_STATE))

    y_p = x[0].reshape(b, l, D_MODEL)
    y_s = x[1].reshape(t, bs, D_MODEL).transpose(1, 0, 2)
    st = lambda k: jnp.stack(outs[k])
    return (y_p, y_s, st("gla_p"), st("gla_s"), st("kp"), st("ks"), st("vp"), st("vs"),
            st("rp"), st("rs"), st("ip"), st("is"))
```

```python
import functools
import math

import jax
import jax.numpy as jnp
import numpy as np
from jax import lax
from jax.experimental import pallas as pl
from jax.experimental.pallas import tpu as pltpu

D_MODEL = 1024
D_FF = 2816
PAST_LEN = 8192
A_HEADS, A_DK, A_DV = 4, 64, 128
A_GATE_RANK = 16
A_GATE_NORM = 16.0
B_HEADS, B_KV_HEADS, B_HEAD_DIM = 8, 2, 64
B_GROUP = B_HEADS // B_KV_HEADS
WINDOW = 128
ROPE_DIM = B_HEAD_DIM // 4
ROPE_THETA = 500000.0
C_WIDTH, C_GROUP, C_STATE = 512, 16, 64
C_GROUPS = C_WIDTH // C_GROUP
MACARON_W = 0.5
NORM_EPS = 1e-6
N_MOD = 9

A_QK = A_HEADS * A_DK
A_V = A_HEADS * A_DV
B_Q = B_HEADS * B_HEAD_DIM
B_KV = B_KV_HEADS * B_HEAD_DIM
S5_N = C_GROUPS * C_STATE

LANES = 128
SUBLANES = 8
TOKEN_TILE = 512
MIXER_STEP_ROWS = 1024
GLA_CHUNK = 64
GLA_BLOCK = 32
GLA_SAMPLE_SEQS = 8
S5_CHUNK = 64
S5_KT = 256
VMEM_LIMIT = 56 * 1024 * 1024

F32 = jnp.float32
BF16 = jnp.bfloat16
NEG_BIG = -1e30
LOG2_E = math.log2(math.e)


def _cparams(sem):
    return pltpu.CompilerParams(dimension_semantics=sem, vmem_limit_bytes=VMEM_LIMIT)


def _dot(a, b):
    return jnp.dot(a, b, preferred_element_type=F32)


def _dot_nt(a, b):
    return lax.dot_general(a, b, (((1,), (1,)), ((), ())), preferred_element_type=F32)


def _sigmoid(x):
    return 1.0 / (1.0 + jnp.exp(-x))


def _silu(x):
    return x * _sigmoid(x)


def _gelu_tanh(x):
    return 0.5 * x * (1.0 + jnp.tanh(math.sqrt(2.0 / math.pi) * (x + 0.044715 * (x * x * x))))


def _log_sigmoid(x):
    return jnp.minimum(x, 0.0) - jnp.log1p(jnp.exp(-jnp.abs(x)))


def _resident(shape, index_map):
    return pl.BlockSpec(shape, index_map, pipeline_mode=pl.Buffered(1))


def _mod_kernel(c_ref, w_ref, b_ref, o_ref):
    c = c_ref[...]
    s = _silu(c).astype(BF16)
    o_ref[...] = _dot(s, w_ref[...].astype(BF16)) + b_ref[...]


MOD_BLOCK = 3 * D_MODEL


def _mod_call(c_all, w_ada, b_ada):
    depth = w_ada.shape[0]
    rows = c_all.shape[0]
    return pl.pallas_call(
        _mod_kernel,
        out_shape=jax.ShapeDtypeStruct((depth, rows, N_MOD * D_MODEL), F32),
        grid=(depth, N_MOD * D_MODEL // MOD_BLOCK),
        in_specs=[
            pl.BlockSpec((rows, D_MODEL), lambda l, j: (0, 0)),
            pl.BlockSpec((None, D_MODEL, MOD_BLOCK), lambda l, j: (l, 0, j)),
            pl.BlockSpec((None, 1, MOD_BLOCK), lambda l, j: (l, 0, j)),
        ],
        out_specs=pl.BlockSpec((None, rows, MOD_BLOCK), lambda l, j: (l, 0, j)),
        compiler_params=_cparams(("arbitrary", "arbitrary")),
        name="ada_mod",
    )(c_all, w_ada, b_ada.reshape(depth, 1, N_MOD * D_MODEL))


class _Rows:
    def __init__(self, n_prompt_seq, seq_len, n_sample_seq, n_sample_tok):
        self.b = n_prompt_seq
        self.l = seq_len
        self.bs = n_sample_seq
        self.t = n_sample_tok
        self.np_rows = n_prompt_seq * seq_len
        self.ns_rows = n_sample_seq * n_sample_tok
        assert seq_len % TOKEN_TILE == 0 and self.ns_rows == TOKEN_TILE
        assert n_sample_seq % SUBLANES == 0 and seq_len % MIXER_STEP_ROWS == 0
        self.rows = self.np_rows + self.ns_rows
        self.n_prompt_tiles = self.np_rows // TOKEN_TILE
        self.tiles_per_seq = seq_len // TOKEN_TILE
        self.n_tiles = self.n_prompt_tiles + 1
        assert n_prompt_seq <= SUBLANES


def _mod_specs(rows, layer, j):
    return [
        pl.BlockSpec((None, rows.bs, D_MODEL), lambda i, l=layer, j=j: (l, 0, j)),
        pl.BlockSpec((None, SUBLANES, D_MODEL), lambda i, l=layer, j=j, r=rows.bs // SUBLANES: (l, r, j)),
    ]


def _mod_value(rows, ms_ref, mp_ref):
    i = pl.program_id(0)
    is_sample = i == rows.n_prompt_tiles
    seq = jnp.minimum(i // rows.tiles_per_seq, rows.b - 1)
    s = ms_ref[...]
    s = jnp.concatenate([s] * rows.t, axis=0)
    p = mp_ref[pl.ds(seq, 1), :]
    return jnp.where(is_sample, s, p)


def _ada_norm(x, g, shift, scale):
    var = jnp.mean(x * x, axis=-1, keepdims=True)
    return (x * lax.rsqrt(var + NORM_EPS)) * g * (1.0 + scale) + shift


def _select_rows(rows, p_ref, s_ref):
    is_sample = pl.program_id(0) == rows.n_prompt_tiles
    return jnp.where(is_sample, s_ref[...], p_ref[...])


def _pair_specs(rows, width):
    last = rows.n_prompt_tiles - 1
    return [
        pl.BlockSpec((TOKEN_TILE, width), lambda i, last=last: (jnp.minimum(i, last), 0)),
        pl.BlockSpec((TOKEN_TILE, width), lambda i: (0, 0)),
    ]


MXU_TILE = 256
FF_SPLITS = (0, 6 * MXU_TILE, D_FF)
assert D_FF % MXU_TILE == 0


FF_LOAD_CHUNKS = 8
FF_LOAD_SLOTS = 4


def _ffn_load_weights(layer, hbm, vmem, stages, sems):
    jobs = []
    for m in range(3):
        fam = 0 if m < 2 else 1
        n_rows = stages[fam].shape[1]
        assert hbm[m].shape[1] == n_rows * FF_LOAD_CHUNKS
        for c in range(FF_LOAD_CHUNKS):
            jobs.append((m, fam, c * n_rows, n_rows))
    slots, seen = [], [0, 0]
    for _, fam, _, _ in jobs:
        slots.append(seen[fam] % FF_LOAD_SLOTS)
        seen[fam] += 1

    def copy(n):
        m, fam, r0, n_rows = jobs[n]
        return pltpu.make_async_copy(hbm[m].at[layer, pl.ds(r0, n_rows), :], stages[fam].at[slots[n]],
                                     sems[fam].at[slots[n]])

    ahead = FF_LOAD_SLOTS - 1
    for n in range(min(ahead, len(jobs))):
        copy(n).start()
    for n, (m, fam, r0, n_rows) in enumerate(jobs):
        if n + ahead < len(jobs):
            copy(n + ahead).start()
        copy(n).wait()
        vmem[m][r0:r0 + n_rows, :] = stages[fam][slots[n]].astype(BF16)


def _ffn_kernel(*refs, rows, layer, x_pair, final_norm):
    n_in = 11 + int(x_pair) + int(final_norm)
    n_out = 2 if final_norm else 1
    ins, outs, scratch = refs[:n_in], refs[n_in:n_in + n_out], refs[n_in + n_out:]
    if x_pair:
        x = _select_rows(rows, ins[0], ins[1])
    else:
        x = ins[0][...]
    g_ref, shs, shp, scs, scp, gts, gtp, wg_hbm, wu_hbm, wd_hbm = ins[1 + int(x_pair):11 + int(x_pair)]
    wg_ref, wu_ref, wd_ref, stage_a, stage_d, sem_a, sem_d = scratch

    @pl.when(pl.program_id(0) == 0)
    def _():
        _ffn_load_weights(layer, (wg_hbm, wu_hbm, wd_hbm), (wg_ref, wu_ref, wd_ref),
                          (stage_a, stage_d), (sem_a, sem_d))

    h = _ada_norm(x, g_ref[...], _mod_value(rows, shs, shp), _mod_value(rows, scs, scp)).astype(BF16)
    chunks = [slice(lo, hi) for lo, hi in zip(FF_SPLITS[:-1], FF_SPLITS[1:])]
    gate_up = [(_dot(h, wg_ref[:, sl]), _dot(h, wu_ref[:, sl])) for sl in chunks]
    act = [(_silu(gate) * up).astype(BF16) for gate, up in gate_up]
    acc = _dot(act[0], wd_ref[chunks[0], :])
    for a, sl in zip(act[1:], chunks[1:]):
        acc = acc + _dot(a, wd_ref[sl, :])
    out = x + (MACARON_W * _mod_value(rows, gts, gtp)) * acc
    if not final_norm:
        outs[0][...] = out
        return
    var = jnp.mean(out * out, axis=-1, keepdims=True)
    y = out * lax.rsqrt(var + NORM_EPS) * ins[-1][...]
    is_sample = pl.program_id(0) == rows.n_prompt_tiles

    @pl.when(jnp.logical_not(is_sample))
    def _():
        outs[0][...] = y

    @pl.when(is_sample)
    def _():
        outs[1][...] = y


def _ffn_call(rows, layer, mod_base, x, mod, norm_g, wg, wu, wd, final_g=None):
    x_pair = isinstance(x, tuple)
    if x_pair:
        in_specs = _pair_specs(rows, D_MODEL)
        operands = list(x)
    else:
        in_specs = [pl.BlockSpec((TOKEN_TILE, D_MODEL), lambda i: (i, 0))]
        operands = [x]
    in_specs.append(pl.BlockSpec((None, 1, D_MODEL), lambda i, l=layer: (l, 0, 0)))
    operands.append(norm_g)
    for j in range(3):
        in_specs += _mod_specs(rows, layer, mod_base + j)
        operands += [mod, mod]
    in_specs += [pl.BlockSpec(memory_space=pl.ANY)] * 3
    operands += [wg, wu, wd]
    scratch_shapes = [
        pltpu.VMEM((D_MODEL, D_FF), BF16), pltpu.VMEM((D_MODEL, D_FF), BF16), pltpu.VMEM((D_FF, D_MODEL), BF16),
        pltpu.VMEM((FF_LOAD_SLOTS, D_MODEL // FF_LOAD_CHUNKS, D_FF), F32),
        pltpu.VMEM((FF_LOAD_SLOTS, D_FF // FF_LOAD_CHUNKS, D_MODEL), F32),
        pltpu.SemaphoreType.DMA((FF_LOAD_SLOTS,)), pltpu.SemaphoreType.DMA((FF_LOAD_SLOTS,)),
    ]
    if final_g is None:
        out_shape = jax.ShapeDtypeStruct((rows.rows, D_MODEL), F32)
        out_specs = pl.BlockSpec((TOKEN_TILE, D_MODEL), lambda i: (i, 0))
    else:
        in_specs.append(pl.BlockSpec((1, D_MODEL), lambda i: (0, 0)))
        operands.append(final_g)
        out_shape = [jax.ShapeDtypeStruct((rows.np_rows, D_MODEL), F32),
                     jax.ShapeDtypeStruct((rows.ns_rows, D_MODEL), F32)]
        out_specs = _pair_specs(rows, D_MODEL)
    return pl.pallas_call(
        functools.partial(_ffn_kernel, rows=rows, layer=layer, x_pair=x_pair, final_norm=final_g is not None),
        out_shape=out_shape,
        grid=(rows.n_tiles,),
        in_specs=in_specs,
        out_specs=out_specs,
        scratch_shapes=scratch_shapes,
        compiler_params=_cparams(("arbitrary",)),
        name="ada_swiglu",
    )(*operands)


IN_OFF_RANK = 2 * A_QK + A_V
IN_OFF_AR = IN_OFF_RANK + A_GATE_RANK
IN_OFF_BQ = IN_OFF_AR + A_V
IN_OFF_GL = IN_OFF_BQ + B_Q + 2 * B_KV + C_WIDTH
IN_WIDTH = IN_OFF_GL + 3 * D_MODEL


def _rotary(x, cos, s_up, s_dn):
    half = ROPE_DIM // 2
    return x * cos + pltpu.roll(x, half, axis=1) * s_up + pltpu.roll(x, LANES - half, axis=1) * s_dn


def _inproj_kernel(x_ref, g_ref, shs, shp, scs, scp, w_ref, wup_ref, bg_ref,
                   cos_ref, sup_ref, sdn_ref,
                   gq_ref, gk_ref, gv_ref, la_ref, sq_ref, sk_ref, sv_ref, su_ref, *, rows):
    x = x_ref[...]
    h = _ada_norm(x, g_ref[...], _mod_value(rows, shs, shp), _mod_value(rows, scs, scp)).astype(BF16)
    za = _dot_nt(h, w_ref[0:IN_OFF_RANK, :])
    gq_ref[...] = za[:, 0:A_QK] * (A_DK ** -0.5)
    gk_ref[...] = za[:, A_QK:2 * A_QK]
    gv_ref[...] = za[:, 2 * A_QK:]
    z = _dot_nt(h, w_ref[IN_OFF_BQ:IN_OFF_GL, :])
    o = 0
    cos, s_up, s_dn = cos_ref[...], sup_ref[...], sdn_ref[...]
    for c in range(B_Q // LANES):
        sq_ref[:, c * LANES:(c + 1) * LANES] = _rotary(z[:, o + c * LANES:o + (c + 1) * LANES], cos, s_up, s_dn)
    o += B_Q
    sk_ref[...] = _rotary(z[:, o:o + B_KV], cos, s_up, s_dn)
    o += B_KV
    sv_ref[...] = z[:, o:o + B_KV]
    o += B_KV
    su_ref[...] = z[:, o:o + C_WIDTH]
    low = _dot_nt(h, w_ref[IN_OFF_RANK:IN_OFF_RANK + LANES, :]).astype(BF16)
    pre = _dot(low, wup_ref[...]) + bg_ref[...]
    la_ref[...] = _log_sigmoid(pre) * (LOG2_E / A_GATE_NORM)


def _inproj_call(rows, layer, x, mod, norm_g, w_in_t, w_up, b_gate, rot):
    row_spec = lambda w: pl.BlockSpec((TOKEN_TILE, w), lambda i: (i, 0))
    rot_spec = pl.BlockSpec(
        (TOKEN_TILE, LANES),
        lambda i, n=rows.n_prompt_tiles, per=rows.tiles_per_seq: (jnp.where(i < n, i % per, per), 0))
    in_specs = [row_spec(D_MODEL), pl.BlockSpec((None, 1, D_MODEL), lambda i, l=layer: (l, 0, 0))]
    operands = [x, norm_g]
    for j in (3, 4):
        in_specs += _mod_specs(rows, layer, j)
        operands += [mod, mod]
    in_specs += [
        _resident((None, IN_WIDTH, D_MODEL), lambda i, l=layer: (l, 0, 0)),
        _resident((None, LANES, A_QK), lambda i, l=layer: (l, 0, 0)),
        pl.BlockSpec((None, 1, A_QK), lambda i, l=layer: (l, 0, 0)),
        rot_spec, rot_spec, rot_spec,
    ]
    operands += [w_in_t, w_up, b_gate, *rot]
    widths = (A_QK, A_QK, A_V, A_QK, B_Q, B_KV, B_KV, C_WIDTH)
    return pl.pallas_call(
        functools.partial(_inproj_kernel, rows=rows),
        out_shape=[jax.ShapeDtypeStruct((rows.rows, w), F32) for w in widths],
        grid=(rows.n_tiles,),
        in_specs=in_specs,
        out_specs=[row_spec(w) for w in widths],
        compiler_params=_cparams(("arbitrary",)),
        name="mixer_in_proj",
    )(*operands)


def _cumsum_rows(x):
    n = x.shape[0]
    row = lax.broadcasted_iota(jnp.int32, x.shape, 0)
    s = 1
    while s < n:
        x = x + jnp.where(row >= s, pltpu.roll(x, s, axis=0), 0.0)
        s *= 2
    return x


def _head_blockdiag(x, lanes_per_head):
    lane_head = lax.broadcasted_iota(jnp.int32, x.shape, 1) // lanes_per_head
    return jnp.concatenate([jnp.where(lane_head == h, x, 0.0) for h in range(A_HEADS)], axis=0)


def _gla_kernel(*refs, chunk, block, n_chunks, n_seqs, single_step, has_state):
    if has_state:
        q_ref, k_ref, v_ref, la_ref, gn_ref, s0_ref, o_ref, so_ref, st_ref = refs
    else:
        q_ref, k_ref, v_ref, la_ref, gn_ref, o_ref, so_ref, st_ref = refs
    n_blocks = chunk // block
    st_shape = (A_V, A_QK)
    st_row_head = lax.broadcasted_iota(jnp.int32, st_shape, 0) // A_DV
    st_lane_head = lax.broadcasted_iota(jnp.int32, st_shape, 1) // A_DK
    st_mask = st_row_head == st_lane_head
    srow = lax.broadcasted_iota(jnp.int32, (block, A_HEADS * block), 0)
    scol = lax.broadcasted_iota(jnp.int32, (block, A_HEADS * block), 1) % block
    causal = scol <= srow
    gn = gn_ref[...]

    def init_state(sq):
        st_ref[sq] = jnp.zeros(st_shape, F32)
        if has_state:
            for h in range(A_HEADS):
                st_ref[sq, h * A_DV:(h + 1) * A_DV, h * A_DK:(h + 1) * A_DK] = s0_ref[sq, h].T

    def emit_state(sq):
        for h in range(A_HEADS):
            so_ref[sq, h] = st_ref[sq, h * A_DV:(h + 1) * A_DV, h * A_DK:(h + 1) * A_DK].T

    if single_step:
        for sq in range(n_seqs):
            init_state(sq)
    else:
        pl.when(pl.program_id(1) == 0)(lambda: init_state(0))

    units = [(sq, c) for sq in range(n_seqs) for c in range(n_chunks)]
    rows_of = {u: slice((u[0] * n_chunks + u[1]) * chunk, (u[0] * n_chunks + u[1] + 1) * chunk) for u in units}

    q, k, v, b = {}, {}, {}, {}
    for u in units:
        q[u], k[u], v[u] = q_ref[rows_of[u], :], k_ref[rows_of[u], :], v_ref[rows_of[u], :]
        b[u] = _cumsum_rows(la_ref[rows_of[u], :])

    s_diag, s_off, upd, q_in = {}, {}, {}, {}
    for u in units:
        for i in range(n_blocks):
            lo, hi = i * block, (i + 1) * block
            qi, ki, bi = q[u][lo:hi], k[u][lo:hi], b[u][lo:hi]
            mid = b[u][lo + block // 2 - 1:lo + block // 2, :]
            qd = (qi * jnp.exp2(bi - mid)).astype(BF16)
            kd = _head_blockdiag(ki * jnp.exp2(mid - bi), A_DK).astype(BF16)
            s_diag[u, i] = _dot_nt(qd, kd)
            if i > 0:
                ref = b[u][lo - 1:lo, :]
                qo = (qi * jnp.exp2(bi - ref)).astype(BF16)
                ko = _head_blockdiag(k[u][:lo] * jnp.exp2(ref - b[u][:lo]), A_DK).astype(BF16)
                s_off[u, i] = _dot_nt(qo, ko)
        b_last = b[u][chunk - 1:chunk, :]
        upd[u] = _dot(v[u].T.astype(BF16), (k[u] * jnp.exp2(b_last - b[u])).astype(BF16))
        q_in[u] = (q[u] * jnp.exp2(b[u])).astype(BF16)

    intra = {}
    for u in units:
        outs = []
        for i in range(n_blocks):
            lo, hi = i * block, (i + 1) * block
            att = jnp.where(causal, s_diag[u, i], 0.0).astype(BF16)
            oi = _dot(att, _head_blockdiag(v[u][lo:hi], A_DV).astype(BF16))
            if i > 0:
                oi = oi + _dot(s_off[u, i].astype(BF16), _head_blockdiag(v[u][:lo], A_DV).astype(BF16))
            outs.append(oi)
        intra[u] = jnp.concatenate(outs, axis=0) if n_blocks > 1 else outs[0]

    for u in units:
        sq = u[0]
        st = st_ref[sq]
        o = _dot_nt(q_in[u], st.astype(BF16)) + intra[u]
        st_ref[sq] = st * jnp.exp2(b[u][chunk - 1:chunk, :]) + jnp.where(st_mask, upd[u], 0.0)
        normed = []
        for h in range(A_HEADS):
            oh = o[:, h * A_DV:(h + 1) * A_DV]
            var = jnp.mean(oh * oh, axis=-1, keepdims=True)
            normed.append(oh * lax.rsqrt(var + NORM_EPS) * gn)
        o_ref[rows_of[u], :] = jnp.concatenate(normed, axis=1)

    if single_step:
        for sq in range(n_seqs):
            emit_state(sq)
    else:
        pl.when(pl.program_id(1) == pl.num_programs(1) - 1)(lambda: emit_state(0))


def _gla_call(layer, q, k, v, la, gnorm, s0, n_seq, seq_rows, step_rows, chunk, block, seqs_per_step=1,
              col_blocks=(0, 0, 0, 0)):
    steps = seq_rows // step_rows
    sps = seqs_per_step
    assert sps == 1 or steps == 1
    blk_rows = sps * step_rows
    row_spec = lambda w, cb=0: pl.BlockSpec((blk_rows, w), lambda s, j, steps=steps, cb=cb: (s * steps + j, cb))
    in_specs = [row_spec(A_QK, col_blocks[0]), row_spec(A_QK, col_blocks[1]), row_spec(A_V, col_blocks[2]),
                row_spec(A_QK, col_blocks[3]),
                pl.BlockSpec((None, 1, A_DV), lambda s, j, l=layer: (l, 0, 0))]
    operands = [q, k, v, la, gnorm]
    if s0 is not None:
        in_specs.append(pl.BlockSpec((None, sps, A_HEADS, A_DK, A_DV), lambda s, j, l=layer: (l, s, 0, 0, 0)))
        operands.append(s0)
    return pl.pallas_call(
        functools.partial(_gla_kernel, chunk=chunk, block=block, n_chunks=step_rows // chunk,
                          n_seqs=sps, single_step=steps == 1, has_state=s0 is not None),
        out_shape=[jax.ShapeDtypeStruct((n_seq * seq_rows, A_V), F32),
                   jax.ShapeDtypeStruct((n_seq, A_HEADS, A_DK, A_DV), F32)],
        grid=(n_seq // sps, steps),
        in_specs=in_specs,
        out_specs=[row_spec(A_V),
                   pl.BlockSpec((sps, A_HEADS, A_DK, A_DV), lambda s, j: (s, 0, 0, 0))],
        scratch_shapes=[pltpu.VMEM((sps, A_V, A_QK), F32)],
        compiler_params=_cparams(("arbitrary", "arbitrary")),
        name="gla_chunked",
    )(*operands)


SWA_STEP_BLOCKS = MIXER_STEP_ROWS // WINDOW


def _swa_kernel(sink_ref, q_ref, kc_ref, vc_ref, kp_ref, vp_ref, o_ref, *, layer):
    tk = 2 * WINDOW
    kj = lax.broadcasted_iota(jnp.int32, (tk, WINDOW), 0)
    qi = lax.broadcasted_iota(jnp.int32, (tk, WINDOW), 1) + WINDOW
    band = (kj <= qi) & (kj >= qi - WINDOW)
    zeros = jnp.zeros((B_HEAD_DIM, WINDOW), F32)
    chains = [(blk, h) for blk in range(SWA_STEP_BLOCKS) for h in range(B_KV_HEADS)]

    keys, vals_t, q_t, valid = [], [], [], []
    for blk in range(SWA_STEP_BLOCKS):
        cur = slice(blk * WINDOW, (blk + 1) * WINDOW)
        if blk == 0:
            k_prev, v_prev = kp_ref[...], vp_ref[...]
            valid.append(band & (kj >= jnp.where(pl.program_id(1) > 0, 0, WINDOW)))
        else:
            prev = slice((blk - 1) * WINDOW, blk * WINDOW)
            k_prev, v_prev = kc_ref[prev, :], vc_ref[prev, :]
            valid.append(band)
        keys.append(jnp.concatenate([k_prev, kc_ref[cur, :]], axis=0).astype(BF16))
        vals_t.append(jnp.concatenate([v_prev, vc_ref[cur, :]], axis=0).T.astype(BF16))
        q_t.append(q_ref[cur, :].T * (LOG2_E * B_HEAD_DIM ** -0.5))

    scores = {}
    for blk, h in chains:
        cols = []
        for g in range(B_GROUP):
            hd = h * B_GROUP + g
            qg = q_t[blk][hd * B_HEAD_DIM:(hd + 1) * B_HEAD_DIM]
            cols.append(jnp.concatenate([qg, zeros] if h == 0 else [zeros, qg], axis=0))
        scores[blk, h] = _dot(keys[blk], jnp.concatenate(cols, axis=1).astype(BF16))

    probs, inv = {}, {}
    for blk, h in chains:
        ps = []
        for g in range(B_GROUP):
            sink = sink_ref[layer, h * B_GROUP + g] * LOG2_E
            sg = jnp.where(valid[blk], scores[blk, h][:, g * WINDOW:(g + 1) * WINDOW], NEG_BIG)
            m = jnp.maximum(jnp.max(sg, axis=0, keepdims=True), sink)
            p = jnp.exp2(sg - m)
            inv[blk, h, g] = 1.0 / (jnp.sum(p, axis=0, keepdims=True) + jnp.exp2(sink - m))
            ps.append(p.astype(BF16))
        probs[blk, h] = jnp.concatenate(ps, axis=1)

    out_t = {}
    for blk, h in chains:
        out_t[blk, h] = _dot(vals_t[blk][h * B_HEAD_DIM:(h + 1) * B_HEAD_DIM], probs[blk, h])

    for blk in range(SWA_STEP_BLOCKS):
        rows = [out_t[blk, h][:, g * WINDOW:(g + 1) * WINDOW] * inv[blk, h, g]
                for h in range(B_KV_HEADS) for g in range(B_GROUP)]
        o_ref[blk * WINDOW:(blk + 1) * WINDOW, :] = jnp.concatenate(rows, axis=0).T


def _swa_call(layer, sinks, q, k, v, n_seq, blocks_per_seq):
    sb = SWA_STEP_BLOCKS
    assert blocks_per_seq % sb == 0
    steps = blocks_per_seq // sb
    cur = lambda w: pl.BlockSpec((sb * WINDOW, w), lambda s, n, steps=steps: (s * steps + n, 0))
    prev = pl.BlockSpec((WINDOW, B_KV),
                        lambda s, n, nb=blocks_per_seq, sb=sb: (s * nb + jnp.maximum(n * sb - 1, 0), 0))
    return pl.pallas_call(
        functools.partial(_swa_kernel, layer=layer),
        out_shape=jax.ShapeDtypeStruct((n_seq * blocks_per_seq * WINDOW, B_Q), F32),
        grid=(n_seq, steps),
        in_specs=[pl.BlockSpec(memory_space=pltpu.SMEM), cur(B_Q), cur(B_KV), cur(B_KV), prev, prev],
        out_specs=cur(B_Q),
        compiler_params=_cparams(("arbitrary", "arbitrary")),
        name="swa_banded",
    )(sinks, q, k, v, k, v)


SWA_SAMPLE_SEQS = 8
SWA_SAMPLE_ROWS = SUBLANES


def _swa_sample_kernel(sink_ref, q_ref, kn_ref, vn_ref, kt_ref, vt_ref, o_ref, nk_ref, nv_ref, *, layer, n_tok):
    pr = SWA_SAMPLE_ROWS
    keep = WINDOW - n_tok
    lane = lax.broadcasted_iota(jnp.int32, (B_HEAD_DIM, WINDOW), 1)
    rows = B_GROUP * pr
    tok = lax.broadcasted_iota(jnp.int32, (rows, WINDOW), 0) % pr
    grp = lax.broadcasted_iota(jnp.int32, (rows, 1), 0) // pr
    pos = lax.broadcasted_iota(jnp.int32, (rows, WINDOW), 1)
    valid_new = pos - keep <= tok
    valid_old = (pos >= tok) & (pos < n_tok)

    def shifted_in(old, new_rows):
        padded = jnp.concatenate([jnp.zeros((WINDOW - pr, B_KV), F32), new_rows], axis=0)
        cols = pltpu.roll(padded, n_tok, axis=0).T
        return [jnp.where(lane < keep, pltpu.roll(old[h], keep, axis=1),
                          cols[h * B_HEAD_DIM:(h + 1) * B_HEAD_DIM]) for h in range(B_KV_HEADS)]

    chains = [(i, h) for i in range(SWA_SAMPLE_SEQS) for h in range(B_KV_HEADS)]
    k_old, v_old, k_new, v_new = {}, {}, {}, {}
    for i in range(SWA_SAMPLE_SEQS):
        rs = slice(i * pr, (i + 1) * pr)
        ko = [kt_ref[i, h] for h in range(B_KV_HEADS)]
        vo = [vt_ref[i, h] for h in range(B_KV_HEADS)]
        kn, vn = shifted_in(ko, kn_ref[rs, :]), shifted_in(vo, vn_ref[rs, :])
        for h in range(B_KV_HEADS):
            nk_ref[i, h] = kn[h]
            nv_ref[i, h] = vn[h]
            k_old[i, h], v_old[i, h] = ko[h].astype(BF16), vo[h].astype(BF16)
            k_new[i, h], v_new[i, h] = kn[h].astype(BF16), vn[h].astype(BF16)

    s_new, s_old = {}, {}
    for c in chains:
        q = (q_ref[c[0], c[1]] * (LOG2_E * B_HEAD_DIM ** -0.5)).astype(BF16)
        s_new[c], s_old[c] = _dot(q, k_new[c]), _dot(q, k_old[c])

    p_new, p_old, inv = {}, {}, {}
    for c in chains:
        sn = jnp.where(valid_new, s_new[c], NEG_BIG)
        so = jnp.where(valid_old, s_old[c], NEG_BIG)
        sink = jnp.zeros((rows, 1), F32)
        for g in range(B_GROUP):
            sink = jnp.where(grp == g, sink_ref[layer, c[1] * B_GROUP + g] * LOG2_E, sink)
        m = jnp.maximum(jnp.maximum(jnp.max(sn, axis=-1, keepdims=True),
                                    jnp.max(so, axis=-1, keepdims=True)), sink)
        pn, po = jnp.exp2(sn - m), jnp.exp2(so - m)
        inv[c] = 1.0 / (jnp.sum(pn, axis=-1, keepdims=True) + jnp.sum(po, axis=-1, keepdims=True)
                        + jnp.exp2(sink - m))
        p_new[c], p_old[c] = pn.astype(BF16), po.astype(BF16)

    for c in chains:
        o = _dot_nt(p_new[c], v_new[c]) + _dot_nt(p_old[c], v_old[c])
        o_ref[c[0], c[1]] = o * inv[c]


def _swa_sample_call(layer, sinks, q, k_new, v_new, kt, vt, n_seq, n_tok, col_blocks=(0, 0)):
    sb, pr = SWA_SAMPLE_SEQS, SWA_SAMPLE_ROWS
    qspec = pl.BlockSpec((sb, B_KV_HEADS, B_GROUP * pr, B_HEAD_DIM), lambda s: (s, 0, 0, 0))
    new_rows = lambda cb: pl.BlockSpec((sb * pr, B_KV), lambda s, cb=cb: (s, cb))
    buf_in = pl.BlockSpec((None, sb, B_KV_HEADS, B_HEAD_DIM, WINDOW), lambda s, l=layer: (l, s, 0, 0, 0))
    buf_out = pl.BlockSpec((sb, B_KV_HEADS, B_HEAD_DIM, WINDOW), lambda s: (s, 0, 0, 0))
    buf_sds = jax.ShapeDtypeStruct((n_seq, B_KV_HEADS, B_HEAD_DIM, WINDOW), F32)
    return pl.pallas_call(
        functools.partial(_swa_sample_kernel, layer=layer, n_tok=n_tok),
        out_shape=[jax.ShapeDtypeStruct((n_seq, B_KV_HEADS, B_GROUP * pr, B_HEAD_DIM), F32), buf_sds, buf_sds],
        grid=(n_seq // sb,),
        in_specs=[pl.BlockSpec(memory_space=pltpu.SMEM), qspec, new_rows(col_blocks[0]), new_rows(col_blocks[1]),
                  buf_in, buf_in],
        out_specs=[qspec, buf_out, buf_out],
        compiler_params=_cparams(("arbitrary",)),
        name="swa_decode",
    )(sinks, q, k_new, v_new, kt, vt)


def _s5_prep_kernel(lr_ref, li_ref, ldt_ref, br_ref, bi_ref, ctr_ref, cti_ref,
                    bw_ref, cr_ref, ci_ref, ar_ref, ai_ref, er_ref, ei_ref, fr_ref, fi_ref):
    lr, li = lr_ref[...], li_ref[...]
    dt = jnp.exp(ldt_ref[...])
    rho, th = lr * dt, li * dt
    mag = jnp.exp(rho)
    a_re, a_im = mag * jnp.cos(th), mag * jnp.sin(th)
    den = lr * lr + li * li
    nr, ni = a_re - 1.0, a_im
    f_re = (nr * lr + ni * li) / den
    f_im = (ni * lr - nr * li) / den
    br, bi = br_ref[...], bi_ref[...]
    bb_re = f_re * br - f_im * bi
    bb_im = f_re * bi + f_im * br
    lane_group = lax.broadcasted_iota(jnp.int32, bb_re.shape, 1) // C_STATE
    place = lambda bb: jnp.concatenate(
        [jnp.where(lane_group == g, bb, 0.0) for g in range(C_GROUPS)], axis=0).astype(BF16)
    placed_re, placed_im = place(bb_re), place(bb_im)
    for n in range(S5_N // MXU_TILE):
        src = slice(n * MXU_TILE, (n + 1) * MXU_TILE)
        bw_ref[:, 2 * n * MXU_TILE:(2 * n + 1) * MXU_TILE] = placed_re[:, src]
        bw_ref[:, (2 * n + 1) * MXU_TILE:(2 * n + 2) * MXU_TILE] = placed_im[:, src]
    spread_c = lax.broadcasted_iota(jnp.int32, (LANES, C_WIDTH), 0)
    spread_j = lax.broadcasted_iota(jnp.int32, (LANES, C_WIDTH), 1) % C_GROUP
    spread = jnp.where(spread_c == spread_j, 1.0, 0.0).astype(BF16)
    row_group = lax.broadcasted_iota(jnp.int32, cr_ref.shape, 0) // C_STATE
    col_group = lax.broadcasted_iota(jnp.int32, cr_ref.shape, 1) // C_GROUP
    own = row_group == col_group
    cr_ref[...] = jnp.where(own, _dot(ctr_ref[...].astype(BF16), spread), 0.0).astype(BF16)
    ci_ref[...] = jnp.where(own, _dot(cti_ref[...].astype(BF16), spread), 0.0).astype(BF16)
    ar_ref[...] = a_re
    ai_ref[...] = a_im
    s = (lax.broadcasted_iota(jnp.int32, er_ref.shape, 0) + 1).astype(F32)
    cs, sn = jnp.cos(s * th), jnp.sin(s * th)
    e_mag, f_mag = jnp.exp(-(s * rho)), jnp.exp(s * rho)
    er_ref[...] = e_mag * cs
    ei_ref[...] = -(e_mag * sn)
    fr_ref[...] = f_mag * cs
    fi_ref[...] = f_mag * sn


def _s5_prep_call(lr, li, ldt, b_re_t, b_im_t, c_re_t, c_im_t):
    depth = lr.shape[0]
    lsel = lambda l: (l, 0, 0)
    vec = pl.BlockSpec((None, 1, S5_N), lsel)
    mat = pl.BlockSpec((None, C_GROUP, S5_N), lsel)
    c_in = pl.BlockSpec((None, S5_N, LANES), lsel)
    cmat = pl.BlockSpec((None, S5_N, C_WIDTH), lsel)
    tab = pl.BlockSpec((None, S5_CHUNK, S5_N), lsel)
    sds = lambda r: jax.ShapeDtypeStruct((depth, r, S5_N), F32)
    c_sds = jax.ShapeDtypeStruct((depth, S5_N, C_WIDTH), BF16)
    return pl.pallas_call(
        _s5_prep_kernel,
        out_shape=[jax.ShapeDtypeStruct((depth, C_WIDTH, 2 * S5_N), BF16), c_sds, c_sds,
                   sds(1), sds(1)] + [sds(S5_CHUNK)] * 4,
        grid=(depth,),
        in_specs=[vec, vec, vec, mat, mat, c_in, c_in],
        out_specs=[pl.BlockSpec((None, C_WIDTH, 2 * S5_N), lsel), cmat, cmat, vec, vec, tab, tab, tab, tab],
        compiler_params=_cparams(("arbitrary",)),
        name="s5_discretise",
    )(lr, li, ldt, b_re_t, b_im_t, c_re_t, c_im_t)


def _cmul(ar, ai, br, bi):
    return ar * br - ai * bi, ar * bi + ai * br


def _s5_prompt_kernel(u_ref, bw_ref, cr_ref, ci_ref, d_ref, er_ref, ei_ref, fr_ref, fi_ref,
                      y_ref, hs_ref, car_ref, cai_ref, xr_ref, xi_ref, hr_ref, hi_ref, *, n_chunks):
    step = pl.program_id(1)
    t = S5_CHUNK

    @pl.when(step == 0)
    def _():
        car_ref[...] = jnp.zeros(car_ref.shape, F32)
        cai_ref[...] = jnp.zeros(cai_ref.shape, F32)

    groups = [slice(n * MXU_TILE, (n + 1) * MXU_TILE) for n in range(S5_N // MXU_TILE)]
    halves = [slice(0, LANES), slice(LANES, MXU_TILE)]
    group_cols = [[slice(g.start + hv.start, g.start + hv.stop) for hv in halves] for g in groups]

    u = u_ref[...]
    ub = u.astype(BF16)
    tiles_per_kt = len(groups) // (C_WIDTH // S5_KT)
    for n, cols in enumerate(group_cols):
        rows_k = slice(n // tiles_per_kt * S5_KT, (n // tiles_per_kt + 1) * S5_KT)
        bu = _dot(ub[:, rows_k], bw_ref[rows_k, 2 * n * MXU_TILE:(2 * n + 2) * MXU_TILE])
        for c in range(n_chunks):
            rs = slice(c * t, (c + 1) * t)
            for cs, hv in zip(cols, halves):
                xr, xi = _cmul(er_ref[:, cs], ei_ref[:, cs], bu[rs, hv],
                               bu[rs, MXU_TILE + hv.start:MXU_TILE + hv.stop])
                xr_ref[rs, cs] = xr.astype(BF16)
                xi_ref[rs, cs] = xi.astype(BF16)

    row = lax.broadcasted_iota(jnp.int32, (t, t), 0)
    col = lax.broadcasted_iota(jnp.int32, (t, t), 1)
    tril = jnp.where(col <= row, 1.0, 0.0).astype(BF16)

    carry = [(car_ref[:, g], cai_ref[:, g]) for g in groups]
    for c in range(n_chunks):
        rs = slice(c * t, (c + 1) * t)
        sums = [(_dot(tril, xr_ref[rs, g]), _dot(tril, xi_ref[rs, g])) for g in groups]
        for n, (g, cols) in enumerate(zip(groups, group_cols)):
            sum_re, sum_im = sums[n][0] + carry[n][0], sums[n][1] + carry[n][1]
            hs = [_cmul(fr_ref[:, cs], fi_ref[:, cs], sum_re[:, hv], sum_im[:, hv]) for cs, hv in zip(cols, halves)]
            hr = jnp.concatenate([h[0] for h in hs], axis=1)
            hi = jnp.concatenate([h[1] for h in hs], axis=1)
            hr_ref[rs, g] = hr.astype(BF16)
            hi_ref[rs, g] = hi.astype(BF16)
            carry[n] = (hr[t - 1:t, :], hi[t - 1:t, :])
    for g, (car, cai) in zip(groups, carry):
        car_ref[:, g] = car
        cai_ref[:, g] = cai

    n_out = C_WIDTH // S5_KT
    k_rows = S5_N // n_out
    for j in range(n_out):
        ks = slice(j * k_rows, (j + 1) * k_rows)
        cs = slice(j * S5_KT, (j + 1) * S5_KT)
        y_ref[:, cs] = (_dot(hr_ref[:, ks], cr_ref[ks, cs]) - _dot(hi_ref[:, ks], ci_ref[ks, cs])
                        + d_ref[:, cs] * u[:, cs])

    @pl.when(step == pl.num_programs(1) - 1)
    def _():
        hs_ref[0:1, :] = car_ref[...]
        hs_ref[1:2, :] = cai_ref[...]


def _s5_prompt_call(layer, u, bw, c_re, c_im, d, tabs, n_seq, seq_rows, step_rows):
    steps = seq_rows // step_rows
    row_spec = pl.BlockSpec((step_rows, C_WIDTH), lambda s, j, steps=steps: (s * steps + j, 0))
    lsel = lambda s, j, l=layer: (l, 0, 0)
    tab = pl.BlockSpec((None, S5_CHUNK, S5_N), lsel)
    return pl.pallas_call(
        functools.partial(_s5_prompt_kernel, n_chunks=step_rows // S5_CHUNK),
        out_shape=[jax.ShapeDtypeStruct((n_seq * seq_rows, C_WIDTH), F32),
                   jax.ShapeDtypeStruct((n_seq, 2, S5_N), F32)],
        grid=(n_seq, steps),
        in_specs=[row_spec,
                  _resident((None, C_WIDTH, 2 * S5_N), lsel),
                  _resident((None, S5_N, C_WIDTH), lsel),
                  _resident((None, S5_N, C_WIDTH), lsel),
                  pl.BlockSpec((None, 1, C_WIDTH), lsel),
                  tab, tab, tab, tab],
        out_specs=[row_spec, pl.BlockSpec((None, 2, S5_N), lambda s, j: (s, 0, 0))],
        scratch_shapes=[pltpu.VMEM((1, S5_N), F32), pltpu.VMEM((1, S5_N), F32)]
                       + [pltpu.VMEM((step_rows, S5_N), BF16)] * 4,
        compiler_params=_cparams(("arbitrary", "arbitrary")),
        name="s5_chunked_scan",
    )(u, bw, c_re, c_im, d, *tabs)


def _s5_sample_kernel(u_ref, h0r_ref, h0i_ref, ar_ref, ai_ref, bw_ref, cr_ref, ci_ref, d_ref,
                      y_ref, hr_ref, hi_ref, *, n_tok, n_seq):
    hr, hi = h0r_ref[...], h0i_ref[...]
    a_re, a_im = ar_ref[...], ai_ref[...]
    for t in range(n_tok):
        u = u_ref[t * n_seq:(t + 1) * n_seq, :]
        bu = _dot(u.astype(BF16), bw_ref[...])
        tiles = range(S5_N // MXU_TILE)
        bu_re = jnp.concatenate([bu[:, 2 * n * MXU_TILE:(2 * n + 1) * MXU_TILE] for n in tiles], axis=1)
        bu_im = jnp.concatenate([bu[:, (2 * n + 1) * MXU_TILE:(2 * n + 2) * MXU_TILE] for n in tiles], axis=1)
        hr, hi = _cmul(a_re, a_im, hr, hi)
        hr, hi = hr + bu_re, hi + bu_im
        y_ref[t * n_seq:(t + 1) * n_seq, :] = (_dot(hr.astype(BF16), cr_ref[...]) - _dot(hi.astype(BF16), ci_ref[...])
                                               + d_ref[...] * u)
    hr_ref[...] = hr
    hi_ref[...] = hi


def _s5_sample_call(layer, u, h0_re, h0_im, a_re, a_im, bw, c_re, c_im, d, n_tok, n_seq):
    lsel = lambda i, l=layer: (l, 0, 0)
    full = lambda shape: pl.BlockSpec(shape, lambda i: (0,) * len(shape))
    return pl.pallas_call(
        functools.partial(_s5_sample_kernel, n_tok=n_tok, n_seq=n_seq),
        out_shape=[jax.ShapeDtypeStruct((n_tok * n_seq, C_WIDTH), F32),
                   jax.ShapeDtypeStruct((n_seq, S5_N), F32),
                   jax.ShapeDtypeStruct((n_seq, S5_N), F32)],
        grid=(1,),
        in_specs=[full((n_tok * n_seq, C_WIDTH)),
                  pl.BlockSpec((None, n_seq, S5_N), lsel),
                  pl.BlockSpec((None, n_seq, S5_N), lsel),
                  pl.BlockSpec((None, 1, S5_N), lsel),
                  pl.BlockSpec((None, 1, S5_N), lsel),
                  pl.BlockSpec((None, C_WIDTH, 2 * S5_N), lsel),
                  pl.BlockSpec((None, S5_N, C_WIDTH), lsel),
                  pl.BlockSpec((None, S5_N, C_WIDTH), lsel),
                  pl.BlockSpec((None, 1, C_WIDTH), lsel)],
        out_specs=[full((n_tok * n_seq, C_WIDTH)), full((n_seq, S5_N)), full((n_seq, S5_N))],
        compiler_params=_cparams(("arbitrary",)),
        name="s5_step_scan",
    )(u, h0_re, h0_im, a_re, a_im, bw, c_re, c_im, d)


def _merge_kernel(x_ref, g_ref, shs, shp, scs, scp, gts, gtp,
                  oap, oas, obp, obs, ocp, ocs,
                  w_ref, wa_ref, wb_ref, wc_ref, wglu_ref, wo_ref, o_ref, *, rows):
    x = x_ref[...]
    br_b = _dot(_select_rows(rows, obp, obs).astype(BF16), wb_ref[...])
    glu = _dot(_gelu_tanh(_select_rows(rows, ocp, ocs)).astype(BF16), wglu_ref[...])
    h = _ada_norm(x, g_ref[...], _mod_value(rows, shs, shp), _mod_value(rows, scs, scp)).astype(BF16)
    z = _dot_nt(h, w_ref[IN_OFF_GL:IN_WIDTH, :])
    z_ar = _dot_nt(h, w_ref[IN_OFF_AR:IN_OFF_BQ, :])
    oc = glu[:, :C_WIDTH] * _sigmoid(glu[:, C_WIDTH:])
    br_c = _dot(oc.astype(BF16), wc_ref[...])
    oa = _select_rows(rows, oap, oas) * _silu(z_ar)
    br_a = _dot(oa.astype(BF16), wa_ref[...])
    mix = (_sigmoid(z[:, D_MODEL:2 * D_MODEL]) * br_b
           + _sigmoid(z[:, 2 * D_MODEL:3 * D_MODEL]) * br_c
           + _sigmoid(z[:, 0:D_MODEL]) * br_a)
    out = _dot(mix.astype(BF16), wo_ref[...])
    o_ref[...] = x + _mod_value(rows, gts, gtp) * out


def _merge_call(rows, layer, x, mod, norm_g, oa, ob, oc, w_in_t, wa, wb, wc, wglu, wo):
    in_specs = [pl.BlockSpec((TOKEN_TILE, D_MODEL), lambda i: (i, 0)),
                pl.BlockSpec((None, 1, D_MODEL), lambda i, l=layer: (l, 0, 0))]
    operands = [x, norm_g]
    for j in (3, 4, 5):
        in_specs += _mod_specs(rows, layer, j)
        operands += [mod, mod]
    for pair, w in ((oa, A_V), (ob, B_Q), (oc, C_WIDTH)):
        in_specs += _pair_specs(rows, w)
        operands += list(pair)
    lsel = lambda i, l=layer: (l, 0, 0)
    in_specs += [_resident((None, IN_WIDTH, D_MODEL), lsel),
                 _resident((None, A_V, D_MODEL), lsel),
                 _resident((None, B_Q, D_MODEL), lsel),
                 _resident((None, C_WIDTH, D_MODEL), lsel),
                 _resident((None, C_WIDTH, 2 * C_WIDTH), lsel),
                 _resident((None, D_MODEL, D_MODEL), lsel)]
    operands += [w_in_t, wa, wb, wc, wglu, wo]
    return pl.pallas_call(
        functools.partial(_merge_kernel, rows=rows),
        out_shape=jax.ShapeDtypeStruct((rows.rows, D_MODEL), F32),
        grid=(rows.n_tiles,),
        in_specs=in_specs,
        out_specs=pl.BlockSpec((TOKEN_TILE, D_MODEL), lambda i: (i, 0)),
        compiler_params=_cparams(("arbitrary",)),
        name="branch_merge",
    )(*operands)


def _rotary_tables(rows):
    half = ROPE_DIM // 2
    inv_freq = ROPE_THETA ** (-jnp.arange(half, dtype=F32) / half)
    pos_p = jnp.arange(rows.l, dtype=jnp.int32)
    pos_s = PAST_LEN + jnp.repeat(jnp.arange(rows.t, dtype=jnp.int32), rows.bs)
    pos = jnp.concatenate([pos_p, pos_s])
    ang = pos.astype(F32)[:, None] * inv_freq[None, :]
    cos, sin = jnp.cos(ang), jnp.sin(ang)
    n = pos.shape[0]
    ones = jnp.ones((n, B_HEAD_DIM - ROPE_DIM), F32)
    zeros = jnp.zeros((n, B_HEAD_DIM - ROPE_DIM), F32)
    z8 = jnp.zeros((n, half), F32)
    per_head = lambda parts: jnp.tile(jnp.concatenate(parts, axis=1), (1, LANES // B_HEAD_DIM))
    return (per_head([cos, cos, ones]), per_head([z8, sin, zeros]), per_head([-sin, z8, zeros]))


def _sample_to_seq(a, rows, pad_to):
    w = a.shape[-1]
    a = a.reshape(rows.t, rows.bs, w).transpose(1, 0, 2)
    a = jnp.pad(a, ((0, 0), (0, pad_to - rows.t), (0, 0)))
    return a.reshape(rows.bs * pad_to, w)


def _seq_to_sample(a, rows, pad_to):
    w = a.shape[-1]
    return a.reshape(rows.bs, pad_to, w)[:, :rows.t].transpose(1, 0, 2).reshape(rows.ns_rows, w)


def kernel(x_prompt, x_sample, state_gla, cache_swa_k, cache_swa_v, state_s5_re, state_s5_im, c_prompt, c_sample, w_ada, b_ada, norm_ffn1, ffn1_w_gate, ffn1_w_up, ffn1_w_down, norm_mix, w_in, gla_w_gate_up, gla_b_gate, gla_norm, swa_sinks, s5_a_re, s5_a_im, s5_b_re, s5_b_im, s5_c_re, s5_c_im, s5_d, s5_log_dt, s5_w_glu, w_branch_a, w_branch_b, w_branch_c, w_out, norm_ffn2, ffn2_w_gate, ffn2_w_up, ffn2_w_down, norm_final):
    depth = w_ada.shape[0]
    b, l, _ = x_prompt.shape
    bs, t, _ = x_sample.shape
    rows = _Rows(b, l, bs, t)
    np_rows = rows.np_rows

    x = (x_prompt.reshape(np_rows, D_MODEL), x_sample.transpose(1, 0, 2).reshape(rows.ns_rows, D_MODEL))
    c_all = jnp.concatenate([c_sample, c_prompt, jnp.zeros((SUBLANES - b, D_MODEL), F32)], axis=0)
    mod = _mod_call(c_all, w_ada, b_ada)

    assert w_in.shape[-1] == IN_WIDTH
    w_in_t = w_in.transpose(0, 2, 1).astype(BF16)
    w_up = jnp.pad(gla_w_gate_up, ((0, 0), (0, LANES - A_GATE_RANK), (0, 0))).astype(BF16)
    cast = lambda w: w.astype(BF16)
    f1g, f1u, f1d = ffn1_w_gate, ffn1_w_up, ffn1_w_down
    f2g, f2u, f2d = ffn2_w_gate, ffn2_w_up, ffn2_w_down
    wa, wb, wc, wglu, wo = cast(w_branch_a), cast(w_branch_b), cast(w_branch_c), cast(s5_w_glu), cast(w_out)
    vec3 = lambda a: a.reshape(depth, 1, a.shape[-1])
    n1, nm, n2 = vec3(norm_ffn1), vec3(norm_mix), vec3(norm_ffn2)
    b_gate, g_norm, s5d = vec3(gla_b_gate), vec3(gla_norm), vec3(s5_d)
    rot = _rotary_tables(rows)

    flat = lambda a: a.reshape(depth, 1, S5_N)
    ldt = flat(jnp.broadcast_to(s5_log_dt[:, :, None], (depth, C_GROUPS, C_STATE)))
    to_cols = lambda bm: bm.transpose(0, 3, 1, 2).reshape(depth, C_GROUP, S5_N)
    to_rows = lambda cm: jnp.pad(cm.transpose(0, 1, 3, 2).reshape(depth, S5_N, C_GROUP),
                                 ((0, 0), (0, 0), (0, LANES - C_GROUP)))
    s5_bw, s5_cr, s5_ci, a_re, a_im, e_re, e_im, f_re, f_im = _s5_prep_call(
        flat(s5_a_re), flat(s5_a_im), ldt, to_cols(s5_b_re), to_cols(s5_b_im), to_rows(s5_c_re), to_rows(s5_c_im))
    tabs = (e_re, e_im, f_re, f_im)
    h0_re = state_s5_re.reshape(depth, bs, S5_N)
    h0_im = state_s5_im.reshape(depth, bs, S5_N)
    cache_kt = cache_swa_k.transpose(0, 1, 3, 4, 2)
    cache_vt = cache_swa_v.transpose(0, 1, 3, 4, 2)

    outs = {k: [] for k in ("gla_p", "gla_s", "kp", "ks", "vp", "vs", "rp", "rs", "ip", "is")}
    pad8 = SWA_SAMPLE_ROWS
    for layer in range(depth):
        x = _ffn_call(rows, layer, 0, x, mod, n1, f1g, f1u, f1d)
        gq, gk, gv, la, sq, sk, sv, su = _inproj_call(rows, layer, x, mod, nm, w_in_t, w_up, b_gate, rot)

        oa_p, st_p = _gla_call(layer, gq, gk, gv, la, g_norm, None, b, l, MIXER_STEP_ROWS, GLA_CHUNK, GLA_BLOCK)
        parts = dict(gq=gq, gk=gk, la=la, sk=sk, sv=sv, gv=gv)
        seq_in = _sample_to_seq(jnp.concatenate([a[np_rows:] for a in parts.values()], axis=1), rows, pad8)
        starts = dict(zip(parts, np.cumsum([0] + [a.shape[1] for a in parts.values()])))
        col_block = lambda name: int(starts[name]) // parts[name].shape[1]
        assert all(int(starts[n]) % parts[n].shape[1] == 0 for n in parts)
        oa_s, st_s = _gla_call(layer, seq_in, seq_in, seq_in, seq_in, g_norm, state_gla,
                               bs, pad8, pad8, pad8, pad8, seqs_per_step=GLA_SAMPLE_SEQS,
                               col_blocks=tuple(col_block(n) for n in ("gq", "gk", "gv", "la")))
        oa_s = _seq_to_sample(oa_s, rows, pad8)

        ob_p = _swa_call(layer, swa_sinks, sq, sk, sv, b, l // WINDOW)
        q_s = sq[np_rows:].reshape(t, bs, B_KV_HEADS, B_GROUP, B_HEAD_DIM).transpose(1, 2, 3, 0, 4)
        q_s = jnp.pad(q_s, ((0, 0), (0, 0), (0, 0), (0, pad8 - t), (0, 0)))
        q_s = q_s.reshape(bs, B_KV_HEADS, B_GROUP * pad8, B_HEAD_DIM)
        ob_s, kt_new, vt_new = _swa_sample_call(layer, swa_sinks, q_s, seq_in, seq_in, cache_kt, cache_vt, bs, t,
                                                col_blocks=(col_block("sk"), col_block("sv")))
        ob_s = ob_s.reshape(bs, B_KV_HEADS, B_GROUP, pad8, B_HEAD_DIM)[:, :, :, :t]
        ob_s = ob_s.transpose(3, 0, 1, 2, 4).reshape(rows.ns_rows, B_Q)

        oc_p, hs_p = _s5_prompt_call(layer, su, s5_bw, s5_cr, s5_ci, s5d, tabs, b, l, MIXER_STEP_ROWS)
        oc_s, hr_s, hi_s = _s5_sample_call(layer, su[np_rows:], h0_re, h0_im, a_re, a_im,
                                           s5_bw, s5_cr, s5_ci, s5d, t, bs)

        x = _merge_call(rows, layer, x, mod, nm, (oa_p, oa_s), (ob_p, ob_s), (oc_p, oc_s),
                        w_in_t, wa, wb, wc, wglu, wo)
        x = _ffn_call(rows, layer, 6, x, mod, n2, f2g, f2u, f2d,
                      final_g=norm_final.reshape(1, D_MODEL) if layer == depth - 1 else None)

        outs["gla_p"].append(st_p)
        outs["gla_s"].append(st_s)
        kv_shape = (b, WINDOW, B_KV_HEADS, B_HEAD_DIM)
        window_rows = lambda a: jnp.stack([a[(i + 1) * l - WINDOW:(i + 1) * l] for i in range(b)])
        outs["kp"].append(window_rows(sk).reshape(kv_shape))
        outs["vp"].append(window_rows(sv).reshape(kv_shape))
        outs["ks"].append(kt_new.transpose(0, 3, 1, 2))
        outs["vs"].append(vt_new.transpose(0, 3, 1, 2))
        outs["rp"].append(hs_p[:, 0].reshape(b, C_GROUPS, C_STATE))
        outs["ip"].append(hs_p[:, 1].reshape(b, C_GROUPS, C_STATE))
        outs["rs"].append(hr_s.reshape(bs, C_GROUPS, C_STATE))
        outs["is"].append(hi_s.reshape(bs, C_GROUPS, C_STATE))

    y_p = x[0].reshape(b, l, D_MODEL)
    y_s = x[1].reshape(t, bs, D_MODEL).transpose(1, 0, 2)
    st = lambda k: jnp.stack(outs[k])
    return (y_p, y_s, st("gla_p"), st("gla_s"), st("kp"), st("ks"), st("vp"), st("vs"),
            st("rp"), st("rs"), st("ip"), st("is"))
```

```python
import functools
import math

import jax
import jax.numpy as jnp
import numpy as np
from jax import lax
from jax.experimental import pallas as pl
from jax.experimental.pallas import tpu as pltpu

D_MODEL = 1024
D_FF = 2816
PAST_LEN = 8192
A_HEADS, A_DK, A_DV = 4, 64, 128
A_GATE_RANK = 16
A_GATE_NORM = 16.0
B_HEADS, B_KV_HEADS, B_HEAD_DIM = 8, 2, 64
B_GROUP = B_HEADS // B_KV_HEADS
WINDOW = 128
ROPE_DIM = B_HEAD_DIM // 4
ROPE_THETA = 500000.0
C_WIDTH, C_GROUP, C_STATE = 512, 16, 64
C_GROUPS = C_WIDTH // C_GROUP
MACARON_W = 0.5
NORM_EPS = 1e-6
N_MOD = 9

A_QK = A_HEADS * A_DK
A_V = A_HEADS * A_DV
B_Q = B_HEADS * B_HEAD_DIM
B_KV = B_KV_HEADS * B_HEAD_DIM
S5_N = C_GROUPS * C_STATE

LANES = 128
SUBLANES = 8
TOKEN_TILE = 512
MIXER_STEP_ROWS = 1024
GLA_CHUNK = 64
GLA_BLOCK = 32
GLA_SAMPLE_SEQS = 16
S5_CHUNK = 64
S5_KT = 256
VMEM_LIMIT = 56 * 1024 * 1024

F32 = jnp.float32
BF16 = jnp.bfloat16
NEG_BIG = -1e30
LOG2_E = math.log2(math.e)


def _cparams(sem):
    return pltpu.CompilerParams(dimension_semantics=sem, vmem_limit_bytes=VMEM_LIMIT)


def _dot(a, b):
    return jnp.dot(a, b, preferred_element_type=F32)


def _dot_nt(a, b):
    return lax.dot_general(a, b, (((1,), (1,)), ((), ())), preferred_element_type=F32)


def _sigmoid(x):
    return 1.0 / (1.0 + jnp.exp(-x))


def _silu(x):
    return x * _sigmoid(x)


def _gelu_tanh(x):
    return 0.5 * x * (1.0 + jnp.tanh(math.sqrt(2.0 / math.pi) * (x + 0.044715 * (x * x * x))))


def _log_sigmoid(x):
    return jnp.minimum(x, 0.0) - jnp.log1p(jnp.exp(-jnp.abs(x)))


def _resident(shape, index_map):
    return pl.BlockSpec(shape, index_map, pipeline_mode=pl.Buffered(1))


def _mod_kernel(c_ref, w_ref, b_ref, o_ref):
    c = c_ref[...]
    s = _silu(c).astype(BF16)
    o_ref[...] = _dot(s, w_ref[...].astype(BF16)) + b_ref[...]


MOD_BLOCK = 3 * D_MODEL


def _mod_call(c_all, w_ada, b_ada):
    depth = w_ada.shape[0]
    rows = c_all.shape[0]
    return pl.pallas_call(
        _mod_kernel,
        out_shape=jax.ShapeDtypeStruct((depth, rows, N_MOD * D_MODEL), F32),
        grid=(depth, N_MOD * D_MODEL // MOD_BLOCK),
        in_specs=[
            pl.BlockSpec((rows, D_MODEL), lambda l, j: (0, 0)),
            pl.BlockSpec((None, D_MODEL, MOD_BLOCK), lambda l, j: (l, 0, j)),
            pl.BlockSpec((None, 1, MOD_BLOCK), lambda l, j: (l, 0, j)),
        ],
        out_specs=pl.BlockSpec((None, rows, MOD_BLOCK), lambda l, j: (l, 0, j)),
        compiler_params=_cparams(("arbitrary", "arbitrary")),
        name="ada_mod",
    )(c_all, w_ada, b_ada.reshape(depth, 1, N_MOD * D_MODEL))


class _Rows:
    def __init__(self, n_prompt_seq, seq_len, n_sample_seq, n_sample_tok):
        self.b = n_prompt_seq
        self.l = seq_len
        self.bs = n_sample_seq
        self.t = n_sample_tok
        self.np_rows = n_prompt_seq * seq_len
        self.ns_rows = n_sample_seq * n_sample_tok
        assert seq_len % TOKEN_TILE == 0 and self.ns_rows == TOKEN_TILE
        assert n_sample_seq % SUBLANES == 0 and seq_len % MIXER_STEP_ROWS == 0
        self.rows = self.np_rows + self.ns_rows
        self.n_prompt_tiles = self.np_rows // TOKEN_TILE
        self.tiles_per_seq = seq_len // TOKEN_TILE
        self.n_tiles = self.n_prompt_tiles + 1
        assert n_prompt_seq <= SUBLANES


def _mod_specs(rows, layer, j):
    return [
        pl.BlockSpec((None, rows.bs, D_MODEL), lambda i, l=layer, j=j: (l, 0, j)),
        pl.BlockSpec((None, SUBLANES, D_MODEL), lambda i, l=layer, j=j, r=rows.bs // SUBLANES: (l, r, j)),
    ]


def _mod_value(rows, ms_ref, mp_ref):
    i = pl.program_id(0)
    is_sample = i == rows.n_prompt_tiles
    seq = jnp.minimum(i // rows.tiles_per_seq, rows.b - 1)
    s = ms_ref[...]
    s = jnp.concatenate([s] * rows.t, axis=0)
    p = mp_ref[pl.ds(seq, 1), :]
    return jnp.where(is_sample, s, p)


def _ada_norm(x, g, shift, scale):
    var = jnp.mean(x * x, axis=-1, keepdims=True)
    return (x * lax.rsqrt(var + NORM_EPS)) * g * (1.0 + scale) + shift


def _select_rows(rows, p_ref, s_ref):
    is_sample = pl.program_id(0) == rows.n_prompt_tiles
    return jnp.where(is_sample, s_ref[...], p_ref[...])


def _pair_specs(rows, width):
    last = rows.n_prompt_tiles - 1
    return [
        pl.BlockSpec((TOKEN_TILE, width), lambda i, last=last: (jnp.minimum(i, last), 0)),
        pl.BlockSpec((TOKEN_TILE, width), lambda i: (0, 0)),
    ]


MXU_TILE = 256
FF_SPLITS = (0, 6 * MXU_TILE, D_FF)
assert D_FF % MXU_TILE == 0


FF_ROW_HALF = TOKEN_TILE // 2
FF_LOAD_CHUNKS = 8
FF_LOAD_SLOTS = 4


def _ffn_load_weights(layer, hbm, vmem, stages, sems):
    jobs = []
    for m in range(3):
        fam = 0 if m < 2 else 1
        n_rows = stages[fam].shape[1]
        assert hbm[m].shape[1] == n_rows * FF_LOAD_CHUNKS
        for c in range(FF_LOAD_CHUNKS):
            jobs.append((m, fam, c * n_rows, n_rows))
    slots, seen = [], [0, 0]
    for _, fam, _, _ in jobs:
        slots.append(seen[fam] % FF_LOAD_SLOTS)
        seen[fam] += 1

    def copy(n):
        m, fam, r0, n_rows = jobs[n]
        return pltpu.make_async_copy(hbm[m].at[layer, pl.ds(r0, n_rows), :], stages[fam].at[slots[n]],
                                     sems[fam].at[slots[n]])

    ahead = FF_LOAD_SLOTS - 1
    for n in range(min(ahead, len(jobs))):
        copy(n).start()
    for n, (m, fam, r0, n_rows) in enumerate(jobs):
        if n + ahead < len(jobs):
            copy(n + ahead).start()
        copy(n).wait()
        vmem[m][r0:r0 + n_rows, :] = stages[fam][slots[n]].astype(BF16)


def _ffn_kernel(*refs, rows, layer, x_pair, final_norm):
    n_in = 11 + int(x_pair) + int(final_norm)
    n_out = 2 if final_norm else 1
    ins, outs, scratch = refs[:n_in], refs[n_in:n_in + n_out], refs[n_in + n_out:]
    if x_pair:
        x = _select_rows(rows, ins[0], ins[1])
    else:
        x = ins[0][...]
    g_ref, shs, shp, scs, scp, gts, gtp, wg_hbm, wu_hbm, wd_hbm = ins[1 + int(x_pair):11 + int(x_pair)]
    wg_ref, wu_ref, wd_ref, stage_a, stage_d, sem_a, sem_d = scratch

    @pl.when(pl.program_id(0) == 0)
    def _():
        _ffn_load_weights(layer, (wg_hbm, wu_hbm, wd_hbm), (wg_ref, wu_ref, wd_ref),
                          (stage_a, stage_d), (sem_a, sem_d))

    h = _ada_norm(x, g_ref[...], _mod_value(rows, shs, shp), _mod_value(rows, scs, scp)).astype(BF16)
    chunks = [slice(lo, hi) for lo, hi in zip(FF_SPLITS[:-1], FF_SPLITS[1:])]
    halves = [slice(r * FF_ROW_HALF, (r + 1) * FF_ROW_HALF) for r in range(TOKEN_TILE // FF_ROW_HALF)]
    units = [(rs, sl) for rs in halves for sl in chunks]
    gate_up = [(_dot(h[rs], wg_ref[:, sl]), _dot(h[rs], wu_ref[:, sl])) for rs, sl in units]
    act = [(_silu(gate) * up).astype(BF16) for gate, up in gate_up]
    accs = []
    for r in range(len(halves)):
        part = None
        for c, sl in enumerate(chunks):
            d = _dot(act[r * len(chunks) + c], wd_ref[sl, :])
            part = d if part is None else part + d
        accs.append(part)
    acc = jnp.concatenate(accs, axis=0)
    out = x + (MACARON_W * _mod_value(rows, gts, gtp)) * acc
    if not final_norm:
        outs[0][...] = out
        return
    var = jnp.mean(out * out, axis=-1, keepdims=True)
    y = out * lax.rsqrt(var + NORM_EPS) * ins[-1][...]
    is_sample = pl.program_id(0) == rows.n_prompt_tiles

    @pl.when(jnp.logical_not(is_sample))
    def _():
        outs[0][...] = y

    @pl.when(is_sample)
    def _():
        outs[1][...] = y


def _ffn_call(rows, layer, mod_base, x, mod, norm_g, wg, wu, wd, final_g=None):
    x_pair = isinstance(x, tuple)
    if x_pair:
        in_specs = _pair_specs(rows, D_MODEL)
        operands = list(x)
    else:
        in_specs = [pl.BlockSpec((TOKEN_TILE, D_MODEL), lambda i: (i, 0))]
        operands = [x]
    in_specs.append(pl.BlockSpec((None, 1, D_MODEL), lambda i, l=layer: (l, 0, 0)))
    operands.append(norm_g)
    for j in range(3):
        in_specs += _mod_specs(rows, layer, mod_base + j)
        operands += [mod, mod]
    in_specs += [pl.BlockSpec(memory_space=pl.ANY)] * 3
    operands += [wg, wu, wd]
    scratch_shapes = [
        pltpu.VMEM((D_MODEL, D_FF), BF16), pltpu.VMEM((D_MODEL, D_FF), BF16), pltpu.VMEM((D_FF, D_MODEL), BF16),
        pltpu.VMEM((FF_LOAD_SLOTS, D_MODEL // FF_LOAD_CHUNKS, D_FF), F32),
        pltpu.VMEM((FF_LOAD_SLOTS, D_FF // FF_LOAD_CHUNKS, D_MODEL), F32),
        pltpu.SemaphoreType.DMA((FF_LOAD_SLOTS,)), pltpu.SemaphoreType.DMA((FF_LOAD_SLOTS,)),
    ]
    if final_g is None:
        out_shape = jax.ShapeDtypeStruct((rows.rows, D_MODEL), F32)
        out_specs = pl.BlockSpec((TOKEN_TILE, D_MODEL), lambda i: (i, 0))
    else:
        in_specs.append(pl.BlockSpec((1, D_MODEL), lambda i: (0, 0)))
        operands.append(final_g)
        out_shape = [jax.ShapeDtypeStruct((rows.np_rows, D_MODEL), F32),
                     jax.ShapeDtypeStruct((rows.ns_rows, D_MODEL), F32)]
        out_specs = _pair_specs(rows, D_MODEL)
    return pl.pallas_call(
        functools.partial(_ffn_kernel, rows=rows, layer=layer, x_pair=x_pair, final_norm=final_g is not None),
        out_shape=out_shape,
        grid=(rows.n_tiles,),
        in_specs=in_specs,
        out_specs=out_specs,
        scratch_shapes=scratch_shapes,
        compiler_params=_cparams(("arbitrary",)),
        name="ada_swiglu",
    )(*operands)


IN_OFF_RANK = 2 * A_QK + A_V
IN_OFF_AR = IN_OFF_RANK + A_GATE_RANK
IN_OFF_BQ = IN_OFF_AR + A_V
IN_OFF_GL = IN_OFF_BQ + B_Q + 2 * B_KV + C_WIDTH
IN_WIDTH = IN_OFF_GL + 3 * D_MODEL


def _rotary(x, cos, s_up, s_dn):
    half = ROPE_DIM // 2
    return x * cos + pltpu.roll(x, half, axis=1) * s_up + pltpu.roll(x, LANES - half, axis=1) * s_dn


def _inproj_kernel(x_ref, g_ref, shs, shp, scs, scp, w_ref, wup_ref, bg_ref,
                   cos_ref, sup_ref, sdn_ref,
                   gq_ref, gk_ref, gv_ref, la_ref, sq_ref, sk_ref, sv_ref, su_ref, *, rows):
    x = x_ref[...]
    h = _ada_norm(x, g_ref[...], _mod_value(rows, shs, shp), _mod_value(rows, scs, scp)).astype(BF16)
    za = _dot_nt(h, w_ref[0:IN_OFF_RANK, :])
    gq_ref[...] = za[:, 0:A_QK] * (A_DK ** -0.5)
    gk_ref[...] = za[:, A_QK:2 * A_QK]
    gv_ref[...] = za[:, 2 * A_QK:]
    z = _dot_nt(h, w_ref[IN_OFF_BQ:IN_OFF_GL, :])
    o = 0
    cos, s_up, s_dn = cos_ref[...], sup_ref[...], sdn_ref[...]
    for c in range(B_Q // LANES):
        sq_ref[:, c * LANES:(c + 1) * LANES] = _rotary(z[:, o + c * LANES:o + (c + 1) * LANES], cos, s_up, s_dn)
    o += B_Q
    sk_ref[...] = _rotary(z[:, o:o + B_KV], cos, s_up, s_dn)
    o += B_KV
    sv_ref[...] = z[:, o:o + B_KV]
    o += B_KV
    su_ref[...] = z[:, o:o + C_WIDTH]
    low = _dot_nt(h, w_ref[IN_OFF_RANK:IN_OFF_RANK + LANES, :]).astype(BF16)
    pre = _dot(low, wup_ref[...]) + bg_ref[...]
    la_ref[...] = _log_sigmoid(pre) * (LOG2_E / A_GATE_NORM)


def _inproj_call(rows, layer, x, mod, norm_g, w_in_t, w_up, b_gate, rot):
    row_spec = lambda w: pl.BlockSpec((TOKEN_TILE, w), lambda i: (i, 0))
    rot_spec = pl.BlockSpec(
        (TOKEN_TILE, LANES),
        lambda i, n=rows.n_prompt_tiles, per=rows.tiles_per_seq: (jnp.where(i < n, i % per, per), 0))
    in_specs = [row_spec(D_MODEL), pl.BlockSpec((None, 1, D_MODEL), lambda i, l=layer: (l, 0, 0))]
    operands = [x, norm_g]
    for j in (3, 4):
        in_specs += _mod_specs(rows, layer, j)
        operands += [mod, mod]
    in_specs += [
        _resident((None, IN_WIDTH, D_MODEL), lambda i, l=layer: (l, 0, 0)),
        _resident((None, LANES, A_QK), lambda i, l=layer: (l, 0, 0)),
        pl.BlockSpec((None, 1, A_QK), lambda i, l=layer: (l, 0, 0)),
        rot_spec, rot_spec, rot_spec,
    ]
    operands += [w_in_t, w_up, b_gate, *rot]
    widths = (A_QK, A_QK, A_V, A_QK, B_Q, B_KV, B_KV, C_WIDTH)
    return pl.pallas_call(
        functools.partial(_inproj_kernel, rows=rows),
        out_shape=[jax.ShapeDtypeStruct((rows.rows, w), F32) for w in widths],
        grid=(rows.n_tiles,),
        in_specs=in_specs,
        out_specs=[row_spec(w) for w in widths],
        compiler_params=_cparams(("arbitrary",)),
        name="mixer_in_proj",
    )(*operands)


def _cumsum_rows(x):
    n = x.shape[0]
    row = lax.broadcasted_iota(jnp.int32, x.shape, 0)
    s = 1
    while s < n:
        x = x + jnp.where(row >= s, pltpu.roll(x, s, axis=0), 0.0)
        s *= 2
    return x


def _head_blockdiag(x, lanes_per_head):
    lane_head = lax.broadcasted_iota(jnp.int32, x.shape, 1) // lanes_per_head
    return jnp.concatenate([jnp.where(lane_head == h, x, 0.0) for h in range(A_HEADS)], axis=0)


def _gla_kernel(*refs, chunk, block, n_chunks, n_seqs, single_step, has_state):
    if has_state:
        q_ref, k_ref, v_ref, la_ref, gn_ref, s0_ref, o_ref, so_ref, st_ref = refs
    else:
        q_ref, k_ref, v_ref, la_ref, gn_ref, o_ref, so_ref, st_ref = refs
    n_blocks = chunk // block
    st_shape = (A_V, A_QK)
    st_row_head = lax.broadcasted_iota(jnp.int32, st_shape, 0) // A_DV
    st_lane_head = lax.broadcasted_iota(jnp.int32, st_shape, 1) // A_DK
    st_mask = st_row_head == st_lane_head
    srow = lax.broadcasted_iota(jnp.int32, (block, A_HEADS * block), 0)
    scol = lax.broadcasted_iota(jnp.int32, (block, A_HEADS * block), 1) % block
    causal = scol <= srow
    gn = gn_ref[...]

    def init_state(sq):
        st_ref[sq] = jnp.zeros(st_shape, F32)
        if has_state:
            for h in range(A_HEADS):
                st_ref[sq, h * A_DV:(h + 1) * A_DV, h * A_DK:(h + 1) * A_DK] = s0_ref[sq, h].T

    def emit_state(sq):
        for h in range(A_HEADS):
            so_ref[sq, h] = st_ref[sq, h * A_DV:(h + 1) * A_DV, h * A_DK:(h + 1) * A_DK].T

    if single_step:
        for sq in range(n_seqs):
            init_state(sq)
    else:
        pl.when(pl.program_id(1) == 0)(lambda: init_state(0))

    units = [(sq, c) for sq in range(n_seqs) for c in range(n_chunks)]
    rows_of = {u: slice((u[0] * n_chunks + u[1]) * chunk, (u[0] * n_chunks + u[1] + 1) * chunk) for u in units}

    q, k, v, b = {}, {}, {}, {}
    for u in units:
        q[u], k[u], v[u] = q_ref[rows_of[u], :], k_ref[rows_of[u], :], v_ref[rows_of[u], :]
        b[u] = _cumsum_rows(la_ref[rows_of[u], :])

    s_diag, s_off, upd, q_in = {}, {}, {}, {}
    for u in units:
        for i in range(n_blocks):
            lo, hi = i * block, (i + 1) * block
            qi, ki, bi = q[u][lo:hi], k[u][lo:hi], b[u][lo:hi]
            mid = b[u][lo + block // 2 - 1:lo + block // 2, :]
            qd = (qi * jnp.exp2(bi - mid)).astype(BF16)
            kd = _head_blockdiag(ki * jnp.exp2(mid - bi), A_DK).astype(BF16)
            s_diag[u, i] = _dot_nt(qd, kd)
            if i > 0:
                ref = b[u][lo - 1:lo, :]
                qo = (qi * jnp.exp2(bi - ref)).astype(BF16)
                ko = _head_blockdiag(k[u][:lo] * jnp.exp2(ref - b[u][:lo]), A_DK).astype(BF16)
                s_off[u, i] = _dot_nt(qo, ko)
        b_last = b[u][chunk - 1:chunk, :]
        upd[u] = _dot(v[u].T.astype(BF16), (k[u] * jnp.exp2(b_last - b[u])).astype(BF16))
        q_in[u] = (q[u] * jnp.exp2(b[u])).astype(BF16)

    intra = {}
    for u in units:
        outs = []
        for i in range(n_blocks):
            lo, hi = i * block, (i + 1) * block
            att = jnp.where(causal, s_diag[u, i], 0.0).astype(BF16)
            oi = _dot(att, _head_blockdiag(v[u][lo:hi], A_DV).astype(BF16))
            if i > 0:
                oi = oi + _dot(s_off[u, i].astype(BF16), _head_blockdiag(v[u][:lo], A_DV).astype(BF16))
            outs.append(oi)
        intra[u] = jnp.concatenate(outs, axis=0) if n_blocks > 1 else outs[0]

    for u in units:
        sq = u[0]
        st = st_ref[sq]
        o = _dot_nt(q_in[u], st.astype(BF16)) + intra[u]
        st_ref[sq] = st * jnp.exp2(b[u][chunk - 1:chunk, :]) + jnp.where(st_mask, upd[u], 0.0)
        normed = []
        for h in range(A_HEADS):
            oh = o[:, h * A_DV:(h + 1) * A_DV]
            var = jnp.mean(oh * oh, axis=-1, keepdims=True)
            normed.append(oh * lax.rsqrt(var + NORM_EPS) * gn)
        o_ref[rows_of[u], :] = jnp.concatenate(normed, axis=1)

    if single_step:
        for sq in range(n_seqs):
            emit_state(sq)
    else:
        pl.when(pl.program_id(1) == pl.num_programs(1) - 1)(lambda: emit_state(0))


def _gla_call(layer, q, k, v, la, gnorm, s0, n_seq, seq_rows, step_rows, chunk, block, seqs_per_step=1,
              col_blocks=(0, 0, 0, 0)):
    steps = seq_rows // step_rows
    sps = seqs_per_step
    assert sps == 1 or steps == 1
    blk_rows = sps * step_rows
    row_spec = lambda w, cb=0: pl.BlockSpec((blk_rows, w), lambda s, j, steps=steps, cb=cb: (s * steps + j, cb))
    in_specs = [row_spec(A_QK, col_blocks[0]), row_spec(A_QK, col_blocks[1]), row_spec(A_V, col_blocks[2]),
                row_spec(A_QK, col_blocks[3]),
                pl.BlockSpec((None, 1, A_DV), lambda s, j, l=layer: (l, 0, 0))]
    operands = [q, k, v, la, gnorm]
    if s0 is not None:
        in_specs.append(pl.BlockSpec((None, sps, A_HEADS, A_DK, A_DV), lambda s, j, l=layer: (l, s, 0, 0, 0)))
        operands.append(s0)
    return pl.pallas_call(
        functools.partial(_gla_kernel, chunk=chunk, block=block, n_chunks=step_rows // chunk,
                          n_seqs=sps, single_step=steps == 1, has_state=s0 is not None),
        out_shape=[jax.ShapeDtypeStruct((n_seq * seq_rows, A_V), F32),
                   jax.ShapeDtypeStruct((n_seq, A_HEADS, A_DK, A_DV), F32)],
        grid=(n_seq // sps, steps),
        in_specs=in_specs,
        out_specs=[row_spec(A_V),
                   pl.BlockSpec((sps, A_HEADS, A_DK, A_DV), lambda s, j: (s, 0, 0, 0))],
        scratch_shapes=[pltpu.VMEM((sps, A_V, A_QK), F32)],
        compiler_params=_cparams(("arbitrary", "arbitrary")),
        name="gla_chunked",
    )(*operands)


SWA_STEP_BLOCKS = MIXER_STEP_ROWS // WINDOW


def _swa_kernel(sink_ref, q_ref, kc_ref, vc_ref, kp_ref, vp_ref, o_ref, *, layer):
    tk = 2 * WINDOW
    kj = lax.broadcasted_iota(jnp.int32, (tk, WINDOW), 0)
    qi = lax.broadcasted_iota(jnp.int32, (tk, WINDOW), 1) + WINDOW
    band = (kj <= qi) & (kj >= qi - WINDOW)
    zeros = jnp.zeros((B_HEAD_DIM, WINDOW), F32)
    chains = [(blk, h) for blk in range(SWA_STEP_BLOCKS) for h in range(B_KV_HEADS)]

    keys, vals_t, q_t, valid = [], [], [], []
    for blk in range(SWA_STEP_BLOCKS):
        cur = slice(blk * WINDOW, (blk + 1) * WINDOW)
        if blk == 0:
            k_prev, v_prev = kp_ref[...], vp_ref[...]
            valid.append(band & (kj >= jnp.where(pl.program_id(1) > 0, 0, WINDOW)))
        else:
            prev = slice((blk - 1) * WINDOW, blk * WINDOW)
            k_prev, v_prev = kc_ref[prev, :], vc_ref[prev, :]
            valid.append(band)
        keys.append(jnp.concatenate([k_prev, kc_ref[cur, :]], axis=0).astype(BF16))
        vals_t.append(jnp.concatenate([v_prev, vc_ref[cur, :]], axis=0).T.astype(BF16))
        q_t.append(q_ref[cur, :].T * (LOG2_E * B_HEAD_DIM ** -0.5))

    scores = {}
    for blk, h in chains:
        cols = []
        for g in range(B_GROUP):
            hd = h * B_GROUP + g
            qg = q_t[blk][hd * B_HEAD_DIM:(hd + 1) * B_HEAD_DIM]
            cols.append(jnp.concatenate([qg, zeros] if h == 0 else [zeros, qg], axis=0))
        scores[blk, h] = _dot(keys[blk], jnp.concatenate(cols, axis=1).astype(BF16))

    probs, inv = {}, {}
    for blk, h in chains:
        ps = []
        for g in range(B_GROUP):
            sink = sink_ref[layer, h * B_GROUP + g] * LOG2_E
            sg = jnp.where(valid[blk], scores[blk, h][:, g * WINDOW:(g + 1) * WINDOW], NEG_BIG)
            m = jnp.maximum(jnp.max(sg, axis=0, keepdims=True), sink)
            p = jnp.exp2(sg - m)
            inv[blk, h, g] = 1.0 / (jnp.sum(p, axis=0, keepdims=True) + jnp.exp2(sink - m))
            ps.append(p.astype(BF16))
        probs[blk, h] = jnp.concatenate(ps, axis=1)

    out_t = {}
    for blk, h in chains:
        out_t[blk, h] = _dot(vals_t[blk][h * B_HEAD_DIM:(h + 1) * B_HEAD_DIM], probs[blk, h])

    for blk in range(SWA_STEP_BLOCKS):
        rows = [out_t[blk, h][:, g * WINDOW:(g + 1) * WINDOW] * inv[blk, h, g]
                for h in range(B_KV_HEADS) for g in range(B_GROUP)]
        o_ref[blk * WINDOW:(blk + 1) * WINDOW, :] = jnp.concatenate(rows, axis=0).T


def _swa_call(layer, sinks, q, k, v, n_seq, blocks_per_seq):
    sb = SWA_STEP_BLOCKS
    assert blocks_per_seq % sb == 0
    steps = blocks_per_seq // sb
    cur = lambda w: pl.BlockSpec((sb * WINDOW, w), lambda s, n, steps=steps: (s * steps + n, 0))
    prev = pl.BlockSpec((WINDOW, B_KV),
                        lambda s, n, nb=blocks_per_seq, sb=sb: (s * nb + jnp.maximum(n * sb - 1, 0), 0))
    return pl.pallas_call(
        functools.partial(_swa_kernel, layer=layer),
        out_shape=jax.ShapeDtypeStruct((n_seq * blocks_per_seq * WINDOW, B_Q), F32),
        grid=(n_seq, steps),
        in_specs=[pl.BlockSpec(memory_space=pltpu.SMEM), cur(B_Q), cur(B_KV), cur(B_KV), prev, prev],
        out_specs=cur(B_Q),
        compiler_params=_cparams(("arbitrary", "arbitrary")),
        name="swa_banded",
    )(sinks, q, k, v, k, v)


SWA_SAMPLE_SEQS = 16
SWA_SAMPLE_ROWS = SUBLANES


def _swa_sample_kernel(sink_ref, q_ref, kn_ref, vn_ref, kt_ref, vt_ref, o_ref, nk_ref, nv_ref, *, layer, n_tok):
    pr = SWA_SAMPLE_ROWS
    keep = WINDOW - n_tok
    lane = lax.broadcasted_iota(jnp.int32, (B_HEAD_DIM, WINDOW), 1)
    rows = B_GROUP * pr
    tok = lax.broadcasted_iota(jnp.int32, (rows, WINDOW), 0) % pr
    grp = lax.broadcasted_iota(jnp.int32, (rows, 1), 0) // pr
    pos = lax.broadcasted_iota(jnp.int32, (rows, WINDOW), 1)
    valid_new = pos - keep <= tok
    valid_old = (pos >= tok) & (pos < n_tok)

    def shifted_in(old, new_rows):
        padded = jnp.concatenate([jnp.zeros((WINDOW - pr, B_KV), F32), new_rows], axis=0)
        cols = pltpu.roll(padded, n_tok, axis=0).T
        return [jnp.where(lane < keep, pltpu.roll(old[h], keep, axis=1),
                          cols[h * B_HEAD_DIM:(h + 1) * B_HEAD_DIM]) for h in range(B_KV_HEADS)]

    chains = [(i, h) for i in range(SWA_SAMPLE_SEQS) for h in range(B_KV_HEADS)]
    k_old, v_old, k_new, v_new = {}, {}, {}, {}
    for i in range(SWA_SAMPLE_SEQS):
        rs = slice(i * pr, (i + 1) * pr)
        ko = [kt_ref[i, h] for h in range(B_KV_HEADS)]
        vo = [vt_ref[i, h] for h in range(B_KV_HEADS)]
        kn, vn = shifted_in(ko, kn_ref[rs, :]), shifted_in(vo, vn_ref[rs, :])
        for h in range(B_KV_HEADS):
            nk_ref[i, h] = kn[h]
            nv_ref[i, h] = vn[h]
            k_old[i, h], v_old[i, h] = ko[h].astype(BF16), vo[h].astype(BF16)
            k_new[i, h], v_new[i, h] = kn[h].astype(BF16), vn[h].astype(BF16)

    s_new, s_old = {}, {}
    for c in chains:
        q = (q_ref[c[0], c[1]] * (LOG2_E * B_HEAD_DIM ** -0.5)).astype(BF16)
        s_new[c], s_old[c] = _dot(q, k_new[c]), _dot(q, k_old[c])

    p_new, p_old, inv = {}, {}, {}
    for c in chains:
        sn = jnp.where(valid_new, s_new[c], NEG_BIG)
        so = jnp.where(valid_old, s_old[c], NEG_BIG)
        sink = jnp.zeros((rows, 1), F32)
        for g in range(B_GROUP):
            sink = jnp.where(grp == g, sink_ref[layer, c[1] * B_GROUP + g] * LOG2_E, sink)
        m = jnp.maximum(jnp.maximum(jnp.max(sn, axis=-1, keepdims=True),
                                    jnp.max(so, axis=-1, keepdims=True)), sink)
        pn, po = jnp.exp2(sn - m), jnp.exp2(so - m)
        inv[c] = 1.0 / (jnp.sum(pn, axis=-1, keepdims=True) + jnp.sum(po, axis=-1, keepdims=True)
                        + jnp.exp2(sink - m))
        p_new[c], p_old[c] = pn.astype(BF16), po.astype(BF16)

    for c in chains:
        o = _dot_nt(p_new[c], v_new[c]) + _dot_nt(p_old[c], v_old[c])
        o_ref[c[0], c[1]] = o * inv[c]


def _swa_sample_call(layer, sinks, q, k_new, v_new, kt, vt, n_seq, n_tok, col_blocks=(0, 0)):
    sb, pr = SWA_SAMPLE_SEQS, SWA_SAMPLE_ROWS
    qspec = pl.BlockSpec((sb, B_KV_HEADS, B_GROUP * pr, B_HEAD_DIM), lambda s: (s, 0, 0, 0))
    new_rows = lambda cb: pl.BlockSpec((sb * pr, B_KV), lambda s, cb=cb: (s, cb))
    buf_in = pl.BlockSpec((None, sb, B_KV_HEADS, B_HEAD_DIM, WINDOW), lambda s, l=layer: (l, s, 0, 0, 0))
    buf_out = pl.BlockSpec((sb, B_KV_HEADS, B_HEAD_DIM, WINDOW), lambda s: (s, 0, 0, 0))
    buf_sds = jax.ShapeDtypeStruct((n_seq, B_KV_HEADS, B_HEAD_DIM, WINDOW), F32)
    return pl.pallas_call(
        functools.partial(_swa_sample_kernel, layer=layer, n_tok=n_tok),
        out_shape=[jax.ShapeDtypeStruct((n_seq, B_KV_HEADS, B_GROUP * pr, B_HEAD_DIM), F32), buf_sds, buf_sds],
        grid=(n_seq // sb,),
        in_specs=[pl.BlockSpec(memory_space=pltpu.SMEM), qspec, new_rows(col_blocks[0]), new_rows(col_blocks[1]),
                  buf_in, buf_in],
        out_specs=[qspec, buf_out, buf_out],
        compiler_params=_cparams(("arbitrary",)),
        name="swa_decode",
    )(sinks, q, k_new, v_new, kt, vt)


def _s5_prep_kernel(lr_ref, li_ref, ldt_ref, br_ref, bi_ref, ctr_ref, cti_ref,
                    bw_ref, cr_ref, ci_ref, ar_ref, ai_ref, er_ref, ei_ref, fr_ref, fi_ref):
    lr, li = lr_ref[...], li_ref[...]
    dt = jnp.exp(ldt_ref[...])
    rho, th = lr * dt, li * dt
    mag = jnp.exp(rho)
    a_re, a_im = mag * jnp.cos(th), mag * jnp.sin(th)
    den = lr * lr + li * li
    nr, ni = a_re - 1.0, a_im
    f_re = (nr * lr + ni * li) / den
    f_im = (ni * lr - nr * li) / den
    br, bi = br_ref[...], bi_ref[...]
    bb_re = f_re * br - f_im * bi
    bb_im = f_re * bi + f_im * br
    lane_group = lax.broadcasted_iota(jnp.int32, bb_re.shape, 1) // C_STATE
    place = lambda bb: jnp.concatenate(
        [jnp.where(lane_group == g, bb, 0.0) for g in range(C_GROUPS)], axis=0).astype(BF16)
    placed_re, placed_im = place(bb_re), place(bb_im)
    for n in range(S5_N // MXU_TILE):
        src = slice(n * MXU_TILE, (n + 1) * MXU_TILE)
        bw_ref[:, 2 * n * MXU_TILE:(2 * n + 1) * MXU_TILE] = placed_re[:, src]
        bw_ref[:, (2 * n + 1) * MXU_TILE:(2 * n + 2) * MXU_TILE] = placed_im[:, src]
    spread_c = lax.broadcasted_iota(jnp.int32, (LANES, C_WIDTH), 0)
    spread_j = lax.broadcasted_iota(jnp.int32, (LANES, C_WIDTH), 1) % C_GROUP
    spread = jnp.where(spread_c == spread_j, 1.0, 0.0).astype(BF16)
    row_group = lax.broadcasted_iota(jnp.int32, cr_ref.shape, 0) // C_STATE
    col_group = lax.broadcasted_iota(jnp.int32, cr_ref.shape, 1) // C_GROUP
    own = row_group == col_group
    cr_ref[...] = jnp.where(own, _dot(ctr_ref[...].astype(BF16), spread), 0.0).astype(BF16)
    ci_ref[...] = jnp.where(own, _dot(cti_ref[...].astype(BF16), spread), 0.0).astype(BF16)
    ar_ref[...] = a_re
    ai_ref[...] = a_im
    s = (lax.broadcasted_iota(jnp.int32, er_ref.shape, 0) + 1).astype(F32)
    cs, sn = jnp.cos(s * th), jnp.sin(s * th)
    e_mag, f_mag = jnp.exp(-(s * rho)), jnp.exp(s * rho)
    er_ref[...] = e_mag * cs
    ei_ref[...] = -(e_mag * sn)
    fr_ref[...] = f_mag * cs
    fi_ref[...] = f_mag * sn


def _s5_prep_call(lr, li, ldt, b_re_t, b_im_t, c_re_t, c_im_t):
    depth = lr.shape[0]
    lsel = lambda l: (l, 0, 0)
    vec = pl.BlockSpec((None, 1, S5_N), lsel)
    mat = pl.BlockSpec((None, C_GROUP, S5_N), lsel)
    c_in = pl.BlockSpec((None, S5_N, LANES), lsel)
    cmat = pl.BlockSpec((None, S5_N, C_WIDTH), lsel)
    tab = pl.BlockSpec((None, S5_CHUNK, S5_N), lsel)
    sds = lambda r: jax.ShapeDtypeStruct((depth, r, S5_N), F32)
    c_sds = jax.ShapeDtypeStruct((depth, S5_N, C_WIDTH), BF16)
    return pl.pallas_call(
        _s5_prep_kernel,
        out_shape=[jax.ShapeDtypeStruct((depth, C_WIDTH, 2 * S5_N), BF16), c_sds, c_sds,
                   sds(1), sds(1)] + [sds(S5_CHUNK)] * 4,
        grid=(depth,),
        in_specs=[vec, vec, vec, mat, mat, c_in, c_in],
        out_specs=[pl.BlockSpec((None, C_WIDTH, 2 * S5_N), lsel), cmat, cmat, vec, vec, tab, tab, tab, tab],
        compiler_params=_cparams(("arbitrary",)),
        name="s5_discretise",
    )(lr, li, ldt, b_re_t, b_im_t, c_re_t, c_im_t)


def _cmul(ar, ai, br, bi):
    return ar * br - ai * bi, ar * bi + ai * br


def _s5_prompt_kernel(u_ref, bw_ref, cr_ref, ci_ref, d_ref, er_ref, ei_ref, fr_ref, fi_ref,
                      y_ref, hs_ref, car_ref, cai_ref, xr_ref, xi_ref, hr_ref, hi_ref, *, n_chunks):
    step = pl.program_id(1)
    t = S5_CHUNK

    @pl.when(step == 0)
    def _():
        car_ref[...] = jnp.zeros(car_ref.shape, F32)
        cai_ref[...] = jnp.zeros(cai_ref.shape, F32)

    groups = [slice(n * MXU_TILE, (n + 1) * MXU_TILE) for n in range(S5_N // MXU_TILE)]
    halves = [slice(0, LANES), slice(LANES, MXU_TILE)]
    group_cols = [[slice(g.start + hv.start, g.start + hv.stop) for hv in halves] for g in groups]

    u = u_ref[...]
    ub = u.astype(BF16)
    tiles_per_kt = len(groups) // (C_WIDTH // S5_KT)
    for n, cols in enumerate(group_cols):
        rows_k = slice(n // tiles_per_kt * S5_KT, (n // tiles_per_kt + 1) * S5_KT)
        bu = _dot(ub[:, rows_k], bw_ref[rows_k, 2 * n * MXU_TILE:(2 * n + 2) * MXU_TILE])
        for c in range(n_chunks):
            rs = slice(c * t, (c + 1) * t)
            for cs, hv in zip(cols, halves):
                xr, xi = _cmul(er_ref[:, cs], ei_ref[:, cs], bu[rs, hv],
                               bu[rs, MXU_TILE + hv.start:MXU_TILE + hv.stop])
                xr_ref[rs, cs] = xr.astype(BF16)
                xi_ref[rs, cs] = xi.astype(BF16)

    row = lax.broadcasted_iota(jnp.int32, (t, t), 0)
    col = lax.broadcasted_iota(jnp.int32, (t, t), 1)
    tril = jnp.where(col <= row, 1.0, 0.0).astype(BF16)

    carry = [(car_ref[:, g], cai_ref[:, g]) for g in groups]
    for c in range(n_chunks):
        rs = slice(c * t, (c + 1) * t)
        sums = [(_dot(tril, xr_ref[rs, g]), _dot(tril, xi_ref[rs, g])) for g in groups]
        for n, (g, cols) in enumerate(zip(groups, group_cols)):
            sum_re, sum_im = sums[n][0] + carry[n][0], sums[n][1] + carry[n][1]
            hs = [_cmul(fr_ref[:, cs], fi_ref[:, cs], sum_re[:, hv], sum_im[:, hv]) for cs, hv in zip(cols, halves)]
            hr = jnp.concatenate([h[0] for h in hs], axis=1)
            hi = jnp.concatenate([h[1] for h in hs], axis=1)
            hr_ref[rs, g] = hr.astype(BF16)
            hi_ref[rs, g] = hi.astype(BF16)
            carry[n] = (hr[t - 1:t, :], hi[t - 1:t, :])
    for g, (car, cai) in zip(groups, carry):
        car_ref[:, g] = car
        cai_ref[:, g] = cai

    n_out = C_WIDTH // S5_KT
    k_rows = S5_N // n_out
    for j in range(n_out):
        ks = slice(j * k_rows, (j + 1) * k_rows)
        cs = slice(j * S5_KT, (j + 1) * S5_KT)
        y_ref[:, cs] = (_dot(hr_ref[:, ks], cr_ref[ks, cs]) - _dot(hi_ref[:, ks], ci_ref[ks, cs])
                        + d_ref[:, cs] * u[:, cs])

    @pl.when(step == pl.num_programs(1) - 1)
    def _():
        hs_ref[0:1, :] = car_ref[...]
        hs_ref[1:2, :] = cai_ref[...]


def _s5_prompt_call(layer, u, bw, c_re, c_im, d, tabs, n_seq, seq_rows, step_rows):
    steps = seq_rows // step_rows
    row_spec = pl.BlockSpec((step_rows, C_WIDTH), lambda s, j, steps=steps: (s * steps + j, 0))
    lsel = lambda s, j, l=layer: (l, 0, 0)
    tab = pl.BlockSpec((None, S5_CHUNK, S5_N), lsel)
    return pl.pallas_call(
        functools.partial(_s5_prompt_kernel, n_chunks=step_rows // S5_CHUNK),
        out_shape=[jax.ShapeDtypeStruct((n_seq * seq_rows, C_WIDTH), F32),
                   jax.ShapeDtypeStruct((n_seq, 2, S5_N), F32)],
        grid=(n_seq, steps),
        in_specs=[row_spec,
                  _resident((None, C_WIDTH, 2 * S5_N), lsel),
                  _resident((None, S5_N, C_WIDTH), lsel),
                  _resident((None, S5_N, C_WIDTH), lsel),
                  pl.BlockSpec((None, 1, C_WIDTH), lsel),
                  tab, tab, tab, tab],
        out_specs=[row_spec, pl.BlockSpec((None, 2, S5_N), lambda s, j: (s, 0, 0))],
        scratch_shapes=[pltpu.VMEM((1, S5_N), F32), pltpu.VMEM((1, S5_N), F32)]
                       + [pltpu.VMEM((step_rows, S5_N), BF16)] * 4,
        compiler_params=_cparams(("arbitrary", "arbitrary")),
        name="s5_chunked_scan",
    )(u, bw, c_re, c_im, d, *tabs)


def _s5_sample_kernel(u_ref, h0r_ref, h0i_ref, ar_ref, ai_ref, bw_ref, cr_ref, ci_ref, d_ref,
                      y_ref, hr_ref, hi_ref, *, n_tok, n_seq):
    hr, hi = h0r_ref[...], h0i_ref[...]
    a_re, a_im = ar_ref[...], ai_ref[...]
    for t in range(n_tok):
        u = u_ref[t * n_seq:(t + 1) * n_seq, :]
        bu = _dot(u.astype(BF16), bw_ref[...])
        tiles = range(S5_N // MXU_TILE)
        bu_re = jnp.concatenate([bu[:, 2 * n * MXU_TILE:(2 * n + 1) * MXU_TILE] for n in tiles], axis=1)
        bu_im = jnp.concatenate([bu[:, (2 * n + 1) * MXU_TILE:(2 * n + 2) * MXU_TILE] for n in tiles], axis=1)
        hr, hi = _cmul(a_re, a_im, hr, hi)
        hr, hi = hr + bu_re, hi + bu_im
        y_ref[t * n_seq:(t + 1) * n_seq, :] = (_dot(hr.astype(BF16), cr_ref[...]) - _dot(hi.astype(BF16), ci_ref[...])
                                               + d_ref[...] * u)
    hr_ref[...] = hr
    hi_ref[...] = hi


def _s5_sample_call(layer, u, h0_re, h0_im, a_re, a_im, bw, c_re, c_im, d, n_tok, n_seq):
    lsel = lambda i, l=layer: (l, 0, 0)
    full = lambda shape: pl.BlockSpec(shape, lambda i: (0,) * len(shape))
    return pl.pallas_call(
        functools.partial(_s5_sample_kernel, n_tok=n_tok, n_seq=n_seq),
        out_shape=[jax.ShapeDtypeStruct((n_tok * n_seq, C_WIDTH), F32),
                   jax.ShapeDtypeStruct((n_seq, S5_N), F32),
                   jax.ShapeDtypeStruct((n_seq, S5_N), F32)],
        grid=(1,),
        in_specs=[full((n_tok * n_seq, C_WIDTH)),
                  pl.BlockSpec((None, n_seq, S5_N), lsel),
                  pl.BlockSpec((None, n_seq, S5_N), lsel),
                  pl.BlockSpec((None, 1, S5_N), lsel),
                  pl.BlockSpec((None, 1, S5_N), lsel),
                  pl.BlockSpec((None, C_WIDTH, 2 * S5_N), lsel),
                  pl.BlockSpec((None, S5_N, C_WIDTH), lsel),
                  pl.BlockSpec((None, S5_N, C_WIDTH), lsel),
                  pl.BlockSpec((None, 1, C_WIDTH), lsel)],
        out_specs=[full((n_tok * n_seq, C_WIDTH)), full((n_seq, S5_N)), full((n_seq, S5_N))],
        compiler_params=_cparams(("arbitrary",)),
        name="s5_step_scan",
    )(u, h0_re, h0_im, a_re, a_im, bw, c_re, c_im, d)


def _merge_kernel(x_ref, g_ref, shs, shp, scs, scp, gts, gtp,
                  oap, oas, obp, obs, ocp, ocs,
                  w_ref, wa_ref, wb_ref, wc_ref, wglu_ref, wo_ref, o_ref, *, rows):
    x = x_ref[...]
    br_b = _dot(_select_rows(rows, obp, obs).astype(BF16), wb_ref[...])
    glu = _dot(_gelu_tanh(_select_rows(rows, ocp, ocs)).astype(BF16), wglu_ref[...])
    h = _ada_norm(x, g_ref[...], _mod_value(rows, shs, shp), _mod_value(rows, scs, scp)).astype(BF16)
    z = _dot_nt(h, w_ref[IN_OFF_GL:IN_WIDTH, :])
    z_ar = _dot_nt(h, w_ref[IN_OFF_AR:IN_OFF_BQ, :])
    oc = glu[:, :C_WIDTH] * _sigmoid(glu[:, C_WIDTH:])
    br_c = _dot(oc.astype(BF16), wc_ref[...])
    oa = _select_rows(rows, oap, oas) * _silu(z_ar)
    br_a = _dot(oa.astype(BF16), wa_ref[...])
    mix = (_sigmoid(z[:, D_MODEL:2 * D_MODEL]) * br_b
           + _sigmoid(z[:, 2 * D_MODEL:3 * D_MODEL]) * br_c
           + _sigmoid(z[:, 0:D_MODEL]) * br_a)
    out = _dot(mix.astype(BF16), wo_ref[...])
    o_ref[...] = x + _mod_value(rows, gts, gtp) * out


def _merge_call(rows, layer, x, mod, norm_g, oa, ob, oc, w_in_t, wa, wb, wc, wglu, wo):
    in_specs = [pl.BlockSpec((TOKEN_TILE, D_MODEL), lambda i: (i, 0)),
                pl.BlockSpec((None, 1, D_MODEL), lambda i, l=layer: (l, 0, 0))]
    operands = [x, norm_g]
    for j in (3, 4, 5):
        in_specs += _mod_specs(rows, layer, j)
        operands += [mod, mod]
    for pair, w in ((oa, A_V), (ob, B_Q), (oc, C_WIDTH)):
        in_specs += _pair_specs(rows, w)
        operands += list(pair)
    lsel = lambda i, l=layer: (l, 0, 0)
    in_specs += [_resident((None, IN_WIDTH, D_MODEL), lsel),
                 _resident((None, A_V, D_MODEL), lsel),
                 _resident((None, B_Q, D_MODEL), lsel),
                 _resident((None, C_WIDTH, D_MODEL), lsel),
                 _resident((None, C_WIDTH, 2 * C_WIDTH), lsel),
                 _resident((None, D_MODEL, D_MODEL), lsel)]
    operands += [w_in_t, wa, wb, wc, wglu, wo]
    return pl.pallas_call(
        functools.partial(_merge_kernel, rows=rows),
        out_shape=jax.ShapeDtypeStruct((rows.rows, D_MODEL), F32),
        grid=(rows.n_tiles,),
        in_specs=in_specs,
        out_specs=pl.BlockSpec((TOKEN_TILE, D_MODEL), lambda i: (i, 0)),
        compiler_params=_cparams(("arbitrary",)),
        name="branch_merge",
    )(*operands)


def _rotary_tables(rows):
    half = ROPE_DIM // 2
    inv_freq = ROPE_THETA ** (-jnp.arange(half, dtype=F32) / half)
    pos_p = jnp.arange(rows.l, dtype=jnp.int32)
    pos_s = PAST_LEN + jnp.repeat(jnp.arange(rows.t, dtype=jnp.int32), rows.bs)
    pos = jnp.concatenate([pos_p, pos_s])
    ang = pos.astype(F32)[:, None] * inv_freq[None, :]
    cos, sin = jnp.cos(ang), jnp.sin(ang)
    n = pos.shape[0]
    ones = jnp.ones((n, B_HEAD_DIM - ROPE_DIM), F32)
    zeros = jnp.zeros((n, B_HEAD_DIM - ROPE_DIM), F32)
    z8 = jnp.zeros((n, half), F32)
    per_head = lambda parts: jnp.tile(jnp.concatenate(parts, axis=1), (1, LANES // B_HEAD_DIM))
    return (per_head([cos, cos, ones]), per_head([z8, sin, zeros]), per_head([-sin, z8, zeros]))


def _sample_to_seq(a, rows, pad_to):
    w = a.shape[-1]
    a = a.reshape(rows.t, rows.bs, w).transpose(1, 0, 2)
    a = jnp.pad(a, ((0, 0), (0, pad_to - rows.t), (0, 0)))
    return a.reshape(rows.bs * pad_to, w)


def _seq_to_sample(a, rows, pad_to):
    w = a.shape[-1]
    return a.reshape(rows.bs, pad_to, w)[:, :rows.t].transpose(1, 0, 2).reshape(rows.ns_rows, w)


def kernel(x_prompt, x_sample, state_gla, cache_swa_k, cache_swa_v, state_s5_re, state_s5_im, c_prompt, c_sample, w_ada, b_ada, norm_ffn1, ffn1_w_gate, ffn1_w_up, ffn1_w_down, norm_mix, w_in, gla_w_gate_up, gla_b_gate, gla_norm, swa_sinks, s5_a_re, s5_a_im, s5_b_re, s5_b_im, s5_c_re, s5_c_im, s5_d, s5_log_dt, s5_w_glu, w_branch_a, w_branch_b, w_branch_c, w_out, norm_ffn2, ffn2_w_gate, ffn2_w_up, ffn2_w_down, norm_final):
    depth = w_ada.shape[0]
    b, l, _ = x_prompt.shape
    bs, t, _ = x_sample.shape
    rows = _Rows(b, l, bs, t)
    np_rows = rows.np_rows

    x = (x_prompt.reshape(np_rows, D_MODEL), x_sample.transpose(1, 0, 2).reshape(rows.ns_rows, D_MODEL))
    c_all = jnp.concatenate([c_sample, c_prompt, jnp.zeros((SUBLANES - b, D_MODEL), F32)], axis=0)
    mod = _mod_call(c_all, w_ada, b_ada)

    assert w_in.shape[-1] == IN_WIDTH
    w_in_t = w_in.transpose(0, 2, 1).astype(BF16)
    w_up = jnp.pad(gla_w_gate_up, ((0, 0), (0, LANES - A_GATE_RANK), (0, 0))).astype(BF16)
    cast = lambda w: w.astype(BF16)
    f1g, f1u, f1d = ffn1_w_gate, ffn1_w_up, ffn1_w_down
    f2g, f2u, f2d = ffn2_w_gate, ffn2_w_up, ffn2_w_down
    wa, wb, wc, wglu, wo = cast(w_branch_a), cast(w_branch_b), cast(w_branch_c), cast(s5_w_glu), cast(w_out)
    vec3 = lambda a: a.reshape(depth, 1, a.shape[-1])
    n1, nm, n2 = vec3(norm_ffn1), vec3(norm_mix), vec3(norm_ffn2)
    b_gate, g_norm, s5d = vec3(gla_b_gate), vec3(gla_norm), vec3(s5_d)
    rot = _rotary_tables(rows)

    flat = lambda a: a.reshape(depth, 1, S5_N)
    ldt = flat(jnp.broadcast_to(s5_log_dt[:, :, None], (depth, C_GROUPS, C_STATE)))
    to_cols = lambda bm: bm.transpose(0, 3, 1, 2).reshape(depth, C_GROUP, S5_N)
    to_rows = lambda cm: jnp.pad(cm.transpose(0, 1, 3, 2).reshape(depth, S5_N, C_GROUP),
                                 ((0, 0), (0, 0), (0, LANES - C_GROUP)))
    s5_bw, s5_cr, s5_ci, a_re, a_im, e_re, e_im, f_re, f_im = _s5_prep_call(
        flat(s5_a_re), flat(s5_a_im), ldt, to_cols(s5_b_re), to_cols(s5_b_im), to_rows(s5_c_re), to_rows(s5_c_im))
    tabs = (e_re, e_im, f_re, f_im)
    h0_re = state_s5_re.reshape(depth, bs, S5_N)
    h0_im = state_s5_im.reshape(depth, bs, S5_N)
    cache_kt = cache_swa_k.transpose(0, 1, 3, 4, 2)
    cache_vt = cache_swa_v.transpose(0, 1, 3, 4, 2)

    outs = {k: [] for k in ("gla_p", "gla_s", "kp", "ks", "vp", "vs", "rp", "rs", "ip", "is")}
    pad8 = SWA_SAMPLE_ROWS
    for layer in range(depth):
        x = _ffn_call(rows, layer, 0, x, mod, n1, f1g, f1u, f1d)
        gq, gk, gv, la, sq, sk, sv, su = _inproj_call(rows, layer, x, mod, nm, w_in_t, w_up, b_gate, rot)

        oa_p, st_p = _gla_call(layer, gq, gk, gv, la, g_norm, None, b, l, MIXER_STEP_ROWS, GLA_CHUNK, GLA_BLOCK)
        parts = dict(gq=gq, gk=gk, la=la, sk=sk, sv=sv, gv=gv)
        seq_in = _sample_to_seq(jnp.concatenate([a[np_rows:] for a in parts.values()], axis=1), rows, pad8)
        starts = dict(zip(parts, np.cumsum([0] + [a.shape[1] for a in parts.values()])))
        col_block = lambda name: int(starts[name]) // parts[name].shape[1]
        assert all(int(starts[n]) % parts[n].shape[1] == 0 for n in parts)
        oa_s, st_s = _gla_call(layer, seq_in, seq_in, seq_in, seq_in, g_norm, state_gla,
                               bs, pad8, pad8, pad8, pad8, seqs_per_step=GLA_SAMPLE_SEQS,
                               col_blocks=tuple(col_block(n) for n in ("gq", "gk", "gv", "la")))
        oa_s = _seq_to_sample(oa_s, rows, pad8)

        ob_p = _swa_call(layer, swa_sinks, sq, sk, sv, b, l // WINDOW)
        q_s = sq[np_rows:].reshape(t, bs, B_KV_HEADS, B_GROUP, B_HEAD_DIM).transpose(1, 2, 3, 0, 4)
        q_s = jnp.pad(q_s, ((0, 0), (0, 0), (0, 0), (0, pad8 - t), (0, 0)))
        q_s = q_s.reshape(bs, B_KV_HEADS, B_GROUP * pad8, B_HEAD_DIM)
        ob_s, kt_new, vt_new = _swa_sample_call(layer, swa_sinks, q_s, seq_in, seq_in, cache_kt, cache_vt, bs, t,
                                                col_blocks=(col_block("sk"), col_block("sv")))
        ob_s = ob_s.reshape(bs, B_KV_HEADS, B_GROUP, pad8, B_HEAD_DIM)[:, :, :, :t]
        ob_s = ob_s.transpose(3, 0, 1, 2, 4).reshape(rows.ns_rows, B_Q)

        oc_p, hs_p = _s5_prompt_call(layer, su, s5_bw, s5_cr, s5_ci, s5d, tabs, b, l, MIXER_STEP_ROWS)
        oc_s, hr_s, hi_s = _s5_sample_call(layer, su[np_rows:], h0_re, h0_im, a_re, a_im,
                                           s5_bw, s5_cr, s5_ci, s5d, t, bs)

        x = _merge_call(rows, layer, x, mod, nm, (oa_p, oa_s), (ob_p, ob_s), (oc_p, oc_s),
                        w_in_t, wa, wb, wc, wglu, wo)
        x = _ffn_call(rows, layer, 6, x, mod, n2, f2g, f2u, f2d,
                      final_g=norm_final.reshape(1, D_MODEL) if layer == depth - 1 else None)

        outs["gla_p"].append(st_p)
        outs["gla_s"].append(st_s)
        kv_shape = (b, WINDOW, B_KV_HEADS, B_HEAD_DIM)
        window_rows = lambda a: jnp.stack([a[(i + 1) * l - WINDOW:(i + 1) * l] for i in range(b)])
        outs["kp"].append(window_rows(sk).reshape(kv_shape))
        outs["vp"].append(window_rows(sv).reshape(kv_shape))
        outs["ks"].append(kt_new.transpose(0, 3, 1, 2))
        outs["vs"].append(vt_new.transpose(0, 3, 1, 2))
        outs["rp"].append(hs_p[:, 0].reshape(b, C_GROUPS, C_STATE))
        outs["ip"].append(hs_p[:, 1].reshape(b, C_GROUPS, C_STATE))
        outs["rs"].append(hr_s.reshape(bs, C_GROUPS, C_STATE))
        outs["is"].append(hi_s.reshape(bs, C_GROUPS, C_STATE))

    y_p = x[0].reshape(b, l, D_MODEL)
    y_s = x[1].reshape(t, bs, D_MODEL).transpose(1, 0, 2)
    st = lambda k: jnp.stack(outs[k])
    return (y_p, y_s, st("gla_p"), st("gla_s"), st("kp"), st("ks"), st("vp"), st("vs"),
            st("rp"), st("rs"), st("ip"), st("is"))
```
